```python
import math
import jax
import jax.numpy as jnp
from jax import lax
import numpy as np


D_MODEL = 1024
BATCH = 8
SEQ = 2048
DEPTH = 2

CTX_LEN = 256
GRID_W = 64

ML_HEADS = 4
ML_DH = D_MODEL // 16
ML_W = ML_HEADS * ML_DH
ML_CHUNK = 64
ML_M_INIT = -1e30
DA_HEADS = 4
DA_DH = D_MODEL // 16
DA_VD = 2 * DA_DH
DA_W = DA_HEADS * DA_VD
DA_QBLOCK = 128
ROPE_THETA = 10000.0
HY_W = D_MODEL // 4
HY_ORDER = 2
HY_BANDS = 8
HY_POS_DIM = 1 + 2 * HY_BANDS
HY_FFN = 64
HY_SHIFT = 0.05
HY_TARGET = 1e-2
HY_FAST = 0.3
HY_SLOW = 1.5
MIX_W = ML_W + DA_W + HY_W
ML_COLS = 4 * ML_W + 4 * ML_HEADS
DA_COLS = 3 * DA_W
HY_COLS = (HY_ORDER + 1) * HY_W
IN_COLS = ML_COLS + DA_COLS + HY_COLS
N_EXPERTS = 256
TOP_K = 8
N_GROUPS = 8
TOPK_GROUPS = 4
EXPERT_FF = D_MODEL // 4
SHARED_FF = EXPERT_FF
ROUTED_SCALE = 2.5
MOE_BLOCK = 128
EPS = 1e-6

kernel_name = 'hybrid_mlstm_diffattn_hyena_moe_dit'

F32 = jnp.float32


def rms_norm(x, g):
    xf = x.astype(F32)
    y = xf * lax.rsqrt(jnp.mean(xf * xf, axis=-1, keepdims=True) + EPS)
    return y.astype(x.dtype) * g


def axial_rope_tables(n_tokens):
    rows = n_tokens // GRID_W
    r = jnp.repeat(jnp.arange(rows, dtype=F32), GRID_W)
    col = jnp.tile(jnp.arange(GRID_W, dtype=F32), rows)
    n_freq = DA_DH // 4
    inv = ROPE_THETA ** (-jnp.arange(n_freq, dtype=F32) / n_freq)
    ar = r[:, None] * inv
    ac = col[:, None] * inv
    ang = jnp.concatenate([ar, ar, ac, ac], axis=-1)
    return jnp.cos(ang), jnp.sin(ang)


def apply_axial_rope(x, cos, sin):
    a1, a2, b1, b2 = jnp.split(x, 4, axis=-1)
    rot = jnp.concatenate([-a2, a1, -b2, b1], axis=-1)
    cos = cos[:, None, None, :].astype(x.dtype)
    sin = sin[:, None, None, :].astype(x.dtype)
    return x * cos + rot * sin


def mlstm_chunkwise(q, k, v, i_pre, f_pre, state):
    B, H, T, dh = q.shape
    nc = T // ML_CHUNK

    def chunks(a):
        return jnp.moveaxis(a.reshape((B, H, nc, ML_CHUNK) + a.shape[3:]), 2, 0)

    logf = jax.nn.log_sigmoid(f_pre)
    tril = jnp.tril(jnp.ones((ML_CHUNK, ML_CHUNK), dtype=bool))

    def step(carry, inp):
        C, n, m = carry
        qc, kc, vc, ic, lfc = inp
        b = jnp.cumsum(lfc, axis=-1)
        dmat = jnp.where(tril, b[..., :, None] - b[..., None, :] + ic[..., None, :], -jnp.inf)
        inter = b + m[..., None]
        m_t = jnp.maximum(inter, jnp.max(dmat, axis=-1))
        w = jnp.exp(dmat - m_t[..., None])
        s = jnp.einsum('bhtd,bhsd->bhts', qc, kc) * w
        carry_w = jnp.exp(inter - m_t)
        num = jnp.einsum('bhts,bhse->bhte', s, vc) + carry_w[..., None] * jnp.einsum('bhed,bhtd->bhte', C, qc)
        den = jnp.sum(s, axis=-1) + carry_w * jnp.einsum('bhd,bhtd->bht', n, qc)
        h = num / jnp.maximum(jnp.abs(den), jnp.exp(-m_t))[..., None]
        b_end = b[..., -1]
        g = b_end[..., None] - b + ic
        m_new = jnp.maximum(b_end + m, jnp.max(g, axis=-1))
        ws = jnp.exp(g - m_new[..., None])
        decay = jnp.exp(b_end + m - m_new)
        C_new = decay[..., None, None] * C + jnp.einsum('bhs,bhse,bhsd->bhed', ws, vc, kc)
        n_new = decay[..., None] * n + jnp.einsum('bhs,bhsd->bhd', ws, kc)
        return (C_new, n_new, m_new), h

    state, hs = lax.scan(step, state, (chunks(q), chunks(k), chunks(v), chunks(i_pre), chunks(logf)))
    h = jnp.moveaxis(hs, 0, 2).reshape(B, H, T, dh)
    return h, state


def mlstm_parse(u, gate_b):
    B, T, _ = u.shape
    q, k, v, o, gates = jnp.split(u, [ML_W, 2 * ML_W, 3 * ML_W, 4 * ML_W], axis=-1)

    def heads(a):
        return a.reshape(B, T, ML_HEADS, ML_DH).transpose(0, 2, 1, 3).astype(F32)

    gates = (gates + gate_b).astype(F32).reshape(B, T, 4, ML_HEADS).transpose(2, 0, 3, 1)
    return heads(q), heads(k) * (ML_DH ** -0.5), heads(v), heads(o), gates


def mlstm_finish(h, o, norm_g, dtype):
    B, H, T, dh = h.shape
    hn = rms_norm(h, norm_g.reshape(ML_HEADS, 1, ML_DH).astype(F32))
    y = jax.nn.sigmoid(o) * hn
    return y.transpose(0, 2, 1, 3).reshape(B, T, ML_W).astype(dtype)


def mlstm_mixer(ul, uc, gate_b, norm_g, need_ctx):
    ql, kl, vl, ol, gl = mlstm_parse(ul, gate_b)
    qc, kc, vc, oc, gc = mlstm_parse(uc, gate_b)
    B = ul.shape[0]
    init = (jnp.zeros((B, ML_HEADS, ML_DH, ML_DH), F32),
            jnp.zeros((B, ML_HEADS, ML_DH), F32),
            jnp.full((B, ML_HEADS), ML_M_INIT, F32))

    def flip(a):
        return jnp.flip(a, axis=2)

    h_cf, st_f = mlstm_chunkwise(qc, kc, vc, gc[0], gc[1], init)
    h_lf, _ = mlstm_chunkwise(ql, kl, vl, gl[0], gl[1], st_f)
    h_cb, st_b = mlstm_chunkwise(flip(qc), flip(kc), flip(vc), flip(gc[2]), flip(gc[3]), init)
    h_lb, _ = mlstm_chunkwise(flip(ql), flip(kl), flip(vl), flip(gl[2]), flip(gl[3]), st_b)
    out_l = mlstm_finish(h_lf + flip(h_lb), ol, norm_g, ul.dtype)
    out_c = mlstm_finish(h_cf + flip(h_cb), oc, norm_g, uc.dtype) if need_ctx else None
    return out_l, out_c


def diff_parse(u, qg, kg):
    B, T, _ = u.shape
    q, k, v = jnp.split(u, [DA_W, 2 * DA_W], axis=-1)
    q = rms_norm(q.reshape(B, T, DA_HEADS, 2, DA_DH), qg)
    k = rms_norm(k.reshape(B, T, DA_HEADS, 2, DA_DH), kg)
    v = v.reshape(B, T, DA_HEADS, DA_VD)
    return q, k, v


def diff_block(qb, k, v, lam):
    s = jnp.einsum('bhqmd,bhkmd->bhmqk', qb, k).astype(F32) * (DA_DH ** -0.5)
    a = jax.nn.softmax(s, axis=-1)
    a = a[:, :, 0] - lam * a[:, :, 1]
    return jnp.einsum('bhqk,bhkd->bhqd', a.astype(v.dtype), v)


def diff_finish(o, subln_g, lam_init):
    B, H, T, _ = o.shape
    o = rms_norm(o, subln_g) * (1.0 - lam_init)
    return jnp.moveaxis(o, 1, 2).reshape(B, T, DA_W)


def diff_attn_mixer(ul, uc, qg, kg, lam_p, subln_g, lam_init, cos, sin, need_ctx):
    ql, kl, vl = diff_parse(ul, qg, kg)
    qc, kc, vc = diff_parse(uc, qg, kg)
    ql = apply_axial_rope(ql, cos, sin)
    kl = apply_axial_rope(kl, cos, sin)
    ql, kl, vl, qc, kc, vc = [jnp.moveaxis(a, 1, 2) for a in (ql, kl, vl, qc, kc, vc)]
    lp = lam_p.astype(F32)
    lam = jnp.exp(jnp.sum(lp[0] * lp[1])) - jnp.exp(jnp.sum(lp[2] * lp[3])) + lam_init
    k_all = jnp.concatenate([kc, kl], axis=2)
    v_all = jnp.concatenate([vc, vl], axis=2)
    B, H, S = ql.shape[:3]
    nb = S // DA_QBLOCK
    qblocks = jnp.moveaxis(ql.reshape(B, H, nb, DA_QBLOCK, 2, DA_DH), 2, 0)
    ob = lax.map(lambda qb: diff_block(qb, k_all, v_all, lam), qblocks)
    o_l = jnp.moveaxis(ob, 0, 2).reshape(B, H, S, DA_VD)
    out_l = diff_finish(o_l, subln_g, lam_init)
    out_c = diff_finish(diff_block(qc, kc, vc, lam), subln_g, lam_init) if need_ctx else None
    return out_l, out_c


def short_conv(u, w, b):
    up = jnp.pad(u, ((0, 0), (1, 1), (0, 0)))
    return up[:, :-2] * w[0] + up[:, 1:-1] * w[1] + up[:, 2:] * w[2] + b


def hyena_filter_freq(L, w1, b1, w2, b2, w3, freq):
    t01 = jnp.linspace(0.0, 1.0, L, dtype=F32)[:, None]
    wpos = (2.0 * math.pi / L) * jnp.arange(L, dtype=F32)[:, None]
    bands = jnp.linspace(1e-4, HY_BANDS - 1, HY_BANDS, dtype=F32)
    feats = jnp.concatenate([t01, jnp.cos(wpos * bands), -jnp.sin(wpos * bands)], axis=-1)
    freq = freq.astype(F32)
    h = jnp.sin(freq[0] * (feats @ w1.astype(F32) + b1.astype(F32)))
    h = jnp.sin(freq[1] * (h @ w2.astype(F32) + b2.astype(F32)))
    h = (h @ w3.astype(F32)).reshape(L, HY_ORDER, 2, HY_W)
    deltas = jnp.abs(jnp.linspace(math.log(HY_TARGET) / HY_SLOW, math.log(HY_TARGET) / HY_FAST, HY_W, dtype=F32))
    h = h * (jnp.exp(-t01 * deltas) + HY_SHIFT)[:, None, None, :]
    hf, hb = h[:, :, 0], h[:, :, 1]
    kern = jnp.concatenate([hf[:1] + hb[:1], hf[1:], jnp.zeros_like(hf[:1]), hb[:0:-1]], axis=0)
    kern = kern * lax.rsqrt(jnp.sum(kern * kern, axis=0, keepdims=True) + EPS)
    return jnp.fft.rfft(kern, axis=0)


def long_conv(z, kf):
    L = z.shape[1]
    zf = jnp.fft.rfft(z.astype(F32), n=2 * L, axis=1)
    return jnp.fft.irfft(zf * kf[None], n=2 * L, axis=1)[:, :L].astype(z.dtype)


def hyena_seq(u, conv_w, conv_b, fw1, fb1, fw2, fb2, fw3, ffreq, skip):
    L = u.shape[1]
    u = short_conv(u, conv_w, conv_b)
    v, x1, x2 = jnp.split(u, 3, axis=-1)
    kf = hyena_filter_freq(L, fw1, fb1, fw2, fb2, fw3, ffreq)
    z = x1 * (long_conv(v, kf[:, 0]) + skip[0] * v)
    z = x2 * (long_conv(z, kf[:, 1]) + skip[1] * z)
    return z


def token_mixer(hl, hc, w_in, w_out, gate_b, ml_norm_g, qg, kg, lam_p, subln_g, lam_init, cos, sin,
                conv_w, conv_b, fw1, fb1, fw2, fb2, fw3, ffreq, skip, need_ctx):
    ul = hl @ w_in
    n_ctx_cols = IN_COLS if need_ctx else ML_COLS + DA_COLS
    uc = hc @ w_in[:, :n_ctx_cols]
    ml_l, da_l, hy_l = ul[..., :ML_COLS], ul[..., ML_COLS:ML_COLS + DA_COLS], ul[..., ML_COLS + DA_COLS:]
    ml_c, da_c = uc[..., :ML_COLS], uc[..., ML_COLS:ML_COLS + DA_COLS]
    m_l, m_c = mlstm_mixer(ml_l, ml_c, gate_b, ml_norm_g, need_ctx)
    d_l, d_c = diff_attn_mixer(da_l, da_c, qg, kg, lam_p, subln_g, lam_init, cos, sin, need_ctx)
    y_l = hyena_seq(hy_l, conv_w, conv_b, fw1, fb1, fw2, fb2, fw3, ffreq, skip)
    out_l = jnp.concatenate([m_l, d_l, y_l], axis=-1) @ w_out
    out_c = None
    if need_ctx:
        y_c = hyena_seq(uc[..., ML_COLS + DA_COLS:], conv_w, conv_b, fw1, fb1, fw2, fb2, fw3, ffreq, skip)
        out_c = jnp.concatenate([m_c, d_c, y_c], axis=-1) @ w_out
    return out_l, out_c


def routed_experts(h, top_idx, top_w, w1, w3, w2):
    T, D = h.shape
    E = w1.shape[0]
    A = T * TOP_K
    n_blocks = -(-(A + E * (MOE_BLOCK - 1)) // MOE_BLOCK)
    P = n_blocks * MOE_BLOCK
    flat_e = top_idx.reshape(-1)
    flat_tok = jnp.arange(A, dtype=jnp.int32) // TOP_K
    flat_w = top_w.reshape(-1)
    order = jnp.argsort(flat_e)
    sorted_e = flat_e[order]
    counts = jnp.bincount(flat_e, length=E)
    starts = jnp.cumsum(counts) - counts
    pcounts = (counts + MOE_BLOCK - 1) // MOE_BLOCK * MOE_BLOCK
    pends = jnp.cumsum(pcounts)
    pstarts = pends - pcounts
    dest = pstarts[sorted_e] + jnp.arange(A, dtype=jnp.int32) - starts[sorted_e]
    buf_tok = jnp.zeros((P,), jnp.int32).at[dest].set(flat_tok[order])
    buf_w = jnp.zeros((P,), h.dtype).at[dest].set(flat_w[order])
    block_e = jnp.minimum(jnp.searchsorted(pends, jnp.arange(n_blocks) * MOE_BLOCK, side='right'), E - 1)

    def body(acc, blk):
        tok, wt, e = blk
        xb = h[tok]
        yb = (jax.nn.silu(xb @ w1[e]) * (xb @ w3[e])) @ w2[e]
        return acc.at[tok].add(wt[:, None] * yb), None

    acc, _ = lax.scan(body, jnp.zeros_like(h),
                      (buf_tok.reshape(n_blocks, MOE_BLOCK), buf_w.reshape(n_blocks, MOE_BLOCK), block_e))
    return acc


def moe_ffn(h, router_w, router_b, ew1, ew3, ew2, sw1, sw3, sw2):
    T = h.shape[0]
    scores = jax.nn.sigmoid((h @ router_w).astype(F32))
    sel = scores + router_b.astype(F32)
    grp = sel.reshape(T, N_GROUPS, N_EXPERTS // N_GROUPS)
    grp_score = jnp.sum(lax.top_k(grp, 2)[0], axis=-1)
    _, gidx = lax.top_k(grp_score, TOPK_GROUPS)
    gmask = jnp.sum(jax.nn.one_hot(gidx, N_GROUPS, dtype=F32), axis=1) > 0
    sel = jnp.where(jnp.repeat(gmask, N_EXPERTS // N_GROUPS, axis=1), sel, -jnp.inf)
    _, idx = lax.top_k(sel, TOP_K)
    w = jnp.take_along_axis(scores, idx, axis=1)
    w = w / jnp.sum(w, axis=-1, keepdims=True) * ROUTED_SCALE
    routed = routed_experts(h, idx, w.astype(h.dtype), ew1, ew3, ew2)
    shared = (jax.nn.silu(h @ sw1) * (h @ sw3)) @ sw2
    return routed + shared


def setup_inputs(seed: int = 0) -> dict:
    key = jax.random.key(seed)
    ks = iter(jax.random.split(key, 48))
    L = DEPTH
    D = D_MODEL

    def nrm(shape, scale):
        return jax.random.normal(next(ks), shape, F32) * scale

    gate_b = jnp.concatenate([
        nrm((L, ML_HEADS), 0.1),
        jnp.linspace(3.0, 6.0, ML_HEADS, dtype=F32)[None] + nrm((L, ML_HEADS), 0.1),
        nrm((L, ML_HEADS), 0.1),
        jnp.linspace(3.0, 6.0, ML_HEADS, dtype=F32)[None] + nrm((L, ML_HEADS), 0.1)], axis=-1)
    return {
        'x': nrm((BATCH, SEQ, D), 1.0),
        'c': nrm((BATCH, D), 1.0),
        'ctx': nrm((BATCH, CTX_LEN, D), 1.0),
        'c_ctx': nrm((D,), 1.0),
        'w_mod': nrm((L, D, 6 * D), 0.5 * D ** -0.5),
        'b_mod': nrm((L, 6 * D), 0.02),
        'norm1_g': 1.0 + nrm((L, D), 0.02),
        'norm2_g': 1.0 + nrm((L, D), 0.02),
        'w_in': nrm((L, D, IN_COLS), D ** -0.5),
        'mlstm_gate_b': gate_b,
        'mlstm_norm_g': 1.0 + nrm((L, ML_W), 0.02),
        'da_qnorm_g': 1.0 + nrm((L, DA_DH), 0.02),
        'da_knorm_g': 1.0 + nrm((L, DA_DH), 0.02),
        'da_lambda': nrm((L, 4, DA_DH), 0.1),
        'da_subln_g': 1.0 + nrm((L, DA_VD), 0.02),
        'hy_conv_w': nrm((L, 3, HY_COLS), 3 ** -0.5),
        'hy_conv_b': nrm((L, HY_COLS), 0.02),
        'hy_w1': nrm((L, HY_POS_DIM, HY_FFN), HY_POS_DIM ** -0.5),
        'hy_b1': nrm((L, HY_FFN), 0.1),
        'hy_w2': nrm((L, HY_FFN, HY_FFN), HY_FFN ** -0.5),
        'hy_b2': nrm((L, HY_FFN), 0.1),
        'hy_w3': nrm((L, HY_FFN, HY_ORDER * 2 * HY_W), HY_FFN ** -0.5),
        'hy_freq': 1.0 + nrm((L, 2, HY_FFN), 0.02),
        'hy_skip': nrm((L, HY_ORDER, HY_W), 1.0),
        'w_out': nrm((L, MIX_W, D), MIX_W ** -0.5),
        'router_w': nrm((L, D, N_EXPERTS), D ** -0.5),
        'router_b': nrm((L, N_EXPERTS), 0.01),
        'exp_w1': nrm((L, N_EXPERTS, D, EXPERT_FF), D ** -0.5),
        'exp_w3': nrm((L, N_EXPERTS, D, EXPERT_FF), D ** -0.5),
        'exp_w2': nrm((L, N_EXPERTS, EXPERT_FF, D), EXPERT_FF ** -0.5),
        'sh_w1': nrm((L, D, SHARED_FF), D ** -0.5),
        'sh_w3': nrm((L, D, SHARED_FF), D ** -0.5),
        'sh_w2': nrm((L, SHARED_FF, D), SHARED_FF ** -0.5),
    }


def reference(x, c, ctx, c_ctx, w_mod, b_mod, norm1_g, norm2_g, w_in, mlstm_gate_b, mlstm_norm_g,
              da_qnorm_g, da_knorm_g, da_lambda, da_subln_g, hy_conv_w, hy_conv_b, hy_w1, hy_b1,
              hy_w2, hy_b2, hy_w3, hy_freq, hy_skip, w_out, router_w, router_b, exp_w1, exp_w3,
              exp_w2, sh_w1, sh_w3, sh_w2):
    B, S, D = x.shape
    n_ctx = ctx.shape[1]
    cos, sin = axial_rope_tables(S)
    xl, xc = x, ctx
    sc = jax.nn.silu(c)
    scc = jax.nn.silu(c_ctx)
    for l in range(DEPTH):
        last = l == DEPTH - 1
        lam_init = 0.8 - 0.6 * math.exp(-0.3 * l)
        sh1, s1, g1, sh2, s2, g2 = [m[:, None, :] for m in jnp.split(sc @ w_mod[l] + b_mod[l], 6, axis=-1)]
        csh1, cs1, cg1, csh2, cs2, cg2 = jnp.split(scc @ w_mod[l] + b_mod[l], 6, axis=-1)
        hl = rms_norm(xl, norm1_g[l]) * (1.0 + s1) + sh1
        hc = rms_norm(xc, norm1_g[l]) * (1.0 + cs1) + csh1
        ol, oc = token_mixer(hl, hc, w_in[l], w_out[l], mlstm_gate_b[l], mlstm_norm_g[l],
                             da_qnorm_g[l], da_knorm_g[l], da_lambda[l], da_subln_g[l], lam_init, cos, sin,
                             hy_conv_w[l], hy_conv_b[l], hy_w1[l], hy_b1[l], hy_w2[l], hy_b2[l], hy_w3[l],
                             hy_freq[l], hy_skip[l], not last)
        xl = xl + g1 * ol
        h2l = rms_norm(xl, norm2_g[l]) * (1.0 + s2) + sh2
        if last:
            tokens = h2l.reshape(B * S, D)
        else:
            xc = xc + cg1 * oc
            h2c = rms_norm(xc, norm2_g[l]) * (1.0 + cs2) + csh2
            tokens = jnp.concatenate([h2l.reshape(B * S, D), h2c.reshape(B * n_ctx, D)], axis=0)
        y = moe_ffn(tokens, router_w[l], router_b[l], exp_w1[l], exp_w3[l], exp_w2[l],
                    sh_w1[l], sh_w3[l], sh_w2[l])
        xl = xl + g2 * y[:B * S].reshape(B, S, D)
        if not last:
            xc = xc + cg2 * y[B * S:].reshape(B, n_ctx, D)
    return xl
```

```python
import functools
import math

import jax
import jax.numpy as jnp
from jax import lax
from jax.experimental import pallas as pl
from jax.experimental.pallas import tpu as pltpu

F32 = jnp.float32
BF16 = jnp.bfloat16

EPS = 1e-6
GRID_W = 64
ROPE_THETA = 10000.0
ML_HEADS = 4
ML_CHUNK = 256
ML_M_INIT = -1e30
DA_HEADS = 4
HY_ORDER = 2
HY_BANDS = 8
HY_SHIFT = 0.05
HY_TARGET = 1e-2
HY_FAST = 0.3
HY_SLOW = 1.5
N_GROUPS = 8
TOPK_GROUPS = 4
TOP_K = 8
ROUTED_SCALE = 2.5
MOE_BLOCK = 128
LANES = 128
VMEM_LIMIT = 56 * 1024 * 1024


def _cparams(sem):
    return pltpu.CompilerParams(dimension_semantics=sem, vmem_limit_bytes=VMEM_LIMIT)


def _dot(a, b):
    return jnp.dot(a, b, preferred_element_type=F32)


def _dot_nt(a, b):
    return lax.dot_general(a, b, (((1,), (1,)), ((), ())), preferred_element_type=F32)


def _dot_tn(a, b):
    return lax.dot_general(a, b, (((0,), (0,)), ((), ())), preferred_element_type=F32)


def _split3(a):
    hi = a.astype(BF16)
    r = a - hi.astype(F32)
    mid = r.astype(BF16)
    lo = (r - mid.astype(F32)).astype(BF16)
    return hi, mid, lo


def _split2(a):
    hi = a.astype(BF16)
    lo = (a - hi.astype(F32)).astype(BF16)
    return hi, lo


def _norm_mod_mm_kernel(x_ref, g_ref, sc_ref, sh_ref, w_ref, o_ref, xn_ref):
    @pl.when(pl.program_id(1) == 0)
    def _():
        x = x_ref[...]
        y = x * lax.rsqrt(jnp.mean(x * x, axis=-1, keepdims=True) + EPS) * g_ref[...]
        xn_ref[...] = (y * (1.0 + sc_ref[0]) + sh_ref[0]).astype(BF16)

    o_ref[...] = _dot(xn_ref[...], w_ref[...])


def _norm_mod_matmul(x, g, scale, shift, w, seg_rows, tm, tn):
    R, D = x.shape
    N = w.shape[1]
    last = scale.shape[0] - 1
    mod_map = lambda i, j: (jnp.minimum(i * tm // seg_rows, last), 0, 0)
    return pl.pallas_call(
        _norm_mod_mm_kernel,
        grid=(R // tm, N // tn),
        in_specs=[
            pl.BlockSpec((tm, D), lambda i, j: (i, 0)),
            pl.BlockSpec((1, D), lambda i, j: (0, 0)),
            pl.BlockSpec((1, 1, D), mod_map),
            pl.BlockSpec((1, 1, D), mod_map),
            pl.BlockSpec((D, tn), lambda i, j: (0, j)),
        ],
        out_specs=pl.BlockSpec((tm, tn), lambda i, j: (i, j)),
        out_shape=jax.ShapeDtypeStruct((R, N), F32),
        scratch_shapes=[pltpu.VMEM((tm, D), BF16)],
        compiler_params=_cparams(("parallel", "arbitrary")),
        name="norm_mod_matmul",
    )(x, g, scale, shift, w)


def _out_proj_kernel(m_ref, d_ref, y_ref, wm_ref, wd_ref, wy_ref, x_ref, gate_ref, o_ref):
    acc = _dot(m_ref[...], wm_ref[...]) + _dot(d_ref[...], wd_ref[...]) + _dot(y_ref[...], wy_ref[...])
    o_ref[...] = x_ref[...] + gate_ref[0] * acc


def _out_proj_residual(m, d, y, wm, wd, wy, x, gate, n_rows, seg_rows, tm):
    R, D = n_rows, x.shape[1]
    last = gate.shape[0] - 1
    row = lambda i: (i, 0)
    full = lambda i: (0, 0)
    return pl.pallas_call(
        _out_proj_kernel,
        grid=(R // tm,),
        in_specs=[
            pl.BlockSpec((tm, m.shape[1]), row),
            pl.BlockSpec((tm, d.shape[1]), row),
            pl.BlockSpec((tm, y.shape[1]), row),
            pl.BlockSpec(wm.shape, full),
            pl.BlockSpec(wd.shape, full),
            pl.BlockSpec(wy.shape, full),
            pl.BlockSpec((tm, D), row),
            pl.BlockSpec((1, 1, D), lambda i: (jnp.minimum(i * tm // seg_rows, last), 0, 0)),
        ],
        out_specs=pl.BlockSpec((tm, D), row),
        out_shape=jax.ShapeDtypeStruct((R, D), F32),
        compiler_params=_cparams(("parallel",)),
        name="out_proj_residual",
    )(m, d, y, wm, wd, wy, x, gate)


def _log_sigmoid(x):
    return jnp.minimum(x, 0.0) - jnp.log1p(jnp.exp(-jnp.abs(x)))


def _mlstm_gate_tables(g_ref, r0, L, gb, tril, triu):
    g = g_ref[pl.ds(r0, L), :] + gb
    lf = _log_sigmoid(g)
    gT = g.T
    lfT = lf.T
    parts = _split3(lf)
    partsT = _split3(lfT)
    cs_f = sum(_dot(tril, p) for p in parts)
    cs_b = sum(_dot(triu, p) for p in parts)
    rs_f = sum(_dot(p, triu) for p in partsT)
    rs_b = sum(_dot(p, tril) for p in partsT)
    return g, gT, cs_f, cs_b, rs_f, rs_b


def _mlstm_chunk(q, k, v, i_col, i_row, b_col, b_row, b_end, mask, state):
    C, n, m = state
    qb = q.astype(BF16)
    kb = k.astype(BF16)
    dmat = jnp.where(mask, b_col - b_row + i_row, -jnp.inf)
    inter = b_col + m
    m_t = jnp.maximum(inter, jnp.max(dmat, axis=-1, keepdims=True))
    s = _dot_nt(qb, kb) * jnp.exp(dmat - m_t)
    carry_w = jnp.exp(inter - m_t)
    num = _dot(s.astype(BF16), v.astype(BF16)) + carry_w * _dot_nt(qb, C.astype(BF16))
    den = jnp.sum(s, axis=-1, keepdims=True) + carry_w * jnp.sum(q * n, axis=-1, keepdims=True)
    h = num / jnp.maximum(jnp.abs(den), jnp.exp(-m_t))
    g = b_end - b_col + i_col
    m_new = jnp.maximum(b_end + m, jnp.max(g, axis=0, keepdims=True))
    ws = jnp.exp(g - m_new)
    decay = jnp.exp(b_end + m - m_new)
    C_new = decay * C + _dot_tn((v * ws).astype(BF16), kb)
    n_new = decay * n + jnp.sum(ws * k, axis=0, keepdims=True)
    return h, (C_new, n_new, m_new)


def _mlstm_kernel(ql_ref, kl_ref, vl_ref, ol_ref, gl_ref, qc_ref, kc_ref, vc_ref, oc_ref, gc_ref,
                  gb_ref, ng_ref, outl_ref, outc_ref, hf_ref, hb_ref, *, L, H, dh):
    S = ql_ref.shape[0]
    n_ctx = qc_ref.shape[0]
    row = lax.broadcasted_iota(jnp.int32, (L, L), 0)
    col = lax.broadcasted_iota(jnp.int32, (L, L), 1)
    lower = col <= row
    upper = col >= row
    tril = lower.astype(BF16)
    triu = upper.astype(BF16)
    gb = gb_ref[...]
    k_scale = dh ** -0.5

    def both_dirs(refs_f, r0_f, refs_b, r0_b, hoff_f, hoff_b, state):
        new_state = []
        for d, (refs, r0, hoff, h_ref) in enumerate(((refs_f, r0_f, hoff_f, hf_ref),
                                                      (refs_b, r0_b, hoff_b, hb_ref))):
            q_ref, k_ref, v_ref, g_ref = refs
            g, gT, cs_f, cs_b, rs_f, rs_b = _mlstm_gate_tables(g_ref, r0, L, gb, tril, triu)
            cs, rs, mask = (cs_f, rs_f, lower) if d == 0 else (cs_b, rs_b, upper)
            end_row = L - 1 if d == 0 else 0
            for hh in range(H):
                ic = 2 * d * H + hh
                fc = ic + H
                lanes = slice(hh * dh, (hh + 1) * dh)
                q = q_ref[pl.ds(r0, L), lanes]
                k = k_ref[pl.ds(r0, L), lanes] * k_scale
                v = v_ref[pl.ds(r0, L), lanes]
                h, st = _mlstm_chunk(q, k, v, g[:, ic:ic + 1], gT[ic:ic + 1, :], cs[:, fc:fc + 1],
                                     rs[fc:fc + 1, :], cs[end_row:end_row + 1, fc:fc + 1], mask,
                                     state[d * H + hh])
                h_ref[pl.ds(hoff + r0, L), lanes] = h
                new_state.append(st)
        return tuple(new_state)

    state = tuple((jnp.zeros((dh, dh), F32), jnp.zeros((1, dh), F32), jnp.full((1, 1), ML_M_INIT, F32))
                  for _ in range(2 * H))
    ctx_refs = (qc_ref, kc_ref, vc_ref, gc_ref)
    lat_refs = (ql_ref, kl_ref, vl_ref, gl_ref)
    n_cc = n_ctx // L
    for c in range(n_cc):
        state = both_dirs(ctx_refs, c * L, ctx_refs, (n_cc - 1 - c) * L, 0, 0, state)
    n_lc = S // L

    def body(c, st):
        r_f = pl.multiple_of(c * L, L)
        r_b = pl.multiple_of((n_lc - 1 - c) * L, L)
        return both_dirs(lat_refs, r_f, lat_refs, r_b, n_ctx, n_ctx, st)

    lax.fori_loop(0, n_lc, body, state)

    def finish(o_ref, out_ref, hoff, rows):
        def fbody(c, carry):
            r0 = pl.multiple_of(c * L, L)
            hs = hf_ref[pl.ds(hoff + r0, L), :] + hb_ref[pl.ds(hoff + r0, L), :]
            for hh in range(H):
                lanes = slice(hh * dh, (hh + 1) * dh)
                hv = hs[:, lanes]
                hn = hv * lax.rsqrt(jnp.mean(hv * hv, axis=-1, keepdims=True) + EPS) * ng_ref[:, lanes]
                out_ref[pl.ds(r0, L), lanes] = (jax.nn.sigmoid(o_ref[pl.ds(r0, L), lanes]) * hn).astype(out_ref.dtype)
            return carry
        lax.fori_loop(0, rows // L, fbody, 0)

    finish(ol_ref, outl_ref, n_ctx, S)
    finish(oc_ref, outc_ref, 0, n_ctx)


def _mlstm_mixer(u, gate_b, norm_g, B, S, n_ctx, col0, gate_col0):
    W = norm_g.shape[1]
    H = ML_HEADS
    dh = W // H
    cb = col0 // W
    gcb = gate_col0 // LANES
    cblk = (B * S) // n_ctx

    def lat(j):
        return pl.BlockSpec((S, W), lambda b: (b, cb + j))

    def ctx(j):
        return pl.BlockSpec((n_ctx, W), lambda b: (cblk + b, cb + j))

    one = lambda b: (0, 0)
    out_l, out_c = pl.pallas_call(
        functools.partial(_mlstm_kernel, L=ML_CHUNK, H=H, dh=dh),
        grid=(B,),
        in_specs=[lat(0), lat(1), lat(2), lat(3), pl.BlockSpec((S, LANES), lambda b: (b, gcb)),
                  ctx(0), ctx(1), ctx(2), ctx(3), pl.BlockSpec((n_ctx, LANES), lambda b: (cblk + b, gcb)),
                  pl.BlockSpec((1, LANES), one), pl.BlockSpec((1, W), one)],
        out_specs=[pl.BlockSpec((S, W), lambda b: (b, 0)), pl.BlockSpec((n_ctx, W), lambda b: (b, 0))],
        out_shape=[jax.ShapeDtypeStruct((B * S, W), BF16), jax.ShapeDtypeStruct((B * n_ctx, W), BF16)],
        scratch_shapes=[pltpu.VMEM((n_ctx + S, W), F32), pltpu.VMEM((n_ctx + S, W), F32)],
        compiler_params=_cparams(("parallel",)),
        name="mlstm",
    )(u, u, u, u, u, u, u, u, u, u, gate_b, norm_g)
    return jnp.concatenate([out_l, out_c], axis=0)


def _da_prep_kernel(q_ref, k_ref, v_ref, cos_ref, sin_ref, qg_ref, kg_ref, seg_ref, qo_ref, ko_ref, vo_ref, *, dh):
    cos = cos_ref[...]
    sin = sin_ref[...]
    seg = seg_ref[...]
    W = q_ref.shape[1]
    lane = lax.broadcasted_iota(jnp.int32, (1, W), 1)
    quarter = dh // 4
    first = (lane % (2 * quarter)) < quarter

    def norm_rope(x, g):
        hi, lo = _split2(x * x)
        ms = (_dot(hi, seg) + _dot(lo, seg)) * (1.0 / dh)
        xn = x * lax.rsqrt(ms + EPS) * g
        rot = jnp.where(first, -pltpu.roll(xn, W - quarter, 1), pltpu.roll(xn, quarter, 1))
        return xn * cos + rot * sin

    qo_ref[...] = (norm_rope(q_ref[...], qg_ref[...]) * (dh ** -0.5)).astype(BF16)
    ko_ref[...] = norm_rope(k_ref[...], kg_ref[...]).astype(BF16)
    vo_ref[...] = v_ref[...].astype(BF16)


def _da_prep(u, cos, sin, qg, kg, seg, n_lat_rows, S, col0, tm, dh):
    R = u.shape[0]
    W = qg.shape[1]
    cb = col0 // W
    n_lat = n_lat_rows // tm
    per_seq = S // tm
    tab = lambda i: (jnp.where(i < n_lat, i % per_seq, per_seq), 0)
    one = lambda i: (0, 0)
    row = lambda i: (i, 0)

    def ucol(j):
        return pl.BlockSpec((tm, W), lambda i: (i, cb + j))

    return pl.pallas_call(
        functools.partial(_da_prep_kernel, dh=dh),
        grid=(R // tm,),
        in_specs=[ucol(0), ucol(1), ucol(2), pl.BlockSpec((tm, W), tab), pl.BlockSpec((tm, W), tab),
                  pl.BlockSpec((1, W), one), pl.BlockSpec((1, W), one), pl.BlockSpec((W, W), one)],
        out_specs=[pl.BlockSpec((tm, W), row)] * 3,
        out_shape=[jax.ShapeDtypeStruct((R, W), BF16)] * 3,
        compiler_params=_cparams(("parallel",)),
        name="da_prep",
    )(u, u, u, cos, sin, qg, kg, seg)


def _da_attn_kernel(*refs, n_kv, dh, lam_init):
    q_ref = refs[0]
    k_refs = refs[1:1 + n_kv]
    v_refs = refs[1 + n_kv:1 + 2 * n_kv]
    lam_ref, sg_ref, o_ref = refs[1 + 2 * n_kv:]
    lp = lam_ref[...]
    lam = (jnp.exp(jnp.sum(lp[0:1] * lp[1:2], axis=-1, keepdims=True))
           - jnp.exp(jnp.sum(lp[2:3] * lp[3:4], axis=-1, keepdims=True)) + lam_init)
    q = q_ref[...]
    acc = None
    probs = []
    for mp in range(2):
        lanes = slice(mp * dh, (mp + 1) * dh)
        s = [_dot_nt(q[:, lanes], k_ref[:, lanes]) for k_ref in k_refs]
        mx = functools.reduce(jnp.maximum, [jnp.max(si, axis=-1, keepdims=True) for si in s])
        p = [jnp.exp(si - mx) for si in s]
        den = sum(jnp.sum(pi, axis=-1, keepdims=True) for pi in p)
        probs.append([pi / den for pi in p])
    for j in range(n_kv):
        a = (probs[0][j] - lam * probs[1][j]).astype(BF16)
        t = _dot(a, v_refs[j][...])
        acc = t if acc is None else acc + t
    o = acc * lax.rsqrt(jnp.mean(acc * acc, axis=-1, keepdims=True) + EPS) * sg_ref[...]
    o_ref[...] = (o * (1.0 - lam_init)).astype(o_ref.dtype)


def _da_attention(q, k, v, lam_p, subln_g, lam_init, B, q_rows, q_blk0, kv_segs, tq, dh):
    H = DA_HEADS
    vd = 2 * dh
    nq = q_rows // tq
    q0 = q_blk0

    def kv_spec(rows, blk0):
        return pl.BlockSpec((rows, vd), lambda b, h, i: (blk0 + b, h))

    kspecs = [kv_spec(r, b0) for r, b0 in kv_segs]
    one = lambda b, h, i: (0, 0)
    return pl.pallas_call(
        functools.partial(_da_attn_kernel, n_kv=len(kv_segs), dh=dh, lam_init=lam_init),
        grid=(B, H, nq),
        in_specs=[pl.BlockSpec((tq, vd), lambda b, h, i: (q0 + b * nq + i, h))] + kspecs + kspecs
                 + [pl.BlockSpec(lam_p.shape, one), pl.BlockSpec((1, vd), one)],
        out_specs=pl.BlockSpec((tq, vd), lambda b, h, i: (b * nq + i, h)),
        out_shape=jax.ShapeDtypeStruct((B * q_rows, H * vd), BF16),
        compiler_params=_cparams(("parallel", "parallel", "arbitrary")),
        name="da_attention",
    )(q, *([k] * len(kv_segs)), *([v] * len(kv_segs)), lam_p, subln_g)


def _axial_rope_tables(S, dh, reps, pad_rows):
    rows = S // GRID_W
    r = jnp.repeat(jnp.arange(rows, dtype=F32), GRID_W)
    col = jnp.tile(jnp.arange(GRID_W, dtype=F32), rows)
    n_freq = dh // 4
    inv = ROPE_THETA ** (-jnp.arange(n_freq, dtype=F32) / n_freq)
    ar = r[:, None] * inv
    ac = col[:, None] * inv
    ang = jnp.concatenate([ar, ar, ac, ac], axis=-1)
    cos = jnp.concatenate([jnp.tile(jnp.cos(ang), (1, reps)), jnp.ones((pad_rows, dh * reps), F32)], axis=0)
    sin = jnp.concatenate([jnp.tile(jnp.sin(ang), (1, reps)), jnp.zeros((pad_rows, dh * reps), F32)], axis=0)
    return cos, sin


def _diff_attn_mixer(u, cos, sin, qg, kg, lam_p, subln_g, lam_init, B, S, n_ctx, col0, need_ctx):
    dh = qg.shape[0]
    W = DA_HEADS * 2 * dh
    seg = (jnp.arange(W)[:, None] // dh == jnp.arange(W)[None, :] // dh).astype(BF16)
    tm = 512
    q, k, v = _da_prep(u, cos, sin, jnp.tile(qg, W // dh)[None], jnp.tile(kg, W // dh)[None], seg,
                       B * S, S, col0, tm, dh)
    sg = subln_g[None]
    ctx_blk0 = (B * S) // n_ctx
    tq = 256
    out_l = _da_attention(q, k, v, lam_p, sg, lam_init, B, S, 0, [(n_ctx, ctx_blk0), (S, 0)], tq, dh)
    if not need_ctx:
        return out_l
    out_c = _da_attention(q, k, v, lam_p, sg, lam_init, B, n_ctx, (B * S) // n_ctx, [(n_ctx, ctx_blk0)], n_ctx, dh)
    return jnp.concatenate([out_l, out_c], axis=0)


def _hy_conv_kernel(v_ref, x1_ref, x2_ref, w_ref, b_ref, vo_ref, x1o_ref, x2o_ref):
    L, W = v_ref.shape
    row = lax.broadcasted_iota(jnp.int32, (L, 1), 0)
    for j, (i_ref, o_ref) in enumerate(((v_ref, vo_ref), (x1_ref, x1o_ref), (x2_ref, x2o_ref))):
        lanes = slice(j * W, (j + 1) * W)
        u = i_ref[...]
        prev = jnp.where(row == 0, 0.0, pltpu.roll(u, 1, 0))
        nxt = jnp.where(row == L - 1, 0.0, pltpu.roll(u, L - 1, 0))
        o_ref[...] = prev * w_ref[0:1, lanes] + u * w_ref[1:2, lanes] + nxt * w_ref[2:3, lanes] + b_ref[:, lanes]


def _hy_short_conv(u, conv_w, conv_b, n_seg, L, blk0, col0):
    W = conv_w.shape[1] // 3
    cb = col0 // W
    one = lambda b: (0, 0)

    def ucol(j):
        return pl.BlockSpec((L, W), lambda b: (blk0 + b, cb + j))

    return pl.pallas_call(
        _hy_conv_kernel,
        grid=(n_seg,),
        in_specs=[ucol(0), ucol(1), ucol(2), pl.BlockSpec(conv_w.shape, one), pl.BlockSpec(conv_b.shape, one)],
        out_specs=[pl.BlockSpec((L, W), lambda b: (b, 0))] * 3,
        out_shape=[jax.ShapeDtypeStruct((n_seg * L, W), F32)] * 3,
        compiler_params=_cparams(("parallel",)),
        name="hy_short_conv",
    )(u, u, u, conv_w, conv_b)


def _hy_fwd_kernel(c_ref, s_ref, z_ref, *rest, raw):
    z = z_ref[...].astype(BF16)
    zr = _dot(c_ref[...], z)
    zi = _dot(s_ref[...], z)
    if raw:
        yr_ref, yi_ref = rest
        yr_ref[...] = zr
        yi_ref[...] = zi
    else:
        a_ref, b_ref, d_ref, yr_ref, yi_ref = rest
        yr_ref[...] = (zr * a_ref[...] - zi * b_ref[...]).astype(yr_ref.dtype)
        yi_ref[...] = (zr * b_ref[...] + zi * d_ref[...]).astype(yi_ref.dtype)


def _hy_fwd(cm, sm, z, coefs, n_seg, L, tk):
    W = z.shape[1]
    nk = L // tk
    raw = coefs is None
    mat = pl.BlockSpec((tk, L), lambda i, b: (i, 0))
    cf = pl.BlockSpec((tk, W), lambda i, b: (i, 0))
    out = pl.BlockSpec((tk, W), lambda i, b: (b * nk + i, 0))
    odt = F32 if raw else BF16
    return pl.pallas_call(
        functools.partial(_hy_fwd_kernel, raw=raw),
        grid=(nk, n_seg),
        in_specs=[mat, mat, pl.BlockSpec((L, W), lambda i, b: (b, 0))] + ([] if raw else [cf, cf, cf]),
        out_specs=[out, out],
        out_shape=[jax.ShapeDtypeStruct((n_seg * L, W), odt)] * 2,
        compiler_params=_cparams(("parallel", "arbitrary")),
        name="hy_dft_fwd",
    )(cm, sm, z, *(() if raw else coefs))


def _hy_inv_kernel(c_ref, st_ref, yr_ref, yi_ref, x_ref, vz_ref, skip_ref, o_ref):
    y = _dot(c_ref[...], yr_ref[...]) + _dot(st_ref[...], yi_ref[...])
    o_ref[...] = (x_ref[...] * (y + skip_ref[...] * vz_ref[...])).astype(o_ref.dtype)


def _hy_inv(cm, smt, yr, yi, xg, vz, skip, n_seg, L, tt, out_dtype):
    W = yr.shape[1]
    nt = L // tt
    mat = pl.BlockSpec((tt, L), lambda i, b: (i, 0))
    seq = pl.BlockSpec((L, W), lambda i, b: (b, 0))
    row = pl.BlockSpec((tt, W), lambda i, b: (b * nt + i, 0))
    return pl.pallas_call(
        _hy_inv_kernel,
        grid=(nt, n_seg),
        in_specs=[mat, mat, seq, seq, row, row, pl.BlockSpec((1, W), lambda i, b: (0, 0))],
        out_specs=row,
        out_shape=jax.ShapeDtypeStruct((n_seg * L, W), out_dtype),
        compiler_params=_cparams(("parallel", "arbitrary")),
        name="hy_dft_inv",
    )(cm, smt, yr, yi, xg, vz, skip)


def _dft_mats(L):
    k = jnp.arange(L, dtype=jnp.int32)
    kn = (k[:, None] * k[None, :]) % (2 * L)
    ang = kn.astype(F32) * (math.pi / L)
    cm = jnp.cos(ang)
    sm = -jnp.sin(ang)
    sm = sm.at[0].set(jnp.where(k % 2 == 0, 1.0, -1.0))
    return cm.astype(BF16), sm.astype(BF16), sm.T.astype(BF16)


def _hyena_filters(L, w1, b1, w2, b2, w3, freq, W):
    t01 = jnp.linspace(0.0, 1.0, L, dtype=F32)[:, None]
    wpos = (2.0 * math.pi / L) * jnp.arange(L, dtype=F32)[:, None]
    bands = jnp.linspace(1e-4, HY_BANDS - 1, HY_BANDS, dtype=F32)
    feats = jnp.concatenate([t01, jnp.cos(wpos * bands), -jnp.sin(wpos * bands)], axis=-1)
    hp = lax.Precision.HIGHEST
    h = jnp.sin(freq[0] * (jnp.dot(feats, w1, precision=hp) + b1))
    h = jnp.sin(freq[1] * (jnp.dot(h, w2, precision=hp) + b2))
    h = jnp.dot(h, w3, precision=hp).reshape(L, HY_ORDER, 2, W)
    deltas = jnp.abs(jnp.linspace(math.log(HY_TARGET) / HY_SLOW, math.log(HY_TARGET) / HY_FAST, W, dtype=F32))
    h = h * (jnp.exp(-t01 * deltas) + HY_SHIFT)[:, None, None, :]
    hf, hb = h[:, :, 0], h[:, :, 1]
    hf = hf.at[0].add(hb[0])
    hb = hb.at[0].set(0.0)
    scale = lax.rsqrt(jnp.sum(hf * hf, axis=0, keepdims=True) + jnp.sum(hb * hb, axis=0, keepdims=True) + EPS)
    return (hf * scale).reshape(L, HY_ORDER * W), (hb * scale).reshape(L, HY_ORDER * W)


def _hyena_spectrum(mats, L, w1, b1, w2, b2, w3, freq, W):
    cm, sm, _ = mats
    hf, hb = _hyena_filters(L, w1, b1, w2, b2, w3, freq, W)
    cols = jnp.concatenate([hf[:, :W], hf[:, W:], hb[:, :W], hb[:, W:]], axis=0)
    tk = min(L, 512)
    gr, gi = _hy_fwd(cm, sm, cols, None, 2 * HY_ORDER, L, tk)
    gr = gr.reshape(2, HY_ORDER, L, W)
    gi = gi.reshape(2, HY_ORDER, L, W)
    kr = gr[0] + gr[1]
    ki = gi[0] - gi[1]
    nyq = gi[0, :, 0] + gi[1, :, 0]
    n = 2.0 * L
    wk = jnp.full((L, 1), 2.0 / n, F32).at[0].set(1.0 / n)
    a = kr * wk
    bm = (ki * wk).at[:, 0].set(0.0)
    dd = a.at[:, 0].set(nyq / n)
    return [(a[o], bm[o], dd[o]) for o in range(HY_ORDER)]


def _hyena_seq(u, mats, spec, conv_w, conv_b, skip, n_seg, L, blk0, col0):
    cm, sm, smt = mats
    t = min(L, 512)
    v, x1, x2 = _hy_short_conv(u, conv_w, conv_b, n_seg, L, blk0, col0)
    yr, yi = _hy_fwd(cm, sm, v, spec[0], n_seg, L, t)
    z = _hy_inv(cm, smt, yr, yi, x1, v, skip[0:1], n_seg, L, t, F32)
    yr, yi = _hy_fwd(cm, sm, z, spec[1], n_seg, L, t)
    return _hy_inv(cm, smt, yr, yi, x2, z, skip[1:2], n_seg, L, t, BF16)


def _moe_router_kernel(x_ref, g_ref, sc_ref, sh_ref, whi_ref, wlo_ref, rb_ref, h_ref, idx_ref, wts_ref):
    x = x_ref[...]
    y = x * lax.rsqrt(jnp.mean(x * x, axis=-1, keepdims=True) + EPS) * g_ref[...]
    h = y * (1.0 + sc_ref[0]) + sh_ref[0]
    h_ref[...] = h
    hi, lo = _split2(h)
    logits = _dot(hi, whi_ref[...]) + _dot(hi, wlo_ref[...]) + _dot(lo, whi_ref[...])
    scores = jax.nn.sigmoid(logits)
    sel = scores + rb_ref[...]
    tm, E = sel.shape
    gsz = E // N_GROUPS
    neg = -jnp.inf
    lane = lax.broadcasted_iota(jnp.int32, (1, E), 1).astype(F32)
    glane = lax.broadcasted_iota(jnp.int32, (1, gsz), 1).astype(F32)
    gscore = []
    for g in range(N_GROUPS):
        blk = sel[:, g * gsz:(g + 1) * gsz]
        m1 = jnp.max(blk, axis=-1, keepdims=True)
        first = jnp.min(jnp.where(blk == m1, glane, float(gsz)), axis=-1, keepdims=True)
        m2 = jnp.max(jnp.where(glane == first, neg, blk), axis=-1, keepdims=True)
        gscore.append(m1 + m2)
    group_of_lane = lax.broadcasted_iota(jnp.int32, (1, E), 1) // gsz
    keep = jnp.zeros((tm, E), F32)
    for g in range(N_GROUPS):
        rank = jnp.zeros((tm, 1), F32)
        for o in range(N_GROUPS):
            if o != g:
                ahead = (gscore[o] >= gscore[g]) if o < g else (gscore[o] > gscore[g])
                rank = rank + jnp.where(ahead, 1.0, 0.0)
        keep = jnp.where(group_of_lane == g, jnp.where(rank < TOPK_GROUPS, 1.0, 0.0), keep)
    work = jnp.where(keep > 0.0, sel, neg)
    out_lane = lax.broadcasted_iota(jnp.int32, (1, idx_ref.shape[1]), 1)
    idx_out = jnp.zeros(idx_ref.shape, F32)
    w_out = jnp.zeros(wts_ref.shape, F32)
    total = jnp.zeros((tm, 1), F32)
    for j in range(TOP_K):
        mx = jnp.max(work, axis=-1, keepdims=True)
        am = jnp.min(jnp.where(work == mx, lane, float(E)), axis=-1, keepdims=True)
        hit = lane == am
        wj = jnp.sum(jnp.where(hit, scores, 0.0), axis=-1, keepdims=True)
        work = jnp.where(hit, neg, work)
        idx_out = jnp.where(out_lane == j, am, idx_out)
        w_out = jnp.where(out_lane == j, wj, w_out)
        total = total + wj
    idx_ref[...] = idx_out.astype(jnp.int32)
    wts_ref[...] = w_out / total * ROUTED_SCALE


def _moe_router(x, n_tok, g, scale, shift, whi, wlo, rb, seg_rows, tm):
    D = x.shape[1]
    E = whi.shape[1]
    last = scale.shape[0] - 1
    row = lambda i: (i, 0)
    one = lambda i: (0, 0)
    mod = lambda i: (jnp.minimum(i * tm // seg_rows, last), 0, 0)
    return pl.pallas_call(
        _moe_router_kernel,
        grid=(n_tok // tm,),
        in_specs=[pl.BlockSpec((tm, D), row), pl.BlockSpec((1, D), one), pl.BlockSpec((1, 1, D), mod),
                  pl.BlockSpec((1, 1, D), mod), pl.BlockSpec((D, E), one), pl.BlockSpec((D, E), one),
                  pl.BlockSpec((1, E), one)],
        out_specs=[pl.BlockSpec((tm, D), row), pl.BlockSpec((tm, LANES), row), pl.BlockSpec((tm, LANES), row)],
        out_shape=[jax.ShapeDtypeStruct((n_tok, D), F32), jax.ShapeDtypeStruct((n_tok, LANES), jnp.int32),
                   jax.ShapeDtypeStruct((n_tok, LANES), F32)],
        compiler_params=_cparams(("parallel",)),
        name="moe_router",
    )(x, g, scale, shift, whi, wlo, rb)


def _moe_plan(idx, E, bm):
    n_tok, K = idx.shape
    onehot = jnp.sum((idx[:, :, None] == jnp.arange(E, dtype=jnp.int32)[None, None, :]).astype(jnp.int32), axis=1)
    csum = jnp.cumsum(onehot, axis=0)
    counts = csum[-1]
    pcounts = (counts + bm - 1) // bm * bm
    pends = jnp.cumsum(pcounts)
    pstarts = pends - pcounts
    pos = jnp.take_along_axis(csum - onehot + pstarts[None, :], idx, axis=1)
    n_blocks = -(-(n_tok * K + E * (bm - 1)) // bm)
    block_e = jnp.minimum(jnp.searchsorted(pends, jnp.arange(n_blocks, dtype=jnp.int32) * bm, side='right'), E - 1)
    n_used = (pends[-1] // bm).astype(jnp.int32).reshape(1)
    return pos.astype(jnp.int32).reshape(-1), block_e.astype(jnp.int32), n_used, n_blocks


def _moe_dispatch_kernel(pos_ref, h_ref, xs_in_ref, xs_ref, sem, *, K):
    del xs_in_ref
    n = pos_ref.shape[0]

    def row_copy(a):
        return pltpu.make_async_copy(h_ref.at[pl.ds(a // K, 1)], xs_ref.at[pl.ds(pos_ref[a], 1)], sem)

    def start(a, carry):
        row_copy(a).start()
        return carry

    def wait(a, carry):
        row_copy(a).wait()
        return carry

    lax.fori_loop(0, n, start, 0)
    lax.fori_loop(0, n, wait, 0)


def _moe_dispatch(pos, h, n_rows, K, tm):
    n_tok, D = h.shape
    xs0 = jnp.zeros((n_rows, D), h.dtype)
    return pl.pallas_call(
        functools.partial(_moe_dispatch_kernel, K=K),
        grid=(n_tok // tm,),
        in_specs=[pl.BlockSpec((tm * K,), lambda i: (i,), memory_space=pltpu.SMEM),
                  pl.BlockSpec((tm, D), lambda i: (i, 0)),
                  pl.BlockSpec(memory_space=pl.ANY)],
        out_specs=pl.BlockSpec(memory_space=pl.ANY),
        out_shape=jax.ShapeDtypeStruct((n_rows, D), h.dtype),
        scratch_shapes=[pltpu.SemaphoreType.DMA(())],
        input_output_aliases={2: 0},
        compiler_params=_cparams(("arbitrary",)),
        name="moe_dispatch",
    )(pos, h, xs0)


def _moe_expert_kernel(be_ref, nu_ref, x_ref, w1_ref, w3_ref, w2_ref, y_ref):
    del be_ref
    used = pl.program_id(0) < nu_ref[0]

    @pl.when(used)
    def _():
        x = x_ref[...].astype(BF16)
        a = _dot(x, w1_ref[0].astype(BF16))
        b = _dot(x, w3_ref[0].astype(BF16))
        y_ref[...] = _dot((jax.nn.silu(a) * b).astype(BF16), w2_ref[0].astype(BF16))

    @pl.when(jnp.logical_not(used))
    def _():
        y_ref[...] = jnp.zeros_like(y_ref)


def _moe_experts(block_e, n_used, xs, w1, w3, w2, bm):
    P, D = xs.shape
    F = w1.shape[2]
    grid_spec = pltpu.PrefetchScalarGridSpec(
        num_scalar_prefetch=2,
        grid=(P // bm,),
        in_specs=[pl.BlockSpec((bm, D), lambda i, be, nu: (i, 0)),
                  pl.BlockSpec((1, D, F), lambda i, be, nu: (be[i], 0, 0)),
                  pl.BlockSpec((1, D, F), lambda i, be, nu: (be[i], 0, 0)),
                  pl.BlockSpec((1, F, D), lambda i, be, nu: (be[i], 0, 0))],
        out_specs=pl.BlockSpec((bm, D), lambda i, be, nu: (i, 0)),
    )
    return pl.pallas_call(
        _moe_expert_kernel,
        grid_spec=grid_spec,
        out_shape=jax.ShapeDtypeStruct((P, D), F32),
        compiler_params=_cparams(("arbitrary",)),
        name="moe_experts",
    )(block_e, n_used, xs, w1, w3, w2)


def _moe_combine_kernel(pos_ref, wts_ref, x_ref, h_ref, sw1_ref, sw3_ref, sw2_ref, gate_ref, ys_ref, o_ref,
                        gath_ref, sem, *, K):
    n = pos_ref.shape[0]

    def row_copy(a):
        return pltpu.make_async_copy(ys_ref.at[pl.ds(pos_ref[a], 1)], gath_ref.at[a % K, pl.ds(a // K, 1)], sem)

    def start(a, carry):
        row_copy(a).start()
        return carry

    def wait(a, carry):
        row_copy(a).wait()
        return carry

    lax.fori_loop(0, n, start, 0)
    h = h_ref[...].astype(BF16)
    mid = jax.nn.silu(_dot(h, sw1_ref[...])) * _dot(h, sw3_ref[...])
    acc = _dot(mid.astype(BF16), sw2_ref[...])
    lax.fori_loop(0, n, wait, 0)
    wts = wts_ref[...]
    for j in range(K):
        acc = acc + wts[:, j:j + 1] * gath_ref[j]
    o_ref[...] = x_ref[...] + gate_ref[0] * acc


def _moe_combine(pos, wts, x, h, sw1, sw3, sw2, gate, ys, n_tok, K, seg_rows, tm):
    D = x.shape[1]
    last = gate.shape[0] - 1
    row = lambda i: (i, 0)
    one = lambda i: (0, 0)
    return pl.pallas_call(
        functools.partial(_moe_combine_kernel, K=K),
        grid=(n_tok // tm,),
        in_specs=[pl.BlockSpec((tm * K,), lambda i: (i,), memory_space=pltpu.SMEM),
                  pl.BlockSpec((tm, LANES), row), pl.BlockSpec((tm, D), row), pl.BlockSpec((tm, D), row),
                  pl.BlockSpec(sw1.shape, one), pl.BlockSpec(sw3.shape, one), pl.BlockSpec(sw2.shape, one),
                  pl.BlockSpec((1, 1, D), lambda i: (jnp.minimum(i * tm // seg_rows, last), 0, 0)),
                  pl.BlockSpec(memory_space=pl.ANY)],
        out_specs=pl.BlockSpec((tm, D), row),
        out_shape=jax.ShapeDtypeStruct((n_tok, D), F32),
        scratch_shapes=[pltpu.VMEM((K, tm, D), F32), pltpu.SemaphoreType.DMA(())],
        compiler_params=_cparams(("arbitrary",)),
        name="moe_combine",
    )(pos, wts, x, h, sw1, sw3, sw2, gate, ys)


def _moe_ffn_residual(x, n_tok, norm_g, scale, shift, gate, router_w, router_b, ew1, ew3, ew2, sw1, sw3, sw2, seg_rows):
    E = router_w.shape[1]
    K = TOP_K
    whi, wlo = _split2(router_w)
    h, idx, wts = _moe_router(x, n_tok, norm_g, scale, shift, whi, wlo, router_b[None], seg_rows, 256)
    pos, block_e, n_used, n_blocks = _moe_plan(idx[:, :K], E, MOE_BLOCK)
    xs = _moe_dispatch(pos, h, n_blocks * MOE_BLOCK, K, 128)
    ys = _moe_experts(block_e, n_used, xs, ew1, ew3, ew2, MOE_BLOCK)
    return _moe_combine(pos, wts, x, h, sw1.astype(BF16), sw3.astype(BF16), sw2.astype(BF16), gate, ys,
                        n_tok, K, seg_rows, 128)


def kernel(x, c, ctx, c_ctx, w_mod, b_mod, norm1_g, norm2_g, w_in, mlstm_gate_b, mlstm_norm_g, da_qnorm_g, da_knorm_g, da_lambda, da_subln_g, hy_conv_w, hy_conv_b, hy_w1, hy_b1, hy_w2, hy_b2, hy_w3, hy_freq, hy_skip, w_out, router_w, router_b, exp_w1, exp_w3, exp_w2, sh_w1, sh_w3, sh_w2):
    B, S, D = x.shape
    n_ctx = ctx.shape[1]
    depth = w_in.shape[0]
    n_lat = B * S
    ml_w = mlstm_norm_g.shape[1]
    da_dh = da_qnorm_g.shape[1]
    da_w = DA_HEADS * 2 * da_dh
    hy_w = hy_skip.shape[2]
    n_gates = 4 * ML_HEADS
    ml_col, da_col = 0, 4 * ml_w
    hy_col = da_col + 3 * da_w
    gate_col = hy_col + 3 * hy_w
    tm = 512

    X = jnp.concatenate([x.reshape(n_lat, D), ctx.reshape(B * n_ctx, D)], axis=0)
    sc = jax.nn.silu(jnp.concatenate([c, c_ctx[None]], axis=0))
    cos, sin = _axial_rope_tables(S, da_dh, da_w // da_dh, tm)
    mats_l = _dft_mats(S)
    mats_c = _dft_mats(n_ctx)
    for l in range(depth):
        last = l == depth - 1
        lam_init = 0.8 - 0.6 * math.exp(-0.3 * l)
        mods = (jnp.dot(sc, w_mod[l], precision=lax.Precision.HIGHEST) + b_mod[l]).reshape(B + 1, 6, 1, D)
        sh1, s1, g1, sh2, s2, g2 = [mods[:, i] for i in range(6)]
        wl = w_in[l]
        w_big = jnp.concatenate([wl[:, :4 * ml_w], wl[:, 4 * ml_w + n_gates:], wl[:, 4 * ml_w:4 * ml_w + n_gates],
                                 jnp.zeros((D, LANES - n_gates), F32)], axis=1).astype(BF16)
        U = _norm_mod_matmul(X, norm1_g[l][None], s1, sh1, w_big, S, tm, w_big.shape[1] // 3)
        gb = jnp.concatenate([mlstm_gate_b[l], jnp.zeros((LANES - n_gates,), F32)])[None]
        m_out = _mlstm_mixer(U, gb, mlstm_norm_g[l][None], B, S, n_ctx, ml_col, gate_col)
        d_out = _diff_attn_mixer(U, cos, sin, da_qnorm_g[l], da_knorm_g[l], da_lambda[l], da_subln_g[l], lam_init,
                                 B, S, n_ctx, da_col, not last)
        hy_args = (hy_w1[l], hy_b1[l], hy_w2[l], hy_b2[l], hy_w3[l], hy_freq[l], hy_w)
        y_out = _hyena_seq(U, mats_l, _hyena_spectrum(mats_l, S, *hy_args), hy_conv_w[l], hy_conv_b[l][None],
                           hy_skip[l], B, S, 0, hy_col)
        n_rows = n_lat
        if not last:
            y_ctx = _hyena_seq(U, mats_c, _hyena_spectrum(mats_c, n_ctx, *hy_args), hy_conv_w[l], hy_conv_b[l][None],
                               hy_skip[l], B, n_ctx, n_lat // n_ctx, hy_col)
            y_out = jnp.concatenate([y_out, y_ctx], axis=0)
            n_rows = n_lat + B * n_ctx
        wo = w_out[l].astype(BF16)
        X = _out_proj_residual(m_out, d_out, y_out, wo[:ml_w], wo[ml_w:ml_w + da_w], wo[ml_w + da_w:], X, g1,
                               n_rows, S, tm)
        X = _moe_ffn_residual(X, n_rows, norm2_g[l][None], s2, sh2, g2, router_w[l], router_b[l],
                              exp_w1[l], exp_w3[l], exp_w2[l], sh_w1[l], sh_w3[l], sh_w2[l], S)
    return X[:n_lat].reshape(B, S, D)
```

```python
import functools
import math

import jax
import jax.numpy as jnp
from jax import lax
from jax.experimental import pallas as pl
from jax.experimental.pallas import tpu as pltpu

F32 = jnp.float32
BF16 = jnp.bfloat16

EPS = 1e-6
GRID_W = 64
ROPE_THETA = 10000.0
ML_HEADS = 4
ML_CHUNK = 256
ML_M_INIT = -1e30
DA_HEADS = 4
HY_ORDER = 2
HY_BANDS = 8
HY_SHIFT = 0.05
HY_TARGET = 1e-2
HY_FAST = 0.3
HY_SLOW = 1.5
N_GROUPS = 8
TOPK_GROUPS = 4
TOP_K = 8
ROUTED_SCALE = 2.5
MOE_BLOCK = 128
LANES = 128
VMEM_LIMIT = 56 * 1024 * 1024


def _cparams(sem):
    return pltpu.CompilerParams(dimension_semantics=sem, vmem_limit_bytes=VMEM_LIMIT)


def _dot(a, b):
    return jnp.dot(a, b, preferred_element_type=F32)


def _dot_nt(a, b):
    return lax.dot_general(a, b, (((1,), (1,)), ((), ())), preferred_element_type=F32)


def _dot_tn(a, b):
    return lax.dot_general(a, b, (((0,), (0,)), ((), ())), preferred_element_type=F32)


def _split3(a):
    hi = a.astype(BF16)
    r = a - hi.astype(F32)
    mid = r.astype(BF16)
    lo = (r - mid.astype(F32)).astype(BF16)
    return hi, mid, lo


def _split2(a):
    hi = a.astype(BF16)
    lo = (a - hi.astype(F32)).astype(BF16)
    return hi, lo


def _norm_mod_mm_kernel(x_ref, g_ref, sc_ref, sh_ref, w_ref, o_ref, xn_ref):
    @pl.when(pl.program_id(1) == 0)
    def _():
        x = x_ref[...]
        y = x * lax.rsqrt(jnp.mean(x * x, axis=-1, keepdims=True) + EPS) * g_ref[...]
        xn_ref[...] = (y * (1.0 + sc_ref[0]) + sh_ref[0]).astype(BF16)

    o_ref[...] = _dot(xn_ref[...], w_ref[...])


def _norm_mod_matmul(x, g, scale, shift, w, seg_rows, tm, tn):
    R, D = x.shape
    N = w.shape[1]
    last = scale.shape[0] - 1
    mod_map = lambda i, j: (jnp.minimum(i * tm // seg_rows, last), 0, 0)
    return pl.pallas_call(
        _norm_mod_mm_kernel,
        grid=(R // tm, N // tn),
        in_specs=[
            pl.BlockSpec((tm, D), lambda i, j: (i, 0)),
            pl.BlockSpec((1, D), lambda i, j: (0, 0)),
            pl.BlockSpec((1, 1, D), mod_map),
            pl.BlockSpec((1, 1, D), mod_map),
            pl.BlockSpec((D, tn), lambda i, j: (0, j)),
        ],
        out_specs=pl.BlockSpec((tm, tn), lambda i, j: (i, j)),
        out_shape=jax.ShapeDtypeStruct((R, N), F32),
        scratch_shapes=[pltpu.VMEM((tm, D), BF16)],
        compiler_params=_cparams(("parallel", "arbitrary")),
        name="norm_mod_matmul",
    )(x, g, scale, shift, w)


def _out_proj_kernel(m_ref, d_ref, y_ref, wm_ref, wd_ref, wy_ref, x_ref, gate_ref, o_ref):
    acc = _dot(m_ref[...], wm_ref[...]) + _dot(d_ref[...], wd_ref[...]) + _dot(y_ref[...], wy_ref[...])
    o_ref[...] = x_ref[...] + gate_ref[0] * acc


def _out_proj_residual(m, d, y, wm, wd, wy, x, gate, n_rows, seg_rows, tm):
    R, D = n_rows, x.shape[1]
    last = gate.shape[0] - 1
    row = lambda i: (i, 0)
    full = lambda i: (0, 0)
    return pl.pallas_call(
        _out_proj_kernel,
        grid=(R // tm,),
        in_specs=[
            pl.BlockSpec((tm, m.shape[1]), row),
            pl.BlockSpec((tm, d.shape[1]), row),
            pl.BlockSpec((tm, y.shape[1]), row),
            pl.BlockSpec(wm.shape, full),
            pl.BlockSpec(wd.shape, full),
            pl.BlockSpec(wy.shape, full),
            pl.BlockSpec((tm, D), row),
            pl.BlockSpec((1, 1, D), lambda i: (jnp.minimum(i * tm // seg_rows, last), 0, 0)),
        ],
        out_specs=pl.BlockSpec((tm, D), row),
        out_shape=jax.ShapeDtypeStruct((R, D), F32),
        compiler_params=_cparams(("parallel",)),
        name="out_proj_residual",
    )(m, d, y, wm, wd, wy, x, gate)


def _log_sigmoid(x):
    return jnp.minimum(x, 0.0) - jnp.log1p(jnp.exp(-jnp.abs(x)))


def _mlstm_gate_tables(g_ref, r0, L, gb, tril, triu):
    g = g_ref[pl.ds(r0, L), :] + gb
    lf = _log_sigmoid(g)
    gT = g.T
    lfT = lf.T
    parts = _split3(lf)
    partsT = _split3(lfT)
    cs_f = sum(_dot(tril, p) for p in parts)
    cs_b = sum(_dot(triu, p) for p in parts)
    rs_f = sum(_dot(p, triu) for p in partsT)
    rs_b = sum(_dot(p, tril) for p in partsT)
    return g, gT, cs_f, cs_b, rs_f, rs_b


def _mlstm_chunk(q, k, v, i_col, i_row, b_col, b_row, b_end, mask, state):
    C, n, m = state
    qb = q.astype(BF16)
    kb = k.astype(BF16)
    dmat = jnp.where(mask, b_col - b_row + i_row, -jnp.inf)
    inter = b_col + m
    m_t = jnp.maximum(inter, jnp.max(dmat, axis=-1, keepdims=True))
    s = _dot_nt(qb, kb) * jnp.exp(dmat - m_t)
    carry_w = jnp.exp(inter - m_t)
    num = _dot(s.astype(BF16), v.astype(BF16)) + carry_w * _dot_nt(qb, C.astype(BF16))
    den = jnp.sum(s, axis=-1, keepdims=True) + carry_w * jnp.sum(q * n, axis=-1, keepdims=True)
    h = num / jnp.maximum(jnp.abs(den), jnp.exp(-m_t))
    g = b_end - b_col + i_col
    m_new = jnp.maximum(b_end + m, jnp.max(g, axis=0, keepdims=True))
    ws = jnp.exp(g - m_new)
    decay = jnp.exp(b_end + m - m_new)
    C_new = decay * C + _dot_tn((v * ws).astype(BF16), kb)
    n_new = decay * n + jnp.sum(ws * k, axis=0, keepdims=True)
    return h, (C_new, n_new, m_new)


def _mlstm_kernel(ql_ref, kl_ref, vl_ref, ol_ref, gl_ref, qc_ref, kc_ref, vc_ref, oc_ref, gc_ref,
                  gb_ref, ng_ref, outl_ref, outc_ref, hf_ref, hb_ref, *, L, H, dh):
    S = ql_ref.shape[0]
    n_ctx = qc_ref.shape[0]
    row = lax.broadcasted_iota(jnp.int32, (L, L), 0)
    col = lax.broadcasted_iota(jnp.int32, (L, L), 1)
    lower = col <= row
    upper = col >= row
    tril = lower.astype(BF16)
    triu = upper.astype(BF16)
    gb = gb_ref[...]
    k_scale = dh ** -0.5

    def both_dirs(refs_f, r0_f, refs_b, r0_b, hoff_f, hoff_b, state):
        new_state = []
        for d, (refs, r0, hoff, h_ref) in enumerate(((refs_f, r0_f, hoff_f, hf_ref),
                                                      (refs_b, r0_b, hoff_b, hb_ref))):
            q_ref, k_ref, v_ref, g_ref = refs
            g, gT, cs_f, cs_b, rs_f, rs_b = _mlstm_gate_tables(g_ref, r0, L, gb, tril, triu)
            cs, rs, mask = (cs_f, rs_f, lower) if d == 0 else (cs_b, rs_b, upper)
            end_row = L - 1 if d == 0 else 0
            for hh in range(H):
                ic = 2 * d * H + hh
                fc = ic + H
                lanes = slice(hh * dh, (hh + 1) * dh)
                q = q_ref[pl.ds(r0, L), lanes]
                k = k_ref[pl.ds(r0, L), lanes] * k_scale
                v = v_ref[pl.ds(r0, L), lanes]
                h, st = _mlstm_chunk(q, k, v, g[:, ic:ic + 1], gT[ic:ic + 1, :], cs[:, fc:fc + 1],
                                     rs[fc:fc + 1, :], cs[end_row:end_row + 1, fc:fc + 1], mask,
                                     state[d * H + hh])
                h_ref[pl.ds(hoff + r0, L), lanes] = h
                new_state.append(st)
        return tuple(new_state)

    state = tuple((jnp.zeros((dh, dh), F32), jnp.zeros((1, dh), F32), jnp.full((1, 1), ML_M_INIT, F32))
                  for _ in range(2 * H))
    ctx_refs = (qc_ref, kc_ref, vc_ref, gc_ref)
    lat_refs = (ql_ref, kl_ref, vl_ref, gl_ref)
    n_cc = n_ctx // L
    for c in range(n_cc):
        state = both_dirs(ctx_refs, c * L, ctx_refs, (n_cc - 1 - c) * L, 0, 0, state)
    n_lc = S // L

    def body(c, st):
        r_f = pl.multiple_of(c * L, L)
        r_b = pl.multiple_of((n_lc - 1 - c) * L, L)
        return both_dirs(lat_refs, r_f, lat_refs, r_b, n_ctx, n_ctx, st)

    lax.fori_loop(0, n_lc, body, state)

    def finish(o_ref, out_ref, hoff, rows):
        def fbody(c, carry):
            r0 = pl.multiple_of(c * L, L)
            hs = hf_ref[pl.ds(hoff + r0, L), :] + hb_ref[pl.ds(hoff + r0, L), :]
            for hh in range(H):
                lanes = slice(hh * dh, (hh + 1) * dh)
                hv = hs[:, lanes]
                hn = hv * lax.rsqrt(jnp.mean(hv * hv, axis=-1, keepdims=True) + EPS) * ng_ref[:, lanes]
                out_ref[pl.ds(r0, L), lanes] = (jax.nn.sigmoid(o_ref[pl.ds(r0, L), lanes]) * hn).astype(out_ref.dtype)
            return carry
        lax.fori_loop(0, rows // L, fbody, 0)

    finish(ol_ref, outl_ref, n_ctx, S)
    finish(oc_ref, outc_ref, 0, n_ctx)


def _mlstm_mixer(u, gate_b, norm_g, B, S, n_ctx, col0, gate_col0):
    W = norm_g.shape[1]
    H = ML_HEADS
    dh = W // H
    cb = col0 // W
    gcb = gate_col0 // LANES
    cblk = (B * S) // n_ctx

    def lat(j):
        return pl.BlockSpec((S, W), lambda b: (b, cb + j))

    def ctx(j):
        return pl.BlockSpec((n_ctx, W), lambda b: (cblk + b, cb + j))

    one = lambda b: (0, 0)
    out_l, out_c = pl.pallas_call(
        functools.partial(_mlstm_kernel, L=ML_CHUNK, H=H, dh=dh),
        grid=(B,),
        in_specs=[lat(0), lat(1), lat(2), lat(3), pl.BlockSpec((S, LANES), lambda b: (b, gcb)),
                  ctx(0), ctx(1), ctx(2), ctx(3), pl.BlockSpec((n_ctx, LANES), lambda b: (cblk + b, gcb)),
                  pl.BlockSpec((1, LANES), one), pl.BlockSpec((1, W), one)],
        out_specs=[pl.BlockSpec((S, W), lambda b: (b, 0)), pl.BlockSpec((n_ctx, W), lambda b: (b, 0))],
        out_shape=[jax.ShapeDtypeStruct((B * S, W), BF16), jax.ShapeDtypeStruct((B * n_ctx, W), BF16)],
        scratch_shapes=[pltpu.VMEM((n_ctx + S, W), F32), pltpu.VMEM((n_ctx + S, W), F32)],
        compiler_params=_cparams(("parallel",)),
        name="mlstm",
    )(u, u, u, u, u, u, u, u, u, u, gate_b, norm_g)
    return jnp.concatenate([out_l, out_c], axis=0)


def _da_prep_kernel(q_ref, k_ref, v_ref, cos_ref, sin_ref, qg_ref, kg_ref, seg_ref, qo_ref, ko_ref, vo_ref, *, dh):
    cos = cos_ref[...]
    sin = sin_ref[...]
    seg = seg_ref[...]
    W = q_ref.shape[1]
    lane = lax.broadcasted_iota(jnp.int32, (1, W), 1)
    quarter = dh // 4
    first = (lane % (2 * quarter)) < quarter

    def norm_rope(x, g):
        hi, lo = _split2(x * x)
        ms = (_dot(hi, seg) + _dot(lo, seg)) * (1.0 / dh)
        xn = x * lax.rsqrt(ms + EPS) * g
        rot = jnp.where(first, -pltpu.roll(xn, W - quarter, 1), pltpu.roll(xn, quarter, 1))
        return xn * cos + rot * sin

    qo_ref[...] = (norm_rope(q_ref[...], qg_ref[...]) * (dh ** -0.5)).astype(BF16)
    ko_ref[...] = norm_rope(k_ref[...], kg_ref[...]).astype(BF16)
    vo_ref[...] = v_ref[...].astype(BF16)


def _da_prep(u, cos, sin, qg, kg, seg, n_lat_rows, S, col0, tm, dh):
    R = u.shape[0]
    W = qg.shape[1]
    cb = col0 // W
    n_lat = n_lat_rows // tm
    per_seq = S // tm
    tab = lambda i: (jnp.where(i < n_lat, i % per_seq, per_seq), 0)
    one = lambda i: (0, 0)
    row = lambda i: (i, 0)

    def ucol(j):
        return pl.BlockSpec((tm, W), lambda i: (i, cb + j))

    return pl.pallas_call(
        functools.partial(_da_prep_kernel, dh=dh),
        grid=(R // tm,),
        in_specs=[ucol(0), ucol(1), ucol(2), pl.BlockSpec((tm, W), tab), pl.BlockSpec((tm, W), tab),
                  pl.BlockSpec((1, W), one), pl.BlockSpec((1, W), one), pl.BlockSpec((W, W), one)],
        out_specs=[pl.BlockSpec((tm, W), row)] * 3,
        out_shape=[jax.ShapeDtypeStruct((R, W), BF16)] * 3,
        compiler_params=_cparams(("parallel",)),
        name="da_prep",
    )(u, u, u, cos, sin, qg, kg, seg)


def _da_attn_kernel(*refs, n_kv, dh, lam_init):
    q_ref = refs[0]
    k_refs = refs[1:1 + n_kv]
    v_refs = refs[1 + n_kv:1 + 2 * n_kv]
    lam_ref, sg_ref, o_ref = refs[1 + 2 * n_kv:]
    lp = lam_ref[...]
    lam = (jnp.exp(jnp.sum(lp[0:1] * lp[1:2], axis=-1, keepdims=True))
           - jnp.exp(jnp.sum(lp[2:3] * lp[3:4], axis=-1, keepdims=True)) + lam_init)
    q = q_ref[...]
    acc = None
    probs = []
    for mp in range(2):
        lanes = slice(mp * dh, (mp + 1) * dh)
        s = [_dot_nt(q[:, lanes], k_ref[:, lanes]) for k_ref in k_refs]
        mx = functools.reduce(jnp.maximum, [jnp.max(si, axis=-1, keepdims=True) for si in s])
        p = [jnp.exp(si - mx) for si in s]
        den = sum(jnp.sum(pi, axis=-1, keepdims=True) for pi in p)
        probs.append([pi / den for pi in p])
    for j in range(n_kv):
        a = (probs[0][j] - lam * probs[1][j]).astype(BF16)
        t = _dot(a, v_refs[j][...])
        acc = t if acc is None else acc + t
    o = acc * lax.rsqrt(jnp.mean(acc * acc, axis=-1, keepdims=True) + EPS) * sg_ref[...]
    o_ref[...] = (o * (1.0 - lam_init)).astype(o_ref.dtype)


def _da_attention(q, k, v, lam_p, subln_g, lam_init, B, q_rows, q_blk0, kv_segs, tq, dh):
    H = DA_HEADS
    vd = 2 * dh
    nq = q_rows // tq
    q0 = q_blk0

    def kv_spec(rows, blk0):
        return pl.BlockSpec((rows, vd), lambda b, h, i: (blk0 + b, h))

    kspecs = [kv_spec(r, b0) for r, b0 in kv_segs]
    one = lambda b, h, i: (0, 0)
    return pl.pallas_call(
        functools.partial(_da_attn_kernel, n_kv=len(kv_segs), dh=dh, lam_init=lam_init),
        grid=(B, H, nq),
        in_specs=[pl.BlockSpec((tq, vd), lambda b, h, i: (q0 + b * nq + i, h))] + kspecs + kspecs
                 + [pl.BlockSpec(lam_p.shape, one), pl.BlockSpec((1, vd), one)],
        out_specs=pl.BlockSpec((tq, vd), lambda b, h, i: (b * nq + i, h)),
        out_shape=jax.ShapeDtypeStruct((B * q_rows, H * vd), BF16),
        compiler_params=_cparams(("parallel", "parallel", "arbitrary")),
        name="da_attention",
    )(q, *([k] * len(kv_segs)), *([v] * len(kv_segs)), lam_p, subln_g)


def _axial_rope_tables(S, dh, reps, pad_rows):
    rows = S // GRID_W
    r = jnp.repeat(jnp.arange(rows, dtype=F32), GRID_W)
    col = jnp.tile(jnp.arange(GRID_W, dtype=F32), rows)
    n_freq = dh // 4
    inv = ROPE_THETA ** (-jnp.arange(n_freq, dtype=F32) / n_freq)
    ar = r[:, None] * inv
    ac = col[:, None] * inv
    ang = jnp.concatenate([ar, ar, ac, ac], axis=-1)
    cos = jnp.concatenate([jnp.tile(jnp.cos(ang), (1, reps)), jnp.ones((pad_rows, dh * reps), F32)], axis=0)
    sin = jnp.concatenate([jnp.tile(jnp.sin(ang), (1, reps)), jnp.zeros((pad_rows, dh * reps), F32)], axis=0)
    return cos, sin


def _diff_attn_mixer(u, cos, sin, qg, kg, lam_p, subln_g, lam_init, B, S, n_ctx, col0, need_ctx):
    dh = qg.shape[0]
    W = DA_HEADS * 2 * dh
    seg = (jnp.arange(W)[:, None] // dh == jnp.arange(W)[None, :] // dh).astype(BF16)
    tm = 512
    q, k, v = _da_prep(u, cos, sin, jnp.tile(qg, W // dh)[None], jnp.tile(kg, W // dh)[None], seg,
                       B * S, S, col0, tm, dh)
    sg = subln_g[None]
    ctx_blk0 = (B * S) // n_ctx
    tq = 256
    out_l = _da_attention(q, k, v, lam_p, sg, lam_init, B, S, 0, [(n_ctx, ctx_blk0), (S, 0)], tq, dh)
    if not need_ctx:
        return out_l
    out_c = _da_attention(q, k, v, lam_p, sg, lam_init, B, n_ctx, (B * S) // n_ctx, [(n_ctx, ctx_blk0)], n_ctx, dh)
    return jnp.concatenate([out_l, out_c], axis=0)


def _hy_conv_kernel(v_ref, x1_ref, x2_ref, w_ref, b_ref, vo_ref, x1o_ref, x2o_ref):
    L, W = v_ref.shape
    row = lax.broadcasted_iota(jnp.int32, (L, 1), 0)
    for j, (i_ref, o_ref) in enumerate(((v_ref, vo_ref), (x1_ref, x1o_ref), (x2_ref, x2o_ref))):
        lanes = slice(j * W, (j + 1) * W)
        u = i_ref[...]
        prev = jnp.where(row == 0, 0.0, pltpu.roll(u, 1, 0))
        nxt = jnp.where(row == L - 1, 0.0, pltpu.roll(u, L - 1, 0))
        o_ref[...] = prev * w_ref[0:1, lanes] + u * w_ref[1:2, lanes] + nxt * w_ref[2:3, lanes] + b_ref[:, lanes]


def _hy_short_conv(u, conv_w, conv_b, n_seg, L, blk0, col0):
    W = conv_w.shape[1] // 3
    cb = col0 // W
    one = lambda b: (0, 0)

    def ucol(j):
        return pl.BlockSpec((L, W), lambda b: (blk0 + b, cb + j))

    return pl.pallas_call(
        _hy_conv_kernel,
        grid=(n_seg,),
        in_specs=[ucol(0), ucol(1), ucol(2), pl.BlockSpec(conv_w.shape, one), pl.BlockSpec(conv_b.shape, one)],
        out_specs=[pl.BlockSpec((L, W), lambda b: (b, 0))] * 3,
        out_shape=[jax.ShapeDtypeStruct((n_seg * L, W), F32)] * 3,
        compiler_params=_cparams(("parallel",)),
        name="hy_short_conv",
    )(u, u, u, conv_w, conv_b)


def _hy_fwd_kernel(c_ref, s_ref, z_ref, *rest, raw):
    z = z_ref[...].astype(BF16)
    zr = _dot(c_ref[...], z)
    zi = _dot(s_ref[...], z)
    if raw:
        yr_ref, yi_ref = rest
        yr_ref[...] = zr
        yi_ref[...] = zi
    else:
        a_ref, b_ref, d_ref, yr_ref, yi_ref = rest
        yr_ref[...] = (zr * a_ref[...] - zi * b_ref[...]).astype(yr_ref.dtype)
        yi_ref[...] = (zr * b_ref[...] + zi * d_ref[...]).astype(yi_ref.dtype)


def _hy_fwd(cm, sm, z, coefs, n_seg, L, tk):
    W = z.shape[1]
    nk = L // tk
    raw = coefs is None
    mat = pl.BlockSpec((tk, L), lambda i, b: (i, 0))
    cf = pl.BlockSpec((tk, W), lambda i, b: (i, 0))
    out = pl.BlockSpec((tk, W), lambda i, b: (b * nk + i, 0))
    odt = F32 if raw else BF16
    return pl.pallas_call(
        functools.partial(_hy_fwd_kernel, raw=raw),
        grid=(nk, n_seg),
        in_specs=[mat, mat, pl.BlockSpec((L, W), lambda i, b: (b, 0))] + ([] if raw else [cf, cf, cf]),
        out_specs=[out, out],
        out_shape=[jax.ShapeDtypeStruct((n_seg * L, W), odt)] * 2,
        compiler_params=_cparams(("parallel", "arbitrary")),
        name="hy_dft_fwd",
    )(cm, sm, z, *(() if raw else coefs))


def _hy_inv_kernel(c_ref, st_ref, yr_ref, yi_ref, x_ref, vz_ref, skip_ref, o_ref):
    y = _dot(c_ref[...], yr_ref[...]) + _dot(st_ref[...], yi_ref[...])
    o_ref[...] = (x_ref[...] * (y + skip_ref[...] * vz_ref[...])).astype(o_ref.dtype)


def _hy_inv(cm, smt, yr, yi, xg, vz, skip, n_seg, L, tt, out_dtype):
    W = yr.shape[1]
    nt = L // tt
    mat = pl.BlockSpec((tt, L), lambda i, b: (i, 0))
    seq = pl.BlockSpec((L, W), lambda i, b: (b, 0))
    row = pl.BlockSpec((tt, W), lambda i, b: (b * nt + i, 0))
    return pl.pallas_call(
        _hy_inv_kernel,
        grid=(nt, n_seg),
        in_specs=[mat, mat, seq, seq, row, row, pl.BlockSpec((1, W), lambda i, b: (0, 0))],
        out_specs=row,
        out_shape=jax.ShapeDtypeStruct((n_seg * L, W), out_dtype),
        compiler_params=_cparams(("parallel", "arbitrary")),
        name="hy_dft_inv",
    )(cm, smt, yr, yi, xg, vz, skip)


def _dft_mats(L):
    k = jnp.arange(L, dtype=jnp.int32)
    kn = (k[:, None] * k[None, :]) % (2 * L)
    ang = kn.astype(F32) * (math.pi / L)
    cm = jnp.cos(ang)
    sm = -jnp.sin(ang)
    sm = sm.at[0].set(jnp.where(k % 2 == 0, 1.0, -1.0))
    return cm.astype(BF16), sm.astype(BF16), sm.T.astype(BF16)


def _hyena_filters(L, w1, b1, w2, b2, w3, freq, W):
    t01 = jnp.linspace(0.0, 1.0, L, dtype=F32)[:, None]
    wpos = (2.0 * math.pi / L) * jnp.arange(L, dtype=F32)[:, None]
    bands = jnp.linspace(1e-4, HY_BANDS - 1, HY_BANDS, dtype=F32)
    feats = jnp.concatenate([t01, jnp.cos(wpos * bands), -jnp.sin(wpos * bands)], axis=-1)
    hp = lax.Precision.HIGHEST
    h = jnp.sin(freq[0] * (jnp.dot(feats, w1, precision=hp) + b1))
    h = jnp.sin(freq[1] * (jnp.dot(h, w2, precision=hp) + b2))
    h = jnp.dot(h, w3, precision=hp).reshape(L, HY_ORDER, 2, W)
    deltas = jnp.abs(jnp.linspace(math.log(HY_TARGET) / HY_SLOW, math.log(HY_TARGET) / HY_FAST, W, dtype=F32))
    h = h * (jnp.exp(-t01 * deltas) + HY_SHIFT)[:, None, None, :]
    hf, hb = h[:, :, 0], h[:, :, 1]
    hf = hf.at[0].add(hb[0])
    hb = hb.at[0].set(0.0)
    scale = lax.rsqrt(jnp.sum(hf * hf, axis=0, keepdims=True) + jnp.sum(hb * hb, axis=0, keepdims=True) + EPS)
    return (hf * scale).reshape(L, HY_ORDER * W), (hb * scale).reshape(L, HY_ORDER * W)


def _hyena_spectrum(mats, L, w1, b1, w2, b2, w3, freq, W):
    cm, sm, _ = mats
    hf, hb = _hyena_filters(L, w1, b1, w2, b2, w3, freq, W)
    cols = jnp.concatenate([hf[:, :W], hf[:, W:], hb[:, :W], hb[:, W:]], axis=0)
    tk = min(L, 512)
    gr, gi = _hy_fwd(cm, sm, cols, None, 2 * HY_ORDER, L, tk)
    gr = gr.reshape(2, HY_ORDER, L, W)
    gi = gi.reshape(2, HY_ORDER, L, W)
    kr = gr[0] + gr[1]
    ki = gi[0] - gi[1]
    nyq = gi[0, :, 0] + gi[1, :, 0]
    n = 2.0 * L
    wk = jnp.full((L, 1), 2.0 / n, F32).at[0].set(1.0 / n)
    a = kr * wk
    bm = (ki * wk).at[:, 0].set(0.0)
    dd = a.at[:, 0].set(nyq / n)
    return [(a[o], bm[o], dd[o]) for o in range(HY_ORDER)]


def _hyena_seq(u, mats, spec, conv_w, conv_b, skip, n_seg, L, blk0, col0):
    cm, sm, smt = mats
    t = min(L, 512)
    v, x1, x2 = _hy_short_conv(u, conv_w, conv_b, n_seg, L, blk0, col0)
    yr, yi = _hy_fwd(cm, sm, v, spec[0], n_seg, L, t)
    z = _hy_inv(cm, smt, yr, yi, x1, v, skip[0:1], n_seg, L, t, F32)
    yr, yi = _hy_fwd(cm, sm, z, spec[1], n_seg, L, t)
    return _hy_inv(cm, smt, yr, yi, x2, z, skip[1:2], n_seg, L, t, BF16)


def _moe_router_kernel(x_ref, g_ref, sc_ref, sh_ref, whi_ref, wlo_ref, rb_ref, h_ref, idx_ref, wts_ref):
    x = x_ref[...]
    y = x * lax.rsqrt(jnp.mean(x * x, axis=-1, keepdims=True) + EPS) * g_ref[...]
    h = y * (1.0 + sc_ref[0]) + sh_ref[0]
    h_ref[...] = h
    hi, lo = _split2(h)
    logits = _dot(hi, whi_ref[...]) + _dot(hi, wlo_ref[...]) + _dot(lo, whi_ref[...])
    scores = jax.nn.sigmoid(logits)
    sel = scores + rb_ref[...]
    tm, E = sel.shape
    gsz = E // N_GROUPS
    neg = -jnp.inf
    lane = lax.broadcasted_iota(jnp.int32, (1, E), 1).astype(F32)
    glane = lax.broadcasted_iota(jnp.int32, (1, gsz), 1).astype(F32)
    gscore = []
    for g in range(N_GROUPS):
        blk = sel[:, g * gsz:(g + 1) * gsz]
        m1 = jnp.max(blk, axis=-1, keepdims=True)
        first = jnp.min(jnp.where(blk == m1, glane, float(gsz)), axis=-1, keepdims=True)
        m2 = jnp.max(jnp.where(glane == first, neg, blk), axis=-1, keepdims=True)
        gscore.append(m1 + m2)
    group_of_lane = lax.broadcasted_iota(jnp.int32, (1, E), 1) // gsz
    keep = jnp.zeros((tm, E), F32)
    for g in range(N_GROUPS):
        rank = jnp.zeros((tm, 1), F32)
        for o in range(N_GROUPS):
            if o != g:
                ahead = (gscore[o] >= gscore[g]) if o < g else (gscore[o] > gscore[g])
                rank = rank + jnp.where(ahead, 1.0, 0.0)
        keep = jnp.where(group_of_lane == g, jnp.where(rank < TOPK_GROUPS, 1.0, 0.0), keep)
    work = jnp.where(keep > 0.0, sel, neg)
    out_lane = lax.broadcasted_iota(jnp.int32, (1, idx_ref.shape[1]), 1)
    idx_out = jnp.zeros(idx_ref.shape, F32)
    w_out = jnp.zeros(wts_ref.shape, F32)
    total = jnp.zeros((tm, 1), F32)
    for j in range(TOP_K):
        mx = jnp.max(work, axis=-1, keepdims=True)
        am = jnp.min(jnp.where(work == mx, lane, float(E)), axis=-1, keepdims=True)
        hit = lane == am
        wj = jnp.sum(jnp.where(hit, scores, 0.0), axis=-1, keepdims=True)
        work = jnp.where(hit, neg, work)
        idx_out = jnp.where(out_lane == j, am, idx_out)
        w_out = jnp.where(out_lane == j, wj, w_out)
        total = total + wj
    idx_ref[...] = idx_out.astype(jnp.int32)
    wts_ref[...] = w_out / total * ROUTED_SCALE


def _moe_router(x, n_tok, g, scale, shift, whi, wlo, rb, seg_rows, tm):
    D = x.shape[1]
    E = whi.shape[1]
    last = scale.shape[0] - 1
    row = lambda i: (i, 0)
    one = lambda i: (0, 0)
    mod = lambda i: (jnp.minimum(i * tm // seg_rows, last), 0, 0)
    return pl.pallas_call(
        _moe_router_kernel,
        grid=(n_tok // tm,),
        in_specs=[pl.BlockSpec((tm, D), row), pl.BlockSpec((1, D), one), pl.BlockSpec((1, 1, D), mod),
                  pl.BlockSpec((1, 1, D), mod), pl.BlockSpec((D, E), one), pl.BlockSpec((D, E), one),
                  pl.BlockSpec((1, E), one)],
        out_specs=[pl.BlockSpec((tm, D), row), pl.BlockSpec((tm, LANES), row), pl.BlockSpec((tm, LANES), row)],
        out_shape=[jax.ShapeDtypeStruct((n_tok, D), F32), jax.ShapeDtypeStruct((n_tok, LANES), jnp.int32),
                   jax.ShapeDtypeStruct((n_tok, LANES), F32)],
        compiler_params=_cparams(("parallel",)),
        name="moe_router",
    )(x, g, scale, shift, whi, wlo, rb)


def _moe_plan(idx, E, bm):
    n_tok, K = idx.shape
    onehot = jnp.sum((idx[:, :, None] == jnp.arange(E, dtype=jnp.int32)[None, None, :]).astype(jnp.int32), axis=1)
    csum = jnp.cumsum(onehot, axis=0)
    counts = csum[-1]
    pcounts = (counts + bm - 1) // bm * bm
    pends = jnp.cumsum(pcounts)
    pstarts = pends - pcounts
    pos = jnp.take_along_axis(csum - onehot + pstarts[None, :], idx, axis=1)
    n_blocks = -(-(n_tok * K + E * (bm - 1)) // bm)
    starts = jnp.arange(n_blocks, dtype=jnp.int32) * bm
    block_e = jnp.minimum(jnp.sum((pends[None, :] <= starts[:, None]).astype(jnp.int32), axis=1), E - 1)
    n_used = (pends[-1] // bm).astype(jnp.int32).reshape(1)
    return pos.astype(jnp.int32).reshape(-1), block_e.astype(jnp.int32), n_used, n_blocks


def _per_token_rows(tm, K, copy):
    def start(t, carry):
        for j in range(K):
            copy(t, j).start(priority=j % 2)
        return carry

    def wait(t, carry):
        for j in range(K):
            copy(t, j).wait()
        return carry

    return (lambda: lax.fori_loop(0, tm, start, 0, unroll=2)), (lambda: lax.fori_loop(0, tm, wait, 0, unroll=2))


def _moe_dispatch_kernel(pos_ref, h_ref, xs_in_ref, xs_ref, sem, *, K):
    del xs_in_ref
    tm = h_ref.shape[0]

    def row_copy(t, j):
        return pltpu.make_async_copy(h_ref.at[pl.ds(t, 1)], xs_ref.at[pl.ds(pos_ref[t * K + j], 1)], sem)

    start, wait = _per_token_rows(tm, K, row_copy)
    start()
    wait()


def _moe_dispatch(pos, h, n_rows, K, tm):
    n_tok, D = h.shape
    xs0 = jnp.zeros((n_rows, D), h.dtype)
    return pl.pallas_call(
        functools.partial(_moe_dispatch_kernel, K=K),
        grid=(n_tok // tm,),
        in_specs=[pl.BlockSpec((tm * K,), lambda i: (i,), memory_space=pltpu.SMEM),
                  pl.BlockSpec((tm, D), lambda i: (i, 0)),
                  pl.BlockSpec(memory_space=pl.ANY)],
        out_specs=pl.BlockSpec(memory_space=pl.ANY),
        out_shape=jax.ShapeDtypeStruct((n_rows, D), h.dtype),
        scratch_shapes=[pltpu.SemaphoreType.DMA(())],
        input_output_aliases={2: 0},
        compiler_params=_cparams(("arbitrary",)),
        name="moe_dispatch",
    )(pos, h, xs0)


def _moe_expert_kernel(be_ref, nu_ref, x_ref, w1_ref, w3_ref, w2_ref, y_ref, w1b_ref, w3b_ref, w2b_ref):
    i = pl.program_id(0)
    used = i < nu_ref[0]
    new_expert = jnp.logical_or(i == 0, be_ref[i] != be_ref[jnp.maximum(i - 1, 0)])

    @pl.when(jnp.logical_and(used, new_expert))
    def _():
        w1b_ref[...] = w1_ref[0, 0].astype(BF16)
        w3b_ref[...] = w3_ref[0, 0].astype(BF16)
        w2b_ref[...] = w2_ref[0, 0].astype(BF16)

    @pl.when(used)
    def _():
        x = x_ref[...].astype(BF16)
        a = _dot(x, w1b_ref[...])
        b = _dot(x, w3b_ref[...])
        y_ref[...] = _dot((jax.nn.silu(a) * b).astype(BF16), w2b_ref[...])

    @pl.when(jnp.logical_not(used))
    def _():
        y_ref[...] = jnp.zeros_like(y_ref)


def _moe_experts(block_e, n_used, xs, w1, w3, w2, layer, bm):
    P, D = xs.shape
    F = w1.shape[3]
    grid_spec = pltpu.PrefetchScalarGridSpec(
        num_scalar_prefetch=2,
        grid=(P // bm,),
        in_specs=[pl.BlockSpec((bm, D), lambda i, be, nu: (i, 0)),
                  pl.BlockSpec((1, 1, D, F), lambda i, be, nu: (layer, be[i], 0, 0)),
                  pl.BlockSpec((1, 1, D, F), lambda i, be, nu: (layer, be[i], 0, 0)),
                  pl.BlockSpec((1, 1, F, D), lambda i, be, nu: (layer, be[i], 0, 0))],
        out_specs=pl.BlockSpec((bm, D), lambda i, be, nu: (i, 0)),
        scratch_shapes=[pltpu.VMEM((D, F), BF16), pltpu.VMEM((D, F), BF16), pltpu.VMEM((F, D), BF16)],
    )
    return pl.pallas_call(
        _moe_expert_kernel,
        grid_spec=grid_spec,
        out_shape=jax.ShapeDtypeStruct((P, D), F32),
        compiler_params=_cparams(("arbitrary",)),
        name="moe_experts",
    )(block_e, n_used, xs, w1, w3, w2)


def _moe_combine_kernel(pos_ref, wts_ref, x_ref, h_ref, sw1_ref, sw3_ref, sw2_ref, gate_ref, ys_ref, o_ref,
                        gath_ref, sem, *, K):
    tm = x_ref.shape[0]

    def row_copy(t, j):
        return pltpu.make_async_copy(ys_ref.at[pl.ds(pos_ref[t * K + j], 1)], gath_ref.at[j, pl.ds(t, 1)], sem)

    start, wait = _per_token_rows(tm, K, row_copy)
    start()
    h = h_ref[...].astype(BF16)
    mid = jax.nn.silu(_dot(h, sw1_ref[...])) * _dot(h, sw3_ref[...])
    acc = _dot(mid.astype(BF16), sw2_ref[...])
    wait()
    wts = wts_ref[...]
    for j in range(K):
        acc = acc + wts[:, j:j + 1] * gath_ref[j]
    o_ref[...] = x_ref[...] + gate_ref[0] * acc


def _moe_combine(pos, wts, x, h, sw1, sw3, sw2, gate, ys, n_tok, K, seg_rows, tm):
    D = x.shape[1]
    last = gate.shape[0] - 1
    row = lambda i: (i, 0)
    one = lambda i: (0, 0)
    return pl.pallas_call(
        functools.partial(_moe_combine_kernel, K=K),
        grid=(n_tok // tm,),
        in_specs=[pl.BlockSpec((tm * K,), lambda i: (i,), memory_space=pltpu.SMEM),
                  pl.BlockSpec((tm, LANES), row), pl.BlockSpec((tm, D), row), pl.BlockSpec((tm, D), row),
                  pl.BlockSpec(sw1.shape, one), pl.BlockSpec(sw3.shape, one), pl.BlockSpec(sw2.shape, one),
                  pl.BlockSpec((1, 1, D), lambda i: (jnp.minimum(i * tm // seg_rows, last), 0, 0)),
                  pl.BlockSpec(memory_space=pl.ANY)],
        out_specs=pl.BlockSpec((tm, D), row),
        out_shape=jax.ShapeDtypeStruct((n_tok, D), F32),
        scratch_shapes=[pltpu.VMEM((K, tm, D), F32), pltpu.SemaphoreType.DMA(())],
        compiler_params=_cparams(("arbitrary",)),
        name="moe_combine",
    )(pos, wts, x, h, sw1, sw3, sw2, gate, ys)


def _moe_ffn_residual(x, n_tok, norm_g, scale, shift, gate, router_w, router_b, ew1, ew3, ew2, layer, sw1, sw3, sw2,
                      seg_rows):
    E = router_w.shape[1]
    K = TOP_K
    whi, wlo = _split2(router_w)
    h, idx, wts = _moe_router(x, n_tok, norm_g, scale, shift, whi, wlo, router_b[None], seg_rows, 256)
    pos, block_e, n_used, n_blocks = _moe_plan(idx[:, :K], E, MOE_BLOCK)
    xs = _moe_dispatch(pos, h, n_blocks * MOE_BLOCK, K, 128)
    ys = _moe_experts(block_e, n_used, xs, ew1, ew3, ew2, layer, MOE_BLOCK)
    return _moe_combine(pos, wts, x, h, sw1.astype(BF16), sw3.astype(BF16), sw2.astype(BF16), gate, ys,
                        n_tok, K, seg_rows, 128)


def kernel(x, c, ctx, c_ctx, w_mod, b_mod, norm1_g, norm2_g, w_in, mlstm_gate_b, mlstm_norm_g, da_qnorm_g, da_knorm_g, da_lambda, da_subln_g, hy_conv_w, hy_conv_b, hy_w1, hy_b1, hy_w2, hy_b2, hy_w3, hy_freq, hy_skip, w_out, router_w, router_b, exp_w1, exp_w3, exp_w2, sh_w1, sh_w3, sh_w2):
    B, S, D = x.shape
    n_ctx = ctx.shape[1]
    depth = w_in.shape[0]
    n_lat = B * S
    ml_w = mlstm_norm_g.shape[1]
    da_dh = da_qnorm_g.shape[1]
    da_w = DA_HEADS * 2 * da_dh
    hy_w = hy_skip.shape[2]
    n_gates = 4 * ML_HEADS
    ml_col, da_col = 0, 4 * ml_w
    hy_col = da_col + 3 * da_w
    gate_col = hy_col + 3 * hy_w
    tm = 512

    X = jnp.concatenate([x.reshape(n_lat, D), ctx.reshape(B * n_ctx, D)], axis=0)
    sc = jax.nn.silu(jnp.concatenate([c, c_ctx[None]], axis=0))
    cos, sin = _axial_rope_tables(S, da_dh, da_w // da_dh, tm)
    mats_l = _dft_mats(S)
    mats_c = _dft_mats(n_ctx)
    for l in range(depth):
        last = l == depth - 1
        lam_init = 0.8 - 0.6 * math.exp(-0.3 * l)
        mods = (jnp.dot(sc, w_mod[l], precision=lax.Precision.HIGHEST) + b_mod[l]).reshape(B + 1, 6, 1, D)
        sh1, s1, g1, sh2, s2, g2 = [mods[:, i] for i in range(6)]
        wl = w_in[l]
        w_big = jnp.concatenate([wl[:, :4 * ml_w], wl[:, 4 * ml_w + n_gates:], wl[:, 4 * ml_w:4 * ml_w + n_gates],
                                 jnp.zeros((D, LANES - n_gates), F32)], axis=1).astype(BF16)
        U = _norm_mod_matmul(X, norm1_g[l][None], s1, sh1, w_big, S, tm, w_big.shape[1] // 3)
        gb = jnp.concatenate([mlstm_gate_b[l], jnp.zeros((LANES - n_gates,), F32)])[None]
        m_out = _mlstm_mixer(U, gb, mlstm_norm_g[l][None], B, S, n_ctx, ml_col, gate_col)
        d_out = _diff_attn_mixer(U, cos, sin, da_qnorm_g[l], da_knorm_g[l], da_lambda[l], da_subln_g[l], lam_init,
                                 B, S, n_ctx, da_col, not last)
        hy_args = (hy_w1[l], hy_b1[l], hy_w2[l], hy_b2[l], hy_w3[l], hy_freq[l], hy_w)
        y_out = _hyena_seq(U, mats_l, _hyena_spectrum(mats_l, S, *hy_args), hy_conv_w[l], hy_conv_b[l][None],
                           hy_skip[l], B, S, 0, hy_col)
        n_rows = n_lat
        if not last:
            y_ctx = _hyena_seq(U, mats_c, _hyena_spectrum(mats_c, n_ctx, *hy_args), hy_conv_w[l], hy_conv_b[l][None],
                               hy_skip[l], B, n_ctx, n_lat // n_ctx, hy_col)
            y_out = jnp.concatenate([y_out, y_ctx], axis=0)
            n_rows = n_lat + B * n_ctx
        wo = w_out[l].astype(BF16)
        X = _out_proj_residual(m_out, d_out, y_out, wo[:ml_w], wo[ml_w:ml_w + da_w], wo[ml_w + da_w:], X, g1,
                               n_rows, S, tm)
        X = _moe_ffn_residual(X, n_rows, norm2_g[l][None], s2, sh2, g2, router_w[l], router_b[l],
                              exp_w1, exp_w3, exp_w2, l, sh_w1[l], sh_w3[l], sh_w2[l], S)
    return X[:n_lat].reshape(B, S, D)
```

```python
import functools
import math

import jax
import jax.numpy as jnp
from jax import lax
from jax.experimental import pallas as pl
from jax.experimental.pallas import tpu as pltpu

F32 = jnp.float32
BF16 = jnp.bfloat16

EPS = 1e-6
GRID_W = 64
ROPE_THETA = 10000.0
ML_HEADS = 4
ML_CHUNK = 256
ML_M_INIT = -1e30
DA_HEADS = 4
HY_ORDER = 2
HY_BANDS = 8
HY_SHIFT = 0.05
HY_TARGET = 1e-2
HY_FAST = 0.3
HY_SLOW = 1.5
N_GROUPS = 8
TOPK_GROUPS = 4
TOP_K = 8
ROUTED_SCALE = 2.5
MOE_BLOCK = 128
LANES = 128
VMEM_LIMIT = 56 * 1024 * 1024


def _cparams(sem):
    return pltpu.CompilerParams(dimension_semantics=sem, vmem_limit_bytes=VMEM_LIMIT)


def _dot(a, b):
    return jnp.dot(a, b, preferred_element_type=F32)


def _dot_nt(a, b):
    return lax.dot_general(a, b, (((1,), (1,)), ((), ())), preferred_element_type=F32)


def _dot_tn(a, b):
    return lax.dot_general(a, b, (((0,), (0,)), ((), ())), preferred_element_type=F32)


def _split3(a):
    hi = a.astype(BF16)
    r = a - hi.astype(F32)
    mid = r.astype(BF16)
    lo = (r - mid.astype(F32)).astype(BF16)
    return hi, mid, lo


def _split2(a):
    hi = a.astype(BF16)
    lo = (a - hi.astype(F32)).astype(BF16)
    return hi, lo


def _norm_mod_mm_kernel(x_ref, g_ref, sc_ref, sh_ref, w_ref, o_ref, xn_ref):
    @pl.when(pl.program_id(1) == 0)
    def _():
        x = x_ref[...]
        y = x * lax.rsqrt(jnp.mean(x * x, axis=-1, keepdims=True) + EPS) * g_ref[...]
        xn_ref[...] = (y * (1.0 + sc_ref[0]) + sh_ref[0]).astype(BF16)

    o_ref[...] = _dot(xn_ref[...], w_ref[...])


def _norm_mod_matmul(x, g, scale, shift, w, seg_rows, tm, tn):
    R, D = x.shape
    N = w.shape[1]
    last = scale.shape[0] - 1
    mod_map = lambda i, j: (jnp.minimum(i * tm // seg_rows, last), 0, 0)
    return pl.pallas_call(
        _norm_mod_mm_kernel,
        grid=(R // tm, N // tn),
        in_specs=[
            pl.BlockSpec((tm, D), lambda i, j: (i, 0)),
            pl.BlockSpec((1, D), lambda i, j: (0, 0)),
            pl.BlockSpec((1, 1, D), mod_map),
            pl.BlockSpec((1, 1, D), mod_map),
            pl.BlockSpec((D, tn), lambda i, j: (0, j)),
        ],
        out_specs=pl.BlockSpec((tm, tn), lambda i, j: (i, j)),
        out_shape=jax.ShapeDtypeStruct((R, N), F32),
        scratch_shapes=[pltpu.VMEM((tm, D), BF16)],
        compiler_params=_cparams(("parallel", "arbitrary")),
        name="norm_mod_matmul",
    )(x, g, scale, shift, w)


def _out_proj_kernel(m_ref, d_ref, y_ref, wm_ref, wd_ref, wy_ref, x_ref, gate_ref, o_ref):
    acc = _dot(m_ref[...], wm_ref[...]) + _dot(d_ref[...], wd_ref[...]) + _dot(y_ref[...], wy_ref[...])
    o_ref[...] = x_ref[...] + gate_ref[0] * acc


def _out_proj_residual(m, d, y, wm, wd, wy, x, gate, n_rows, seg_rows, tm):
    R, D = n_rows, x.shape[1]
    last = gate.shape[0] - 1
    row = lambda i: (i, 0)
    full = lambda i: (0, 0)
    return pl.pallas_call(
        _out_proj_kernel,
        grid=(R // tm,),
        in_specs=[
            pl.BlockSpec((tm, m.shape[1]), row),
            pl.BlockSpec((tm, d.shape[1]), row),
            pl.BlockSpec((tm, y.shape[1]), row),
            pl.BlockSpec(wm.shape, full),
            pl.BlockSpec(wd.shape, full),
            pl.BlockSpec(wy.shape, full),
            pl.BlockSpec((tm, D), row),
            pl.BlockSpec((1, 1, D), lambda i: (jnp.minimum(i * tm // seg_rows, last), 0, 0)),
        ],
        out_specs=pl.BlockSpec((tm, D), row),
        out_shape=jax.ShapeDtypeStruct((R, D), F32),
        compiler_params=_cparams(("parallel",)),
        name="out_proj_residual",
    )(m, d, y, wm, wd, wy, x, gate)


def _log_sigmoid(x):
    return jnp.minimum(x, 0.0) - jnp.log1p(jnp.exp(-jnp.abs(x)))


def _mlstm_gate_tables(g_ref, r0, L, gb, tril, triu):
    g = g_ref[pl.ds(r0, L), :] + gb
    lf = _log_sigmoid(g)
    gT = g.T
    lfT = lf.T
    parts = _split3(lf)
    partsT = _split3(lfT)
    cs_f = sum(_dot(tril, p) for p in parts)
    cs_b = sum(_dot(triu, p) for p in parts)
    rs_f = sum(_dot(p, triu) for p in partsT)
    rs_b = sum(_dot(p, tril) for p in partsT)
    return g, gT, cs_f, cs_b, rs_f, rs_b


def _mlstm_chunk(q, k, v, i_col, i_row, b_col, b_row, b_end, mask, state):
    C, n, m = state
    qb = q.astype(BF16)
    kb = k.astype(BF16)
    dmat = jnp.where(mask, b_col - b_row + i_row, -jnp.inf)
    inter = b_col + m
    m_t = jnp.maximum(inter, jnp.max(dmat, axis=-1, keepdims=True))
    s = _dot_nt(qb, kb) * jnp.exp(dmat - m_t)
    carry_w = jnp.exp(inter - m_t)
    num = _dot(s.astype(BF16), v.astype(BF16)) + carry_w * _dot_nt(qb, C.astype(BF16))
    den = jnp.sum(s, axis=-1, keepdims=True) + carry_w * jnp.sum(q * n, axis=-1, keepdims=True)
    h = num / jnp.maximum(jnp.abs(den), jnp.exp(-m_t))
    g = b_end - b_col + i_col
    m_new = jnp.maximum(b_end + m, jnp.max(g, axis=0, keepdims=True))
    ws = jnp.exp(g - m_new)
    decay = jnp.exp(b_end + m - m_new)
    C_new = decay * C + _dot_tn((v * ws).astype(BF16), kb)
    n_new = decay * n + jnp.sum(ws * k, axis=0, keepdims=True)
    return h, (C_new, n_new, m_new)


def _mlstm_kernel(ql_ref, kl_ref, vl_ref, ol_ref, gl_ref, qc_ref, kc_ref, vc_ref, oc_ref, gc_ref,
                  gb_ref, ng_ref, outl_ref, outc_ref, hf_ref, hb_ref, *, L, H, dh):
    S = ql_ref.shape[0]
    n_ctx = qc_ref.shape[0]
    row = lax.broadcasted_iota(jnp.int32, (L, L), 0)
    col = lax.broadcasted_iota(jnp.int32, (L, L), 1)
    lower = col <= row
    upper = col >= row
    tril = lower.astype(BF16)
    triu = upper.astype(BF16)
    gb = gb_ref[...]
    k_scale = dh ** -0.5

    def both_dirs(refs_f, r0_f, refs_b, r0_b, hoff_f, hoff_b, state):
        new_state = []
        for d, (refs, r0, hoff, h_ref) in enumerate(((refs_f, r0_f, hoff_f, hf_ref),
                                                      (refs_b, r0_b, hoff_b, hb_ref))):
            q_ref, k_ref, v_ref, g_ref = refs
            g, gT, cs_f, cs_b, rs_f, rs_b = _mlstm_gate_tables(g_ref, r0, L, gb, tril, triu)
            cs, rs, mask = (cs_f, rs_f, lower) if d == 0 else (cs_b, rs_b, upper)
            end_row = L - 1 if d == 0 else 0
            for hh in range(H):
                ic = 2 * d * H + hh
                fc = ic + H
                lanes = slice(hh * dh, (hh + 1) * dh)
                q = q_ref[pl.ds(r0, L), lanes]
                k = k_ref[pl.ds(r0, L), lanes] * k_scale
                v = v_ref[pl.ds(r0, L), lanes]
                h, st = _mlstm_chunk(q, k, v, g[:, ic:ic + 1], gT[ic:ic + 1, :], cs[:, fc:fc + 1],
                                     rs[fc:fc + 1, :], cs[end_row:end_row + 1, fc:fc + 1], mask,
                                     state[d * H + hh])
                h_ref[pl.ds(hoff + r0, L), lanes] = h
                new_state.append(st)
        return tuple(new_state)

    state = tuple((jnp.zeros((dh, dh), F32), jnp.zeros((1, dh), F32), jnp.full((1, 1), ML_M_INIT, F32))
                  for _ in range(2 * H))
    ctx_refs = (qc_ref, kc_ref, vc_ref, gc_ref)
    lat_refs = (ql_ref, kl_ref, vl_ref, gl_ref)
    n_cc = n_ctx // L
    for c in range(n_cc):
        state = both_dirs(ctx_refs, c * L, ctx_refs, (n_cc - 1 - c) * L, 0, 0, state)
    n_lc = S // L

    def body(c, st):
        r_f = pl.multiple_of(c * L, L)
        r_b = pl.multiple_of((n_lc - 1 - c) * L, L)
        return both_dirs(lat_refs, r_f, lat_refs, r_b, n_ctx, n_ctx, st)

    lax.fori_loop(0, n_lc, body, state)

    def finish(o_ref, out_ref, hoff, rows):
        def fbody(c, carry):
            r0 = pl.multiple_of(c * L, L)
            hs = hf_ref[pl.ds(hoff + r0, L), :] + hb_ref[pl.ds(hoff + r0, L), :]
            for hh in range(H):
                lanes = slice(hh * dh, (hh + 1) * dh)
                hv = hs[:, lanes]
                hn = hv * lax.rsqrt(jnp.mean(hv * hv, axis=-1, keepdims=True) + EPS) * ng_ref[:, lanes]
                out_ref[pl.ds(r0, L), lanes] = (jax.nn.sigmoid(o_ref[pl.ds(r0, L), lanes]) * hn).astype(out_ref.dtype)
            return carry
        lax.fori_loop(0, rows // L, fbody, 0)

    finish(ol_ref, outl_ref, n_ctx, S)
    finish(oc_ref, outc_ref, 0, n_ctx)


def _mlstm_mixer(u, gate_b, norm_g, B, S, n_ctx, col0, gate_col0):
    W = norm_g.shape[1]
    H = ML_HEADS
    dh = W // H
    cb = col0 // W
    gcb = gate_col0 // LANES
    cblk = (B * S) // n_ctx

    def lat(j):
        return pl.BlockSpec((S, W), lambda b: (b, cb + j))

    def ctx(j):
        return pl.BlockSpec((n_ctx, W), lambda b: (cblk + b, cb + j))

    one = lambda b: (0, 0)
    out_l, out_c = pl.pallas_call(
        functools.partial(_mlstm_kernel, L=ML_CHUNK, H=H, dh=dh),
        grid=(B,),
        in_specs=[lat(0), lat(1), lat(2), lat(3), pl.BlockSpec((S, LANES), lambda b: (b, gcb)),
                  ctx(0), ctx(1), ctx(2), ctx(3), pl.BlockSpec((n_ctx, LANES), lambda b: (cblk + b, gcb)),
                  pl.BlockSpec((1, LANES), one), pl.BlockSpec((1, W), one)],
        out_specs=[pl.BlockSpec((S, W), lambda b: (b, 0)), pl.BlockSpec((n_ctx, W), lambda b: (b, 0))],
        out_shape=[jax.ShapeDtypeStruct((B * S, W), BF16), jax.ShapeDtypeStruct((B * n_ctx, W), BF16)],
        scratch_shapes=[pltpu.VMEM((n_ctx + S, W), F32), pltpu.VMEM((n_ctx + S, W), F32)],
        compiler_params=_cparams(("parallel",)),
        name="mlstm",
    )(u, u, u, u, u, u, u, u, u, u, gate_b, norm_g)
    return jnp.concatenate([out_l, out_c], axis=0)


def _da_prep_kernel(q_ref, k_ref, v_ref, cos_ref, sin_ref, qg_ref, kg_ref, seg_ref, qo_ref, ko_ref, vo_ref, *, dh):
    cos = cos_ref[...]
    sin = sin_ref[...]
    seg = seg_ref[...]
    W = q_ref.shape[1]
    lane = lax.broadcasted_iota(jnp.int32, (1, W), 1)
    quarter = dh // 4
    first = (lane % (2 * quarter)) < quarter

    def norm_rope(x, g):
        hi, lo = _split2(x * x)
        ms = (_dot(hi, seg) + _dot(lo, seg)) * (1.0 / dh)
        xn = x * lax.rsqrt(ms + EPS) * g
        rot = jnp.where(first, -pltpu.roll(xn, W - quarter, 1), pltpu.roll(xn, quarter, 1))
        return xn * cos + rot * sin

    qo_ref[...] = (norm_rope(q_ref[...], qg_ref[...]) * (dh ** -0.5)).astype(BF16)
    ko_ref[...] = norm_rope(k_ref[...], kg_ref[...]).astype(BF16)
    vo_ref[...] = v_ref[...].astype(BF16)


def _da_prep(u, cos, sin, qg, kg, seg, n_lat_rows, S, col0, tm, dh):
    R = u.shape[0]
    W = qg.shape[1]
    cb = col0 // W
    n_lat = n_lat_rows // tm
    per_seq = S // tm
    tab = lambda i: (jnp.where(i < n_lat, i % per_seq, per_seq), 0)
    one = lambda i: (0, 0)
    row = lambda i: (i, 0)

    def ucol(j):
        return pl.BlockSpec((tm, W), lambda i: (i, cb + j))

    return pl.pallas_call(
        functools.partial(_da_prep_kernel, dh=dh),
        grid=(R // tm,),
        in_specs=[ucol(0), ucol(1), ucol(2), pl.BlockSpec((tm, W), tab), pl.BlockSpec((tm, W), tab),
                  pl.BlockSpec((1, W), one), pl.BlockSpec((1, W), one), pl.BlockSpec((W, W), one)],
        out_specs=[pl.BlockSpec((tm, W), row)] * 3,
        out_shape=[jax.ShapeDtypeStruct((R, W), BF16)] * 3,
        compiler_params=_cparams(("parallel",)),
        name="da_prep",
    )(u, u, u, cos, sin, qg, kg, seg)


def _da_attn_kernel(*refs, n_kv, dh, lam_init):
    q_ref = refs[0]
    k_refs = refs[1:1 + n_kv]
    v_refs = refs[1 + n_kv:1 + 2 * n_kv]
    lam_ref, sg_ref, o_ref = refs[1 + 2 * n_kv:]
    lp = lam_ref[...]
    lam = (jnp.exp(jnp.sum(lp[0:1] * lp[1:2], axis=-1, keepdims=True))
           - jnp.exp(jnp.sum(lp[2:3] * lp[3:4], axis=-1, keepdims=True)) + lam_init)
    q = q_ref[...]
    acc = None
    probs = []
    for mp in range(2):
        lanes = slice(mp * dh, (mp + 1) * dh)
        s = [_dot_nt(q[:, lanes], k_ref[:, lanes]) for k_ref in k_refs]
        mx = functools.reduce(jnp.maximum, [jnp.max(si, axis=-1, keepdims=True) for si in s])
        p = [jnp.exp(si - mx) for si in s]
        den = sum(jnp.sum(pi, axis=-1, keepdims=True) for pi in p)
        probs.append([pi / den for pi in p])
    for j in range(n_kv):
        a = (probs[0][j] - lam * probs[1][j]).astype(BF16)
        t = _dot(a, v_refs[j][...])
        acc = t if acc is None else acc + t
    o = acc * lax.rsqrt(jnp.mean(acc * acc, axis=-1, keepdims=True) + EPS) * sg_ref[...]
    o_ref[...] = (o * (1.0 - lam_init)).astype(o_ref.dtype)


def _da_attention(q, k, v, lam_p, subln_g, lam_init, B, q_rows, q_blk0, kv_segs, tq, dh):
    H = DA_HEADS
    vd = 2 * dh
    nq = q_rows // tq
    q0 = q_blk0

    def kv_spec(rows, blk0):
        return pl.BlockSpec((rows, vd), lambda b, h, i: (blk0 + b, h))

    kspecs = [kv_spec(r, b0) for r, b0 in kv_segs]
    one = lambda b, h, i: (0, 0)
    return pl.pallas_call(
        functools.partial(_da_attn_kernel, n_kv=len(kv_segs), dh=dh, lam_init=lam_init),
        grid=(B, H, nq),
        in_specs=[pl.BlockSpec((tq, vd), lambda b, h, i: (q0 + b * nq + i, h))] + kspecs + kspecs
                 + [pl.BlockSpec(lam_p.shape, one), pl.BlockSpec((1, vd), one)],
        out_specs=pl.BlockSpec((tq, vd), lambda b, h, i: (b * nq + i, h)),
        out_shape=jax.ShapeDtypeStruct((B * q_rows, H * vd), BF16),
        compiler_params=_cparams(("parallel", "parallel", "arbitrary")),
        name="da_attention",
    )(q, *([k] * len(kv_segs)), *([v] * len(kv_segs)), lam_p, subln_g)


def _axial_rope_tables(S, dh, reps, pad_rows):
    rows = S // GRID_W
    r = jnp.repeat(jnp.arange(rows, dtype=F32), GRID_W)
    col = jnp.tile(jnp.arange(GRID_W, dtype=F32), rows)
    n_freq = dh // 4
    inv = ROPE_THETA ** (-jnp.arange(n_freq, dtype=F32) / n_freq)
    ar = r[:, None] * inv
    ac = col[:, None] * inv
    ang = jnp.concatenate([ar, ar, ac, ac], axis=-1)
    cos = jnp.concatenate([jnp.tile(jnp.cos(ang), (1, reps)), jnp.ones((pad_rows, dh * reps), F32)], axis=0)
    sin = jnp.concatenate([jnp.tile(jnp.sin(ang), (1, reps)), jnp.zeros((pad_rows, dh * reps), F32)], axis=0)
    return cos, sin


def _diff_attn_mixer(u, cos, sin, qg, kg, lam_p, subln_g, lam_init, B, S, n_ctx, col0, need_ctx):
    dh = qg.shape[0]
    W = DA_HEADS * 2 * dh
    seg = (jnp.arange(W)[:, None] // dh == jnp.arange(W)[None, :] // dh).astype(BF16)
    tm = 512
    q, k, v = _da_prep(u, cos, sin, jnp.tile(qg, W // dh)[None], jnp.tile(kg, W // dh)[None], seg,
                       B * S, S, col0, tm, dh)
    sg = subln_g[None]
    ctx_blk0 = (B * S) // n_ctx
    tq = 256
    out_l = _da_attention(q, k, v, lam_p, sg, lam_init, B, S, 0, [(n_ctx, ctx_blk0), (S, 0)], tq, dh)
    if not need_ctx:
        return out_l
    out_c = _da_attention(q, k, v, lam_p, sg, lam_init, B, n_ctx, (B * S) // n_ctx, [(n_ctx, ctx_blk0)], n_ctx, dh)
    return jnp.concatenate([out_l, out_c], axis=0)


def _hy_conv_kernel(v_ref, x1_ref, x2_ref, w_ref, b_ref, vo_ref, x1o_ref, x2o_ref):
    L, W = v_ref.shape
    row = lax.broadcasted_iota(jnp.int32, (L, 1), 0)
    for j, (i_ref, o_ref) in enumerate(((v_ref, vo_ref), (x1_ref, x1o_ref), (x2_ref, x2o_ref))):
        lanes = slice(j * W, (j + 1) * W)
        u = i_ref[...]
        prev = jnp.where(row == 0, 0.0, pltpu.roll(u, 1, 0))
        nxt = jnp.where(row == L - 1, 0.0, pltpu.roll(u, L - 1, 0))
        o_ref[...] = prev * w_ref[0:1, lanes] + u * w_ref[1:2, lanes] + nxt * w_ref[2:3, lanes] + b_ref[:, lanes]


def _hy_short_conv(u, conv_w, conv_b, n_seg, L, blk0, col0):
    W = conv_w.shape[1] // 3
    cb = col0 // W
    one = lambda b: (0, 0)

    def ucol(j):
        return pl.BlockSpec((L, W), lambda b: (blk0 + b, cb + j))

    return pl.pallas_call(
        _hy_conv_kernel,
        grid=(n_seg,),
        in_specs=[ucol(0), ucol(1), ucol(2), pl.BlockSpec(conv_w.shape, one), pl.BlockSpec(conv_b.shape, one)],
        out_specs=[pl.BlockSpec((L, W), lambda b: (b, 0))] * 3,
        out_shape=[jax.ShapeDtypeStruct((n_seg * L, W), F32)] * 3,
        compiler_params=_cparams(("parallel",)),
        name="hy_short_conv",
    )(u, u, u, conv_w, conv_b)


def _hy_fwd_kernel(c_ref, s_ref, z_ref, *rest, raw):
    z = z_ref[...].astype(BF16)
    zr = _dot(c_ref[...], z)
    zi = _dot(s_ref[...], z)
    if raw:
        yr_ref, yi_ref = rest
        yr_ref[...] = zr
        yi_ref[...] = zi
    else:
        a_ref, b_ref, d_ref, yr_ref, yi_ref = rest
        yr_ref[...] = (zr * a_ref[...] - zi * b_ref[...]).astype(yr_ref.dtype)
        yi_ref[...] = (zr * b_ref[...] + zi * d_ref[...]).astype(yi_ref.dtype)


def _hy_fwd(cm, sm, z, coefs, n_seg, L, tk):
    W = z.shape[1]
    nk = L // tk
    raw = coefs is None
    mat = pl.BlockSpec((tk, L), lambda i, b: (i, 0))
    cf = pl.BlockSpec((tk, W), lambda i, b: (i, 0))
    out = pl.BlockSpec((tk, W), lambda i, b: (b * nk + i, 0))
    odt = F32 if raw else BF16
    return pl.pallas_call(
        functools.partial(_hy_fwd_kernel, raw=raw),
        grid=(nk, n_seg),
        in_specs=[mat, mat, pl.BlockSpec((L, W), lambda i, b: (b, 0))] + ([] if raw else [cf, cf, cf]),
        out_specs=[out, out],
        out_shape=[jax.ShapeDtypeStruct((n_seg * L, W), odt)] * 2,
        compiler_params=_cparams(("parallel", "arbitrary")),
        name="hy_dft_fwd",
    )(cm, sm, z, *(() if raw else coefs))


def _hy_inv_kernel(c_ref, st_ref, yr_ref, yi_ref, x_ref, vz_ref, skip_ref, o_ref):
    y = _dot(c_ref[...], yr_ref[...]) + _dot(st_ref[...], yi_ref[...])
    o_ref[...] = (x_ref[...] * (y + skip_ref[...] * vz_ref[...])).astype(o_ref.dtype)


def _hy_inv(cm, smt, yr, yi, xg, vz, skip, n_seg, L, tt, out_dtype):
    W = yr.shape[1]
    nt = L // tt
    mat = pl.BlockSpec((tt, L), lambda i, b: (i, 0))
    seq = pl.BlockSpec((L, W), lambda i, b: (b, 0))
    row = pl.BlockSpec((tt, W), lambda i, b: (b * nt + i, 0))
    return pl.pallas_call(
        _hy_inv_kernel,
        grid=(nt, n_seg),
        in_specs=[mat, mat, seq, seq, row, row, pl.BlockSpec((1, W), lambda i, b: (0, 0))],
        out_specs=row,
        out_shape=jax.ShapeDtypeStruct((n_seg * L, W), out_dtype),
        compiler_params=_cparams(("parallel", "arbitrary")),
        name="hy_dft_inv",
    )(cm, smt, yr, yi, xg, vz, skip)


def _dft_mats(L):
    k = jnp.arange(L, dtype=jnp.int32)
    kn = (k[:, None] * k[None, :]) % (2 * L)
    ang = kn.astype(F32) * (math.pi / L)
    cm = jnp.cos(ang)
    sm = -jnp.sin(ang)
    sm = sm.at[0].set(jnp.where(k % 2 == 0, 1.0, -1.0))
    return cm.astype(BF16), sm.astype(BF16), sm.T.astype(BF16)


def _hyena_filters(L, w1, b1, w2, b2, w3, freq, W):
    t01 = jnp.linspace(0.0, 1.0, L, dtype=F32)[:, None]
    wpos = (2.0 * math.pi / L) * jnp.arange(L, dtype=F32)[:, None]
    bands = jnp.linspace(1e-4, HY_BANDS - 1, HY_BANDS, dtype=F32)
    feats = jnp.concatenate([t01, jnp.cos(wpos * bands), -jnp.sin(wpos * bands)], axis=-1)
    hp = lax.Precision.HIGHEST
    h = jnp.sin(freq[0] * (jnp.dot(feats, w1, precision=hp) + b1))
    h = jnp.sin(freq[1] * (jnp.dot(h, w2, precision=hp) + b2))
    h = jnp.dot(h, w3, precision=hp).reshape(L, HY_ORDER, 2, W)
    deltas = jnp.abs(jnp.linspace(math.log(HY_TARGET) / HY_SLOW, math.log(HY_TARGET) / HY_FAST, W, dtype=F32))
    h = h * (jnp.exp(-t01 * deltas) + HY_SHIFT)[:, None, None, :]
    hf, hb = h[:, :, 0], h[:, :, 1]
    hf = hf.at[0].add(hb[0])
    hb = hb.at[0].set(0.0)
    scale = lax.rsqrt(jnp.sum(hf * hf, axis=0, keepdims=True) + jnp.sum(hb * hb, axis=0, keepdims=True) + EPS)
    return (hf * scale).reshape(L, HY_ORDER * W), (hb * scale).reshape(L, HY_ORDER * W)


def _hyena_spectrum(mats, L, w1, b1, w2, b2, w3, freq, W):
    cm, sm, _ = mats
    hf, hb = _hyena_filters(L, w1, b1, w2, b2, w3, freq, W)
    cols = jnp.concatenate([hf[:, :W], hf[:, W:], hb[:, :W], hb[:, W:]], axis=0)
    tk = min(L, 512)
    gr, gi = _hy_fwd(cm, sm, cols, None, 2 * HY_ORDER, L, tk)
    gr = gr.reshape(2, HY_ORDER, L, W)
    gi = gi.reshape(2, HY_ORDER, L, W)
    kr = gr[0] + gr[1]
    ki = gi[0] - gi[1]
    nyq = gi[0, :, 0] + gi[1, :, 0]
    n = 2.0 * L
    wk = jnp.full((L, 1), 2.0 / n, F32).at[0].set(1.0 / n)
    a = kr * wk
    bm = (ki * wk).at[:, 0].set(0.0)
    dd = a.at[:, 0].set(nyq / n)
    return [(a[o], bm[o], dd[o]) for o in range(HY_ORDER)]


def _hyena_seq(u, mats, spec, conv_w, conv_b, skip, n_seg, L, blk0, col0):
    cm, sm, smt = mats
    t = min(L, 512)
    v, x1, x2 = _hy_short_conv(u, conv_w, conv_b, n_seg, L, blk0, col0)
    yr, yi = _hy_fwd(cm, sm, v, spec[0], n_seg, L, t)
    z = _hy_inv(cm, smt, yr, yi, x1, v, skip[0:1], n_seg, L, t, F32)
    yr, yi = _hy_fwd(cm, sm, z, spec[1], n_seg, L, t)
    return _hy_inv(cm, smt, yr, yi, x2, z, skip[1:2], n_seg, L, t, BF16)


def _moe_router_kernel(x_ref, g_ref, sc_ref, sh_ref, whi_ref, wlo_ref, rb_ref, h_ref, idx_ref, wts_ref, cnt_ref):
    x = x_ref[...]
    y = x * lax.rsqrt(jnp.mean(x * x, axis=-1, keepdims=True) + EPS) * g_ref[...]
    h = y * (1.0 + sc_ref[0]) + sh_ref[0]
    h_ref[...] = h
    hi, lo = _split2(h)
    logits = _dot(hi, whi_ref[...]) + _dot(hi, wlo_ref[...]) + _dot(lo, whi_ref[...])
    scores = jax.nn.sigmoid(logits)
    sel = scores + rb_ref[...]
    tm, E = sel.shape
    gsz = E // N_GROUPS
    neg = -jnp.inf
    lane = lax.broadcasted_iota(jnp.int32, (1, E), 1).astype(F32)
    glane = lax.broadcasted_iota(jnp.int32, (1, gsz), 1).astype(F32)
    gscore = []
    for g in range(N_GROUPS):
        blk = sel[:, g * gsz:(g + 1) * gsz]
        m1 = jnp.max(blk, axis=-1, keepdims=True)
        first = jnp.min(jnp.where(blk == m1, glane, float(gsz)), axis=-1, keepdims=True)
        m2 = jnp.max(jnp.where(glane == first, neg, blk), axis=-1, keepdims=True)
        gscore.append(m1 + m2)
    group_of_lane = lax.broadcasted_iota(jnp.int32, (1, E), 1) // gsz
    keep = jnp.zeros((tm, E), F32)
    for g in range(N_GROUPS):
        rank = jnp.zeros((tm, 1), F32)
        for o in range(N_GROUPS):
            if o != g:
                ahead = (gscore[o] >= gscore[g]) if o < g else (gscore[o] > gscore[g])
                rank = rank + jnp.where(ahead, 1.0, 0.0)
        keep = jnp.where(group_of_lane == g, jnp.where(rank < TOPK_GROUPS, 1.0, 0.0), keep)
    work = jnp.where(keep > 0.0, sel, neg)
    out_lane = lax.broadcasted_iota(jnp.int32, (1, idx_ref.shape[1]), 1)
    idx_out = jnp.zeros(idx_ref.shape, F32)
    w_out = jnp.zeros(wts_ref.shape, F32)
    total = jnp.zeros((tm, 1), F32)
    chosen = jnp.zeros((tm, E), F32)
    for j in range(TOP_K):
        mx = jnp.max(work, axis=-1, keepdims=True)
        am = jnp.min(jnp.where(work == mx, lane, float(E)), axis=-1, keepdims=True)
        hit = lane == am
        wj = jnp.sum(jnp.where(hit, scores, 0.0), axis=-1, keepdims=True)
        work = jnp.where(hit, neg, work)
        chosen = jnp.where(hit, 1.0, chosen)
        idx_out = jnp.where(out_lane == j, am, idx_out)
        w_out = jnp.where(out_lane == j, wj, w_out)
        total = total + wj
    idx_ref[...] = idx_out.astype(jnp.int32)
    wts_ref[...] = w_out / total * ROUTED_SCALE
    cnt_ref[0] = jnp.sum(chosen, axis=0, keepdims=True)


def _moe_router(x, n_tok, g, scale, shift, whi, wlo, rb, seg_rows, tm):
    D = x.shape[1]
    E = whi.shape[1]
    last = scale.shape[0] - 1
    row = lambda i: (i, 0)
    one = lambda i: (0, 0)
    mod = lambda i: (jnp.minimum(i * tm // seg_rows, last), 0, 0)
    return pl.pallas_call(
        _moe_router_kernel,
        grid=(n_tok // tm,),
        in_specs=[pl.BlockSpec((tm, D), row), pl.BlockSpec((1, D), one), pl.BlockSpec((1, 1, D), mod),
                  pl.BlockSpec((1, 1, D), mod), pl.BlockSpec((D, E), one), pl.BlockSpec((D, E), one),
                  pl.BlockSpec((1, E), one)],
        out_specs=[pl.BlockSpec((tm, D), row), pl.BlockSpec((tm, LANES), row), pl.BlockSpec((tm, LANES), row),
                   pl.BlockSpec((1, 1, E), lambda i: (i, 0, 0))],
        out_shape=[jax.ShapeDtypeStruct((n_tok, D), F32), jax.ShapeDtypeStruct((n_tok, LANES), jnp.int32),
                   jax.ShapeDtypeStruct((n_tok, LANES), F32), jax.ShapeDtypeStruct((n_tok // tm, 1, E), F32)],
        compiler_params=_cparams(("parallel",)),
        name="moe_router",
    )(x, g, scale, shift, whi, wlo, rb)


def _moe_pos_kernel(idx_ref, base_ref, pos_ref, *, K):
    tm = idx_ref.shape[0]
    E = base_ref.shape[2]
    idx = idx_ref[...]
    lane = lax.broadcasted_iota(jnp.int32, (1, E), 1)
    hits = [lane == idx[:, j:j + 1] for j in range(K)]
    onehot = jnp.zeros((tm, E), F32)
    for hit in hits:
        onehot = jnp.where(hit, 1.0, onehot)
    row = lax.broadcasted_iota(jnp.int32, (tm, tm), 0)
    col = lax.broadcasted_iota(jnp.int32, (tm, tm), 1)
    before = _dot((col < row).astype(BF16), onehot.astype(BF16))
    dest = before + base_ref[0]
    out_lane = lax.broadcasted_iota(jnp.int32, (1, pos_ref.shape[1]), 1)
    out = jnp.zeros(pos_ref.shape, F32)
    for j, hit in enumerate(hits):
        out = jnp.where(out_lane == j, jnp.sum(jnp.where(hit, dest, 0.0), axis=-1, keepdims=True), out)
    pos_ref[...] = out.astype(jnp.int32)


def _moe_plan(idx, cnt, K, bm, tm):
    n_tok = idx.shape[0]
    E = cnt.shape[2]
    cnt = cnt[:, 0, :]
    counts = jnp.sum(cnt, axis=0)
    pcounts = jnp.ceil(counts / bm) * bm
    pends = jnp.cumsum(pcounts)
    base = (pends - pcounts)[None, :] + jnp.cumsum(cnt, axis=0) - cnt
    pos = pl.pallas_call(
        functools.partial(_moe_pos_kernel, K=K),
        grid=(n_tok // tm,),
        in_specs=[pl.BlockSpec((tm, LANES), lambda i: (i, 0)), pl.BlockSpec((1, 1, E), lambda i: (i, 0, 0))],
        out_specs=pl.BlockSpec((tm, LANES), lambda i: (i, 0)),
        out_shape=jax.ShapeDtypeStruct((n_tok, LANES), jnp.int32),
        compiler_params=_cparams(("parallel",)),
        name="moe_positions",
    )(idx, base[:, None, :])
    n_blocks = -(-(n_tok * K + E * (bm - 1)) // bm)
    starts = jnp.arange(n_blocks, dtype=F32) * bm
    block_e = jnp.minimum(jnp.sum((pends[None, :] <= starts[:, None]).astype(jnp.int32), axis=1), E - 1)
    n_used = (pends[-1] / bm).astype(jnp.int32).reshape(1)
    return pos[:, :K].reshape(-1), block_e.astype(jnp.int32), n_used, n_blocks


def _per_token_rows(tm, K, copy):
    def start(t, carry):
        for j in range(K):
            copy(t, j).start(priority=j % 2)
        return carry

    def wait(t, carry):
        for j in range(K):
            copy(t, j).wait()
        return carry

    return (lambda: lax.fori_loop(0, tm, start, 0, unroll=2)), (lambda: lax.fori_loop(0, tm, wait, 0, unroll=2))


def _moe_dispatch_kernel(pos_ref, h_ref, xs_in_ref, xs_ref, sem, *, K):
    del xs_in_ref
    tm = h_ref.shape[0]

    def row_copy(t, j):
        return pltpu.make_async_copy(h_ref.at[pl.ds(t, 1)], xs_ref.at[pl.ds(pos_ref[t * K + j], 1)], sem)

    start, wait = _per_token_rows(tm, K, row_copy)
    start()
    wait()


def _moe_dispatch(pos, h, n_rows, K, tm):
    n_tok, D = h.shape
    xs0 = jnp.zeros((n_rows, D), h.dtype)
    return pl.pallas_call(
        functools.partial(_moe_dispatch_kernel, K=K),
        grid=(n_tok // tm,),
        in_specs=[pl.BlockSpec((tm * K,), lambda i: (i,), memory_space=pltpu.SMEM),
                  pl.BlockSpec((tm, D), lambda i: (i, 0)),
                  pl.BlockSpec(memory_space=pl.ANY)],
        out_specs=pl.BlockSpec(memory_space=pl.ANY),
        out_shape=jax.ShapeDtypeStruct((n_rows, D), h.dtype),
        scratch_shapes=[pltpu.SemaphoreType.DMA(())],
        input_output_aliases={2: 0},
        compiler_params=_cparams(("arbitrary",)),
        name="moe_dispatch",
    )(pos, h, xs0)


def _moe_expert_kernel(be_ref, nu_ref, x_ref, w1_ref, w3_ref, w2_ref, y_ref, w1b_ref, w3b_ref, w2b_ref):
    i = pl.program_id(0)
    used = i < nu_ref[0]
    new_expert = jnp.logical_or(i == 0, be_ref[i] != be_ref[jnp.maximum(i - 1, 0)])

    @pl.when(jnp.logical_and(used, new_expert))
    def _():
        w1b_ref[...] = w1_ref[0, 0].astype(BF16)
        w3b_ref[...] = w3_ref[0, 0].astype(BF16)
        w2b_ref[...] = w2_ref[0, 0].astype(BF16)

    @pl.when(used)
    def _():
        x = x_ref[...].astype(BF16)
        a = _dot(x, w1b_ref[...])
        b = _dot(x, w3b_ref[...])
        y_ref[...] = _dot((jax.nn.silu(a) * b).astype(BF16), w2b_ref[...])

    @pl.when(jnp.logical_not(used))
    def _():
        y_ref[...] = jnp.zeros_like(y_ref)


def _moe_experts(block_e, n_used, xs, w1, w3, w2, layer, bm):
    P, D = xs.shape
    F = w1.shape[3]
    grid_spec = pltpu.PrefetchScalarGridSpec(
        num_scalar_prefetch=2,
        grid=(P // bm,),
        in_specs=[pl.BlockSpec((bm, D), lambda i, be, nu: (i, 0)),
                  pl.BlockSpec((1, 1, D, F), lambda i, be, nu: (layer, be[i], 0, 0)),
                  pl.BlockSpec((1, 1, D, F), lambda i, be, nu: (layer, be[i], 0, 0)),
                  pl.BlockSpec((1, 1, F, D), lambda i, be, nu: (layer, be[i], 0, 0))],
        out_specs=pl.BlockSpec((bm, D), lambda i, be, nu: (i, 0)),
        scratch_shapes=[pltpu.VMEM((D, F), BF16), pltpu.VMEM((D, F), BF16), pltpu.VMEM((F, D), BF16)],
    )
    return pl.pallas_call(
        _moe_expert_kernel,
        grid_spec=grid_spec,
        out_shape=jax.ShapeDtypeStruct((P, D), F32),
        compiler_params=_cparams(("arbitrary",)),
        name="moe_experts",
    )(block_e, n_used, xs, w1, w3, w2)


def _moe_combine_kernel(pos_ref, wts_ref, x_ref, h_ref, sw1_ref, sw3_ref, sw2_ref, gate_ref, ys_ref, o_ref,
                        gath_ref, sem, *, K):
    tm = x_ref.shape[0]

    def row_copy(t, j):
        return pltpu.make_async_copy(ys_ref.at[pl.ds(pos_ref[t * K + j], 1)], gath_ref.at[j, pl.ds(t, 1)], sem)

    start, wait = _per_token_rows(tm, K, row_copy)
    start()
    h = h_ref[...].astype(BF16)
    mid = jax.nn.silu(_dot(h, sw1_ref[...])) * _dot(h, sw3_ref[...])
    acc = _dot(mid.astype(BF16), sw2_ref[...])
    wait()
    wts = wts_ref[...]
    for j in range(K):
        acc = acc + wts[:, j:j + 1] * gath_ref[j]
    o_ref[...] = x_ref[...] + gate_ref[0] * acc


def _moe_combine(pos, wts, x, h, sw1, sw3, sw2, gate, ys, n_tok, K, seg_rows, tm):
    D = x.shape[1]
    last = gate.shape[0] - 1
    row = lambda i: (i, 0)
    one = lambda i: (0, 0)
    return pl.pallas_call(
        functools.partial(_moe_combine_kernel, K=K),
        grid=(n_tok // tm,),
        in_specs=[pl.BlockSpec((tm * K,), lambda i: (i,), memory_space=pltpu.SMEM),
                  pl.BlockSpec((tm, LANES), row), pl.BlockSpec((tm, D), row), pl.BlockSpec((tm, D), row),
                  pl.BlockSpec(sw1.shape, one), pl.BlockSpec(sw3.shape, one), pl.BlockSpec(sw2.shape, one),
                  pl.BlockSpec((1, 1, D), lambda i: (jnp.minimum(i * tm // seg_rows, last), 0, 0)),
                  pl.BlockSpec(memory_space=pl.ANY)],
        out_specs=pl.BlockSpec((tm, D), row),
        out_shape=jax.ShapeDtypeStruct((n_tok, D), F32),
        scratch_shapes=[pltpu.VMEM((K, tm, D), F32), pltpu.SemaphoreType.DMA(())],
        compiler_params=_cparams(("arbitrary",)),
        name="moe_combine",
    )(pos, wts, x, h, sw1, sw3, sw2, gate, ys)


def _moe_ffn_residual(x, n_tok, norm_g, scale, shift, gate, router_w, router_b, ew1, ew3, ew2, layer, sw1, sw3, sw2,
                      seg_rows, bm):
    K = TOP_K
    tm = 256
    whi, wlo = _split2(router_w)
    h, idx, wts, cnt = _moe_router(x, n_tok, norm_g, scale, shift, whi, wlo, router_b[None], seg_rows, tm)
    pos, block_e, n_used, n_blocks = _moe_plan(idx, cnt, K, bm, tm)
    xs = _moe_dispatch(pos, h, n_blocks * bm, K, 128)
    ys = _moe_experts(block_e, n_used, xs, ew1, ew3, ew2, layer, bm)
    return _moe_combine(pos, wts, x, h, sw1.astype(BF16), sw3.astype(BF16), sw2.astype(BF16), gate, ys,
                        n_tok, K, seg_rows, 128)


def kernel(x, c, ctx, c_ctx, w_mod, b_mod, norm1_g, norm2_g, w_in, mlstm_gate_b, mlstm_norm_g, da_qnorm_g, da_knorm_g, da_lambda, da_subln_g, hy_conv_w, hy_conv_b, hy_w1, hy_b1, hy_w2, hy_b2, hy_w3, hy_freq, hy_skip, w_out, router_w, router_b, exp_w1, exp_w3, exp_w2, sh_w1, sh_w3, sh_w2):
    B, S, D = x.shape
    n_ctx = ctx.shape[1]
    depth = w_in.shape[0]
    n_lat = B * S
    ml_w = mlstm_norm_g.shape[1]
    da_dh = da_qnorm_g.shape[1]
    da_w = DA_HEADS * 2 * da_dh
    hy_w = hy_skip.shape[2]
    n_gates = 4 * ML_HEADS
    ml_col, da_col = 0, 4 * ml_w
    hy_col = da_col + 3 * da_w
    gate_col = hy_col + 3 * hy_w
    tm = 512

    X = jnp.concatenate([x.reshape(n_lat, D), ctx.reshape(B * n_ctx, D)], axis=0)
    sc = jax.nn.silu(jnp.concatenate([c, c_ctx[None]], axis=0))
    cos, sin = _axial_rope_tables(S, da_dh, da_w // da_dh, tm)
    mats_l = _dft_mats(S)
    mats_c = _dft_mats(n_ctx)
    for l in range(depth):
        last = l == depth - 1
        lam_init = 0.8 - 0.6 * math.exp(-0.3 * l)
        mods = (jnp.dot(sc, w_mod[l], precision=lax.Precision.HIGHEST) + b_mod[l]).reshape(B + 1, 6, 1, D)
        sh1, s1, g1, sh2, s2, g2 = [mods[:, i] for i in range(6)]
        wl = w_in[l]
        w_big = jnp.concatenate([wl[:, :4 * ml_w], wl[:, 4 * ml_w + n_gates:], wl[:, 4 * ml_w:4 * ml_w + n_gates],
                                 jnp.zeros((D, LANES - n_gates), F32)], axis=1).astype(BF16)
        U = _norm_mod_matmul(X, norm1_g[l][None], s1, sh1, w_big, S, tm, w_big.shape[1] // 3)
        gb = jnp.concatenate([mlstm_gate_b[l], jnp.zeros((LANES - n_gates,), F32)])[None]
        m_out = _mlstm_mixer(U, gb, mlstm_norm_g[l][None], B, S, n_ctx, ml_col, gate_col)
        d_out = _diff_attn_mixer(U, cos, sin, da_qnorm_g[l], da_knorm_g[l], da_lambda[l], da_subln_g[l], lam_init,
                                 B, S, n_ctx, da_col, not last)
        hy_args = (hy_w1[l], hy_b1[l], hy_w2[l], hy_b2[l], hy_w3[l], hy_freq[l], hy_w)
        y_out = _hyena_seq(U, mats_l, _hyena_spectrum(mats_l, S, *hy_args), hy_conv_w[l], hy_conv_b[l][None],
                           hy_skip[l], B, S, 0, hy_col)
        n_rows = n_lat
        if not last:
            y_ctx = _hyena_seq(U, mats_c, _hyena_spectrum(mats_c, n_ctx, *hy_args), hy_conv_w[l], hy_conv_b[l][None],
                               hy_skip[l], B, n_ctx, n_lat // n_ctx, hy_col)
            y_out = jnp.concatenate([y_out, y_ctx], axis=0)
            n_rows = n_lat + B * n_ctx
        wo = w_out[l].astype(BF16)
        X = _out_proj_residual(m_out, d_out, y_out, wo[:ml_w], wo[ml_w:ml_w + da_w], wo[ml_w + da_w:], X, g1,
                               n_rows, S, tm)
        X = _moe_ffn_residual(X, n_rows, norm2_g[l][None], s2, sh2, g2, router_w[l], router_b[l],
                              exp_w1, exp_w3, exp_w2, l, sh_w1[l], sh_w3[l], sh_w2[l], S,
                              MOE_BLOCK if l == 0 else 2 * MOE_BLOCK)
    return X[:n_lat].reshape(B, S, D)
```

```python
import functools
import math

import jax
import jax.numpy as jnp
from jax import lax
from jax.experimental import pallas as pl
from jax.experimental.pallas import tpu as pltpu

F32 = jnp.float32
BF16 = jnp.bfloat16

EPS = 1e-6
GRID_W = 64
ROPE_THETA = 10000.0
ML_HEADS = 4
ML_CHUNK = 256
ML_M_INIT = -1e30
DA_HEADS = 4
HY_ORDER = 2
HY_BANDS = 8
HY_SHIFT = 0.05
HY_TARGET = 1e-2
HY_FAST = 0.3
HY_SLOW = 1.5
N_GROUPS = 8
TOPK_GROUPS = 4
TOP_K = 8
ROUTED_SCALE = 2.5
MOE_BLOCK = 128
LANES = 128
VMEM_LIMIT = 56 * 1024 * 1024


def _cparams(sem):
    return pltpu.CompilerParams(dimension_semantics=sem, vmem_limit_bytes=VMEM_LIMIT)


def _dot(a, b):
    return jnp.dot(a, b, preferred_element_type=F32)


def _dot_nt(a, b):
    return lax.dot_general(a, b, (((1,), (1,)), ((), ())), preferred_element_type=F32)


def _dot_tn(a, b):
    return lax.dot_general(a, b, (((0,), (0,)), ((), ())), preferred_element_type=F32)


def _split3(a):
    hi = a.astype(BF16)
    r = a - hi.astype(F32)
    mid = r.astype(BF16)
    lo = (r - mid.astype(F32)).astype(BF16)
    return hi, mid, lo


def _split2(a):
    hi = a.astype(BF16)
    lo = (a - hi.astype(F32)).astype(BF16)
    return hi, lo


def _norm_mod_mm_kernel(x_ref, g_ref, sc_ref, sh_ref, w_ref, o_ref, xn_ref):
    @pl.when(pl.program_id(1) == 0)
    def _():
        x = x_ref[...]
        y = x * lax.rsqrt(jnp.mean(x * x, axis=-1, keepdims=True) + EPS) * g_ref[...]
        xn_ref[...] = (y * (1.0 + sc_ref[0]) + sh_ref[0]).astype(BF16)

    o_ref[...] = _dot(xn_ref[...], w_ref[...])


def _norm_mod_matmul(x, g, scale, shift, w, seg_rows, tm, tn):
    R, D = x.shape
    N = w.shape[1]
    last = scale.shape[0] - 1
    mod_map = lambda i, j: (jnp.minimum(i * tm // seg_rows, last), 0, 0)
    return pl.pallas_call(
        _norm_mod_mm_kernel,
        grid=(R // tm, N // tn),
        in_specs=[
            pl.BlockSpec((tm, D), lambda i, j: (i, 0)),
            pl.BlockSpec((1, D), lambda i, j: (0, 0)),
            pl.BlockSpec((1, 1, D), mod_map),
            pl.BlockSpec((1, 1, D), mod_map),
            pl.BlockSpec((D, tn), lambda i, j: (0, j)),
        ],
        out_specs=pl.BlockSpec((tm, tn), lambda i, j: (i, j)),
        out_shape=jax.ShapeDtypeStruct((R, N), F32),
        scratch_shapes=[pltpu.VMEM((tm, D), BF16)],
        compiler_params=_cparams(("parallel", "arbitrary")),
        name="norm_mod_matmul",
    )(x, g, scale, shift, w)


def _out_proj_kernel(m_ref, d_ref, y_ref, wm_ref, wd_ref, wy_ref, x_ref, gate_ref, o_ref):
    acc = _dot(m_ref[...], wm_ref[...]) + _dot(d_ref[...], wd_ref[...]) + _dot(y_ref[...], wy_ref[...])
    o_ref[...] = x_ref[...] + gate_ref[0] * acc


def _out_proj_residual(m, d, y, wm, wd, wy, x, gate, n_rows, seg_rows, tm):
    R, D = n_rows, x.shape[1]
    last = gate.shape[0] - 1
    row = lambda i: (i, 0)
    full = lambda i: (0, 0)
    return pl.pallas_call(
        _out_proj_kernel,
        grid=(R // tm,),
        in_specs=[
            pl.BlockSpec((tm, m.shape[1]), row),
            pl.BlockSpec((tm, d.shape[1]), row),
            pl.BlockSpec((tm, y.shape[1]), row),
            pl.BlockSpec(wm.shape, full),
            pl.BlockSpec(wd.shape, full),
            pl.BlockSpec(wy.shape, full),
            pl.BlockSpec((tm, D), row),
            pl.BlockSpec((1, 1, D), lambda i: (jnp.minimum(i * tm // seg_rows, last), 0, 0)),
        ],
        out_specs=pl.BlockSpec((tm, D), row),
        out_shape=jax.ShapeDtypeStruct((R, D), F32),
        compiler_params=_cparams(("parallel",)),
        name="out_proj_residual",
    )(m, d, y, wm, wd, wy, x, gate)


def _log_sigmoid(x):
    return jnp.minimum(x, 0.0) - jnp.log1p(jnp.exp(-jnp.abs(x)))


def _mlstm_gate_tables(g_ref, r0, L, gb, tril, triu):
    g = g_ref[pl.ds(r0, L), :] + gb
    lf = _log_sigmoid(g)
    gT = g.T
    lfT = lf.T
    parts = _split3(lf)
    partsT = _split3(lfT)
    cs_f = sum(_dot(tril, p) for p in parts)
    cs_b = sum(_dot(triu, p) for p in parts)
    rs_f = sum(_dot(p, triu) for p in partsT)
    rs_b = sum(_dot(p, tril) for p in partsT)
    return g, gT, cs_f, cs_b, rs_f, rs_b


def _mlstm_chunk(q, k, v, i_col, i_row, b_col, b_row, b_end, mask, state):
    C, n, m = state
    qb = q.astype(BF16)
    kb = k.astype(BF16)
    dmat = jnp.where(mask, b_col - b_row + i_row, -jnp.inf)
    inter = b_col + m
    m_t = jnp.maximum(inter, jnp.max(dmat, axis=-1, keepdims=True))
    s = _dot_nt(qb, kb) * jnp.exp(dmat - m_t)
    carry_w = jnp.exp(inter - m_t)
    num = _dot(s.astype(BF16), v.astype(BF16)) + carry_w * _dot_nt(qb, C.astype(BF16))
    den = jnp.sum(s, axis=-1, keepdims=True) + carry_w * jnp.sum(q * n, axis=-1, keepdims=True)
    h = num / jnp.maximum(jnp.abs(den), jnp.exp(-m_t))
    g = b_end - b_col + i_col
    m_new = jnp.maximum(b_end + m, jnp.max(g, axis=0, keepdims=True))
    ws = jnp.exp(g - m_new)
    decay = jnp.exp(b_end + m - m_new)
    C_new = decay * C + _dot_tn((v * ws).astype(BF16), kb)
    n_new = decay * n + jnp.sum(ws * k, axis=0, keepdims=True)
    return h, (C_new, n_new, m_new)


def _mlstm_kernel(ql_ref, kl_ref, vl_ref, ol_ref, gl_ref, qc_ref, kc_ref, vc_ref, oc_ref, gc_ref,
                  gb_ref, ng_ref, outl_ref, outc_ref, hf_ref, hb_ref, *, L, H, dh):
    S = ql_ref.shape[0]
    n_ctx = qc_ref.shape[0]
    row = lax.broadcasted_iota(jnp.int32, (L, L), 0)
    col = lax.broadcasted_iota(jnp.int32, (L, L), 1)
    lower = col <= row
    upper = col >= row
    tril = lower.astype(BF16)
    triu = upper.astype(BF16)
    gb = gb_ref[...]
    k_scale = dh ** -0.5

    def both_dirs(refs_f, r0_f, refs_b, r0_b, hoff_f, hoff_b, state):
        new_state = []
        for d, (refs, r0, hoff, h_ref) in enumerate(((refs_f, r0_f, hoff_f, hf_ref),
                                                      (refs_b, r0_b, hoff_b, hb_ref))):
            q_ref, k_ref, v_ref, g_ref = refs
            g, gT, cs_f, cs_b, rs_f, rs_b = _mlstm_gate_tables(g_ref, r0, L, gb, tril, triu)
            cs, rs, mask = (cs_f, rs_f, lower) if d == 0 else (cs_b, rs_b, upper)
            end_row = L - 1 if d == 0 else 0
            for hh in range(H):
                ic = 2 * d * H + hh
                fc = ic + H
                lanes = slice(hh * dh, (hh + 1) * dh)
                q = q_ref[pl.ds(r0, L), lanes]
                k = k_ref[pl.ds(r0, L), lanes] * k_scale
                v = v_ref[pl.ds(r0, L), lanes]
                h, st = _mlstm_chunk(q, k, v, g[:, ic:ic + 1], gT[ic:ic + 1, :], cs[:, fc:fc + 1],
                                     rs[fc:fc + 1, :], cs[end_row:end_row + 1, fc:fc + 1], mask,
                                     state[d * H + hh])
                h_ref[pl.ds(hoff + r0, L), lanes] = h
                new_state.append(st)
        return tuple(new_state)

    state = tuple((jnp.zeros((dh, dh), F32), jnp.zeros((1, dh), F32), jnp.full((1, 1), ML_M_INIT, F32))
                  for _ in range(2 * H))
    ctx_refs = (qc_ref, kc_ref, vc_ref, gc_ref)
    lat_refs = (ql_ref, kl_ref, vl_ref, gl_ref)
    n_cc = n_ctx // L
    for c in range(n_cc):
        state = both_dirs(ctx_refs, c * L, ctx_refs, (n_cc - 1 - c) * L, 0, 0, state)
    n_lc = S // L

    def body(c, st):
        r_f = pl.multiple_of(c * L, L)
        r_b = pl.multiple_of((n_lc - 1 - c) * L, L)
        return both_dirs(lat_refs, r_f, lat_refs, r_b, n_ctx, n_ctx, st)

    lax.fori_loop(0, n_lc, body, state)

    def finish(o_ref, out_ref, hoff, rows):
        def fbody(c, carry):
            r0 = pl.multiple_of(c * L, L)
            hs = hf_ref[pl.ds(hoff + r0, L), :] + hb_ref[pl.ds(hoff + r0, L), :]
            for hh in range(H):
                lanes = slice(hh * dh, (hh + 1) * dh)
                hv = hs[:, lanes]
                hn = hv * lax.rsqrt(jnp.mean(hv * hv, axis=-1, keepdims=True) + EPS) * ng_ref[:, lanes]
                out_ref[pl.ds(r0, L), lanes] = (jax.nn.sigmoid(o_ref[pl.ds(r0, L), lanes]) * hn).astype(out_ref.dtype)
            return carry
        lax.fori_loop(0, rows // L, fbody, 0)

    finish(ol_ref, outl_ref, n_ctx, S)
    finish(oc_ref, outc_ref, 0, n_ctx)


def _mlstm_mixer(u, gate_b, norm_g, B, S, n_ctx, col0, gate_col0):
    W = norm_g.shape[1]
    H = ML_HEADS
    dh = W // H
    cb = col0 // W
    gcb = gate_col0 // LANES
    cblk = (B * S) // n_ctx

    def lat(j):
        return pl.BlockSpec((S, W), lambda b: (b, cb + j))

    def ctx(j):
        return pl.BlockSpec((n_ctx, W), lambda b: (cblk + b, cb + j))

    one = lambda b: (0, 0)
    out_l, out_c = pl.pallas_call(
        functools.partial(_mlstm_kernel, L=ML_CHUNK, H=H, dh=dh),
        grid=(B,),
        in_specs=[lat(0), lat(1), lat(2), lat(3), pl.BlockSpec((S, LANES), lambda b: (b, gcb)),
                  ctx(0), ctx(1), ctx(2), ctx(3), pl.BlockSpec((n_ctx, LANES), lambda b: (cblk + b, gcb)),
                  pl.BlockSpec((1, LANES), one), pl.BlockSpec((1, W), one)],
        out_specs=[pl.BlockSpec((S, W), lambda b: (b, 0)), pl.BlockSpec((n_ctx, W), lambda b: (b, 0))],
        out_shape=[jax.ShapeDtypeStruct((B * S, W), BF16), jax.ShapeDtypeStruct((B * n_ctx, W), BF16)],
        scratch_shapes=[pltpu.VMEM((n_ctx + S, W), F32), pltpu.VMEM((n_ctx + S, W), F32)],
        compiler_params=_cparams(("parallel",)),
        name="mlstm",
    )(u, u, u, u, u, u, u, u, u, u, gate_b, norm_g)
    return jnp.concatenate([out_l, out_c], axis=0)


def _da_prep_kernel(q_ref, k_ref, v_ref, cos_ref, sin_ref, qg_ref, kg_ref, seg_ref, qo_ref, ko_ref, vo_ref, *, dh):
    cos = cos_ref[...]
    sin = sin_ref[...]
    seg = seg_ref[...]
    W = q_ref.shape[1]
    lane = lax.broadcasted_iota(jnp.int32, (1, W), 1)
    quarter = dh // 4
    first = (lane % (2 * quarter)) < quarter

    def norm_rope(x, g):
        hi, lo = _split2(x * x)
        ms = (_dot(hi, seg) + _dot(lo, seg)) * (1.0 / dh)
        xn = x * lax.rsqrt(ms + EPS) * g
        rot = jnp.where(first, -pltpu.roll(xn, W - quarter, 1), pltpu.roll(xn, quarter, 1))
        return xn * cos + rot * sin

    qo_ref[...] = (norm_rope(q_ref[...], qg_ref[...]) * (dh ** -0.5)).astype(BF16)
    ko_ref[...] = norm_rope(k_ref[...], kg_ref[...]).astype(BF16)
    vo_ref[...] = v_ref[...].astype(BF16)


def _da_prep(u, cos, sin, qg, kg, seg, n_lat_rows, S, col0, tm, dh):
    R = u.shape[0]
    W = qg.shape[1]
    cb = col0 // W
    n_lat = n_lat_rows // tm
    per_seq = S // tm
    tab = lambda i: (jnp.where(i < n_lat, i % per_seq, per_seq), 0)
    one = lambda i: (0, 0)
    row = lambda i: (i, 0)

    def ucol(j):
        return pl.BlockSpec((tm, W), lambda i: (i, cb + j))

    return pl.pallas_call(
        functools.partial(_da_prep_kernel, dh=dh),
        grid=(R // tm,),
        in_specs=[ucol(0), ucol(1), ucol(2), pl.BlockSpec((tm, W), tab), pl.BlockSpec((tm, W), tab),
                  pl.BlockSpec((1, W), one), pl.BlockSpec((1, W), one), pl.BlockSpec((W, W), one)],
        out_specs=[pl.BlockSpec((tm, W), row)] * 3,
        out_shape=[jax.ShapeDtypeStruct((R, W), BF16)] * 3,
        compiler_params=_cparams(("parallel",)),
        name="da_prep",
    )(u, u, u, cos, sin, qg, kg, seg)


def _da_attn_kernel(*refs, n_kv, dh, lam_init):
    q_ref = refs[0]
    k_refs = refs[1:1 + n_kv]
    v_refs = refs[1 + n_kv:1 + 2 * n_kv]
    lam_ref, sg_ref, o_ref = refs[1 + 2 * n_kv:]
    lp = lam_ref[...]
    lam = (jnp.exp(jnp.sum(lp[0:1] * lp[1:2], axis=-1, keepdims=True))
           - jnp.exp(jnp.sum(lp[2:3] * lp[3:4], axis=-1, keepdims=True)) + lam_init)
    q = q_ref[...]
    acc = None
    probs = []
    for mp in range(2):
        lanes = slice(mp * dh, (mp + 1) * dh)
        s = [_dot_nt(q[:, lanes], k_ref[:, lanes]) for k_ref in k_refs]
        mx = functools.reduce(jnp.maximum, [jnp.max(si, axis=-1, keepdims=True) for si in s])
        p = [jnp.exp(si - mx) for si in s]
        den = sum(jnp.sum(pi, axis=-1, keepdims=True) for pi in p)
        probs.append([pi / den for pi in p])
    for j in range(n_kv):
        a = (probs[0][j] - lam * probs[1][j]).astype(BF16)
        t = _dot(a, v_refs[j][...])
        acc = t if acc is None else acc + t
    o = acc * lax.rsqrt(jnp.mean(acc * acc, axis=-1, keepdims=True) + EPS) * sg_ref[...]
    o_ref[...] = (o * (1.0 - lam_init)).astype(o_ref.dtype)


def _da_attention(q, k, v, lam_p, subln_g, lam_init, B, q_rows, q_blk0, kv_segs, tq, dh):
    H = DA_HEADS
    vd = 2 * dh
    nq = q_rows // tq
    q0 = q_blk0

    def kv_spec(rows, blk0):
        return pl.BlockSpec((rows, vd), lambda b, h, i: (blk0 + b, h))

    kspecs = [kv_spec(r, b0) for r, b0 in kv_segs]
    one = lambda b, h, i: (0, 0)
    return pl.pallas_call(
        functools.partial(_da_attn_kernel, n_kv=len(kv_segs), dh=dh, lam_init=lam_init),
        grid=(B, H, nq),
        in_specs=[pl.BlockSpec((tq, vd), lambda b, h, i: (q0 + b * nq + i, h))] + kspecs + kspecs
                 + [pl.BlockSpec(lam_p.shape, one), pl.BlockSpec((1, vd), one)],
        out_specs=pl.BlockSpec((tq, vd), lambda b, h, i: (b * nq + i, h)),
        out_shape=jax.ShapeDtypeStruct((B * q_rows, H * vd), BF16),
        compiler_params=_cparams(("parallel", "parallel", "arbitrary")),
        name="da_attention",
    )(q, *([k] * len(kv_segs)), *([v] * len(kv_segs)), lam_p, subln_g)


def _axial_rope_tables(S, dh, reps, pad_rows):
    rows = S // GRID_W
    r = jnp.repeat(jnp.arange(rows, dtype=F32), GRID_W)
    col = jnp.tile(jnp.arange(GRID_W, dtype=F32), rows)
    n_freq = dh // 4
    inv = ROPE_THETA ** (-jnp.arange(n_freq, dtype=F32) / n_freq)
    ar = r[:, None] * inv
    ac = col[:, None] * inv
    ang = jnp.concatenate([ar, ar, ac, ac], axis=-1)
    cos = jnp.concatenate([jnp.tile(jnp.cos(ang), (1, reps)), jnp.ones((pad_rows, dh * reps), F32)], axis=0)
    sin = jnp.concatenate([jnp.tile(jnp.sin(ang), (1, reps)), jnp.zeros((pad_rows, dh * reps), F32)], axis=0)
    return cos, sin


def _diff_attn_mixer(u, cos, sin, qg, kg, lam_p, subln_g, lam_init, B, S, n_ctx, col0, need_ctx):
    dh = qg.shape[0]
    W = DA_HEADS * 2 * dh
    seg = (jnp.arange(W)[:, None] // dh == jnp.arange(W)[None, :] // dh).astype(BF16)
    tm = 512
    q, k, v = _da_prep(u, cos, sin, jnp.tile(qg, W // dh)[None], jnp.tile(kg, W // dh)[None], seg,
                       B * S, S, col0, tm, dh)
    sg = subln_g[None]
    ctx_blk0 = (B * S) // n_ctx
    tq = 256
    out_l = _da_attention(q, k, v, lam_p, sg, lam_init, B, S, 0, [(n_ctx, ctx_blk0), (S, 0)], tq, dh)
    if not need_ctx:
        return out_l
    out_c = _da_attention(q, k, v, lam_p, sg, lam_init, B, n_ctx, (B * S) // n_ctx, [(n_ctx, ctx_blk0)], n_ctx, dh)
    return jnp.concatenate([out_l, out_c], axis=0)


def _hy_conv_kernel(v_ref, x1_ref, x2_ref, w_ref, b_ref, vo_ref, x1o_ref, x2o_ref):
    L, W = v_ref.shape
    row = lax.broadcasted_iota(jnp.int32, (L, 1), 0)
    for j, (i_ref, o_ref) in enumerate(((v_ref, vo_ref), (x1_ref, x1o_ref), (x2_ref, x2o_ref))):
        lanes = slice(j * W, (j + 1) * W)
        u = i_ref[...]
        prev = jnp.where(row == 0, 0.0, pltpu.roll(u, 1, 0))
        nxt = jnp.where(row == L - 1, 0.0, pltpu.roll(u, L - 1, 0))
        o_ref[...] = prev * w_ref[0:1, lanes] + u * w_ref[1:2, lanes] + nxt * w_ref[2:3, lanes] + b_ref[:, lanes]


def _hy_short_conv(u, conv_w, conv_b, n_seg, L, blk0, col0):
    W = conv_w.shape[1] // 3
    cb = col0 // W
    one = lambda b: (0, 0)

    def ucol(j):
        return pl.BlockSpec((L, W), lambda b: (blk0 + b, cb + j))

    return pl.pallas_call(
        _hy_conv_kernel,
        grid=(n_seg,),
        in_specs=[ucol(0), ucol(1), ucol(2), pl.BlockSpec(conv_w.shape, one), pl.BlockSpec(conv_b.shape, one)],
        out_specs=[pl.BlockSpec((L, W), lambda b: (b, 0))] * 3,
        out_shape=[jax.ShapeDtypeStruct((n_seg * L, W), F32)] * 3,
        compiler_params=_cparams(("parallel",)),
        name="hy_short_conv",
    )(u, u, u, conv_w, conv_b)


def _hy_fwd_kernel(c_ref, s_ref, z_ref, *rest, raw):
    z = z_ref[...].astype(BF16)
    zr = _dot(c_ref[...], z)
    zi = _dot(s_ref[...], z)
    if raw:
        yr_ref, yi_ref = rest
        yr_ref[...] = zr
        yi_ref[...] = zi
    else:
        a_ref, b_ref, d_ref, yr_ref, yi_ref = rest
        yr_ref[...] = (zr * a_ref[...] - zi * b_ref[...]).astype(yr_ref.dtype)
        yi_ref[...] = (zr * b_ref[...] + zi * d_ref[...]).astype(yi_ref.dtype)


def _hy_fwd(cm, sm, z, coefs, n_seg, L, tk):
    W = z.shape[1]
    nk = L // tk
    raw = coefs is None
    mat = pl.BlockSpec((tk, L), lambda i, b: (i, 0))
    cf = pl.BlockSpec((tk, W), lambda i, b: (i, 0))
    out = pl.BlockSpec((tk, W), lambda i, b: (b * nk + i, 0))
    odt = F32 if raw else BF16
    return pl.pallas_call(
        functools.partial(_hy_fwd_kernel, raw=raw),
        grid=(nk, n_seg),
        in_specs=[mat, mat, pl.BlockSpec((L, W), lambda i, b: (b, 0))] + ([] if raw else [cf, cf, cf]),
        out_specs=[out, out],
        out_shape=[jax.ShapeDtypeStruct((n_seg * L, W), odt)] * 2,
        compiler_params=_cparams(("parallel", "arbitrary")),
        name="hy_dft_fwd",
    )(cm, sm, z, *(() if raw else coefs))


def _hy_inv_kernel(c_ref, st_ref, yr_ref, yi_ref, x_ref, vz_ref, skip_ref, o_ref):
    y = _dot(c_ref[...], yr_ref[...]) + _dot(st_ref[...], yi_ref[...])
    o_ref[...] = (x_ref[...] * (y + skip_ref[...] * vz_ref[...])).astype(o_ref.dtype)


def _hy_inv(cm, smt, yr, yi, xg, vz, skip, n_seg, L, tt, out_dtype):
    W = yr.shape[1]
    nt = L // tt
    mat = pl.BlockSpec((tt, L), lambda i, b: (i, 0))
    seq = pl.BlockSpec((L, W), lambda i, b: (b, 0))
    row = pl.BlockSpec((tt, W), lambda i, b: (b * nt + i, 0))
    return pl.pallas_call(
        _hy_inv_kernel,
        grid=(nt, n_seg),
        in_specs=[mat, mat, seq, seq, row, row, pl.BlockSpec((1, W), lambda i, b: (0, 0))],
        out_specs=row,
        out_shape=jax.ShapeDtypeStruct((n_seg * L, W), out_dtype),
        compiler_params=_cparams(("parallel", "arbitrary")),
        name="hy_dft_inv",
    )(cm, smt, yr, yi, xg, vz, skip)


def _dft_mats(L):
    k = jnp.arange(L, dtype=jnp.int32)
    kn = (k[:, None] * k[None, :]) % (2 * L)
    ang = kn.astype(F32) * (math.pi / L)
    cm = jnp.cos(ang)
    sm = -jnp.sin(ang)
    sm = sm.at[0].set(jnp.where(k % 2 == 0, 1.0, -1.0))
    return cm.astype(BF16), sm.astype(BF16), sm.T.astype(BF16)


def _hyena_filters(L, w1, b1, w2, b2, w3, freq, W):
    t01 = jnp.linspace(0.0, 1.0, L, dtype=F32)[:, None]
    wpos = (2.0 * math.pi / L) * jnp.arange(L, dtype=F32)[:, None]
    bands = jnp.linspace(1e-4, HY_BANDS - 1, HY_BANDS, dtype=F32)
    feats = jnp.concatenate([t01, jnp.cos(wpos * bands), -jnp.sin(wpos * bands)], axis=-1)
    hp = lax.Precision.HIGHEST
    h = jnp.sin(freq[0] * (jnp.dot(feats, w1, precision=hp) + b1))
    h = jnp.sin(freq[1] * (jnp.dot(h, w2, precision=hp) + b2))
    h = jnp.dot(h, w3, precision=hp).reshape(L, HY_ORDER, 2, W)
    deltas = jnp.abs(jnp.linspace(math.log(HY_TARGET) / HY_SLOW, math.log(HY_TARGET) / HY_FAST, W, dtype=F32))
    h = h * (jnp.exp(-t01 * deltas) + HY_SHIFT)[:, None, None, :]
    hf, hb = h[:, :, 0], h[:, :, 1]
    hf = hf.at[0].add(hb[0])
    hb = hb.at[0].set(0.0)
    scale = lax.rsqrt(jnp.sum(hf * hf, axis=0, keepdims=True) + jnp.sum(hb * hb, axis=0, keepdims=True) + EPS)
    return (hf * scale).reshape(L, HY_ORDER * W), (hb * scale).reshape(L, HY_ORDER * W)


def _hyena_spectrum(mats, L, w1, b1, w2, b2, w3, freq, W):
    cm, sm, _ = mats
    hf, hb = _hyena_filters(L, w1, b1, w2, b2, w3, freq, W)
    cols = jnp.concatenate([hf[:, :W], hf[:, W:], hb[:, :W], hb[:, W:]], axis=0)
    tk = min(L, 512)
    gr, gi = _hy_fwd(cm, sm, cols, None, 2 * HY_ORDER, L, tk)
    gr = gr.reshape(2, HY_ORDER, L, W)
    gi = gi.reshape(2, HY_ORDER, L, W)
    kr = gr[0] + gr[1]
    ki = gi[0] - gi[1]
    nyq = gi[0, :, 0] + gi[1, :, 0]
    n = 2.0 * L
    wk = jnp.full((L, 1), 2.0 / n, F32).at[0].set(1.0 / n)
    a = kr * wk
    bm = (ki * wk).at[:, 0].set(0.0)
    dd = a.at[:, 0].set(nyq / n)
    return [(a[o], bm[o], dd[o]) for o in range(HY_ORDER)]


def _hyena_seq(u, mats, spec, conv_w, conv_b, skip, n_seg, L, blk0, col0):
    cm, sm, smt = mats
    t = min(L, 512)
    v, x1, x2 = _hy_short_conv(u, conv_w, conv_b, n_seg, L, blk0, col0)
    yr, yi = _hy_fwd(cm, sm, v, spec[0], n_seg, L, t)
    z = _hy_inv(cm, smt, yr, yi, x1, v, skip[0:1], n_seg, L, t, F32)
    yr, yi = _hy_fwd(cm, sm, z, spec[1], n_seg, L, t)
    return _hy_inv(cm, smt, yr, yi, x2, z, skip[1:2], n_seg, L, t, BF16)


def _moe_router_kernel(x_ref, g_ref, sc_ref, sh_ref, whi_ref, wlo_ref, rb_ref, h_ref, idx_ref, wts_ref, cnt_ref):
    x = x_ref[...]
    y = x * lax.rsqrt(jnp.mean(x * x, axis=-1, keepdims=True) + EPS) * g_ref[...]
    h = y * (1.0 + sc_ref[0]) + sh_ref[0]
    h_ref[...] = h
    hi, lo = _split2(h)
    logits = _dot(hi, whi_ref[...]) + _dot(hi, wlo_ref[...]) + _dot(lo, whi_ref[...])
    scores = jax.nn.sigmoid(logits)
    sel = scores + rb_ref[...]
    tm, E = sel.shape
    gsz = E // N_GROUPS
    neg = -jnp.inf
    lane = lax.broadcasted_iota(jnp.int32, (1, E), 1).astype(F32)
    glane = lax.broadcasted_iota(jnp.int32, (1, gsz), 1).astype(F32)
    gscore = []
    for g in range(N_GROUPS):
        blk = sel[:, g * gsz:(g + 1) * gsz]
        m1 = jnp.max(blk, axis=-1, keepdims=True)
        first = jnp.min(jnp.where(blk == m1, glane, float(gsz)), axis=-1, keepdims=True)
        m2 = jnp.max(jnp.where(glane == first, neg, blk), axis=-1, keepdims=True)
        gscore.append(m1 + m2)
    group_of_lane = lax.broadcasted_iota(jnp.int32, (1, E), 1) // gsz
    keep = jnp.zeros((tm, E), F32)
    for g in range(N_GROUPS):
        rank = jnp.zeros((tm, 1), F32)
        for o in range(N_GROUPS):
            if o != g:
                ahead = (gscore[o] >= gscore[g]) if o < g else (gscore[o] > gscore[g])
                rank = rank + jnp.where(ahead, 1.0, 0.0)
        keep = jnp.where(group_of_lane == g, jnp.where(rank < TOPK_GROUPS, 1.0, 0.0), keep)
    work = jnp.where(keep > 0.0, sel, neg)
    out_lane = lax.broadcasted_iota(jnp.int32, (1, idx_ref.shape[1]), 1)
    idx_out = jnp.zeros(idx_ref.shape, F32)
    w_out = jnp.zeros(wts_ref.shape, F32)
    total = jnp.zeros((tm, 1), F32)
    chosen = jnp.zeros((tm, E), F32)
    for j in range(TOP_K):
        mx = jnp.max(work, axis=-1, keepdims=True)
        am = jnp.min(jnp.where(work == mx, lane, float(E)), axis=-1, keepdims=True)
        hit = lane == am
        wj = jnp.sum(jnp.where(hit, scores, 0.0), axis=-1, keepdims=True)
        work = jnp.where(hit, neg, work)
        chosen = jnp.where(hit, 1.0, chosen)
        idx_out = jnp.where(out_lane == j, am, idx_out)
        w_out = jnp.where(out_lane == j, wj, w_out)
        total = total + wj
    idx_ref[...] = idx_out.astype(jnp.int32)
    wts_ref[...] = w_out / total * ROUTED_SCALE
    cnt_ref[0] = jnp.sum(chosen, axis=0, keepdims=True)


def _moe_router(x, n_tok, g, scale, shift, whi, wlo, rb, seg_rows, tm):
    D = x.shape[1]
    E = whi.shape[1]
    last = scale.shape[0] - 1
    row = lambda i: (i, 0)
    one = lambda i: (0, 0)
    mod = lambda i: (jnp.minimum(i * tm // seg_rows, last), 0, 0)
    return pl.pallas_call(
        _moe_router_kernel,
        grid=(n_tok // tm,),
        in_specs=[pl.BlockSpec((tm, D), row), pl.BlockSpec((1, D), one), pl.BlockSpec((1, 1, D), mod),
                  pl.BlockSpec((1, 1, D), mod), pl.BlockSpec((D, E), one), pl.BlockSpec((D, E), one),
                  pl.BlockSpec((1, E), one)],
        out_specs=[pl.BlockSpec((tm, D), row), pl.BlockSpec((tm, LANES), row), pl.BlockSpec((tm, LANES), row),
                   pl.BlockSpec((1, 1, E), lambda i: (i, 0, 0))],
        out_shape=[jax.ShapeDtypeStruct((n_tok, D), F32), jax.ShapeDtypeStruct((n_tok, LANES), jnp.int32),
                   jax.ShapeDtypeStruct((n_tok, LANES), F32), jax.ShapeDtypeStruct((n_tok // tm, 1, E), F32)],
        compiler_params=_cparams(("parallel",)),
        name="moe_router",
    )(x, g, scale, shift, whi, wlo, rb)


def _moe_pos_kernel(idx_ref, base_ref, pos_ref, *, K):
    tm = idx_ref.shape[0]
    E = base_ref.shape[2]
    idx = idx_ref[...]
    lane = lax.broadcasted_iota(jnp.int32, (1, E), 1)
    hits = [lane == idx[:, j:j + 1] for j in range(K)]
    onehot = jnp.zeros((tm, E), F32)
    for hit in hits:
        onehot = jnp.where(hit, 1.0, onehot)
    row = lax.broadcasted_iota(jnp.int32, (tm, tm), 0)
    col = lax.broadcasted_iota(jnp.int32, (tm, tm), 1)
    before = _dot((col < row).astype(BF16), onehot.astype(BF16))
    dest = before + base_ref[0]
    out_lane = lax.broadcasted_iota(jnp.int32, (1, pos_ref.shape[1]), 1)
    out = jnp.zeros(pos_ref.shape, F32)
    for j, hit in enumerate(hits):
        out = jnp.where(out_lane == j, jnp.sum(jnp.where(hit, dest, 0.0), axis=-1, keepdims=True), out)
    pos_ref[...] = out.astype(jnp.int32)


def _moe_plan(idx, cnt, K, bm, tm):
    n_tok = idx.shape[0]
    E = cnt.shape[2]
    cnt = cnt[:, 0, :]
    counts = jnp.sum(cnt, axis=0)
    pcounts = jnp.ceil(counts / bm) * bm
    pends = jnp.cumsum(pcounts)
    base = (pends - pcounts)[None, :] + jnp.cumsum(cnt, axis=0) - cnt
    pos = pl.pallas_call(
        functools.partial(_moe_pos_kernel, K=K),
        grid=(n_tok // tm,),
        in_specs=[pl.BlockSpec((tm, LANES), lambda i: (i, 0)), pl.BlockSpec((1, 1, E), lambda i: (i, 0, 0))],
        out_specs=pl.BlockSpec((tm, LANES), lambda i: (i, 0)),
        out_shape=jax.ShapeDtypeStruct((n_tok, LANES), jnp.int32),
        compiler_params=_cparams(("parallel",)),
        name="moe_positions",
    )(idx, base[:, None, :])
    n_blocks = -(-(n_tok * K + E * (bm - 1)) // bm)
    block_start = jnp.concatenate([jnp.zeros((1,), F32), pends / bm]).astype(jnp.int32)
    return pos[:, :K].reshape(-1), block_start, n_blocks


def _per_token_rows(tm, K, copy):
    def start(t, carry):
        for j in range(K):
            copy(t, j).start(priority=j % 2)
        return carry

    def wait(t, carry):
        for j in range(K):
            copy(t, j).wait()
        return carry

    return (lambda: lax.fori_loop(0, tm, start, 0, unroll=2)), (lambda: lax.fori_loop(0, tm, wait, 0, unroll=2))


def _moe_dispatch_kernel(pos_ref, h_ref, xs_in_ref, xs_ref, sem, *, K):
    del xs_in_ref
    tm = h_ref.shape[0]

    def row_copy(t, j):
        return pltpu.make_async_copy(h_ref.at[pl.ds(t, 1)], xs_ref.at[pl.ds(pos_ref[t * K + j], 1)], sem)

    start, wait = _per_token_rows(tm, K, row_copy)
    start()
    wait()


def _moe_dispatch(pos, h, n_rows, K, tm):
    n_tok, D = h.shape
    xs0 = jnp.zeros((n_rows, D), h.dtype)
    return pl.pallas_call(
        functools.partial(_moe_dispatch_kernel, K=K),
        grid=(n_tok // tm,),
        in_specs=[pl.BlockSpec((tm * K,), lambda i: (i,), memory_space=pltpu.SMEM),
                  pl.BlockSpec((tm, D), lambda i: (i, 0)),
                  pl.BlockSpec(memory_space=pl.ANY)],
        out_specs=pl.BlockSpec(memory_space=pl.ANY),
        out_shape=jax.ShapeDtypeStruct((n_rows, D), h.dtype),
        scratch_shapes=[pltpu.SemaphoreType.DMA(())],
        input_output_aliases={2: 0},
        compiler_params=_cparams(("arbitrary",)),
        name="moe_dispatch",
    )(pos, h, xs0)


def _moe_expert_kernel(bs_ref, x_hbm, w1_ref, w3_ref, w2_ref, y_hbm, xbuf, ybuf, w1b_ref, w3b_ref, w2b_ref,
                       xsem, ysem, *, bm, nbuf, n_blocks):
    e = pl.program_id(0)
    last = pl.num_programs(0) - 1
    b0 = bs_ref[e]
    b1 = bs_ref[e + 1]
    n_used = bs_ref[last + 1]

    def x_copy(g):
        slot = g % nbuf
        return pltpu.make_async_copy(x_hbm.at[pl.ds(g * bm, bm)], xbuf.at[slot], xsem.at[slot])

    def y_copy(g):
        slot = g % nbuf
        return pltpu.make_async_copy(ybuf.at[slot], y_hbm.at[pl.ds(g * bm, bm)], ysem.at[slot])

    @pl.when(e == 0)
    def _():
        for p in range(nbuf - 1):
            @pl.when(p < n_used)
            def _():
                x_copy(p).start()

    @pl.when(b1 > b0)
    def _():
        w1b_ref[...] = w1_ref[0, 0].astype(BF16)
        w3b_ref[...] = w3_ref[0, 0].astype(BF16)
        w2b_ref[...] = w2_ref[0, 0].astype(BF16)

    def block(g, carry):
        slot = g % nbuf
        x_copy(g).wait()

        @pl.when(g + nbuf - 1 < n_used)
        def _():
            x_copy(g + nbuf - 1).start()

        @pl.when(g >= nbuf)
        def _():
            y_copy(g - nbuf).wait()

        x = xbuf[slot].astype(BF16)
        a = _dot(x, w1b_ref[...])
        b = _dot(x, w3b_ref[...])
        ybuf[slot] = _dot((jax.nn.silu(a) * b).astype(BF16), w2b_ref[...])
        y_copy(g).start()
        return carry

    lax.fori_loop(b0, b1, block, 0)

    @pl.when(e == last)
    def _():
        def drain(g, carry):
            y_copy(g).wait()
            return carry

        lax.fori_loop(jnp.maximum(n_used - nbuf, 0), n_used, drain, 0)
        ybuf[0] = jnp.zeros(ybuf.shape[1:], ybuf.dtype)

        def zero_copy(g):
            return pltpu.make_async_copy(ybuf.at[0], y_hbm.at[pl.ds(g * bm, bm)], ysem.at[0])

        def fill(g, carry):
            zero_copy(g).start()
            return carry

        def fill_wait(g, carry):
            zero_copy(g).wait()
            return carry

        lax.fori_loop(n_used, n_blocks, fill, 0)
        lax.fori_loop(n_used, n_blocks, fill_wait, 0)


def _moe_experts(block_start, xs, w1, w3, w2, layer, bm):
    P, D = xs.shape
    E, F = w1.shape[1], w1.shape[3]
    nbuf = 4
    grid_spec = pltpu.PrefetchScalarGridSpec(
        num_scalar_prefetch=1,
        grid=(E,),
        in_specs=[pl.BlockSpec(memory_space=pl.ANY),
                  pl.BlockSpec((1, 1, D, F), lambda e, bs: (layer, e, 0, 0)),
                  pl.BlockSpec((1, 1, D, F), lambda e, bs: (layer, e, 0, 0)),
                  pl.BlockSpec((1, 1, F, D), lambda e, bs: (layer, e, 0, 0))],
        out_specs=pl.BlockSpec(memory_space=pl.ANY),
        scratch_shapes=[pltpu.VMEM((nbuf, bm, D), F32), pltpu.VMEM((nbuf, bm, D), F32),
                        pltpu.VMEM((D, F), BF16), pltpu.VMEM((D, F), BF16), pltpu.VMEM((F, D), BF16),
                        pltpu.SemaphoreType.DMA((nbuf,)), pltpu.SemaphoreType.DMA((nbuf,))],
    )
    return pl.pallas_call(
        functools.partial(_moe_expert_kernel, bm=bm, nbuf=nbuf, n_blocks=P // bm),
        grid_spec=grid_spec,
        out_shape=jax.ShapeDtypeStruct((P, D), F32),
        compiler_params=_cparams(("arbitrary",)),
        name="moe_experts",
    )(block_start, xs, w1, w3, w2)


def _moe_combine_kernel(pos_ref, wts_ref, x_ref, h_ref, sw1_ref, sw3_ref, sw2_ref, gate_ref, ys_ref, o_ref,
                        gath_ref, sem, *, K):
    tm = x_ref.shape[0]

    def row_copy(t, j):
        return pltpu.make_async_copy(ys_ref.at[pl.ds(pos_ref[t * K + j], 1)], gath_ref.at[j, pl.ds(t, 1)], sem)

    start, wait = _per_token_rows(tm, K, row_copy)
    start()
    h = h_ref[...].astype(BF16)
    mid = jax.nn.silu(_dot(h, sw1_ref[...])) * _dot(h, sw3_ref[...])
    acc = _dot(mid.astype(BF16), sw2_ref[...])
    wait()
    wts = wts_ref[...]
    for j in range(K):
        acc = acc + wts[:, j:j + 1] * gath_ref[j]
    o_ref[...] = x_ref[...] + gate_ref[0] * acc


def _moe_combine(pos, wts, x, h, sw1, sw3, sw2, gate, ys, n_tok, K, seg_rows, tm):
    D = x.shape[1]
    last = gate.shape[0] - 1
    row = lambda i: (i, 0)
    one = lambda i: (0, 0)
    return pl.pallas_call(
        functools.partial(_moe_combine_kernel, K=K),
        grid=(n_tok // tm,),
        in_specs=[pl.BlockSpec((tm * K,), lambda i: (i,), memory_space=pltpu.SMEM),
                  pl.BlockSpec((tm, LANES), row), pl.BlockSpec((tm, D), row), pl.BlockSpec((tm, D), row),
                  pl.BlockSpec(sw1.shape, one), pl.BlockSpec(sw3.shape, one), pl.BlockSpec(sw2.shape, one),
                  pl.BlockSpec((1, 1, D), lambda i: (jnp.minimum(i * tm // seg_rows, last), 0, 0)),
                  pl.BlockSpec(memory_space=pl.ANY)],
        out_specs=pl.BlockSpec((tm, D), row),
        out_shape=jax.ShapeDtypeStruct((n_tok, D), F32),
        scratch_shapes=[pltpu.VMEM((K, tm, D), F32), pltpu.SemaphoreType.DMA(())],
        compiler_params=_cparams(("arbitrary",)),
        name="moe_combine",
    )(pos, wts, x, h, sw1, sw3, sw2, gate, ys)


def _moe_ffn_residual(x, n_tok, norm_g, scale, shift, gate, router_w, router_b, ew1, ew3, ew2, layer, sw1, sw3, sw2,
                      seg_rows, bm):
    K = TOP_K
    tm = 256
    whi, wlo = _split2(router_w)
    h, idx, wts, cnt = _moe_router(x, n_tok, norm_g, scale, shift, whi, wlo, router_b[None], seg_rows, tm)
    pos, block_start, n_blocks = _moe_plan(idx, cnt, K, bm, tm)
    xs = _moe_dispatch(pos, h, n_blocks * bm, K, 128)
    ys = _moe_experts(block_start, xs, ew1, ew3, ew2, layer, bm)
    return _moe_combine(pos, wts, x, h, sw1.astype(BF16), sw3.astype(BF16), sw2.astype(BF16), gate, ys,
                        n_tok, K, seg_rows, 128)


def kernel(x, c, ctx, c_ctx, w_mod, b_mod, norm1_g, norm2_g, w_in, mlstm_gate_b, mlstm_norm_g, da_qnorm_g, da_knorm_g, da_lambda, da_subln_g, hy_conv_w, hy_conv_b, hy_w1, hy_b1, hy_w2, hy_b2, hy_w3, hy_freq, hy_skip, w_out, router_w, router_b, exp_w1, exp_w3, exp_w2, sh_w1, sh_w3, sh_w2):
    B, S, D = x.shape
    n_ctx = ctx.shape[1]
    depth = w_in.shape[0]
    n_lat = B * S
    ml_w = mlstm_norm_g.shape[1]
    da_dh = da_qnorm_g.shape[1]
    da_w = DA_HEADS * 2 * da_dh
    hy_w = hy_skip.shape[2]
    n_gates = 4 * ML_HEADS
    ml_col, da_col = 0, 4 * ml_w
    hy_col = da_col + 3 * da_w
    gate_col = hy_col + 3 * hy_w
    tm = 512

    X = jnp.concatenate([x.reshape(n_lat, D), ctx.reshape(B * n_ctx, D)], axis=0)
    sc = jax.nn.silu(jnp.concatenate([c, c_ctx[None]], axis=0))
    cos, sin = _axial_rope_tables(S, da_dh, da_w // da_dh, tm)
    mats_l = _dft_mats(S)
    mats_c = _dft_mats(n_ctx)
    for l in range(depth):
        last = l == depth - 1
        lam_init = 0.8 - 0.6 * math.exp(-0.3 * l)
        mods = (jnp.dot(sc, w_mod[l], precision=lax.Precision.HIGHEST) + b_mod[l]).reshape(B + 1, 6, 1, D)
        sh1, s1, g1, sh2, s2, g2 = [mods[:, i] for i in range(6)]
        wl = w_in[l]
        w_big = jnp.concatenate([wl[:, :4 * ml_w], wl[:, 4 * ml_w + n_gates:], wl[:, 4 * ml_w:4 * ml_w + n_gates],
                                 jnp.zeros((D, LANES - n_gates), F32)], axis=1).astype(BF16)
        U = _norm_mod_matmul(X, norm1_g[l][None], s1, sh1, w_big, S, tm, w_big.shape[1] // 3)
        gb = jnp.concatenate([mlstm_gate_b[l], jnp.zeros((LANES - n_gates,), F32)])[None]
        m_out = _mlstm_mixer(U, gb, mlstm_norm_g[l][None], B, S, n_ctx, ml_col, gate_col)
        d_out = _diff_attn_mixer(U, cos, sin, da_qnorm_g[l], da_knorm_g[l], da_lambda[l], da_subln_g[l], lam_init,
                                 B, S, n_ctx, da_col, not last)
        hy_args = (hy_w1[l], hy_b1[l], hy_w2[l], hy_b2[l], hy_w3[l], hy_freq[l], hy_w)
        y_out = _hyena_seq(U, mats_l, _hyena_spectrum(mats_l, S, *hy_args), hy_conv_w[l], hy_conv_b[l][None],
                           hy_skip[l], B, S, 0, hy_col)
        n_rows = n_lat
        if not last:
            y_ctx = _hyena_seq(U, mats_c, _hyena_spectrum(mats_c, n_ctx, *hy_args), hy_conv_w[l], hy_conv_b[l][None],
                               hy_skip[l], B, n_ctx, n_lat // n_ctx, hy_col)
            y_out = jnp.concatenate([y_out, y_ctx], axis=0)
            n_rows = n_lat + B * n_ctx
        wo = w_out[l].astype(BF16)
        X = _out_proj_residual(m_out, d_out, y_out, wo[:ml_w], wo[ml_w:ml_w + da_w], wo[ml_w + da_w:], X, g1,
                               n_rows, S, tm)
        X = _moe_ffn_residual(X, n_rows, norm2_g[l][None], s2, sh2, g2, router_w[l], router_b[l],
                              exp_w1, exp_w3, exp_w2, l, sh_w1[l], sh_w3[l], sh_w2[l], S,
                              MOE_BLOCK)
    return X[:n_lat].reshape(B, S, D)
```

```python
import functools
import math

import jax
import jax.numpy as jnp
from jax import lax
from jax.experimental import pallas as pl
from jax.experimental.pallas import tpu as pltpu

F32 = jnp.float32
BF16 = jnp.bfloat16

EPS = 1e-6
GRID_W = 64
ROPE_THETA = 10000.0
ML_HEADS = 4
ML_CHUNK = 256
ML_M_INIT = -1e30
DA_HEADS = 4
HY_ORDER = 2
HY_BANDS = 8
HY_SHIFT = 0.05
HY_TARGET = 1e-2
HY_FAST = 0.3
HY_SLOW = 1.5
N_GROUPS = 8
TOPK_GROUPS = 4
TOP_K = 8
ROUTED_SCALE = 2.5
MOE_BLOCK = 128
LANES = 128
VMEM_LIMIT = 56 * 1024 * 1024


def _cparams(sem):
    return pltpu.CompilerParams(dimension_semantics=sem, vmem_limit_bytes=VMEM_LIMIT)


def _dot(a, b):
    return jnp.dot(a, b, preferred_element_type=F32)


def _dot_nt(a, b):
    return lax.dot_general(a, b, (((1,), (1,)), ((), ())), preferred_element_type=F32)


def _dot_tn(a, b):
    return lax.dot_general(a, b, (((0,), (0,)), ((), ())), preferred_element_type=F32)


def _split3(a):
    hi = a.astype(BF16)
    r = a - hi.astype(F32)
    mid = r.astype(BF16)
    lo = (r - mid.astype(F32)).astype(BF16)
    return hi, mid, lo


def _split2(a):
    hi = a.astype(BF16)
    lo = (a - hi.astype(F32)).astype(BF16)
    return hi, lo


def _norm_mod_mm_kernel(x_ref, g_ref, sc_ref, sh_ref, w_ref, wt_ref, o_ref, ot_ref, xn_ref):
    @pl.when(pl.program_id(1) == 0)
    def _():
        x = x_ref[...]
        y = x * lax.rsqrt(jnp.mean(x * x, axis=-1, keepdims=True) + EPS) * g_ref[...]
        xn_ref[...] = (y * (1.0 + sc_ref[0]) + sh_ref[0]).astype(BF16)
        ot_ref[...] = _dot_nt(wt_ref[...], xn_ref[...])

    o_ref[...] = _dot(xn_ref[...], w_ref[...])


def _norm_mod_matmul(x, g, scale, shift, w, wt, seg_rows, tm, tn):
    R, D = x.shape
    N = w.shape[1]
    NT = wt.shape[0]
    last = scale.shape[0] - 1
    mod_map = lambda i, j: (jnp.minimum(i * tm // seg_rows, last), 0, 0)
    return pl.pallas_call(
        _norm_mod_mm_kernel,
        grid=(R // tm, N // tn),
        in_specs=[
            pl.BlockSpec((tm, D), lambda i, j: (i, 0)),
            pl.BlockSpec((1, D), lambda i, j: (0, 0)),
            pl.BlockSpec((1, 1, D), mod_map),
            pl.BlockSpec((1, 1, D), mod_map),
            pl.BlockSpec((D, tn), lambda i, j: (0, j)),
            pl.BlockSpec((NT, D), lambda i, j: (0, 0)),
        ],
        out_specs=[pl.BlockSpec((tm, tn), lambda i, j: (i, j)), pl.BlockSpec((NT, tm), lambda i, j: (0, i))],
        out_shape=[jax.ShapeDtypeStruct((R, N), F32), jax.ShapeDtypeStruct((NT, R), F32)],
        scratch_shapes=[pltpu.VMEM((tm, D), BF16)],
        compiler_params=_cparams(("parallel", "arbitrary")),
        name="norm_mod_matmul",
    )(x, g, scale, shift, w, wt)


def _out_proj_kernel(m_ref, d_ref, y_ref, wm_ref, wd_ref, wy_ref, x_ref, gate_ref, o_ref):
    acc = _dot(m_ref[...], wm_ref[...]) + _dot(d_ref[...], wd_ref[...]) + _dot(y_ref[...], wy_ref[...])
    o_ref[...] = x_ref[...] + gate_ref[0] * acc


def _out_proj_residual(m, d, y, wm, wd, wy, x, gate, n_rows, seg_rows, tm):
    R, D = n_rows, x.shape[1]
    last = gate.shape[0] - 1
    row = lambda i: (i, 0)
    full = lambda i: (0, 0)
    return pl.pallas_call(
        _out_proj_kernel,
        grid=(R // tm,),
        in_specs=[
            pl.BlockSpec((tm, m.shape[1]), row),
            pl.BlockSpec((tm, d.shape[1]), row),
            pl.BlockSpec((tm, y.shape[1]), row),
            pl.BlockSpec(wm.shape, full),
            pl.BlockSpec(wd.shape, full),
            pl.BlockSpec(wy.shape, full),
            pl.BlockSpec((tm, D), row),
            pl.BlockSpec((1, 1, D), lambda i: (jnp.minimum(i * tm // seg_rows, last), 0, 0)),
        ],
        out_specs=pl.BlockSpec((tm, D), row),
        out_shape=jax.ShapeDtypeStruct((R, D), F32),
        compiler_params=_cparams(("parallel",)),
        name="out_proj_residual",
    )(m, d, y, wm, wd, wy, x, gate)


def _log_sigmoid(x):
    return jnp.minimum(x, 0.0) - jnp.log1p(jnp.exp(-jnp.abs(x)))


def _mlstm_gate_tables(g_ref, r0, L, gb, tril, triu):
    g = g_ref[pl.ds(r0, L), :] + gb
    lf = _log_sigmoid(g)
    gT = g.T
    lfT = lf.T
    parts = _split3(lf)
    partsT = _split3(lfT)
    cs_f = sum(_dot(tril, p) for p in parts)
    cs_b = sum(_dot(triu, p) for p in parts)
    rs_f = sum(_dot(p, triu) for p in partsT)
    rs_b = sum(_dot(p, tril) for p in partsT)
    return g, gT, cs_f, cs_b, rs_f, rs_b


def _mlstm_chunk(qT, kT, vT, i_row, b_row, c_col, b_end, mask, state):
    C, n, m = state
    qb = qT.astype(BF16)
    kb = kT.astype(BF16)
    dmat = jnp.where(mask, b_row + c_col, -jnp.inf)
    inter = b_row + m
    m_t = jnp.maximum(inter, jnp.max(dmat, axis=0, keepdims=True))
    s = _dot_tn(kb, qb) * jnp.exp(dmat - m_t)
    carry_w = jnp.exp(inter - m_t)
    num = _dot(vT.astype(BF16), s.astype(BF16)) + carry_w * _dot(C.astype(BF16), qb)
    den = jnp.sum(s, axis=0, keepdims=True) + carry_w * jnp.sum(qT * n, axis=0, keepdims=True)
    h = num / jnp.maximum(jnp.abs(den), jnp.exp(-m_t))
    g = b_end - b_row + i_row
    m_new = jnp.maximum(b_end + m, jnp.max(g, axis=-1, keepdims=True))
    ws = jnp.exp(g - m_new)
    decay = jnp.exp(b_end + m - m_new)
    C_new = decay * C + _dot_nt((vT * ws).astype(BF16), kb)
    n_new = decay * n + jnp.sum(kT * ws, axis=-1, keepdims=True)
    return h, (C_new, n_new, m_new)


def _mlstm_kernel(ql_ref, kl_ref, vl_ref, ol_ref, gl_ref, qc_ref, kc_ref, vc_ref, oc_ref, gc_ref,
                  gb_ref, ng_ref, outl_ref, outc_ref, hf_ref, hb_ref, *, L, H, dh):
    S = ql_ref.shape[1]
    n_ctx = qc_ref.shape[1]
    row = lax.broadcasted_iota(jnp.int32, (L, L), 0)
    col = lax.broadcasted_iota(jnp.int32, (L, L), 1)
    lower = col <= row
    upper = col >= row
    tril = lower.astype(BF16)
    triu = upper.astype(BF16)
    gb = gb_ref[...]
    k_scale = dh ** -0.5

    def both_dirs(refs_f, r0_f, refs_b, r0_b, hoff, state):
        new_state = []
        for d, (refs, r0, h_ref) in enumerate(((refs_f, r0_f, hf_ref), (refs_b, r0_b, hb_ref))):
            q_ref, k_ref, v_ref, g_ref = refs
            g, gT, cs_f, cs_b, rs_f, rs_b = _mlstm_gate_tables(g_ref, r0, L, gb, tril, triu)
            cs, rs, mask = (cs_f, rs_f, upper) if d == 0 else (cs_b, rs_b, lower)
            end = L - 1 if d == 0 else 0
            for hh in range(H):
                ic = 2 * d * H + hh
                fc = ic + H
                rows = slice(hh * dh, (hh + 1) * dh)
                qT = q_ref[rows, pl.ds(r0, L)]
                kT = k_ref[rows, pl.ds(r0, L)] * k_scale
                vT = v_ref[rows, pl.ds(r0, L)]
                h, st = _mlstm_chunk(qT, kT, vT, gT[ic:ic + 1, :], rs[fc:fc + 1, :],
                                     g[:, ic:ic + 1] - cs[:, fc:fc + 1], rs[fc:fc + 1, end:end + 1], mask,
                                     state[d * H + hh])
                off = hoff + r0
                h_ref[rows, pl.ds(off if isinstance(off, int) else pl.multiple_of(off, L), L)] = h
                new_state.append(st)
        return tuple(new_state)

    state = tuple((jnp.zeros((dh, dh), F32), jnp.zeros((dh, 1), F32), jnp.full((1, 1), ML_M_INIT, F32))
                  for _ in range(2 * H))
    ctx_refs = (qc_ref, kc_ref, vc_ref, gc_ref)
    lat_refs = (ql_ref, kl_ref, vl_ref, gl_ref)
    n_cc = n_ctx // L
    for c in range(n_cc):
        state = both_dirs(ctx_refs, c * L, ctx_refs, (n_cc - 1 - c) * L, 0, state)
    n_lc = S // L

    def body(c, st):
        r_f = pl.multiple_of(c * L, L)
        r_b = pl.multiple_of((n_lc - 1 - c) * L, L)
        return both_dirs(lat_refs, r_f, lat_refs, r_b, n_ctx, st)

    lax.fori_loop(0, n_lc, body, state)

    def finish(o_ref, out_ref, hoff, n_rows):
        def fbody(c, carry):
            r0 = pl.multiple_of(c * L, L)
            off = pl.multiple_of(hoff + r0, L)
            hs = hf_ref[:, pl.ds(off, L)] + hb_ref[:, pl.ds(off, L)]
            normed = []
            for hh in range(H):
                rows = slice(hh * dh, (hh + 1) * dh)
                hv = hs[rows]
                normed.append(hv * lax.rsqrt(jnp.mean(hv * hv, axis=0, keepdims=True) + EPS) * ng_ref[rows])
            hn = jnp.concatenate(normed, axis=0).T
            out_ref[pl.ds(r0, L), :] = (jax.nn.sigmoid(o_ref[pl.ds(r0, L), :]) * hn).astype(out_ref.dtype)
            return carry
        lax.fori_loop(0, n_rows // L, fbody, 0)

    finish(ol_ref, outl_ref, n_ctx, S)
    finish(oc_ref, outc_ref, 0, n_ctx)


def _mlstm_mixer(uT, u, gate_b, norm_g, B, S, n_ctx, o_col, gate_col):
    W = norm_g.shape[0]
    H = ML_HEADS
    dh = W // H
    ocb = o_col // W
    gcb = gate_col // LANES
    cblk = (B * S) // n_ctx

    def lat_t(j):
        return pl.BlockSpec((W, S), lambda b: (j, b))

    def ctx_t(j):
        return pl.BlockSpec((W, n_ctx), lambda b: (j, cblk + b))

    one = lambda b: (0, 0)
    out_l, out_c = pl.pallas_call(
        functools.partial(_mlstm_kernel, L=ML_CHUNK, H=H, dh=dh),
        grid=(B,),
        in_specs=[lat_t(0), lat_t(1), lat_t(2), pl.BlockSpec((S, W), lambda b: (b, ocb)),
                  pl.BlockSpec((S, LANES), lambda b: (b, gcb)),
                  ctx_t(0), ctx_t(1), ctx_t(2), pl.BlockSpec((n_ctx, W), lambda b: (cblk + b, ocb)),
                  pl.BlockSpec((n_ctx, LANES), lambda b: (cblk + b, gcb)),
                  pl.BlockSpec((1, LANES), one), pl.BlockSpec((W, 1), one)],
        out_specs=[pl.BlockSpec((S, W), lambda b: (b, 0)), pl.BlockSpec((n_ctx, W), lambda b: (b, 0))],
        out_shape=[jax.ShapeDtypeStruct((B * S, W), BF16), jax.ShapeDtypeStruct((B * n_ctx, W), BF16)],
        scratch_shapes=[pltpu.VMEM((W, n_ctx + S), F32), pltpu.VMEM((W, n_ctx + S), F32)],
        compiler_params=_cparams(("parallel",)),
        name="mlstm",
    )(uT, uT, uT, u, u, uT, uT, uT, u, u, gate_b, norm_g)
    return jnp.concatenate([out_l, out_c], axis=0)


def _da_prep_kernel(q_ref, k_ref, v_ref, cos_ref, sin_ref, qg_ref, kg_ref, seg_ref, qo_ref, ko_ref, vo_ref, *, dh):
    cos = cos_ref[...]
    sin = sin_ref[...]
    seg = seg_ref[...]
    W = q_ref.shape[1]
    lane = lax.broadcasted_iota(jnp.int32, (1, W), 1)
    quarter = dh // 4
    first = (lane % (2 * quarter)) < quarter

    def norm_rope(x, g):
        hi, lo = _split2(x * x)
        ms = (_dot(hi, seg) + _dot(lo, seg)) * (1.0 / dh)
        xn = x * lax.rsqrt(ms + EPS) * g
        rot = jnp.where(first, -pltpu.roll(xn, W - quarter, 1), pltpu.roll(xn, quarter, 1))
        return xn * cos + rot * sin

    qo_ref[...] = (norm_rope(q_ref[...], qg_ref[...]) * (dh ** -0.5)).astype(BF16)
    ko_ref[...] = norm_rope(k_ref[...], kg_ref[...]).astype(BF16)
    vo_ref[...] = v_ref[...].astype(BF16)


def _da_prep(u, cos, sin, qg, kg, seg, n_lat_rows, S, col0, tm, dh):
    R = u.shape[0]
    W = qg.shape[1]
    cb = col0 // W
    n_lat = n_lat_rows // tm
    per_seq = S // tm
    tab = lambda i: (jnp.where(i < n_lat, i % per_seq, per_seq), 0)
    one = lambda i: (0, 0)
    row = lambda i: (i, 0)

    def ucol(j):
        return pl.BlockSpec((tm, W), lambda i: (i, cb + j))

    return pl.pallas_call(
        functools.partial(_da_prep_kernel, dh=dh),
        grid=(R // tm,),
        in_specs=[ucol(0), ucol(1), ucol(2), pl.BlockSpec((tm, W), tab), pl.BlockSpec((tm, W), tab),
                  pl.BlockSpec((1, W), one), pl.BlockSpec((1, W), one), pl.BlockSpec((W, W), one)],
        out_specs=[pl.BlockSpec((tm, W), row)] * 3,
        out_shape=[jax.ShapeDtypeStruct((R, W), BF16)] * 3,
        compiler_params=_cparams(("parallel",)),
        name="da_prep",
    )(u, u, u, cos, sin, qg, kg, seg)


def _da_attn_kernel(*refs, n_kv, dh, lam_init):
    q_ref = refs[0]
    k_refs = refs[1:1 + n_kv]
    v_refs = refs[1 + n_kv:1 + 2 * n_kv]
    lam_ref, sg_ref, o_ref = refs[1 + 2 * n_kv:]
    lp = lam_ref[...]
    lam = (jnp.exp(jnp.sum(lp[0:1] * lp[1:2], axis=-1, keepdims=True))
           - jnp.exp(jnp.sum(lp[2:3] * lp[3:4], axis=-1, keepdims=True)) + lam_init)
    q = q_ref[...]
    acc = None
    probs = []
    for mp in range(2):
        lanes = slice(mp * dh, (mp + 1) * dh)
        s = [_dot_nt(q[:, lanes], k_ref[:, lanes]) for k_ref in k_refs]
        mx = functools.reduce(jnp.maximum, [jnp.max(si, axis=-1, keepdims=True) for si in s])
        p = [jnp.exp(si - mx) for si in s]
        den = sum(jnp.sum(pi, axis=-1, keepdims=True) for pi in p)
        probs.append([pi / den for pi in p])
    for j in range(n_kv):
        a = (probs[0][j] - lam * probs[1][j]).astype(BF16)
        t = _dot(a, v_refs[j][...])
        acc = t if acc is None else acc + t
    o = acc * lax.rsqrt(jnp.mean(acc * acc, axis=-1, keepdims=True) + EPS) * sg_ref[...]
    o_ref[...] = (o * (1.0 - lam_init)).astype(o_ref.dtype)


def _da_attention(q, k, v, lam_p, subln_g, lam_init, B, q_rows, q_blk0, kv_segs, tq, dh):
    H = DA_HEADS
    vd = 2 * dh
    nq = q_rows // tq
    q0 = q_blk0

    def kv_spec(rows, blk0):
        return pl.BlockSpec((rows, vd), lambda b, h, i: (blk0 + b, h))

    kspecs = [kv_spec(r, b0) for r, b0 in kv_segs]
    one = lambda b, h, i: (0, 0)
    return pl.pallas_call(
        functools.partial(_da_attn_kernel, n_kv=len(kv_segs), dh=dh, lam_init=lam_init),
        grid=(B, H, nq),
        in_specs=[pl.BlockSpec((tq, vd), lambda b, h, i: (q0 + b * nq + i, h))] + kspecs + kspecs
                 + [pl.BlockSpec(lam_p.shape, one), pl.BlockSpec((1, vd), one)],
        out_specs=pl.BlockSpec((tq, vd), lambda b, h, i: (b * nq + i, h)),
        out_shape=jax.ShapeDtypeStruct((B * q_rows, H * vd), BF16),
        compiler_params=_cparams(("parallel", "parallel", "arbitrary")),
        name="da_attention",
    )(q, *([k] * len(kv_segs)), *([v] * len(kv_segs)), lam_p, subln_g)


def _axial_rope_tables(S, dh, reps, pad_rows):
    rows = S // GRID_W
    r = jnp.repeat(jnp.arange(rows, dtype=F32), GRID_W)
    col = jnp.tile(jnp.arange(GRID_W, dtype=F32), rows)
    n_freq = dh // 4
    inv = ROPE_THETA ** (-jnp.arange(n_freq, dtype=F32) / n_freq)
    ar = r[:, None] * inv
    ac = col[:, None] * inv
    ang = jnp.concatenate([ar, ar, ac, ac], axis=-1)
    cos = jnp.concatenate([jnp.tile(jnp.cos(ang), (1, reps)), jnp.ones((pad_rows, dh * reps), F32)], axis=0)
    sin = jnp.concatenate([jnp.tile(jnp.sin(ang), (1, reps)), jnp.zeros((pad_rows, dh * reps), F32)], axis=0)
    return cos, sin


def _diff_attn_mixer(u, cos, sin, qg, kg, lam_p, subln_g, lam_init, B, S, n_ctx, col0, need_ctx):
    dh = qg.shape[0]
    W = DA_HEADS * 2 * dh
    seg = (jnp.arange(W)[:, None] // dh == jnp.arange(W)[None, :] // dh).astype(BF16)
    tm = 512
    q, k, v = _da_prep(u, cos, sin, jnp.tile(qg, W // dh)[None], jnp.tile(kg, W // dh)[None], seg,
                       B * S, S, col0, tm, dh)
    sg = subln_g[None]
    ctx_blk0 = (B * S) // n_ctx
    tq = 256
    out_l = _da_attention(q, k, v, lam_p, sg, lam_init, B, S, 0, [(n_ctx, ctx_blk0), (S, 0)], tq, dh)
    if not need_ctx:
        return out_l
    out_c = _da_attention(q, k, v, lam_p, sg, lam_init, B, n_ctx, (B * S) // n_ctx, [(n_ctx, ctx_blk0)], n_ctx, dh)
    return jnp.concatenate([out_l, out_c], axis=0)


def _hy_conv_kernel(v_ref, x1_ref, x2_ref, w_ref, b_ref, vo_ref, x1o_ref, x2o_ref):
    L, W = v_ref.shape
    row = lax.broadcasted_iota(jnp.int32, (L, 1), 0)
    for j, (i_ref, o_ref) in enumerate(((v_ref, vo_ref), (x1_ref, x1o_ref), (x2_ref, x2o_ref))):
        lanes = slice(j * W, (j + 1) * W)
        u = i_ref[...]
        prev = jnp.where(row == 0, 0.0, pltpu.roll(u, 1, 0))
        nxt = jnp.where(row == L - 1, 0.0, pltpu.roll(u, L - 1, 0))
        o_ref[...] = prev * w_ref[0:1, lanes] + u * w_ref[1:2, lanes] + nxt * w_ref[2:3, lanes] + b_ref[:, lanes]


def _hy_short_conv(u, conv_w, conv_b, n_seg, L, blk0, col0):
    W = conv_w.shape[1] // 3
    cb = col0 // W
    one = lambda b: (0, 0)

    def ucol(j):
        return pl.BlockSpec((L, W), lambda b: (blk0 + b, cb + j))

    return pl.pallas_call(
        _hy_conv_kernel,
        grid=(n_seg,),
        in_specs=[ucol(0), ucol(1), ucol(2), pl.BlockSpec(conv_w.shape, one), pl.BlockSpec(conv_b.shape, one)],
        out_specs=[pl.BlockSpec((L, W), lambda b: (b, 0))] * 3,
        out_shape=[jax.ShapeDtypeStruct((n_seg * L, W), F32)] * 3,
        compiler_params=_cparams(("parallel",)),
        name="hy_short_conv",
    )(u, u, u, conv_w, conv_b)


def _hy_fwd_kernel(c_ref, s_ref, z_ref, *rest, raw):
    z = z_ref[...].astype(BF16)
    zr = _dot(c_ref[...], z)
    zi = _dot(s_ref[...], z)
    if raw:
        yr_ref, yi_ref = rest
        yr_ref[...] = zr
        yi_ref[...] = zi
    else:
        a_ref, b_ref, d_ref, yr_ref, yi_ref = rest
        yr_ref[...] = (zr * a_ref[...] - zi * b_ref[...]).astype(yr_ref.dtype)
        yi_ref[...] = (zr * b_ref[...] + zi * d_ref[...]).astype(yi_ref.dtype)


def _hy_fwd(cm, sm, z, coefs, n_seg, L, tk):
    W = z.shape[1]
    nk = L // tk
    raw = coefs is None
    mat = pl.BlockSpec((tk, L), lambda i, b: (i, 0))
    cf = pl.BlockSpec((tk, W), lambda i, b: (i, 0))
    out = pl.BlockSpec((tk, W), lambda i, b: (b * nk + i, 0))
    odt = F32 if raw else BF16
    return pl.pallas_call(
        functools.partial(_hy_fwd_kernel, raw=raw),
        grid=(nk, n_seg),
        in_specs=[mat, mat, pl.BlockSpec((L, W), lambda i, b: (b, 0))] + ([] if raw else [cf, cf, cf]),
        out_specs=[out, out],
        out_shape=[jax.ShapeDtypeStruct((n_seg * L, W), odt)] * 2,
        compiler_params=_cparams(("parallel", "arbitrary")),
        name="hy_dft_fwd",
    )(cm, sm, z, *(() if raw else coefs))


def _hy_inv_kernel(c_ref, st_ref, yr_ref, yi_ref, x_ref, vz_ref, skip_ref, o_ref):
    y = _dot(c_ref[...], yr_ref[...]) + _dot(st_ref[...], yi_ref[...])
    o_ref[...] = (x_ref[...] * (y + skip_ref[...] * vz_ref[...])).astype(o_ref.dtype)


def _hy_inv(cm, smt, yr, yi, xg, vz, skip, n_seg, L, tt, out_dtype):
    W = yr.shape[1]
    nt = L // tt
    mat = pl.BlockSpec((tt, L), lambda i, b: (i, 0))
    seq = pl.BlockSpec((L, W), lambda i, b: (b, 0))
    row = pl.BlockSpec((tt, W), lambda i, b: (b * nt + i, 0))
    return pl.pallas_call(
        _hy_inv_kernel,
        grid=(nt, n_seg),
        in_specs=[mat, mat, seq, seq, row, row, pl.BlockSpec((1, W), lambda i, b: (0, 0))],
        out_specs=row,
        out_shape=jax.ShapeDtypeStruct((n_seg * L, W), out_dtype),
        compiler_params=_cparams(("parallel", "arbitrary")),
        name="hy_dft_inv",
    )(cm, smt, yr, yi, xg, vz, skip)


def _dft_mats(L):
    k = jnp.arange(L, dtype=jnp.int32)
    kn = (k[:, None] * k[None, :]) % (2 * L)
    ang = kn.astype(F32) * (math.pi / L)
    cm = jnp.cos(ang)
    sm = -jnp.sin(ang)
    sm = sm.at[0].set(jnp.where(k % 2 == 0, 1.0, -1.0))
    return cm.astype(BF16), sm.astype(BF16), sm.T.astype(BF16)


def _hyena_filters(L, w1, b1, w2, b2, w3, freq, W):
    t01 = jnp.linspace(0.0, 1.0, L, dtype=F32)[:, None]
    wpos = (2.0 * math.pi / L) * jnp.arange(L, dtype=F32)[:, None]
    bands = jnp.linspace(1e-4, HY_BANDS - 1, HY_BANDS, dtype=F32)
    feats = jnp.concatenate([t01, jnp.cos(wpos * bands), -jnp.sin(wpos * bands)], axis=-1)
    hp = lax.Precision.HIGHEST
    h = jnp.sin(freq[0] * (jnp.dot(feats, w1, precision=hp) + b1))
    h = jnp.sin(freq[1] * (jnp.dot(h, w2, precision=hp) + b2))
    h = jnp.dot(h, w3, precision=hp).reshape(L, HY_ORDER, 2, W)
    deltas = jnp.abs(jnp.linspace(math.log(HY_TARGET) / HY_SLOW, math.log(HY_TARGET) / HY_FAST, W, dtype=F32))
    h = h * (jnp.exp(-t01 * deltas) + HY_SHIFT)[:, None, None, :]
    hf, hb = h[:, :, 0], h[:, :, 1]
    hf = hf.at[0].add(hb[0])
    hb = hb.at[0].set(0.0)
    scale = lax.rsqrt(jnp.sum(hf * hf, axis=0, keepdims=True) + jnp.sum(hb * hb, axis=0, keepdims=True) + EPS)
    return (hf * scale).reshape(L, HY_ORDER * W), (hb * scale).reshape(L, HY_ORDER * W)


def _hyena_spectrum(mats, L, w1, b1, w2, b2, w3, freq, W):
    cm, sm, _ = mats
    hf, hb = _hyena_filters(L, w1, b1, w2, b2, w3, freq, W)
    cols = jnp.concatenate([hf[:, :W], hf[:, W:], hb[:, :W], hb[:, W:]], axis=0)
    tk = min(L, 512)
    gr, gi = _hy_fwd(cm, sm, cols, None, 2 * HY_ORDER, L, tk)
    gr = gr.reshape(2, HY_ORDER, L, W)
    gi = gi.reshape(2, HY_ORDER, L, W)
    kr = gr[0] + gr[1]
    ki = gi[0] - gi[1]
    nyq = gi[0, :, 0] + gi[1, :, 0]
    n = 2.0 * L
    wk = jnp.full((L, 1), 2.0 / n, F32).at[0].set(1.0 / n)
    a = kr * wk
    bm = (ki * wk).at[:, 0].set(0.0)
    dd = a.at[:, 0].set(nyq / n)
    return [(a[o], bm[o], dd[o]) for o in range(HY_ORDER)]


def _hyena_seq(u, mats, spec, conv_w, conv_b, skip, n_seg, L, blk0, col0):
    cm, sm, smt = mats
    t = min(L, 512)
    v, x1, x2 = _hy_short_conv(u, conv_w, conv_b, n_seg, L, blk0, col0)
    yr, yi = _hy_fwd(cm, sm, v, spec[0], n_seg, L, t)
    z = _hy_inv(cm, smt, yr, yi, x1, v, skip[0:1], n_seg, L, t, F32)
    yr, yi = _hy_fwd(cm, sm, z, spec[1], n_seg, L, t)
    return _hy_inv(cm, smt, yr, yi, x2, z, skip[1:2], n_seg, L, t, BF16)


def _moe_router_kernel(x_ref, g_ref, sc_ref, sh_ref, whi_ref, wlo_ref, rb_ref, h_ref, idx_ref, wts_ref, cnt_ref):
    x = x_ref[...]
    y = x * lax.rsqrt(jnp.mean(x * x, axis=-1, keepdims=True) + EPS) * g_ref[...]
    h = y * (1.0 + sc_ref[0]) + sh_ref[0]
    h_ref[...] = h
    hi, lo = _split2(h)
    logits = _dot(hi, whi_ref[...]) + _dot(hi, wlo_ref[...]) + _dot(lo, whi_ref[...])
    scores = jax.nn.sigmoid(logits)
    sel = scores + rb_ref[...]
    tm, E = sel.shape
    gsz = E // N_GROUPS
    neg = -jnp.inf
    lane = lax.broadcasted_iota(jnp.int32, (1, E), 1).astype(F32)
    glane = lax.broadcasted_iota(jnp.int32, (1, gsz), 1).astype(F32)
    gscore = []
    for g in range(N_GROUPS):
        blk = sel[:, g * gsz:(g + 1) * gsz]
        m1 = jnp.max(blk, axis=-1, keepdims=True)
        first = jnp.min(jnp.where(blk == m1, glane, float(gsz)), axis=-1, keepdims=True)
        m2 = jnp.max(jnp.where(glane == first, neg, blk), axis=-1, keepdims=True)
        gscore.append(m1 + m2)
    group_of_lane = lax.broadcasted_iota(jnp.int32, (1, E), 1) // gsz
    keep = jnp.zeros((tm, E), F32)
    for g in range(N_GROUPS):
        rank = jnp.zeros((tm, 1), F32)
        for o in range(N_GROUPS):
            if o != g:
                ahead = (gscore[o] >= gscore[g]) if o < g else (gscore[o] > gscore[g])
                rank = rank + jnp.where(ahead, 1.0, 0.0)
        keep = jnp.where(group_of_lane == g, jnp.where(rank < TOPK_GROUPS, 1.0, 0.0), keep)
    work = jnp.where(keep > 0.0, sel, neg)
    out_lane = lax.broadcasted_iota(jnp.int32, (1, idx_ref.shape[1]), 1)
    idx_out = jnp.zeros(idx_ref.shape, F32)
    w_out = jnp.zeros(wts_ref.shape, F32)
    total = jnp.zeros((tm, 1), F32)
    chosen = jnp.zeros((tm, E), F32)
    for j in range(TOP_K):
        mx = jnp.max(work, axis=-1, keepdims=True)
        am = jnp.min(jnp.where(work == mx, lane, float(E)), axis=-1, keepdims=True)
        hit = lane == am
        wj = jnp.sum(jnp.where(hit, scores, 0.0), axis=-1, keepdims=True)
        work = jnp.where(hit, neg, work)
        chosen = jnp.where(hit, 1.0, chosen)
        idx_out = jnp.where(out_lane == j, am, idx_out)
        w_out = jnp.where(out_lane == j, wj, w_out)
        total = total + wj
    idx_ref[...] = idx_out.astype(jnp.int32)
    wts_ref[...] = w_out / total * ROUTED_SCALE
    cnt_ref[0] = jnp.sum(chosen, axis=0, keepdims=True)


def _moe_router(x, n_tok, g, scale, shift, whi, wlo, rb, seg_rows, tm):
    D = x.shape[1]
    E = whi.shape[1]
    last = scale.shape[0] - 1
    row = lambda i: (i, 0)
    one = lambda i: (0, 0)
    mod = lambda i: (jnp.minimum(i * tm // seg_rows, last), 0, 0)
    return pl.pallas_call(
        _moe_router_kernel,
        grid=(n_tok // tm,),
        in_specs=[pl.BlockSpec((tm, D), row), pl.BlockSpec((1, D), one), pl.BlockSpec((1, 1, D), mod),
                  pl.BlockSpec((1, 1, D), mod), pl.BlockSpec((D, E), one), pl.BlockSpec((D, E), one),
                  pl.BlockSpec((1, E), one)],
        out_specs=[pl.BlockSpec((tm, D), row), pl.BlockSpec((tm, LANES), row), pl.BlockSpec((tm, LANES), row),
                   pl.BlockSpec((1, 1, E), lambda i: (i, 0, 0))],
        out_shape=[jax.ShapeDtypeStruct((n_tok, D), F32), jax.ShapeDtypeStruct((n_tok, LANES), jnp.int32),
                   jax.ShapeDtypeStruct((n_tok, LANES), F32), jax.ShapeDtypeStruct((n_tok // tm, 1, E), F32)],
        compiler_params=_cparams(("parallel",)),
        name="moe_router",
    )(x, g, scale, shift, whi, wlo, rb)


def _moe_pos_kernel(idx_ref, base_ref, pos_ref, *, K):
    tm = idx_ref.shape[0]
    E = base_ref.shape[2]
    idx = idx_ref[...]
    lane = lax.broadcasted_iota(jnp.int32, (1, E), 1)
    hits = [lane == idx[:, j:j + 1] for j in range(K)]
    onehot = jnp.zeros((tm, E), F32)
    for hit in hits:
        onehot = jnp.where(hit, 1.0, onehot)
    row = lax.broadcasted_iota(jnp.int32, (tm, tm), 0)
    col = lax.broadcasted_iota(jnp.int32, (tm, tm), 1)
    before = _dot((col < row).astype(BF16), onehot.astype(BF16))
    dest = before + base_ref[0]
    out_lane = lax.broadcasted_iota(jnp.int32, (1, pos_ref.shape[1]), 1)
    out = jnp.zeros(pos_ref.shape, F32)
    for j, hit in enumerate(hits):
        out = jnp.where(out_lane == j, jnp.sum(jnp.where(hit, dest, 0.0), axis=-1, keepdims=True), out)
    pos_ref[...] = out.astype(jnp.int32)


def _moe_plan(idx, cnt, K, bm, tm):
    n_tok = idx.shape[0]
    E = cnt.shape[2]
    cnt = cnt[:, 0, :]
    counts = jnp.sum(cnt, axis=0)
    pcounts = jnp.ceil(counts / bm) * bm
    pends = jnp.cumsum(pcounts)
    base = (pends - pcounts)[None, :] + jnp.cumsum(cnt, axis=0) - cnt
    pos = pl.pallas_call(
        functools.partial(_moe_pos_kernel, K=K),
        grid=(n_tok // tm,),
        in_specs=[pl.BlockSpec((tm, LANES), lambda i: (i, 0)), pl.BlockSpec((1, 1, E), lambda i: (i, 0, 0))],
        out_specs=pl.BlockSpec((tm, LANES), lambda i: (i, 0)),
        out_shape=jax.ShapeDtypeStruct((n_tok, LANES), jnp.int32),
        compiler_params=_cparams(("parallel",)),
        name="moe_positions",
    )(idx, base[:, None, :])
    n_blocks = -(-(n_tok * K + E * (bm - 1)) // bm)
    block_start = jnp.concatenate([jnp.zeros((1,), F32), pends / bm]).astype(jnp.int32)
    return pos[:, :K].reshape(-1), block_start, n_blocks


def _per_token_rows(tm, K, copy):
    def start(t, carry):
        for j in range(K):
            copy(t, j).start(priority=j % 2)
        return carry

    def wait(t, carry):
        for j in range(K):
            copy(t, j).wait()
        return carry

    return (lambda: lax.fori_loop(0, tm, start, 0, unroll=2)), (lambda: lax.fori_loop(0, tm, wait, 0, unroll=2))


def _moe_dispatch_kernel(pos_ref, h_ref, xs_in_ref, xs_ref, sem, *, K):
    del xs_in_ref
    tm = h_ref.shape[0]

    def row_copy(t, j):
        return pltpu.make_async_copy(h_ref.at[pl.ds(t, 1)], xs_ref.at[pl.ds(pos_ref[t * K + j], 1)], sem)

    start, wait = _per_token_rows(tm, K, row_copy)
    start()
    wait()


def _moe_dispatch(pos, h, n_rows, K, tm):
    n_tok, D = h.shape
    xs0 = jnp.zeros((n_rows, D), h.dtype)
    return pl.pallas_call(
        functools.partial(_moe_dispatch_kernel, K=K),
        grid=(n_tok // tm,),
        in_specs=[pl.BlockSpec((tm * K,), lambda i: (i,), memory_space=pltpu.SMEM),
                  pl.BlockSpec((tm, D), lambda i: (i, 0)),
                  pl.BlockSpec(memory_space=pl.ANY)],
        out_specs=pl.BlockSpec(memory_space=pl.ANY),
        out_shape=jax.ShapeDtypeStruct((n_rows, D), h.dtype),
        scratch_shapes=[pltpu.SemaphoreType.DMA(())],
        input_output_aliases={2: 0},
        compiler_params=_cparams(("arbitrary",)),
        name="moe_dispatch",
    )(pos, h, xs0)


def _moe_expert_kernel(bs_ref, x_hbm, w1_ref, w3_ref, w2_ref, y_hbm, xbuf, ybuf, w1b_ref, w3b_ref, w2b_ref,
                       xsem, ysem, *, bm, nbuf, n_blocks):
    e = pl.program_id(0)
    last = pl.num_programs(0) - 1
    b0 = bs_ref[e]
    b1 = bs_ref[e + 1]
    n_used = bs_ref[last + 1]

    def x_copy(g):
        slot = g % nbuf
        return pltpu.make_async_copy(x_hbm.at[pl.ds(g * bm, bm)], xbuf.at[slot], xsem.at[slot])

    def y_copy(g):
        slot = g % nbuf
        return pltpu.make_async_copy(ybuf.at[slot], y_hbm.at[pl.ds(g * bm, bm)], ysem.at[slot])

    @pl.when(e == 0)
    def _():
        for p in range(nbuf - 1):
            @pl.when(p < n_used)
            def _():
                x_copy(p).start()

    @pl.when(b1 > b0)
    def _():
        w1b_ref[...] = w1_ref[0, 0].astype(BF16)
        w3b_ref[...] = w3_ref[0, 0].astype(BF16)
        w2b_ref[...] = w2_ref[0, 0].astype(BF16)

    def block(g, carry):
        slot = g % nbuf
        x_copy(g).wait()

        @pl.when(g + nbuf - 1 < n_used)
        def _():
            x_copy(g + nbuf - 1).start()

        @pl.when(g >= nbuf)
        def _():
            y_copy(g - nbuf).wait()

        x = xbuf[slot].astype(BF16)
        a = _dot(x, w1b_ref[...])
        b = _dot(x, w3b_ref[...])
        ybuf[slot] = _dot((jax.nn.silu(a) * b).astype(BF16), w2b_ref[...])
        y_copy(g).start()
        return carry

    lax.fori_loop(b0, b1, block, 0)

    @pl.when(e == last)
    def _():
        def drain(g, carry):
            y_copy(g).wait()
            return carry

        lax.fori_loop(jnp.maximum(n_used - nbuf, 0), n_used, drain, 0)
        ybuf[0] = jnp.zeros(ybuf.shape[1:], ybuf.dtype)

        def zero_copy(g):
            return pltpu.make_async_copy(ybuf.at[0], y_hbm.at[pl.ds(g * bm, bm)], ysem.at[0])

        def fill(g, carry):
            zero_copy(g).start()
            return carry

        def fill_wait(g, carry):
            zero_copy(g).wait()
            return carry

        lax.fori_loop(n_used, n_blocks, fill, 0)
        lax.fori_loop(n_used, n_blocks, fill_wait, 0)


def _moe_experts(block_start, xs, w1, w3, w2, layer, bm):
    P, D = xs.shape
    E, F = w1.shape[1], w1.shape[3]
    nbuf = 4
    grid_spec = pltpu.PrefetchScalarGridSpec(
        num_scalar_prefetch=1,
        grid=(E,),
        in_specs=[pl.BlockSpec(memory_space=pl.ANY),
                  pl.BlockSpec((1, 1, D, F), lambda e, bs: (layer, e, 0, 0)),
                  pl.BlockSpec((1, 1, D, F), lambda e, bs: (layer, e, 0, 0)),
                  pl.BlockSpec((1, 1, F, D), lambda e, bs: (layer, e, 0, 0))],
        out_specs=pl.BlockSpec(memory_space=pl.ANY),
        scratch_shapes=[pltpu.VMEM((nbuf, bm, D), F32), pltpu.VMEM((nbuf, bm, D), F32),
                        pltpu.VMEM((D, F), BF16), pltpu.VMEM((D, F), BF16), pltpu.VMEM((F, D), BF16),
                        pltpu.SemaphoreType.DMA((nbuf,)), pltpu.SemaphoreType.DMA((nbuf,))],
    )
    return pl.pallas_call(
        functools.partial(_moe_expert_kernel, bm=bm, nbuf=nbuf, n_blocks=P // bm),
        grid_spec=grid_spec,
        out_shape=jax.ShapeDtypeStruct((P, D), F32),
        compiler_params=_cparams(("arbitrary",)),
        name="moe_experts",
    )(block_start, xs, w1, w3, w2)


def _moe_combine_kernel(pos_ref, wts_ref, x_ref, h_ref, sw1_ref, sw3_ref, sw2_ref, gate_ref, ys_ref, o_ref,
                        gath_ref, sem, *, K):
    tm = x_ref.shape[0]

    def row_copy(t, j):
        return pltpu.make_async_copy(ys_ref.at[pl.ds(pos_ref[t * K + j], 1)], gath_ref.at[j, pl.ds(t, 1)], sem)

    start, wait = _per_token_rows(tm, K, row_copy)
    start()
    h = h_ref[...].astype(BF16)
    mid = jax.nn.silu(_dot(h, sw1_ref[...])) * _dot(h, sw3_ref[...])
    acc = _dot(mid.astype(BF16), sw2_ref[...])
    wait()
    wts = wts_ref[...]
    for j in range(K):
        acc = acc + wts[:, j:j + 1] * gath_ref[j]
    o_ref[...] = x_ref[...] + gate_ref[0] * acc


def _moe_combine(pos, wts, x, h, sw1, sw3, sw2, gate, ys, n_tok, K, seg_rows, tm):
    D = x.shape[1]
    last = gate.shape[0] - 1
    row = lambda i: (i, 0)
    one = lambda i: (0, 0)
    return pl.pallas_call(
        functools.partial(_moe_combine_kernel, K=K),
        grid=(n_tok // tm,),
        in_specs=[pl.BlockSpec((tm * K,), lambda i: (i,), memory_space=pltpu.SMEM),
                  pl.BlockSpec((tm, LANES), row), pl.BlockSpec((tm, D), row), pl.BlockSpec((tm, D), row),
                  pl.BlockSpec(sw1.shape, one), pl.BlockSpec(sw3.shape, one), pl.BlockSpec(sw2.shape, one),
                  pl.BlockSpec((1, 1, D), lambda i: (jnp.minimum(i * tm // seg_rows, last), 0, 0)),
                  pl.BlockSpec(memory_space=pl.ANY)],
        out_specs=pl.BlockSpec((tm, D), row),
        out_shape=jax.ShapeDtypeStruct((n_tok, D), F32),
        scratch_shapes=[pltpu.VMEM((K, tm, D), F32), pltpu.SemaphoreType.DMA(())],
        compiler_params=_cparams(("arbitrary",)),
        name="moe_combine",
    )(pos, wts, x, h, sw1, sw3, sw2, gate, ys)


def _moe_ffn_residual(x, n_tok, norm_g, scale, shift, gate, router_w, router_b, ew1, ew3, ew2, layer, sw1, sw3, sw2,
                      seg_rows, bm):
    K = TOP_K
    tm = 256
    whi, wlo = _split2(router_w)
    h, idx, wts, cnt = _moe_router(x, n_tok, norm_g, scale, shift, whi, wlo, router_b[None], seg_rows, tm)
    pos, block_start, n_blocks = _moe_plan(idx, cnt, K, bm, tm)
    xs = _moe_dispatch(pos, h, n_blocks * bm, K, 128)
    ys = _moe_experts(block_start, xs, ew1, ew3, ew2, layer, bm)
    return _moe_combine(pos, wts, x, h, sw1.astype(BF16), sw3.astype(BF16), sw2.astype(BF16), gate, ys,
                        n_tok, K, seg_rows, 128)


def kernel(x, c, ctx, c_ctx, w_mod, b_mod, norm1_g, norm2_g, w_in, mlstm_gate_b, mlstm_norm_g, da_qnorm_g, da_knorm_g, da_lambda, da_subln_g, hy_conv_w, hy_conv_b, hy_w1, hy_b1, hy_w2, hy_b2, hy_w3, hy_freq, hy_skip, w_out, router_w, router_b, exp_w1, exp_w3, exp_w2, sh_w1, sh_w3, sh_w2):
    B, S, D = x.shape
    n_ctx = ctx.shape[1]
    depth = w_in.shape[0]
    n_lat = B * S
    ml_w = mlstm_norm_g.shape[1]
    da_dh = da_qnorm_g.shape[1]
    da_w = DA_HEADS * 2 * da_dh
    hy_w = hy_skip.shape[2]
    n_gates = 4 * ML_HEADS
    da_col = 0
    hy_col = da_col + 3 * da_w
    o_col = hy_col + 3 * hy_w
    gate_col = o_col + ml_w
    ml_in = 4 * ml_w + n_gates
    tm = 512

    X = jnp.concatenate([x.reshape(n_lat, D), ctx.reshape(B * n_ctx, D)], axis=0)
    sc = jax.nn.silu(jnp.concatenate([c, c_ctx[None]], axis=0))
    cos, sin = _axial_rope_tables(S, da_dh, da_w // da_dh, tm)
    mats_l = _dft_mats(S)
    mats_c = _dft_mats(n_ctx)
    for l in range(depth):
        last = l == depth - 1
        lam_init = 0.8 - 0.6 * math.exp(-0.3 * l)
        mods = (jnp.dot(sc, w_mod[l], precision=lax.Precision.HIGHEST) + b_mod[l]).reshape(B + 1, 6, 1, D)
        sh1, s1, g1, sh2, s2, g2 = [mods[:, i] for i in range(6)]
        wl = w_in[l]
        w_big = jnp.concatenate([wl[:, ml_in:], wl[:, 3 * ml_w:4 * ml_w], wl[:, 4 * ml_w:ml_in],
                                 jnp.zeros((D, LANES - n_gates), F32)], axis=1).astype(BF16)
        w_qkv_t = wl[:, :3 * ml_w].T.astype(BF16)
        U, UT = _norm_mod_matmul(X, norm1_g[l][None], s1, sh1, w_big, w_qkv_t, S, tm, w_big.shape[1] // 3)
        gb = jnp.concatenate([mlstm_gate_b[l], jnp.zeros((LANES - n_gates,), F32)])[None]
        m_out = _mlstm_mixer(UT, U, gb, mlstm_norm_g[l][:, None], B, S, n_ctx, o_col, gate_col)
        d_out = _diff_attn_mixer(U, cos, sin, da_qnorm_g[l], da_knorm_g[l], da_lambda[l], da_subln_g[l], lam_init,
                                 B, S, n_ctx, da_col, not last)
        hy_args = (hy_w1[l], hy_b1[l], hy_w2[l], hy_b2[l], hy_w3[l], hy_freq[l], hy_w)
        y_out = _hyena_seq(U, mats_l, _hyena_spectrum(mats_l, S, *hy_args), hy_conv_w[l], hy_conv_b[l][None],
                           hy_skip[l], B, S, 0, hy_col)
        n_rows = n_lat
        if not last:
            y_ctx = _hyena_seq(U, mats_c, _hyena_spectrum(mats_c, n_ctx, *hy_args), hy_conv_w[l], hy_conv_b[l][None],
                               hy_skip[l], B, n_ctx, n_lat // n_ctx, hy_col)
            y_out = jnp.concatenate([y_out, y_ctx], axis=0)
            n_rows = n_lat + B * n_ctx
        wo = w_out[l].astype(BF16)
        X = _out_proj_residual(m_out, d_out, y_out, wo[:ml_w], wo[ml_w:ml_w + da_w], wo[ml_w + da_w:], X, g1,
                               n_rows, S, tm)
        X = _moe_ffn_residual(X, n_rows, norm2_g[l][None], s2, sh2, g2, router_w[l], router_b[l],
                              exp_w1, exp_w3, exp_w2, l, sh_w1[l], sh_w3[l], sh_w2[l], S,
                              MOE_BLOCK)
    return X[:n_lat].reshape(B, S, D)
```

```python
import functools
import math

import jax
import jax.numpy as jnp
from jax import lax
from jax.experimental import pallas as pl
from jax.experimental.pallas import tpu as pltpu

F32 = jnp.float32
BF16 = jnp.bfloat16

EPS = 1e-6
GRID_W = 64
ROPE_THETA = 10000.0
ML_HEADS = 4
ML_CHUNK = 256
ML_M_INIT = -1e30
DA_HEADS = 4
HY_ORDER = 2
HY_BANDS = 8
HY_SHIFT = 0.05
HY_TARGET = 1e-2
HY_FAST = 0.3
HY_SLOW = 1.5
N_GROUPS = 8
TOPK_GROUPS = 4
TOP_K = 8
ROUTED_SCALE = 2.5
MOE_BLOCK = 128
LANES = 128
VMEM_LIMIT = 56 * 1024 * 1024


def _cparams(sem):
    return pltpu.CompilerParams(dimension_semantics=sem, vmem_limit_bytes=VMEM_LIMIT)


def _dot(a, b):
    return jnp.dot(a, b, preferred_element_type=F32)


def _dot_nt(a, b):
    return lax.dot_general(a, b, (((1,), (1,)), ((), ())), preferred_element_type=F32)


def _dot_tn(a, b):
    return lax.dot_general(a, b, (((0,), (0,)), ((), ())), preferred_element_type=F32)


def _split3(a):
    hi = a.astype(BF16)
    r = a - hi.astype(F32)
    mid = r.astype(BF16)
    lo = (r - mid.astype(F32)).astype(BF16)
    return hi, mid, lo


def _split2(a):
    hi = a.astype(BF16)
    lo = (a - hi.astype(F32)).astype(BF16)
    return hi, lo


def _norm_mod_mm_kernel(x_ref, g_ref, sc_ref, sh_ref, w_ref, wt_ref, o_ref, ot_ref, xn_ref):
    @pl.when(pl.program_id(1) == 0)
    def _():
        x = x_ref[...]
        y = x * lax.rsqrt(jnp.mean(x * x, axis=-1, keepdims=True) + EPS) * g_ref[...]
        xn_ref[...] = (y * (1.0 + sc_ref[0]) + sh_ref[0]).astype(BF16)
        ot_ref[...] = _dot_nt(wt_ref[...], xn_ref[...])

    o_ref[...] = _dot(xn_ref[...], w_ref[...])


def _norm_mod_matmul(x, g, scale, shift, w, wt, seg_rows, tm, tn):
    R, D = x.shape
    N = w.shape[1]
    NT = wt.shape[0]
    last = scale.shape[0] - 1
    mod_map = lambda i, j: (jnp.minimum(i * tm // seg_rows, last), 0, 0)
    return pl.pallas_call(
        _norm_mod_mm_kernel,
        grid=(R // tm, N // tn),
        in_specs=[
            pl.BlockSpec((tm, D), lambda i, j: (i, 0)),
            pl.BlockSpec((1, D), lambda i, j: (0, 0)),
            pl.BlockSpec((1, 1, D), mod_map),
            pl.BlockSpec((1, 1, D), mod_map),
            pl.BlockSpec((D, tn), lambda i, j: (0, j)),
            pl.BlockSpec((NT, D), lambda i, j: (0, 0)),
        ],
        out_specs=[pl.BlockSpec((tm, tn), lambda i, j: (i, j)), pl.BlockSpec((NT, tm), lambda i, j: (0, i))],
        out_shape=[jax.ShapeDtypeStruct((R, N), F32), jax.ShapeDtypeStruct((NT, R), F32)],
        scratch_shapes=[pltpu.VMEM((tm, D), BF16)],
        compiler_params=_cparams(("parallel", "arbitrary")),
        name="norm_mod_matmul",
    )(x, g, scale, shift, w, wt)


def _out_proj_kernel(m_ref, d_ref, y_ref, wm_ref, wd_ref, wy_ref, x_ref, gate_ref, o_ref):
    acc = _dot(m_ref[...], wm_ref[...]) + _dot(d_ref[...], wd_ref[...]) + _dot(y_ref[...], wy_ref[...])
    o_ref[...] = x_ref[...] + gate_ref[0] * acc


def _out_proj_residual(m, d, y, wm, wd, wy, x, gate, n_rows, seg_rows, tm):
    R, D = n_rows, x.shape[1]
    last = gate.shape[0] - 1
    row = lambda i: (i, 0)
    full = lambda i: (0, 0)
    return pl.pallas_call(
        _out_proj_kernel,
        grid=(R // tm,),
        in_specs=[
            pl.BlockSpec((tm, m.shape[1]), row),
            pl.BlockSpec((tm, d.shape[1]), row),
            pl.BlockSpec((tm, y.shape[1]), row),
            pl.BlockSpec(wm.shape, full),
            pl.BlockSpec(wd.shape, full),
            pl.BlockSpec(wy.shape, full),
            pl.BlockSpec((tm, D), row),
            pl.BlockSpec((1, 1, D), lambda i: (jnp.minimum(i * tm // seg_rows, last), 0, 0)),
        ],
        out_specs=pl.BlockSpec((tm, D), row),
        out_shape=jax.ShapeDtypeStruct((R, D), F32),
        compiler_params=_cparams(("parallel",)),
        name="out_proj_residual",
    )(m, d, y, wm, wd, wy, x, gate)


def _log_sigmoid(x):
    return jnp.minimum(x, 0.0) - jnp.log1p(jnp.exp(-jnp.abs(x)))


def _mlstm_gate_tables(g_ref, r0, L, gb, tril, triu):
    g = g_ref[pl.ds(r0, L), :] + gb
    lf = _log_sigmoid(g)
    gT = g.T
    lfT = lf.T
    parts = _split3(lf)
    partsT = _split3(lfT)
    cs_f = sum(_dot(tril, p) for p in parts)
    cs_b = sum(_dot(triu, p) for p in parts)
    rs_f = sum(_dot(p, triu) for p in partsT)
    rs_b = sum(_dot(p, tril) for p in partsT)
    return g, gT, cs_f, cs_b, rs_f, rs_b


def _mlstm_chunk(qT, kT, vT, i_row, b_row, c_col, b_end, mask, state):
    C, n, m = state
    qb = qT.astype(BF16)
    kb = kT.astype(BF16)
    dmat = jnp.where(mask, b_row + c_col, -jnp.inf)
    inter = b_row + m
    m_t = jnp.maximum(inter, jnp.max(dmat, axis=0, keepdims=True))
    s = _dot_tn(kb, qb) * jnp.exp(dmat - m_t)
    carry_w = jnp.exp(inter - m_t)
    num = _dot(vT.astype(BF16), s.astype(BF16)) + carry_w * _dot(C.astype(BF16), qb)
    den = jnp.sum(s, axis=0, keepdims=True) + carry_w * jnp.sum(qT * n, axis=0, keepdims=True)
    h = num / jnp.maximum(jnp.abs(den), jnp.exp(-m_t))
    g = b_end - b_row + i_row
    m_new = jnp.maximum(b_end + m, jnp.max(g, axis=-1, keepdims=True))
    ws = jnp.exp(g - m_new)
    decay = jnp.exp(b_end + m - m_new)
    C_new = decay * C + _dot_nt((vT * ws).astype(BF16), kb)
    n_new = decay * n + jnp.sum(kT * ws, axis=-1, keepdims=True)
    return h, (C_new, n_new, m_new)


def _mlstm_kernel(ql_ref, kl_ref, vl_ref, ol_ref, gl_ref, qc_ref, kc_ref, vc_ref, oc_ref, gc_ref,
                  gb_ref, ng_ref, outl_ref, outc_ref, hf_ref, hb_ref, *, L, H, dh):
    S = ql_ref.shape[1]
    n_ctx = qc_ref.shape[1]
    row = lax.broadcasted_iota(jnp.int32, (L, L), 0)
    col = lax.broadcasted_iota(jnp.int32, (L, L), 1)
    lower = col <= row
    upper = col >= row
    tril = lower.astype(BF16)
    triu = upper.astype(BF16)
    gb = gb_ref[...]
    k_scale = dh ** -0.5

    def both_dirs(refs_f, r0_f, refs_b, r0_b, hoff, state):
        new_state = []
        for d, (refs, r0, h_ref) in enumerate(((refs_f, r0_f, hf_ref), (refs_b, r0_b, hb_ref))):
            q_ref, k_ref, v_ref, g_ref = refs
            g, gT, cs_f, cs_b, rs_f, rs_b = _mlstm_gate_tables(g_ref, r0, L, gb, tril, triu)
            cs, rs, mask = (cs_f, rs_f, upper) if d == 0 else (cs_b, rs_b, lower)
            end = L - 1 if d == 0 else 0
            for hh in range(H):
                ic = 2 * d * H + hh
                fc = ic + H
                rows = slice(hh * dh, (hh + 1) * dh)
                qT = q_ref[rows, pl.ds(r0, L)]
                kT = k_ref[rows, pl.ds(r0, L)] * k_scale
                vT = v_ref[rows, pl.ds(r0, L)]
                h, st = _mlstm_chunk(qT, kT, vT, gT[ic:ic + 1, :], rs[fc:fc + 1, :],
                                     g[:, ic:ic + 1] - cs[:, fc:fc + 1], rs[fc:fc + 1, end:end + 1], mask,
                                     state[d * H + hh])
                off = hoff + r0
                h_ref[rows, pl.ds(off if isinstance(off, int) else pl.multiple_of(off, L), L)] = h
                new_state.append(st)
        return tuple(new_state)

    state = tuple((jnp.zeros((dh, dh), F32), jnp.zeros((dh, 1), F32), jnp.full((1, 1), ML_M_INIT, F32))
                  for _ in range(2 * H))
    ctx_refs = (qc_ref, kc_ref, vc_ref, gc_ref)
    lat_refs = (ql_ref, kl_ref, vl_ref, gl_ref)
    n_cc = n_ctx // L
    for c in range(n_cc):
        state = both_dirs(ctx_refs, c * L, ctx_refs, (n_cc - 1 - c) * L, 0, state)
    n_lc = S // L

    def body(c, st):
        r_f = pl.multiple_of(c * L, L)
        r_b = pl.multiple_of((n_lc - 1 - c) * L, L)
        return both_dirs(lat_refs, r_f, lat_refs, r_b, n_ctx, st)

    lax.fori_loop(0, n_lc, body, state)

    def finish(o_ref, out_ref, hoff, n_rows):
        def fbody(c, carry):
            r0 = pl.multiple_of(c * L, L)
            off = pl.multiple_of(hoff + r0, L)
            hs = hf_ref[:, pl.ds(off, L)] + hb_ref[:, pl.ds(off, L)]
            normed = []
            for hh in range(H):
                rows = slice(hh * dh, (hh + 1) * dh)
                hv = hs[rows]
                normed.append(hv * lax.rsqrt(jnp.mean(hv * hv, axis=0, keepdims=True) + EPS) * ng_ref[rows])
            hn = jnp.concatenate(normed, axis=0).T
            out_ref[pl.ds(r0, L), :] = (jax.nn.sigmoid(o_ref[pl.ds(r0, L), :]) * hn).astype(out_ref.dtype)
            return carry
        lax.fori_loop(0, n_rows // L, fbody, 0)

    finish(ol_ref, outl_ref, n_ctx, S)
    finish(oc_ref, outc_ref, 0, n_ctx)


def _mlstm_mixer(uT, u, gate_b, norm_g, B, S, n_ctx, o_col, gate_col):
    W = norm_g.shape[0]
    H = ML_HEADS
    dh = W // H
    ocb = o_col // W
    gcb = gate_col // LANES
    cblk = (B * S) // n_ctx

    def lat_t(j):
        return pl.BlockSpec((W, S), lambda b: (j, b))

    def ctx_t(j):
        return pl.BlockSpec((W, n_ctx), lambda b: (j, cblk + b))

    one = lambda b: (0, 0)
    out_l, out_c = pl.pallas_call(
        functools.partial(_mlstm_kernel, L=ML_CHUNK, H=H, dh=dh),
        grid=(B,),
        in_specs=[lat_t(0), lat_t(1), lat_t(2), pl.BlockSpec((S, W), lambda b: (b, ocb)),
                  pl.BlockSpec((S, LANES), lambda b: (b, gcb)),
                  ctx_t(0), ctx_t(1), ctx_t(2), pl.BlockSpec((n_ctx, W), lambda b: (cblk + b, ocb)),
                  pl.BlockSpec((n_ctx, LANES), lambda b: (cblk + b, gcb)),
                  pl.BlockSpec((1, LANES), one), pl.BlockSpec((W, 1), one)],
        out_specs=[pl.BlockSpec((S, W), lambda b: (b, 0)), pl.BlockSpec((n_ctx, W), lambda b: (b, 0))],
        out_shape=[jax.ShapeDtypeStruct((B * S, W), BF16), jax.ShapeDtypeStruct((B * n_ctx, W), BF16)],
        scratch_shapes=[pltpu.VMEM((W, n_ctx + S), F32), pltpu.VMEM((W, n_ctx + S), F32)],
        compiler_params=_cparams(("parallel",)),
        name="mlstm",
    )(uT, uT, uT, u, u, uT, uT, uT, u, u, gate_b, norm_g)
    return jnp.concatenate([out_l, out_c], axis=0)


def _da_prep_kernel(q_ref, k_ref, v_ref, cos_ref, sin_ref, qg_ref, kg_ref, seg_ref, qo_ref, ko_ref, vo_ref, *, dh):
    cos = cos_ref[...]
    sin = sin_ref[...]
    seg = seg_ref[...]
    W = q_ref.shape[1]
    lane = lax.broadcasted_iota(jnp.int32, (1, W), 1)
    quarter = dh // 4
    first = (lane % (2 * quarter)) < quarter

    def norm_rope(x, g):
        hi, lo = _split2(x * x)
        ms = (_dot(hi, seg) + _dot(lo, seg)) * (1.0 / dh)
        xn = x * lax.rsqrt(ms + EPS) * g
        rot = jnp.where(first, -pltpu.roll(xn, W - quarter, 1), pltpu.roll(xn, quarter, 1))
        return xn * cos + rot * sin

    qo_ref[...] = (norm_rope(q_ref[...], qg_ref[...]) * (dh ** -0.5)).astype(BF16)
    ko_ref[...] = norm_rope(k_ref[...], kg_ref[...]).astype(BF16)
    vo_ref[...] = v_ref[...].astype(BF16)


def _da_prep(u, cos, sin, qg, kg, seg, n_lat_rows, S, col0, tm, dh):
    R = u.shape[0]
    W = qg.shape[1]
    cb = col0 // W
    n_lat = n_lat_rows // tm
    per_seq = S // tm
    tab = lambda i: (jnp.where(i < n_lat, i % per_seq, per_seq), 0)
    one = lambda i: (0, 0)
    row = lambda i: (i, 0)

    def ucol(j):
        return pl.BlockSpec((tm, W), lambda i: (i, cb + j))

    return pl.pallas_call(
        functools.partial(_da_prep_kernel, dh=dh),
        grid=(R // tm,),
        in_specs=[ucol(0), ucol(1), ucol(2), pl.BlockSpec((tm, W), tab), pl.BlockSpec((tm, W), tab),
                  pl.BlockSpec((1, W), one), pl.BlockSpec((1, W), one), pl.BlockSpec((W, W), one)],
        out_specs=[pl.BlockSpec((tm, W), row)] * 3,
        out_shape=[jax.ShapeDtypeStruct((R, W), BF16)] * 3,
        compiler_params=_cparams(("parallel",)),
        name="da_prep",
    )(u, u, u, cos, sin, qg, kg, seg)


def _da_attn_kernel(*refs, n_kv, dh, lam_init):
    q_ref = refs[0]
    k_refs = refs[1:1 + n_kv]
    v_refs = refs[1 + n_kv:1 + 2 * n_kv]
    lam_ref, sg_ref, o_ref = refs[1 + 2 * n_kv:]
    lp = lam_ref[...]
    lam = (jnp.exp(jnp.sum(lp[0:1] * lp[1:2], axis=-1, keepdims=True))
           - jnp.exp(jnp.sum(lp[2:3] * lp[3:4], axis=-1, keepdims=True)) + lam_init)
    q = q_ref[...]
    acc = None
    probs = []
    for mp in range(2):
        lanes = slice(mp * dh, (mp + 1) * dh)
        s = [_dot_nt(q[:, lanes], k_ref[:, lanes]) for k_ref in k_refs]
        mx = functools.reduce(jnp.maximum, [jnp.max(si, axis=-1, keepdims=True) for si in s])
        p = [jnp.exp(si - mx) for si in s]
        den = sum(jnp.sum(pi, axis=-1, keepdims=True) for pi in p)
        probs.append([pi / den for pi in p])
    for j in range(n_kv):
        a = (probs[0][j] - lam * probs[1][j]).astype(BF16)
        t = _dot(a, v_refs[j][...])
        acc = t if acc is None else acc + t
    o = acc * lax.rsqrt(jnp.mean(acc * acc, axis=-1, keepdims=True) + EPS) * sg_ref[...]
    o_ref[...] = (o * (1.0 - lam_init)).astype(o_ref.dtype)


def _da_attention(q, k, v, lam_p, subln_g, lam_init, B, q_rows, q_blk0, kv_segs, tq, dh):
    H = DA_HEADS
    vd = 2 * dh
    nq = q_rows // tq
    q0 = q_blk0

    def kv_spec(rows, blk0):
        return pl.BlockSpec((rows, vd), lambda b, h, i: (blk0 + b, h))

    kspecs = [kv_spec(r, b0) for r, b0 in kv_segs]
    one = lambda b, h, i: (0, 0)
    return pl.pallas_call(
        functools.partial(_da_attn_kernel, n_kv=len(kv_segs), dh=dh, lam_init=lam_init),
        grid=(B, H, nq),
        in_specs=[pl.BlockSpec((tq, vd), lambda b, h, i: (q0 + b * nq + i, h))] + kspecs + kspecs
                 + [pl.BlockSpec(lam_p.shape, one), pl.BlockSpec((1, vd), one)],
        out_specs=pl.BlockSpec((tq, vd), lambda b, h, i: (b * nq + i, h)),
        out_shape=jax.ShapeDtypeStruct((B * q_rows, H * vd), BF16),
        compiler_params=_cparams(("parallel", "parallel", "arbitrary")),
        name="da_attention",
    )(q, *([k] * len(kv_segs)), *([v] * len(kv_segs)), lam_p, subln_g)


def _axial_rope_tables(S, dh, reps, pad_rows):
    rows = S // GRID_W
    r = jnp.repeat(jnp.arange(rows, dtype=F32), GRID_W)
    col = jnp.tile(jnp.arange(GRID_W, dtype=F32), rows)
    n_freq = dh // 4
    inv = ROPE_THETA ** (-jnp.arange(n_freq, dtype=F32) / n_freq)
    ar = r[:, None] * inv
    ac = col[:, None] * inv
    ang = jnp.concatenate([ar, ar, ac, ac], axis=-1)
    cos = jnp.concatenate([jnp.tile(jnp.cos(ang), (1, reps)), jnp.ones((pad_rows, dh * reps), F32)], axis=0)
    sin = jnp.concatenate([jnp.tile(jnp.sin(ang), (1, reps)), jnp.zeros((pad_rows, dh * reps), F32)], axis=0)
    return cos, sin


def _diff_attn_mixer(u, cos, sin, qg, kg, lam_p, subln_g, lam_init, B, S, n_ctx, col0, need_ctx):
    dh = qg.shape[0]
    W = DA_HEADS * 2 * dh
    seg = (jnp.arange(W)[:, None] // dh == jnp.arange(W)[None, :] // dh).astype(BF16)
    tm = 512
    q, k, v = _da_prep(u, cos, sin, jnp.tile(qg, W // dh)[None], jnp.tile(kg, W // dh)[None], seg,
                       B * S, S, col0, tm, dh)
    sg = subln_g[None]
    ctx_blk0 = (B * S) // n_ctx
    tq = 256
    out_l = _da_attention(q, k, v, lam_p, sg, lam_init, B, S, 0, [(n_ctx, ctx_blk0), (S, 0)], tq, dh)
    if not need_ctx:
        return out_l
    out_c = _da_attention(q, k, v, lam_p, sg, lam_init, B, n_ctx, (B * S) // n_ctx, [(n_ctx, ctx_blk0)], n_ctx, dh)
    return jnp.concatenate([out_l, out_c], axis=0)


def _hy_conv_kernel(v_ref, x1_ref, x2_ref, w_ref, b_ref, vo_ref, x1o_ref, x2o_ref):
    L, W = v_ref.shape
    row = lax.broadcasted_iota(jnp.int32, (L, 1), 0)
    for j, (i_ref, o_ref) in enumerate(((v_ref, vo_ref), (x1_ref, x1o_ref), (x2_ref, x2o_ref))):
        lanes = slice(j * W, (j + 1) * W)
        u = i_ref[...]
        prev = jnp.where(row == 0, 0.0, pltpu.roll(u, 1, 0))
        nxt = jnp.where(row == L - 1, 0.0, pltpu.roll(u, L - 1, 0))
        o_ref[...] = prev * w_ref[0:1, lanes] + u * w_ref[1:2, lanes] + nxt * w_ref[2:3, lanes] + b_ref[:, lanes]


def _hy_short_conv(u, conv_w, conv_b, n_seg, L, blk0, col0):
    W = conv_w.shape[1] // 3
    cb = col0 // W
    one = lambda b: (0, 0)

    def ucol(j):
        return pl.BlockSpec((L, W), lambda b: (blk0 + b, cb + j))

    return pl.pallas_call(
        _hy_conv_kernel,
        grid=(n_seg,),
        in_specs=[ucol(0), ucol(1), ucol(2), pl.BlockSpec(conv_w.shape, one), pl.BlockSpec(conv_b.shape, one)],
        out_specs=[pl.BlockSpec((L, W), lambda b: (b, 0))] * 3,
        out_shape=[jax.ShapeDtypeStruct((n_seg * L, W), F32)] * 3,
        compiler_params=_cparams(("parallel",)),
        name="hy_short_conv",
    )(u, u, u, conv_w, conv_b)


def _hy_fwd_kernel(c_ref, s_ref, z_ref, *rest, raw):
    z = z_ref[...].astype(BF16)
    zr = _dot(c_ref[...], z)
    zi = _dot(s_ref[...], z)
    if raw:
        yr_ref, yi_ref = rest
        yr_ref[...] = zr
        yi_ref[...] = zi
    else:
        a_ref, b_ref, d_ref, yr_ref, yi_ref = rest
        yr_ref[...] = (zr * a_ref[...] - zi * b_ref[...]).astype(yr_ref.dtype)
        yi_ref[...] = (zr * b_ref[...] + zi * d_ref[...]).astype(yi_ref.dtype)


def _hy_fwd(cm, sm, z, coefs, n_seg, L, tk):
    W = z.shape[1]
    nk = L // tk
    raw = coefs is None
    mat = pl.BlockSpec((tk, L), lambda i, b: (i, 0))
    cf = pl.BlockSpec((tk, W), lambda i, b: (i, 0))
    out = pl.BlockSpec((tk, W), lambda i, b: (b * nk + i, 0))
    odt = F32 if raw else BF16
    return pl.pallas_call(
        functools.partial(_hy_fwd_kernel, raw=raw),
        grid=(nk, n_seg),
        in_specs=[mat, mat, pl.BlockSpec((L, W), lambda i, b: (b, 0))] + ([] if raw else [cf, cf, cf]),
        out_specs=[out, out],
        out_shape=[jax.ShapeDtypeStruct((n_seg * L, W), odt)] * 2,
        compiler_params=_cparams(("parallel", "arbitrary")),
        name="hy_dft_fwd",
    )(cm, sm, z, *(() if raw else coefs))


def _hy_inv_kernel(c_ref, st_ref, yr_ref, yi_ref, x_ref, vz_ref, skip_ref, o_ref):
    y = _dot(c_ref[...], yr_ref[...]) + _dot(st_ref[...], yi_ref[...])
    o_ref[...] = (x_ref[...] * (y + skip_ref[...] * vz_ref[...])).astype(o_ref.dtype)


def _hy_inv(cm, smt, yr, yi, xg, vz, skip, n_seg, L, tt, out_dtype):
    W = yr.shape[1]
    nt = L // tt
    mat = pl.BlockSpec((tt, L), lambda i, b: (i, 0))
    seq = pl.BlockSpec((L, W), lambda i, b: (b, 0))
    row = pl.BlockSpec((tt, W), lambda i, b: (b * nt + i, 0))
    return pl.pallas_call(
        _hy_inv_kernel,
        grid=(nt, n_seg),
        in_specs=[mat, mat, seq, seq, row, row, pl.BlockSpec((1, W), lambda i, b: (0, 0))],
        out_specs=row,
        out_shape=jax.ShapeDtypeStruct((n_seg * L, W), out_dtype),
        compiler_params=_cparams(("parallel", "arbitrary")),
        name="hy_dft_inv",
    )(cm, smt, yr, yi, xg, vz, skip)


def _dft_mats(L):
    k = jnp.arange(L, dtype=jnp.int32)
    kn = (k[:, None] * k[None, :]) % (2 * L)
    ang = kn.astype(F32) * (math.pi / L)
    cm = jnp.cos(ang)
    sm = -jnp.sin(ang)
    sm = sm.at[0].set(jnp.where(k % 2 == 0, 1.0, -1.0))
    return cm.astype(BF16), sm.astype(BF16), sm.T.astype(BF16)


def _hyena_filters(L, w1, b1, w2, b2, w3, freq, W):
    t01 = jnp.linspace(0.0, 1.0, L, dtype=F32)[:, None]
    wpos = (2.0 * math.pi / L) * jnp.arange(L, dtype=F32)[:, None]
    bands = jnp.linspace(1e-4, HY_BANDS - 1, HY_BANDS, dtype=F32)
    feats = jnp.concatenate([t01, jnp.cos(wpos * bands), -jnp.sin(wpos * bands)], axis=-1)
    hp = lax.Precision.HIGHEST
    h = jnp.sin(freq[0] * (jnp.dot(feats, w1, precision=hp) + b1))
    h = jnp.sin(freq[1] * (jnp.dot(h, w2, precision=hp) + b2))
    h = jnp.dot(h, w3, precision=hp).reshape(L, HY_ORDER, 2, W)
    deltas = jnp.abs(jnp.linspace(math.log(HY_TARGET) / HY_SLOW, math.log(HY_TARGET) / HY_FAST, W, dtype=F32))
    h = h * (jnp.exp(-t01 * deltas) + HY_SHIFT)[:, None, None, :]
    hf, hb = h[:, :, 0], h[:, :, 1]
    hf = hf.at[0].add(hb[0])
    hb = hb.at[0].set(0.0)
    scale = lax.rsqrt(jnp.sum(hf * hf, axis=0, keepdims=True) + jnp.sum(hb * hb, axis=0, keepdims=True) + EPS)
    return (hf * scale).reshape(L, HY_ORDER * W), (hb * scale).reshape(L, HY_ORDER * W)


def _hyena_spectrum(mats, L, w1, b1, w2, b2, w3, freq, W):
    cm, sm, _ = mats
    hf, hb = _hyena_filters(L, w1, b1, w2, b2, w3, freq, W)
    cols = jnp.concatenate([hf[:, :W], hf[:, W:], hb[:, :W], hb[:, W:]], axis=0)
    tk = min(L, 512)
    gr, gi = _hy_fwd(cm, sm, cols, None, 2 * HY_ORDER, L, tk)
    gr = gr.reshape(2, HY_ORDER, L, W)
    gi = gi.reshape(2, HY_ORDER, L, W)
    kr = gr[0] + gr[1]
    ki = gi[0] - gi[1]
    nyq = gi[0, :, 0] + gi[1, :, 0]
    n = 2.0 * L
    wk = jnp.full((L, 1), 2.0 / n, F32).at[0].set(1.0 / n)
    a = kr * wk
    bm = (ki * wk).at[:, 0].set(0.0)
    dd = a.at[:, 0].set(nyq / n)
    return [(a[o], bm[o], dd[o]) for o in range(HY_ORDER)]


def _hyena_seq(u, mats, spec, conv_w, conv_b, skip, n_seg, L, blk0, col0):
    cm, sm, smt = mats
    t = min(L, 512)
    v, x1, x2 = _hy_short_conv(u, conv_w, conv_b, n_seg, L, blk0, col0)
    yr, yi = _hy_fwd(cm, sm, v, spec[0], n_seg, L, t)
    z = _hy_inv(cm, smt, yr, yi, x1, v, skip[0:1], n_seg, L, t, F32)
    yr, yi = _hy_fwd(cm, sm, z, spec[1], n_seg, L, t)
    return _hy_inv(cm, smt, yr, yi, x2, z, skip[1:2], n_seg, L, t, BF16)


def _pack_rows(x):
    n = x.shape[1] // 2
    lo = pltpu.bitcast(x[:, :n].astype(BF16).astype(F32), jnp.uint32)
    hi = pltpu.bitcast(x[:, n:].astype(BF16).astype(F32), jnp.uint32)
    return (lo >> 16) | (hi & jnp.uint32(0xFFFF0000))


def _unpack_rows(p):
    return pltpu.bitcast(p << 16, F32), pltpu.bitcast(p & jnp.uint32(0xFFFF0000), F32)


def _moe_router_kernel(x_ref, g_ref, sc_ref, sh_ref, whi_ref, wlo_ref, rb_ref, h_ref, idx_ref, wts_ref, cnt_ref):
    x = x_ref[...]
    y = x * lax.rsqrt(jnp.mean(x * x, axis=-1, keepdims=True) + EPS) * g_ref[...]
    h = y * (1.0 + sc_ref[0]) + sh_ref[0]
    h_ref[...] = _pack_rows(h)
    hi, lo = _split2(h)
    logits = _dot(hi, whi_ref[...]) + _dot(hi, wlo_ref[...]) + _dot(lo, whi_ref[...])
    scores = jax.nn.sigmoid(logits)
    sel = scores + rb_ref[...]
    tm, E = sel.shape
    gsz = E // N_GROUPS
    neg = -jnp.inf
    lane = lax.broadcasted_iota(jnp.int32, (1, E), 1).astype(F32)
    glane = lax.broadcasted_iota(jnp.int32, (1, gsz), 1).astype(F32)
    gscore = []
    for g in range(N_GROUPS):
        blk = sel[:, g * gsz:(g + 1) * gsz]
        m1 = jnp.max(blk, axis=-1, keepdims=True)
        first = jnp.min(jnp.where(blk == m1, glane, float(gsz)), axis=-1, keepdims=True)
        m2 = jnp.max(jnp.where(glane == first, neg, blk), axis=-1, keepdims=True)
        gscore.append(m1 + m2)
    group_of_lane = lax.broadcasted_iota(jnp.int32, (1, E), 1) // gsz
    keep = jnp.zeros((tm, E), F32)
    for g in range(N_GROUPS):
        rank = jnp.zeros((tm, 1), F32)
        for o in range(N_GROUPS):
            if o != g:
                ahead = (gscore[o] >= gscore[g]) if o < g else (gscore[o] > gscore[g])
                rank = rank + jnp.where(ahead, 1.0, 0.0)
        keep = jnp.where(group_of_lane == g, jnp.where(rank < TOPK_GROUPS, 1.0, 0.0), keep)
    work = jnp.where(keep > 0.0, sel, neg)
    out_lane = lax.broadcasted_iota(jnp.int32, (1, idx_ref.shape[1]), 1)
    idx_out = jnp.zeros(idx_ref.shape, F32)
    w_out = jnp.zeros(wts_ref.shape, F32)
    total = jnp.zeros((tm, 1), F32)
    chosen = jnp.zeros((tm, E), F32)
    for j in range(TOP_K):
        mx = jnp.max(work, axis=-1, keepdims=True)
        am = jnp.min(jnp.where(work == mx, lane, float(E)), axis=-1, keepdims=True)
        hit = lane == am
        wj = jnp.sum(jnp.where(hit, scores, 0.0), axis=-1, keepdims=True)
        work = jnp.where(hit, neg, work)
        chosen = jnp.where(hit, 1.0, chosen)
        idx_out = jnp.where(out_lane == j, am, idx_out)
        w_out = jnp.where(out_lane == j, wj, w_out)
        total = total + wj
    idx_ref[...] = idx_out.astype(jnp.int32)
    wts_ref[...] = w_out / total * ROUTED_SCALE
    cnt_ref[0] = jnp.sum(chosen, axis=0, keepdims=True)


def _moe_router(x, n_tok, g, scale, shift, whi, wlo, rb, seg_rows, tm):
    D = x.shape[1]
    E = whi.shape[1]
    last = scale.shape[0] - 1
    row = lambda i: (i, 0)
    one = lambda i: (0, 0)
    mod = lambda i: (jnp.minimum(i * tm // seg_rows, last), 0, 0)
    return pl.pallas_call(
        _moe_router_kernel,
        grid=(n_tok // tm,),
        in_specs=[pl.BlockSpec((tm, D), row), pl.BlockSpec((1, D), one), pl.BlockSpec((1, 1, D), mod),
                  pl.BlockSpec((1, 1, D), mod), pl.BlockSpec((D, E), one), pl.BlockSpec((D, E), one),
                  pl.BlockSpec((1, E), one)],
        out_specs=[pl.BlockSpec((tm, D // 2), row), pl.BlockSpec((tm, LANES), row), pl.BlockSpec((tm, LANES), row),
                   pl.BlockSpec((1, 1, E), lambda i: (i, 0, 0))],
        out_shape=[jax.ShapeDtypeStruct((n_tok, D // 2), jnp.uint32), jax.ShapeDtypeStruct((n_tok, LANES), jnp.int32),
                   jax.ShapeDtypeStruct((n_tok, LANES), F32), jax.ShapeDtypeStruct((n_tok // tm, 1, E), F32)],
        compiler_params=_cparams(("parallel",)),
        name="moe_router",
    )(x, g, scale, shift, whi, wlo, rb)


def _moe_pos_kernel(idx_ref, base_ref, pos_ref, *, K):
    tm = idx_ref.shape[0]
    E = base_ref.shape[2]
    idx = idx_ref[...]
    lane = lax.broadcasted_iota(jnp.int32, (1, E), 1)
    hits = [lane == idx[:, j:j + 1] for j in range(K)]
    onehot = jnp.zeros((tm, E), F32)
    for hit in hits:
        onehot = jnp.where(hit, 1.0, onehot)
    row = lax.broadcasted_iota(jnp.int32, (tm, tm), 0)
    col = lax.broadcasted_iota(jnp.int32, (tm, tm), 1)
    before = _dot((col < row).astype(BF16), onehot.astype(BF16))
    dest = before + base_ref[0]
    out_lane = lax.broadcasted_iota(jnp.int32, (1, pos_ref.shape[1]), 1)
    out = jnp.zeros(pos_ref.shape, F32)
    for j, hit in enumerate(hits):
        out = jnp.where(out_lane == j, jnp.sum(jnp.where(hit, dest, 0.0), axis=-1, keepdims=True), out)
    pos_ref[...] = out.astype(jnp.int32)


def _moe_plan(idx, cnt, K, bm, tm):
    n_tok = idx.shape[0]
    E = cnt.shape[2]
    cnt = cnt[:, 0, :]
    counts = jnp.sum(cnt, axis=0)
    pcounts = jnp.ceil(counts / bm) * bm
    pends = jnp.cumsum(pcounts)
    base = (pends - pcounts)[None, :] + jnp.cumsum(cnt, axis=0) - cnt
    pos = pl.pallas_call(
        functools.partial(_moe_pos_kernel, K=K),
        grid=(n_tok // tm,),
        in_specs=[pl.BlockSpec((tm, LANES), lambda i: (i, 0)), pl.BlockSpec((1, 1, E), lambda i: (i, 0, 0))],
        out_specs=pl.BlockSpec((tm, LANES), lambda i: (i, 0)),
        out_shape=jax.ShapeDtypeStruct((n_tok, LANES), jnp.int32),
        compiler_params=_cparams(("parallel",)),
        name="moe_positions",
    )(idx, base[:, None, :])
    n_blocks = -(-(n_tok * K + E * (bm - 1)) // bm)
    block_start = jnp.concatenate([jnp.zeros((1,), F32), pends / bm]).astype(jnp.int32)
    return pos[:, :K].reshape(-1), block_start, n_blocks


def _per_token_rows(tm, K, copy):
    def start(t, carry):
        for j in range(K):
            copy(t, j).start(priority=j % 2)
        return carry

    def wait(t, carry):
        for j in range(K):
            copy(t, j).wait()
        return carry

    return (lambda: lax.fori_loop(0, tm, start, 0, unroll=2)), (lambda: lax.fori_loop(0, tm, wait, 0, unroll=2))


def _moe_dispatch_kernel(pos_ref, prev_ref, h_ref, xs_in_ref, xs_hbm, stage_ref, sems, *, K):
    del xs_in_ref
    tm = h_ref.shape[0]
    i = pl.program_id(0)
    last = pl.num_programs(0) - 1
    slot = lax.bitwise_and(i, 1)

    def tile_copies(s, p_ref):
        def row_copy(t, j):
            return pltpu.make_async_copy(stage_ref.at[s, pl.ds(t, 1)], xs_hbm.at[pl.ds(p_ref[t * K + j], 1)],
                                         sems.at[s])
        return _per_token_rows(tm, K, row_copy)

    start, wait = tile_copies(slot, pos_ref)
    _, wait_prev = tile_copies(1 - slot, prev_ref)
    stage_ref[slot] = h_ref[...]
    start()

    @pl.when(i > 0)
    def _():
        wait_prev()

    @pl.when(i == last)
    def _():
        wait()


def _moe_dispatch(pos, h, n_rows, K, tm):
    n_tok, W = h.shape
    xs0 = jnp.zeros((n_rows, W), h.dtype)
    return pl.pallas_call(
        functools.partial(_moe_dispatch_kernel, K=K),
        grid=(n_tok // tm,),
        in_specs=[pl.BlockSpec((tm * K,), lambda i: (i,), memory_space=pltpu.SMEM),
                  pl.BlockSpec((tm * K,), lambda i: (jnp.maximum(i - 1, 0),), memory_space=pltpu.SMEM),
                  pl.BlockSpec((tm, W), lambda i: (i, 0)),
                  pl.BlockSpec(memory_space=pl.ANY)],
        out_specs=pl.BlockSpec(memory_space=pl.ANY),
        out_shape=jax.ShapeDtypeStruct((n_rows, W), h.dtype),
        scratch_shapes=[pltpu.VMEM((2, tm, W), h.dtype), pltpu.SemaphoreType.DMA((2,))],
        input_output_aliases={3: 0},
        compiler_params=_cparams(("arbitrary",)),
        name="moe_dispatch",
    )(pos, pos, h, xs0)


def _moe_expert_kernel(bs_ref, x_hbm, w1_ref, w3_ref, w2_ref, y_hbm, xbuf, ybuf, w1b_ref, w3b_ref, w2b_ref,
                       xsem, ysem, *, bm, nbuf, n_blocks):
    e = pl.program_id(0)
    last = pl.num_programs(0) - 1
    b0 = bs_ref[e]
    b1 = bs_ref[e + 1]
    n_used = bs_ref[last + 1]

    def x_copy(g):
        slot = lax.bitwise_and(g, nbuf - 1)
        return pltpu.make_async_copy(x_hbm.at[pl.ds(g * bm, bm)], xbuf.at[slot], xsem.at[slot])

    def y_copy(g):
        slot = lax.bitwise_and(g, nbuf - 1)
        return pltpu.make_async_copy(ybuf.at[slot], y_hbm.at[pl.ds(g * bm, bm)], ysem.at[slot])

    @pl.when(e == 0)
    def _():
        for p in range(nbuf - 1):
            @pl.when(p < n_used)
            def _():
                x_copy(p).start()

    @pl.when(b1 > b0)
    def _():
        w1b_ref[...] = w1_ref[0, 0].astype(BF16)
        w3b_ref[...] = w3_ref[0, 0].astype(BF16)
        w2b_ref[...] = w2_ref[0, 0].astype(BF16)

    def block(g, carry):
        slot = lax.bitwise_and(g, nbuf - 1)
        x_copy(g).wait()

        @pl.when(g + nbuf - 1 < n_used)
        def _():
            x_copy(g + nbuf - 1).start()

        @pl.when(g >= nbuf)
        def _():
            y_copy(g - nbuf).wait()

        x = jnp.concatenate(_unpack_rows(xbuf[slot]), axis=1).astype(BF16)
        a = _dot(x, w1b_ref[...])
        b = _dot(x, w3b_ref[...])
        ybuf[slot] = _pack_rows(_dot((jax.nn.silu(a) * b).astype(BF16), w2b_ref[...]))
        y_copy(g).start()
        return carry

    lax.fori_loop(b0, b1, block, 0)

    @pl.when(e == last)
    def _():
        def drain(g, carry):
            y_copy(g).wait()
            return carry

        lax.fori_loop(jnp.maximum(n_used - nbuf, 0), n_used, drain, 0)
        ybuf[0] = jnp.zeros(ybuf.shape[1:], ybuf.dtype)

        def zero_copy(g):
            return pltpu.make_async_copy(ybuf.at[0], y_hbm.at[pl.ds(g * bm, bm)], ysem.at[0])

        def fill(g, carry):
            zero_copy(g).start()
            return carry

        def fill_wait(g, carry):
            zero_copy(g).wait()
            return carry

        lax.fori_loop(n_used, n_blocks, fill, 0)
        lax.fori_loop(n_used, n_blocks, fill_wait, 0)


def _moe_experts(block_start, xs, w1, w3, w2, layer, bm):
    P, W = xs.shape
    E, D, F = w1.shape[1:]
    nbuf = 4
    grid_spec = pltpu.PrefetchScalarGridSpec(
        num_scalar_prefetch=1,
        grid=(E,),
        in_specs=[pl.BlockSpec(memory_space=pl.ANY),
                  pl.BlockSpec((1, 1, D, F), lambda e, bs: (layer, e, 0, 0)),
                  pl.BlockSpec((1, 1, D, F), lambda e, bs: (layer, e, 0, 0)),
                  pl.BlockSpec((1, 1, F, D), lambda e, bs: (layer, e, 0, 0))],
        out_specs=pl.BlockSpec(memory_space=pl.ANY),
        scratch_shapes=[pltpu.VMEM((nbuf, bm, W), xs.dtype), pltpu.VMEM((nbuf, bm, W), xs.dtype),
                        pltpu.VMEM((D, F), BF16), pltpu.VMEM((D, F), BF16), pltpu.VMEM((F, D), BF16),
                        pltpu.SemaphoreType.DMA((nbuf,)), pltpu.SemaphoreType.DMA((nbuf,))],
    )
    return pl.pallas_call(
        functools.partial(_moe_expert_kernel, bm=bm, nbuf=nbuf, n_blocks=P // bm),
        grid_spec=grid_spec,
        out_shape=jax.ShapeDtypeStruct((P, W), xs.dtype),
        compiler_params=_cparams(("arbitrary",)),
        name="moe_experts",
    )(block_start, xs, w1, w3, w2)


def _moe_combine_kernel(pos_ref, next_ref, wts_ref, x_ref, h_ref, sw1_ref, sw3_ref, sw2_ref, gate_ref, ys_hbm, o_ref,
                        gath_ref, sems, *, K):
    tm = x_ref.shape[0]
    i = pl.program_id(0)
    last = pl.num_programs(0) - 1

    slot = lax.bitwise_and(i, 1)

    def tile_copies(s, p_ref):
        def row_copy(t, j):
            return pltpu.make_async_copy(ys_hbm.at[pl.ds(p_ref[t * K + j], 1)], gath_ref.at[s, j, pl.ds(t, 1)],
                                         sems.at[s])
        return _per_token_rows(tm, K, row_copy)

    start, wait = tile_copies(slot, pos_ref)
    start_next, _ = tile_copies(1 - slot, next_ref)

    @pl.when(i == 0)
    def _():
        start()

    @pl.when(i < last)
    def _():
        start_next()

    h = jnp.concatenate(_unpack_rows(h_ref[...]), axis=1).astype(BF16)
    mid = jax.nn.silu(_dot(h, sw1_ref[...])) * _dot(h, sw3_ref[...])
    shared = _dot(mid.astype(BF16), sw2_ref[...])
    wait()
    wts = wts_ref[...]
    W = gath_ref.shape[3]
    acc_lo = shared[:, :W]
    acc_hi = shared[:, W:]
    for j in range(K):
        lo, hi = _unpack_rows(gath_ref[slot, j])
        acc_lo = acc_lo + wts[:, j:j + 1] * lo
        acc_hi = acc_hi + wts[:, j:j + 1] * hi
    gate = gate_ref[0]
    o_ref[:, :W] = x_ref[:, :W] + gate[:, :W] * acc_lo
    o_ref[:, W:] = x_ref[:, W:] + gate[:, W:] * acc_hi


def _moe_combine(pos, wts, x, h, sw1, sw3, sw2, gate, ys, n_tok, K, seg_rows, tm):
    D = x.shape[1]
    W = ys.shape[1]
    n_tiles = n_tok // tm
    last = gate.shape[0] - 1
    row = lambda i: (i, 0)
    one = lambda i: (0, 0)
    return pl.pallas_call(
        functools.partial(_moe_combine_kernel, K=K),
        grid=(n_tiles,),
        in_specs=[pl.BlockSpec((tm * K,), lambda i: (i,), memory_space=pltpu.SMEM),
                  pl.BlockSpec((tm * K,), lambda i: (jnp.minimum(i + 1, n_tiles - 1),), memory_space=pltpu.SMEM),
                  pl.BlockSpec((tm, LANES), row), pl.BlockSpec((tm, D), row), pl.BlockSpec((tm, W), row),
                  pl.BlockSpec(sw1.shape, one), pl.BlockSpec(sw3.shape, one), pl.BlockSpec(sw2.shape, one),
                  pl.BlockSpec((1, 1, D), lambda i: (jnp.minimum(i * tm // seg_rows, last), 0, 0)),
                  pl.BlockSpec(memory_space=pl.ANY)],
        out_specs=pl.BlockSpec((tm, D), row),
        out_shape=jax.ShapeDtypeStruct((n_tok, D), F32),
        scratch_shapes=[pltpu.VMEM((2, K, tm, W), ys.dtype), pltpu.SemaphoreType.DMA((2,))],
        compiler_params=_cparams(("arbitrary",)),
        name="moe_combine",
    )(pos, pos, wts, x, h, sw1, sw3, sw2, gate, ys)


def _moe_ffn_residual(x, n_tok, norm_g, scale, shift, gate, router_w, router_b, ew1, ew3, ew2, layer, sw1, sw3, sw2,
                      seg_rows, bm):
    K = TOP_K
    tm = 256
    whi, wlo = _split2(router_w)
    h, idx, wts, cnt = _moe_router(x, n_tok, norm_g, scale, shift, whi, wlo, router_b[None], seg_rows, tm)
    pos, block_start, n_blocks = _moe_plan(idx, cnt, K, bm, tm)
    xs = _moe_dispatch(pos, h, n_blocks * bm, K, 128)
    ys = _moe_experts(block_start, xs, ew1, ew3, ew2, layer, bm)
    return _moe_combine(pos, wts, x, h, sw1.astype(BF16), sw3.astype(BF16), sw2.astype(BF16), gate, ys,
                        n_tok, K, seg_rows, 128)


def kernel(x, c, ctx, c_ctx, w_mod, b_mod, norm1_g, norm2_g, w_in, mlstm_gate_b, mlstm_norm_g, da_qnorm_g, da_knorm_g, da_lambda, da_subln_g, hy_conv_w, hy_conv_b, hy_w1, hy_b1, hy_w2, hy_b2, hy_w3, hy_freq, hy_skip, w_out, router_w, router_b, exp_w1, exp_w3, exp_w2, sh_w1, sh_w3, sh_w2):
    B, S, D = x.shape
    n_ctx = ctx.shape[1]
    depth = w_in.shape[0]
    n_lat = B * S
    ml_w = mlstm_norm_g.shape[1]
    da_dh = da_qnorm_g.shape[1]
    da_w = DA_HEADS * 2 * da_dh
    hy_w = hy_skip.shape[2]
    n_gates = 4 * ML_HEADS
    da_col = 0
    hy_col = da_col + 3 * da_w
    o_col = hy_col + 3 * hy_w
    gate_col = o_col + ml_w
    ml_in = 4 * ml_w + n_gates
    tm = 512

    X = jnp.concatenate([x.reshape(n_lat, D), ctx.reshape(B * n_ctx, D)], axis=0)
    sc = jax.nn.silu(jnp.concatenate([c, c_ctx[None]], axis=0))
    cos, sin = _axial_rope_tables(S, da_dh, da_w // da_dh, tm)
    mats_l = _dft_mats(S)
    mats_c = _dft_mats(n_ctx)
    for l in range(depth):
        last = l == depth - 1
        lam_init = 0.8 - 0.6 * math.exp(-0.3 * l)
        mods = (jnp.dot(sc, w_mod[l], precision=lax.Precision.HIGHEST) + b_mod[l]).reshape(B + 1, 6, 1, D)
        sh1, s1, g1, sh2, s2, g2 = [mods[:, i] for i in range(6)]
        wl = w_in[l]
        w_big = jnp.concatenate([wl[:, ml_in:], wl[:, 3 * ml_w:4 * ml_w], wl[:, 4 * ml_w:ml_in],
                                 jnp.zeros((D, LANES - n_gates), F32)], axis=1).astype(BF16)
        w_qkv_t = wl[:, :3 * ml_w].T.astype(BF16)
        U, UT = _norm_mod_matmul(X, norm1_g[l][None], s1, sh1, w_big, w_qkv_t, S, tm, w_big.shape[1] // 3)
        gb = jnp.concatenate([mlstm_gate_b[l], jnp.zeros((LANES - n_gates,), F32)])[None]
        m_out = _mlstm_mixer(UT, U, gb, mlstm_norm_g[l][:, None], B, S, n_ctx, o_col, gate_col)
        d_out = _diff_attn_mixer(U, cos, sin, da_qnorm_g[l], da_knorm_g[l], da_lambda[l], da_subln_g[l], lam_init,
                                 B, S, n_ctx, da_col, not last)
        hy_args = (hy_w1[l], hy_b1[l], hy_w2[l], hy_b2[l], hy_w3[l], hy_freq[l], hy_w)
        y_out = _hyena_seq(U, mats_l, _hyena_spectrum(mats_l, S, *hy_args), hy_conv_w[l], hy_conv_b[l][None],
                           hy_skip[l], B, S, 0, hy_col)
        n_rows = n_lat
        if not last:
            y_ctx = _hyena_seq(U, mats_c, _hyena_spectrum(mats_c, n_ctx, *hy_args), hy_conv_w[l], hy_conv_b[l][None],
                               hy_skip[l], B, n_ctx, n_lat // n_ctx, hy_col)
            y_out = jnp.concatenate([y_out, y_ctx], axis=0)
            n_rows = n_lat + B * n_ctx
        wo = w_out[l].astype(BF16)
        X = _out_proj_residual(m_out, d_out, y_out, wo[:ml_w], wo[ml_w:ml_w + da_w], wo[ml_w + da_w:], X, g1,
                               n_rows, S, tm)
        X = _moe_ffn_residual(X, n_rows, norm2_g[l][None], s2, sh2, g2, router_w[l], router_b[l],
                              exp_w1, exp_w3, exp_w2, l, sh_w1[l], sh_w3[l], sh_w2[l], S,
                              MOE_BLOCK)
    return X[:n_lat].reshape(B, S, D)
```

```python
import functools
import math

import jax
import jax.numpy as jnp
from jax import lax
from jax.experimental import pallas as pl
from jax.experimental.pallas import tpu as pltpu

F32 = jnp.float32
BF16 = jnp.bfloat16

EPS = 1e-6
GRID_W = 64
ROPE_THETA = 10000.0
ML_HEADS = 4
ML_CHUNK = 256
ML_M_INIT = -1e30
DA_HEADS = 4
HY_ORDER = 2
HY_BANDS = 8
HY_SHIFT = 0.05
HY_TARGET = 1e-2
HY_FAST = 0.3
HY_SLOW = 1.5
N_GROUPS = 8
TOPK_GROUPS = 4
TOP_K = 8
ROUTED_SCALE = 2.5
MOE_BLOCK = 128
LANES = 128
VMEM_LIMIT = 56 * 1024 * 1024


def _cparams(sem):
    return pltpu.CompilerParams(dimension_semantics=sem, vmem_limit_bytes=VMEM_LIMIT)


def _dot(a, b):
    return jnp.dot(a, b, preferred_element_type=F32)


def _dot_nt(a, b):
    return lax.dot_general(a, b, (((1,), (1,)), ((), ())), preferred_element_type=F32)


def _dot_tn(a, b):
    return lax.dot_general(a, b, (((0,), (0,)), ((), ())), preferred_element_type=F32)


def _split3(a):
    hi = a.astype(BF16)
    r = a - hi.astype(F32)
    mid = r.astype(BF16)
    lo = (r - mid.astype(F32)).astype(BF16)
    return hi, mid, lo


def _split2(a):
    hi = a.astype(BF16)
    lo = (a - hi.astype(F32)).astype(BF16)
    return hi, lo


def _norm_mod_mm_kernel(x_ref, g_ref, sc_ref, sh_ref, w_ref, wt_ref, o_ref, ot_ref, xn_ref):
    @pl.when(pl.program_id(1) == 0)
    def _():
        x = x_ref[...]
        y = x * lax.rsqrt(jnp.mean(x * x, axis=-1, keepdims=True) + EPS) * g_ref[...]
        xn_ref[...] = (y * (1.0 + sc_ref[0]) + sh_ref[0]).astype(BF16)
        ot_ref[...] = _dot_nt(wt_ref[...], xn_ref[...])

    o_ref[...] = _dot(xn_ref[...], w_ref[...])


def _norm_mod_matmul(x, g, scale, shift, w, wt, seg_rows, tm, tn):
    R, D = x.shape
    N = w.shape[1]
    NT = wt.shape[0]
    last = scale.shape[0] - 1
    mod_map = lambda i, j: (jnp.minimum(i * tm // seg_rows, last), 0, 0)
    return pl.pallas_call(
        _norm_mod_mm_kernel,
        grid=(R // tm, N // tn),
        in_specs=[
            pl.BlockSpec((tm, D), lambda i, j: (i, 0)),
            pl.BlockSpec((1, D), lambda i, j: (0, 0)),
            pl.BlockSpec((1, 1, D), mod_map),
            pl.BlockSpec((1, 1, D), mod_map),
            pl.BlockSpec((D, tn), lambda i, j: (0, j)),
            pl.BlockSpec((NT, D), lambda i, j: (0, 0)),
        ],
        out_specs=[pl.BlockSpec((tm, tn), lambda i, j: (i, j)), pl.BlockSpec((NT, tm), lambda i, j: (0, i))],
        out_shape=[jax.ShapeDtypeStruct((R, N), F32), jax.ShapeDtypeStruct((NT, R), F32)],
        scratch_shapes=[pltpu.VMEM((tm, D), BF16)],
        compiler_params=_cparams(("parallel", "arbitrary")),
        name="norm_mod_matmul",
    )(x, g, scale, shift, w, wt)


def _out_proj_kernel(m_ref, d_ref, y_ref, wm_ref, wd_ref, wy_ref, x_ref, gate_ref, o_ref):
    acc = _dot(m_ref[...], wm_ref[...]) + _dot(d_ref[...], wd_ref[...]) + _dot(y_ref[...], wy_ref[...])
    o_ref[...] = x_ref[...] + gate_ref[0] * acc


def _out_proj_residual(m, d, y, wm, wd, wy, x, gate, n_rows, seg_rows, tm):
    R, D = n_rows, x.shape[1]
    last = gate.shape[0] - 1
    row = lambda i: (i, 0)
    full = lambda i: (0, 0)
    return pl.pallas_call(
        _out_proj_kernel,
        grid=(R // tm,),
        in_specs=[
            pl.BlockSpec((tm, m.shape[1]), row),
            pl.BlockSpec((tm, d.shape[1]), row),
            pl.BlockSpec((tm, y.shape[1]), row),
            pl.BlockSpec(wm.shape, full),
            pl.BlockSpec(wd.shape, full),
            pl.BlockSpec(wy.shape, full),
            pl.BlockSpec((tm, D), row),
            pl.BlockSpec((1, 1, D), lambda i: (jnp.minimum(i * tm // seg_rows, last), 0, 0)),
        ],
        out_specs=pl.BlockSpec((tm, D), row),
        out_shape=jax.ShapeDtypeStruct((R, D), F32),
        compiler_params=_cparams(("parallel",)),
        name="out_proj_residual",
    )(m, d, y, wm, wd, wy, x, gate)


def _log_sigmoid(x):
    return jnp.minimum(x, 0.0) - jnp.log1p(jnp.exp(-jnp.abs(x)))


def _mlstm_gate_tables(g_ref, r0, L, gb, tril, triu):
    g = g_ref[pl.ds(r0, L), :] + gb
    lf = _log_sigmoid(g)
    gT = g.T
    lfT = lf.T
    parts = _split3(lf)
    partsT = _split3(lfT)
    cs_f = sum(_dot(tril, p) for p in parts)
    cs_b = sum(_dot(triu, p) for p in parts)
    rs_f = sum(_dot(p, triu) for p in partsT)
    rs_b = sum(_dot(p, tril) for p in partsT)
    return g, gT, cs_f, cs_b, rs_f, rs_b


def _mlstm_chunk(qT, kT, vT, i_row, b_row, c_col, b_end, mask, state):
    C, n, m = state
    qb = qT.astype(BF16)
    kb = kT.astype(BF16)
    dmat = jnp.where(mask, b_row + c_col, -jnp.inf)
    inter = b_row + m
    m_t = jnp.maximum(inter, jnp.max(dmat, axis=0, keepdims=True))
    s = _dot_tn(kb, qb) * jnp.exp(dmat - m_t)
    carry_w = jnp.exp(inter - m_t)
    num = _dot(vT.astype(BF16), s.astype(BF16)) + carry_w * _dot(C.astype(BF16), qb)
    den = jnp.sum(s, axis=0, keepdims=True) + carry_w * jnp.sum(qT * n, axis=0, keepdims=True)
    h = num / jnp.maximum(jnp.abs(den), jnp.exp(-m_t))
    g = b_end - b_row + i_row
    m_new = jnp.maximum(b_end + m, jnp.max(g, axis=-1, keepdims=True))
    ws = jnp.exp(g - m_new)
    decay = jnp.exp(b_end + m - m_new)
    C_new = decay * C + _dot_nt((vT * ws).astype(BF16), kb)
    n_new = decay * n + jnp.sum(kT * ws, axis=-1, keepdims=True)
    return h, (C_new, n_new, m_new)


def _mlstm_kernel(ql_ref, kl_ref, vl_ref, ol_ref, gl_ref, qc_ref, kc_ref, vc_ref, oc_ref, gc_ref,
                  gb_ref, ng_ref, outl_ref, outc_ref, hf_ref, hb_ref, *, L, H, dh):
    S = ql_ref.shape[1]
    n_ctx = qc_ref.shape[1]
    row = lax.broadcasted_iota(jnp.int32, (L, L), 0)
    col = lax.broadcasted_iota(jnp.int32, (L, L), 1)
    lower = col <= row
    upper = col >= row
    tril = lower.astype(BF16)
    triu = upper.astype(BF16)
    gb = gb_ref[...]
    k_scale = dh ** -0.5

    def both_dirs(refs_f, r0_f, refs_b, r0_b, hoff, state):
        new_state = []
        for d, (refs, r0, h_ref) in enumerate(((refs_f, r0_f, hf_ref), (refs_b, r0_b, hb_ref))):
            q_ref, k_ref, v_ref, g_ref = refs
            g, gT, cs_f, cs_b, rs_f, rs_b = _mlstm_gate_tables(g_ref, r0, L, gb, tril, triu)
            cs, rs, mask = (cs_f, rs_f, upper) if d == 0 else (cs_b, rs_b, lower)
            end = L - 1 if d == 0 else 0
            for hh in range(H):
                ic = 2 * d * H + hh
                fc = ic + H
                rows = slice(hh * dh, (hh + 1) * dh)
                qT = q_ref[rows, pl.ds(r0, L)]
                kT = k_ref[rows, pl.ds(r0, L)] * k_scale
                vT = v_ref[rows, pl.ds(r0, L)]
                h, st = _mlstm_chunk(qT, kT, vT, gT[ic:ic + 1, :], rs[fc:fc + 1, :],
                                     g[:, ic:ic + 1] - cs[:, fc:fc + 1], rs[fc:fc + 1, end:end + 1], mask,
                                     state[d * H + hh])
                off = hoff + r0
                h_ref[rows, pl.ds(off if isinstance(off, int) else pl.multiple_of(off, L), L)] = h
                new_state.append(st)
        return tuple(new_state)

    state = tuple((jnp.zeros((dh, dh), F32), jnp.zeros((dh, 1), F32), jnp.full((1, 1), ML_M_INIT, F32))
                  for _ in range(2 * H))
    ctx_refs = (qc_ref, kc_ref, vc_ref, gc_ref)
    lat_refs = (ql_ref, kl_ref, vl_ref, gl_ref)
    n_cc = n_ctx // L
    for c in range(n_cc):
        state = both_dirs(ctx_refs, c * L, ctx_refs, (n_cc - 1 - c) * L, 0, state)
    n_lc = S // L

    def body(c, st):
        r_f = pl.multiple_of(c * L, L)
        r_b = pl.multiple_of((n_lc - 1 - c) * L, L)
        return both_dirs(lat_refs, r_f, lat_refs, r_b, n_ctx, st)

    lax.fori_loop(0, n_lc, body, state)

    def finish(o_ref, out_ref, hoff, n_rows):
        def fbody(c, carry):
            r0 = pl.multiple_of(c * L, L)
            off = pl.multiple_of(hoff + r0, L)
            hs = hf_ref[:, pl.ds(off, L)] + hb_ref[:, pl.ds(off, L)]
            normed = []
            for hh in range(H):
                rows = slice(hh * dh, (hh + 1) * dh)
                hv = hs[rows]
                normed.append(hv * lax.rsqrt(jnp.mean(hv * hv, axis=0, keepdims=True) + EPS) * ng_ref[rows])
            hn = jnp.concatenate(normed, axis=0).T
            out_ref[pl.ds(r0, L), :] = (jax.nn.sigmoid(o_ref[pl.ds(r0, L), :]) * hn).astype(out_ref.dtype)
            return carry
        lax.fori_loop(0, n_rows // L, fbody, 0)

    finish(ol_ref, outl_ref, n_ctx, S)
    finish(oc_ref, outc_ref, 0, n_ctx)


def _mlstm_mixer(uT, u, gate_b, norm_g, B, S, n_ctx, o_col, gate_col):
    W = norm_g.shape[0]
    H = ML_HEADS
    dh = W // H
    ocb = o_col // W
    gcb = gate_col // LANES
    cblk = (B * S) // n_ctx

    def lat_t(j):
        return pl.BlockSpec((W, S), lambda b: (j, b))

    def ctx_t(j):
        return pl.BlockSpec((W, n_ctx), lambda b: (j, cblk + b))

    one = lambda b: (0, 0)
    out_l, out_c = pl.pallas_call(
        functools.partial(_mlstm_kernel, L=ML_CHUNK, H=H, dh=dh),
        grid=(B,),
        in_specs=[lat_t(0), lat_t(1), lat_t(2), pl.BlockSpec((S, W), lambda b: (b, ocb)),
                  pl.BlockSpec((S, LANES), lambda b: (b, gcb)),
                  ctx_t(0), ctx_t(1), ctx_t(2), pl.BlockSpec((n_ctx, W), lambda b: (cblk + b, ocb)),
                  pl.BlockSpec((n_ctx, LANES), lambda b: (cblk + b, gcb)),
                  pl.BlockSpec((1, LANES), one), pl.BlockSpec((W, 1), one)],
        out_specs=[pl.BlockSpec((S, W), lambda b: (b, 0)), pl.BlockSpec((n_ctx, W), lambda b: (b, 0))],
        out_shape=[jax.ShapeDtypeStruct((B * S, W), BF16), jax.ShapeDtypeStruct((B * n_ctx, W), BF16)],
        scratch_shapes=[pltpu.VMEM((W, n_ctx + S), F32), pltpu.VMEM((W, n_ctx + S), F32)],
        compiler_params=_cparams(("parallel",)),
        name="mlstm",
    )(uT, uT, uT, u, u, uT, uT, uT, u, u, gate_b, norm_g)
    return jnp.concatenate([out_l, out_c], axis=0)


def _da_prep_kernel(q_ref, k_ref, v_ref, cos_ref, sin_ref, qg_ref, kg_ref, seg_ref, qo_ref, ko_ref, vo_ref, *, dh):
    cos = cos_ref[...]
    sin = sin_ref[...]
    seg = seg_ref[...]
    W = q_ref.shape[1]
    lane = lax.broadcasted_iota(jnp.int32, (1, W), 1)
    quarter = dh // 4
    first = (lane % (2 * quarter)) < quarter

    def norm_rope(x, g):
        hi, lo = _split2(x * x)
        ms = (_dot(hi, seg) + _dot(lo, seg)) * (1.0 / dh)
        xn = x * lax.rsqrt(ms + EPS) * g
        rot = jnp.where(first, -pltpu.roll(xn, W - quarter, 1), pltpu.roll(xn, quarter, 1))
        return xn * cos + rot * sin

    qo_ref[...] = (norm_rope(q_ref[...], qg_ref[...]) * (dh ** -0.5 * math.log2(math.e))).astype(BF16)
    ko_ref[...] = norm_rope(k_ref[...], kg_ref[...]).astype(BF16)
    vo_ref[...] = v_ref[...].astype(BF16)


def _da_prep(u, cos, sin, qg, kg, seg, n_lat_rows, S, col0, tm, dh):
    R = u.shape[0]
    W = qg.shape[1]
    cb = col0 // W
    n_lat = n_lat_rows // tm
    per_seq = S // tm
    tab = lambda i: (jnp.where(i < n_lat, i % per_seq, per_seq), 0)
    one = lambda i: (0, 0)
    row = lambda i: (i, 0)

    def ucol(j):
        return pl.BlockSpec((tm, W), lambda i: (i, cb + j))

    return pl.pallas_call(
        functools.partial(_da_prep_kernel, dh=dh),
        grid=(R // tm,),
        in_specs=[ucol(0), ucol(1), ucol(2), pl.BlockSpec((tm, W), tab), pl.BlockSpec((tm, W), tab),
                  pl.BlockSpec((1, W), one), pl.BlockSpec((1, W), one), pl.BlockSpec((W, W), one)],
        out_specs=[pl.BlockSpec((tm, W), row)] * 3,
        out_shape=[jax.ShapeDtypeStruct((R, W), BF16)] * 3,
        compiler_params=_cparams(("parallel",)),
        name="da_prep",
    )(u, u, u, cos, sin, qg, kg, seg)


def _da_attn_kernel(*refs, n_kv, dh, lam_init):
    q_ref = refs[0]
    k_refs = refs[1:1 + n_kv]
    v_refs = refs[1 + n_kv:1 + 2 * n_kv]
    lam_ref, sg_ref, o_ref = refs[1 + 2 * n_kv:]
    lp = lam_ref[...]
    lam = (jnp.exp(jnp.sum(lp[0:1] * lp[1:2], axis=-1, keepdims=True))
           - jnp.exp(jnp.sum(lp[2:3] * lp[3:4], axis=-1, keepdims=True)) + lam_init)
    q = q_ref[...]
    acc = None
    nums, dens = [], []
    for mp in range(2):
        lanes = slice(mp * dh, (mp + 1) * dh)
        s = [_dot_nt(q[:, lanes], k_ref[:, lanes]) for k_ref in k_refs]
        mx = functools.reduce(jnp.maximum, [jnp.max(si, axis=-1, keepdims=True) for si in s])
        p = [jnp.exp2(si - mx) for si in s]
        nums.append(p)
        dens.append(sum(jnp.sum(pi, axis=-1, keepdims=True) for pi in p))
    c = lam * dens[0] / dens[1]
    for j in range(n_kv):
        a = (nums[0][j] - c * nums[1][j]).astype(BF16)
        t = _dot(a, v_refs[j][...])
        acc = t if acc is None else acc + t
    acc = acc / dens[0]
    o = acc * lax.rsqrt(jnp.mean(acc * acc, axis=-1, keepdims=True) + EPS) * sg_ref[...]
    o_ref[...] = (o * (1.0 - lam_init)).astype(o_ref.dtype)


def _da_attention(q, k, v, lam_p, subln_g, lam_init, B, q_rows, q_blk0, kv_segs, tq, dh):
    H = DA_HEADS
    vd = 2 * dh
    nq = q_rows // tq
    q0 = q_blk0

    def kv_spec(rows, blk0):
        return pl.BlockSpec((rows, vd), lambda b, h, i: (blk0 + b, h))

    kspecs = [kv_spec(r, b0) for r, b0 in kv_segs]
    one = lambda b, h, i: (0, 0)
    return pl.pallas_call(
        functools.partial(_da_attn_kernel, n_kv=len(kv_segs), dh=dh, lam_init=lam_init),
        grid=(B, H, nq),
        in_specs=[pl.BlockSpec((tq, vd), lambda b, h, i: (q0 + b * nq + i, h))] + kspecs + kspecs
                 + [pl.BlockSpec(lam_p.shape, one), pl.BlockSpec((1, vd), one)],
        out_specs=pl.BlockSpec((tq, vd), lambda b, h, i: (b * nq + i, h)),
        out_shape=jax.ShapeDtypeStruct((B * q_rows, H * vd), BF16),
        compiler_params=_cparams(("parallel", "parallel", "arbitrary")),
        name="da_attention",
    )(q, *([k] * len(kv_segs)), *([v] * len(kv_segs)), lam_p, subln_g)


def _axial_rope_tables(S, dh, reps, pad_rows):
    rows = S // GRID_W
    r = jnp.repeat(jnp.arange(rows, dtype=F32), GRID_W)
    col = jnp.tile(jnp.arange(GRID_W, dtype=F32), rows)
    n_freq = dh // 4
    inv = ROPE_THETA ** (-jnp.arange(n_freq, dtype=F32) / n_freq)
    ar = r[:, None] * inv
    ac = col[:, None] * inv
    ang = jnp.concatenate([ar, ar, ac, ac], axis=-1)
    cos = jnp.concatenate([jnp.tile(jnp.cos(ang), (1, reps)), jnp.ones((pad_rows, dh * reps), F32)], axis=0)
    sin = jnp.concatenate([jnp.tile(jnp.sin(ang), (1, reps)), jnp.zeros((pad_rows, dh * reps), F32)], axis=0)
    return cos, sin


def _diff_attn_mixer(u, cos, sin, qg, kg, lam_p, subln_g, lam_init, B, S, n_ctx, col0, need_ctx):
    dh = qg.shape[0]
    W = DA_HEADS * 2 * dh
    seg = (jnp.arange(W)[:, None] // dh == jnp.arange(W)[None, :] // dh).astype(BF16)
    tm = 512
    q, k, v = _da_prep(u, cos, sin, jnp.tile(qg, W // dh)[None], jnp.tile(kg, W // dh)[None], seg,
                       B * S, S, col0, tm, dh)
    sg = subln_g[None]
    ctx_blk0 = (B * S) // n_ctx
    tq = 256
    out_l = _da_attention(q, k, v, lam_p, sg, lam_init, B, S, 0, [(n_ctx, ctx_blk0), (S, 0)], tq, dh)
    if not need_ctx:
        return out_l
    out_c = _da_attention(q, k, v, lam_p, sg, lam_init, B, n_ctx, (B * S) // n_ctx, [(n_ctx, ctx_blk0)], n_ctx, dh)
    return jnp.concatenate([out_l, out_c], axis=0)


def _hy_conv_kernel(v_ref, x1_ref, x2_ref, w_ref, b_ref, vo_ref, x1o_ref, x2o_ref):
    L, W = v_ref.shape
    row = lax.broadcasted_iota(jnp.int32, (L, 1), 0)
    for j, (i_ref, o_ref) in enumerate(((v_ref, vo_ref), (x1_ref, x1o_ref), (x2_ref, x2o_ref))):
        lanes = slice(j * W, (j + 1) * W)
        u = i_ref[...]
        prev = jnp.where(row == 0, 0.0, pltpu.roll(u, 1, 0))
        nxt = jnp.where(row == L - 1, 0.0, pltpu.roll(u, L - 1, 0))
        o_ref[...] = prev * w_ref[0:1, lanes] + u * w_ref[1:2, lanes] + nxt * w_ref[2:3, lanes] + b_ref[:, lanes]


def _hy_short_conv(u, conv_w, conv_b, n_seg, L, blk0, col0):
    W = conv_w.shape[1] // 3
    cb = col0 // W
    one = lambda b: (0, 0)

    def ucol(j):
        return pl.BlockSpec((L, W), lambda b: (blk0 + b, cb + j))

    return pl.pallas_call(
        _hy_conv_kernel,
        grid=(n_seg,),
        in_specs=[ucol(0), ucol(1), ucol(2), pl.BlockSpec(conv_w.shape, one), pl.BlockSpec(conv_b.shape, one)],
        out_specs=[pl.BlockSpec((L, W), lambda b: (b, 0))] * 3,
        out_shape=[jax.ShapeDtypeStruct((n_seg * L, W), F32)] * 3,
        compiler_params=_cparams(("parallel",)),
        name="hy_short_conv",
    )(u, u, u, conv_w, conv_b)


def _hy_fwd_kernel(c_ref, s_ref, z_ref, *rest, raw):
    z = z_ref[...].astype(BF16)
    zr = _dot(c_ref[...], z)
    zi = _dot(s_ref[...], z)
    if raw:
        yr_ref, yi_ref = rest
        yr_ref[...] = zr
        yi_ref[...] = zi
    else:
        a_ref, b_ref, d_ref, yr_ref, yi_ref = rest
        yr_ref[...] = (zr * a_ref[...] - zi * b_ref[...]).astype(yr_ref.dtype)
        yi_ref[...] = (zr * b_ref[...] + zi * d_ref[...]).astype(yi_ref.dtype)


def _hy_fwd(cm, sm, z, coefs, n_seg, L, tk):
    W = z.shape[1]
    nk = L // tk
    raw = coefs is None
    mat = pl.BlockSpec((tk, L), lambda i, b: (i, 0))
    cf = pl.BlockSpec((tk, W), lambda i, b: (i, 0))
    out = pl.BlockSpec((tk, W), lambda i, b: (b * nk + i, 0))
    odt = F32 if raw else BF16
    return pl.pallas_call(
        functools.partial(_hy_fwd_kernel, raw=raw),
        grid=(nk, n_seg),
        in_specs=[mat, mat, pl.BlockSpec((L, W), lambda i, b: (b, 0))] + ([] if raw else [cf, cf, cf]),
        out_specs=[out, out],
        out_shape=[jax.ShapeDtypeStruct((n_seg * L, W), odt)] * 2,
        compiler_params=_cparams(("parallel", "arbitrary")),
        name="hy_dft_fwd",
    )(cm, sm, z, *(() if raw else coefs))


def _hy_inv_kernel(c_ref, st_ref, yr_ref, yi_ref, x_ref, vz_ref, skip_ref, o_ref):
    y = _dot(c_ref[...], yr_ref[...]) + _dot(st_ref[...], yi_ref[...])
    o_ref[...] = (x_ref[...] * (y + skip_ref[...] * vz_ref[...])).astype(o_ref.dtype)


def _hy_inv(cm, smt, yr, yi, xg, vz, skip, n_seg, L, tt, out_dtype):
    W = yr.shape[1]
    nt = L // tt
    mat = pl.BlockSpec((tt, L), lambda i, b: (i, 0))
    seq = pl.BlockSpec((L, W), lambda i, b: (b, 0))
    row = pl.BlockSpec((tt, W), lambda i, b: (b * nt + i, 0))
    return pl.pallas_call(
        _hy_inv_kernel,
        grid=(nt, n_seg),
        in_specs=[mat, mat, seq, seq, row, row, pl.BlockSpec((1, W), lambda i, b: (0, 0))],
        out_specs=row,
        out_shape=jax.ShapeDtypeStruct((n_seg * L, W), out_dtype),
        compiler_params=_cparams(("parallel", "arbitrary")),
        name="hy_dft_inv",
    )(cm, smt, yr, yi, xg, vz, skip)


def _dft_mats(L):
    k = jnp.arange(L, dtype=jnp.int32)
    kn = (k[:, None] * k[None, :]) % (2 * L)
    ang = kn.astype(F32) * (math.pi / L)
    cm = jnp.cos(ang).astype(BF16)
    sm = (-jnp.sin(ang)).astype(BF16)
    nyq = jnp.where(k % 2 == 0, 1.0, -1.0).astype(BF16)
    return cm, sm.at[0].set(nyq), sm.at[:, 0].set(nyq)


def _hyena_filters(L, w1, b1, w2, b2, w3, freq, W):
    t01 = jnp.linspace(0.0, 1.0, L, dtype=F32)[:, None]
    wpos = (2.0 * math.pi / L) * jnp.arange(L, dtype=F32)[:, None]
    bands = jnp.linspace(1e-4, HY_BANDS - 1, HY_BANDS, dtype=F32)
    feats = jnp.concatenate([t01, jnp.cos(wpos * bands), -jnp.sin(wpos * bands)], axis=-1)
    hp = lax.Precision.HIGHEST
    h = jnp.sin(freq[0] * (jnp.dot(feats, w1, precision=hp) + b1))
    h = jnp.sin(freq[1] * (jnp.dot(h, w2, precision=hp) + b2))
    h = jnp.dot(h, w3, precision=hp).reshape(L, HY_ORDER, 2, W)
    deltas = jnp.abs(jnp.linspace(math.log(HY_TARGET) / HY_SLOW, math.log(HY_TARGET) / HY_FAST, W, dtype=F32))
    h = h * (jnp.exp(-t01 * deltas) + HY_SHIFT)[:, None, None, :]
    hf, hb = h[:, :, 0], h[:, :, 1]
    hf = hf.at[0].add(hb[0])
    hb = hb.at[0].set(0.0)
    scale = lax.rsqrt(jnp.sum(hf * hf, axis=0, keepdims=True) + jnp.sum(hb * hb, axis=0, keepdims=True) + EPS)
    return (hf * scale).reshape(L, HY_ORDER * W), (hb * scale).reshape(L, HY_ORDER * W)


def _hyena_spectrum(mats, L, w1, b1, w2, b2, w3, freq, W):
    cm, sm, _ = mats
    hf, hb = _hyena_filters(L, w1, b1, w2, b2, w3, freq, W)
    cols = jnp.concatenate([hf[:, :W], hf[:, W:], hb[:, :W], hb[:, W:]], axis=0)
    tk = min(L, 512)
    gr, gi = _hy_fwd(cm, sm, cols, None, 2 * HY_ORDER, L, tk)
    gr = gr.reshape(2, HY_ORDER, L, W)
    gi = gi.reshape(2, HY_ORDER, L, W)
    kr = gr[0] + gr[1]
    ki = gi[0] - gi[1]
    nyq = gi[0, :, 0] + gi[1, :, 0]
    n = 2.0 * L
    wk = jnp.full((L, 1), 2.0 / n, F32).at[0].set(1.0 / n)
    a = kr * wk
    bm = (ki * wk).at[:, 0].set(0.0)
    dd = a.at[:, 0].set(nyq / n)
    return [(a[o], bm[o], dd[o]) for o in range(HY_ORDER)]


def _hyena_seq(u, mats, spec, conv_w, conv_b, skip, n_seg, L, blk0, col0):
    cm, sm, smt = mats
    t = min(L, 512)
    v, x1, x2 = _hy_short_conv(u, conv_w, conv_b, n_seg, L, blk0, col0)
    yr, yi = _hy_fwd(cm, sm, v, spec[0], n_seg, L, t)
    z = _hy_inv(cm, smt, yr, yi, x1, v, skip[0:1], n_seg, L, t, F32)
    yr, yi = _hy_fwd(cm, sm, z, spec[1], n_seg, L, t)
    return _hy_inv(cm, smt, yr, yi, x2, z, skip[1:2], n_seg, L, t, BF16)


def _pack_rows(x):
    n = x.shape[1] // 2
    lo = pltpu.bitcast(x[:, :n].astype(BF16).astype(F32), jnp.uint32)
    hi = pltpu.bitcast(x[:, n:].astype(BF16).astype(F32), jnp.uint32)
    return (lo >> 16) | (hi & jnp.uint32(0xFFFF0000))


def _unpack_rows(p):
    return pltpu.bitcast(p << 16, F32), pltpu.bitcast(p & jnp.uint32(0xFFFF0000), F32)


def _moe_router_kernel(x_ref, g_ref, sc_ref, sh_ref, whi_ref, wlo_ref, rb_ref, h_ref, idx_ref, wts_ref, cnt_ref):
    x = x_ref[...]
    y = x * lax.rsqrt(jnp.mean(x * x, axis=-1, keepdims=True) + EPS) * g_ref[...]
    h = y * (1.0 + sc_ref[0]) + sh_ref[0]
    h_ref[...] = _pack_rows(h)
    hi, lo = _split2(h)
    logits = _dot_nt(whi_ref[...], hi) + _dot_nt(wlo_ref[...], hi) + _dot_nt(whi_ref[...], lo)
    scores = jax.nn.sigmoid(logits)
    sel = scores + rb_ref[...]
    E, tm = sel.shape
    gsz = E // N_GROUPS
    neg = -jnp.inf
    erow = lax.broadcasted_iota(jnp.int32, (E, 1), 0).astype(F32)
    grow = lax.broadcasted_iota(jnp.int32, (gsz, 1), 0).astype(F32)
    blocks = [sel[g * gsz:(g + 1) * gsz] for g in range(N_GROUPS)]
    gscore = []
    for blk in blocks:
        m1 = jnp.max(blk, axis=0, keepdims=True)
        first = jnp.min(jnp.where(blk == m1, grow, float(gsz)), axis=0, keepdims=True)
        m2 = jnp.max(jnp.where(grow == first, neg, blk), axis=0, keepdims=True)
        gscore.append(m1 + m2)
    kept = []
    for g in range(N_GROUPS):
        rank = jnp.zeros((1, tm), F32)
        for o in range(N_GROUPS):
            if o != g:
                ahead = (gscore[o] >= gscore[g]) if o < g else (gscore[o] > gscore[g])
                rank = rank + jnp.where(ahead, 1.0, 0.0)
        kept.append(jnp.where(rank < TOPK_GROUPS, blocks[g], neg))
    work = jnp.concatenate(kept, axis=0)
    ids, ws = [], []
    total = jnp.zeros((1, tm), F32)
    chosen = jnp.zeros((E, tm), F32)
    for j in range(TOP_K):
        mx = jnp.max(work, axis=0, keepdims=True)
        am = jnp.min(jnp.where(work == mx, erow, float(E)), axis=0, keepdims=True)
        hit = erow == am
        wj = jnp.sum(jnp.where(hit, scores, 0.0), axis=0, keepdims=True)
        work = jnp.where(hit, neg, work)
        chosen = jnp.where(hit, 1.0, chosen)
        ids.append(am)
        ws.append(wj)
        total = total + wj
    idx_ref[...] = jnp.concatenate(ids, axis=0).astype(jnp.int32)
    wts_ref[...] = jnp.concatenate(ws, axis=0) / total * ROUTED_SCALE
    ones = jnp.ones((8, tm), BF16)
    cnt_ref[0] = _dot_nt(ones, chosen.astype(BF16))[0:1]


def _moe_router(x, n_tok, g, scale, shift, whi, wlo, rb, seg_rows, tm):
    D = x.shape[1]
    E = whi.shape[0]
    K = TOP_K
    last = scale.shape[0] - 1
    row = lambda i: (i, 0)
    col = lambda i: (0, i)
    one = lambda i: (0, 0)
    mod = lambda i: (jnp.minimum(i * tm // seg_rows, last), 0, 0)
    return pl.pallas_call(
        _moe_router_kernel,
        grid=(n_tok // tm,),
        in_specs=[pl.BlockSpec((tm, D), row), pl.BlockSpec((1, D), one), pl.BlockSpec((1, 1, D), mod),
                  pl.BlockSpec((1, 1, D), mod), pl.BlockSpec((E, D), one), pl.BlockSpec((E, D), one),
                  pl.BlockSpec((E, 1), one)],
        out_specs=[pl.BlockSpec((tm, D // 2), row), pl.BlockSpec((K, tm), col), pl.BlockSpec((K, tm), col),
                   pl.BlockSpec((1, 1, E), lambda i: (i, 0, 0))],
        out_shape=[jax.ShapeDtypeStruct((n_tok, D // 2), jnp.uint32), jax.ShapeDtypeStruct((K, n_tok), jnp.int32),
                   jax.ShapeDtypeStruct((K, n_tok), F32), jax.ShapeDtypeStruct((n_tok // tm, 1, E), F32)],
        compiler_params=_cparams(("parallel",)),
        name="moe_router",
    )(x, g, scale, shift, whi, wlo, rb)


def _moe_pos_kernel(idx_ref, base_ref, pos_ref):
    K, tm = idx_ref.shape
    E = base_ref.shape[1]
    idx = idx_ref[...]
    erow = lax.broadcasted_iota(jnp.int32, (E, 1), 0)
    hits = [erow == idx[j:j + 1, :] for j in range(K)]
    onehot = jnp.zeros((E, tm), F32)
    for hit in hits:
        onehot = jnp.where(hit, 1.0, onehot)
    row = lax.broadcasted_iota(jnp.int32, (tm, tm), 0)
    col = lax.broadcasted_iota(jnp.int32, (tm, tm), 1)
    before = _dot(onehot.astype(BF16), (row < col).astype(BF16))
    dest = before + base_ref[0]
    pos = [jnp.sum(jnp.where(hit, dest, 0.0), axis=0, keepdims=True) for hit in hits]
    pos_ref[...] = jnp.concatenate(pos, axis=0).astype(jnp.int32)


def _moe_plan(idx, cnt, bm, tm):
    K, n_tok = idx.shape
    E = cnt.shape[2]
    cnt = cnt[:, 0, :]
    counts = jnp.sum(cnt, axis=0)
    pcounts = jnp.ceil(counts / bm) * bm
    pends = jnp.cumsum(pcounts)
    base = (pends - pcounts)[None, :] + jnp.cumsum(cnt, axis=0) - cnt
    pos = pl.pallas_call(
        _moe_pos_kernel,
        grid=(n_tok // tm,),
        in_specs=[pl.BlockSpec((K, tm), lambda i: (0, i)), pl.BlockSpec((1, E, 1), lambda i: (i, 0, 0))],
        out_specs=pl.BlockSpec((K, tm), lambda i: (0, i)),
        out_shape=jax.ShapeDtypeStruct((K, n_tok), jnp.int32),
        compiler_params=_cparams(("parallel",)),
        name="moe_positions",
    )(idx, base[:, :, None])
    n_blocks = -(-(n_tok * K + E * (bm - 1)) // bm)
    block_start = jnp.concatenate([jnp.zeros((1,), F32), pends / bm]).astype(jnp.int32)
    return pos.T.reshape(-1), block_start, n_blocks


def _per_token_rows(tm, K, copy):
    def start(t, carry):
        for j in range(K):
            copy(t, j).start(priority=j % 2)
        return carry

    def wait(t, carry):
        for j in range(K):
            copy(t, j).wait()
        return carry

    return (lambda: lax.fori_loop(0, tm, start, 0, unroll=2)), (lambda: lax.fori_loop(0, tm, wait, 0, unroll=2))


def _moe_dispatch_kernel(pos_ref, prev_ref, h_ref, xs_in_ref, xs_hbm, stage_ref, sems, *, K):
    del xs_in_ref
    tm = h_ref.shape[0]
    i = pl.program_id(0)
    last = pl.num_programs(0) - 1
    slot = lax.bitwise_and(i, 1)

    def tile_copies(s, p_ref):
        def row_copy(t, j):
            return pltpu.make_async_copy(stage_ref.at[s, pl.ds(t, 1)], xs_hbm.at[pl.ds(p_ref[t * K + j], 1)],
                                         sems.at[s])
        return _per_token_rows(tm, K, row_copy)

    start, wait = tile_copies(slot, pos_ref)
    _, wait_prev = tile_copies(1 - slot, prev_ref)
    stage_ref[slot] = h_ref[...]
    start()

    @pl.when(i > 0)
    def _():
        wait_prev()

    @pl.when(i == last)
    def _():
        wait()


def _moe_dispatch(pos, h, n_rows, K, tm):
    n_tok, W = h.shape
    xs0 = jnp.zeros((n_rows, W), h.dtype)
    return pl.pallas_call(
        functools.partial(_moe_dispatch_kernel, K=K),
        grid=(n_tok // tm,),
        in_specs=[pl.BlockSpec((tm * K,), lambda i: (i,), memory_space=pltpu.SMEM),
                  pl.BlockSpec((tm * K,), lambda i: (jnp.maximum(i - 1, 0),), memory_space=pltpu.SMEM),
                  pl.BlockSpec((tm, W), lambda i: (i, 0)),
                  pl.BlockSpec(memory_space=pl.ANY)],
        out_specs=pl.BlockSpec(memory_space=pl.ANY),
        out_shape=jax.ShapeDtypeStruct((n_rows, W), h.dtype),
        scratch_shapes=[pltpu.VMEM((2, tm, W), h.dtype), pltpu.SemaphoreType.DMA((2,))],
        input_output_aliases={3: 0},
        compiler_params=_cparams(("arbitrary",)),
        name="moe_dispatch",
    )(pos, pos, h, xs0)


def _moe_expert_kernel(bs_ref, x_hbm, w1_ref, w3_ref, w2_ref, y_hbm, xbuf, ybuf, w1b_ref, w3b_ref, w2b_ref,
                       xsem, ysem, *, bm, nbuf, n_blocks):
    e = pl.program_id(0)
    last = pl.num_programs(0) - 1
    b0 = bs_ref[e]
    b1 = bs_ref[e + 1]
    n_used = bs_ref[last + 1]

    def x_copy(g):
        slot = lax.bitwise_and(g, nbuf - 1)
        return pltpu.make_async_copy(x_hbm.at[pl.ds(g * bm, bm)], xbuf.at[slot], xsem.at[slot])

    def y_copy(g):
        slot = lax.bitwise_and(g, nbuf - 1)
        return pltpu.make_async_copy(ybuf.at[slot], y_hbm.at[pl.ds(g * bm, bm)], ysem.at[slot])

    @pl.when(e == 0)
    def _():
        for p in range(nbuf - 1):
            @pl.when(p < n_used)
            def _():
                x_copy(p).start()

    @pl.when(b1 > b0)
    def _():
        w1b_ref[...] = w1_ref[0, 0].astype(BF16)
        w3b_ref[...] = w3_ref[0, 0].astype(BF16)
        w2b_ref[...] = w2_ref[0, 0].astype(BF16)

    def block(g, carry):
        slot = lax.bitwise_and(g, nbuf - 1)
        x_copy(g).wait()

        @pl.when(g + nbuf - 1 < n_used)
        def _():
            x_copy(g + nbuf - 1).start()

        @pl.when(g >= nbuf)
        def _():
            y_copy(g - nbuf).wait()

        x = jnp.concatenate(_unpack_rows(xbuf[slot]), axis=1).astype(BF16)
        a = _dot(x, w1b_ref[...])
        b = _dot(x, w3b_ref[...])
        ybuf[slot] = _pack_rows(_dot((jax.nn.silu(a) * b).astype(BF16), w2b_ref[...]))
        y_copy(g).start()
        return carry

    lax.fori_loop(b0, b1, block, 0)

    @pl.when(e == last)
    def _():
        def drain(g, carry):
            y_copy(g).wait()
            return carry

        lax.fori_loop(jnp.maximum(n_used - nbuf, 0), n_used, drain, 0)
        ybuf[0] = jnp.zeros(ybuf.shape[1:], ybuf.dtype)

        def zero_copy(g):
            return pltpu.make_async_copy(ybuf.at[0], y_hbm.at[pl.ds(g * bm, bm)], ysem.at[0])

        def fill(g, carry):
            zero_copy(g).start()
            return carry

        def fill_wait(g, carry):
            zero_copy(g).wait()
            return carry

        lax.fori_loop(n_used, n_blocks, fill, 0)
        lax.fori_loop(n_used, n_blocks, fill_wait, 0)


def _moe_experts(block_start, xs, w1, w3, w2, layer, bm):
    P, W = xs.shape
    E, D, F = w1.shape[1:]
    nbuf = 4
    grid_spec = pltpu.PrefetchScalarGridSpec(
        num_scalar_prefetch=1,
        grid=(E,),
        in_specs=[pl.BlockSpec(memory_space=pl.ANY),
                  pl.BlockSpec((1, 1, D, F), lambda e, bs: (layer, e, 0, 0)),
                  pl.BlockSpec((1, 1, D, F), lambda e, bs: (layer, e, 0, 0)),
                  pl.BlockSpec((1, 1, F, D), lambda e, bs: (layer, e, 0, 0))],
        out_specs=pl.BlockSpec(memory_space=pl.ANY),
        scratch_shapes=[pltpu.VMEM((nbuf, bm, W), xs.dtype), pltpu.VMEM((nbuf, bm, W), xs.dtype),
                        pltpu.VMEM((D, F), BF16), pltpu.VMEM((D, F), BF16), pltpu.VMEM((F, D), BF16),
                        pltpu.SemaphoreType.DMA((nbuf,)), pltpu.SemaphoreType.DMA((nbuf,))],
    )
    return pl.pallas_call(
        functools.partial(_moe_expert_kernel, bm=bm, nbuf=nbuf, n_blocks=P // bm),
        grid_spec=grid_spec,
        out_shape=jax.ShapeDtypeStruct((P, W), xs.dtype),
        compiler_params=_cparams(("arbitrary",)),
        name="moe_experts",
    )(block_start, xs, w1, w3, w2)


def _moe_combine_kernel(pos_ref, next_ref, wts_ref, x_ref, h_ref, sw1_ref, sw3_ref, sw2_ref, gate_ref, ys_hbm, o_ref,
                        gath_ref, sems, *, K):
    tm = x_ref.shape[0]
    i = pl.program_id(0)
    last = pl.num_programs(0) - 1

    slot = lax.bitwise_and(i, 1)

    def tile_copies(s, p_ref):
        def row_copy(t, j):
            return pltpu.make_async_copy(ys_hbm.at[pl.ds(p_ref[t * K + j], 1)], gath_ref.at[s, j, pl.ds(t, 1)],
                                         sems.at[s])
        return _per_token_rows(tm, K, row_copy)

    start, wait = tile_copies(slot, pos_ref)
    start_next, _ = tile_copies(1 - slot, next_ref)

    @pl.when(i == 0)
    def _():
        start()

    @pl.when(i < last)
    def _():
        start_next()

    h = jnp.concatenate(_unpack_rows(h_ref[...]), axis=1).astype(BF16)
    mid = jax.nn.silu(_dot(h, sw1_ref[...])) * _dot(h, sw3_ref[...])
    shared = _dot(mid.astype(BF16), sw2_ref[...])
    wait()
    wts = wts_ref[...]
    W = gath_ref.shape[3]
    acc_lo = shared[:, :W]
    acc_hi = shared[:, W:]
    for j in range(K):
        lo, hi = _unpack_rows(gath_ref[slot, j])
        acc_lo = acc_lo + wts[:, j:j + 1] * lo
        acc_hi = acc_hi + wts[:, j:j + 1] * hi
    gate = gate_ref[0]
    o_ref[:, :W] = x_ref[:, :W] + gate[:, :W] * acc_lo
    o_ref[:, W:] = x_ref[:, W:] + gate[:, W:] * acc_hi


def _moe_combine(pos, wts, x, h, sw1, sw3, sw2, gate, ys, n_tok, K, seg_rows, tm):
    D = x.shape[1]
    W = ys.shape[1]
    n_tiles = n_tok // tm
    last = gate.shape[0] - 1
    row = lambda i: (i, 0)
    one = lambda i: (0, 0)
    return pl.pallas_call(
        functools.partial(_moe_combine_kernel, K=K),
        grid=(n_tiles,),
        in_specs=[pl.BlockSpec((tm * K,), lambda i: (i,), memory_space=pltpu.SMEM),
                  pl.BlockSpec((tm * K,), lambda i: (jnp.minimum(i + 1, n_tiles - 1),), memory_space=pltpu.SMEM),
                  pl.BlockSpec((tm, K), row), pl.BlockSpec((tm, D), row), pl.BlockSpec((tm, W), row),
                  pl.BlockSpec(sw1.shape, one), pl.BlockSpec(sw3.shape, one), pl.BlockSpec(sw2.shape, one),
                  pl.BlockSpec((1, 1, D), lambda i: (jnp.minimum(i * tm // seg_rows, last), 0, 0)),
                  pl.BlockSpec(memory_space=pl.ANY)],
        out_specs=pl.BlockSpec((tm, D), row),
        out_shape=jax.ShapeDtypeStruct((n_tok, D), F32),
        scratch_shapes=[pltpu.VMEM((2, K, tm, W), ys.dtype), pltpu.SemaphoreType.DMA((2,))],
        compiler_params=_cparams(("arbitrary",)),
        name="moe_combine",
    )(pos, pos, wts, x, h, sw1, sw3, sw2, gate, ys)


def _moe_ffn_residual(x, n_tok, norm_g, scale, shift, gate, router_w, router_b, ew1, ew3, ew2, layer, sw1, sw3, sw2,
                      seg_rows, bm):
    K = TOP_K
    tm = 256
    whi, wlo = _split2(router_w.T)
    h, idx, wts, cnt = _moe_router(x, n_tok, norm_g, scale, shift, whi, wlo, router_b[:, None], seg_rows, tm)
    pos, block_start, n_blocks = _moe_plan(idx, cnt, bm, tm)
    wts = wts.T
    xs = _moe_dispatch(pos, h, n_blocks * bm, K, 128)
    ys = _moe_experts(block_start, xs, ew1, ew3, ew2, layer, bm)
    return _moe_combine(pos, wts, x, h, sw1.astype(BF16), sw3.astype(BF16), sw2.astype(BF16), gate, ys,
                        n_tok, K, seg_rows, 128)


def kernel(x, c, ctx, c_ctx, w_mod, b_mod, norm1_g, norm2_g, w_in, mlstm_gate_b, mlstm_norm_g, da_qnorm_g, da_knorm_g, da_lambda, da_subln_g, hy_conv_w, hy_conv_b, hy_w1, hy_b1, hy_w2, hy_b2, hy_w3, hy_freq, hy_skip, w_out, router_w, router_b, exp_w1, exp_w3, exp_w2, sh_w1, sh_w3, sh_w2):
    B, S, D = x.shape
    n_ctx = ctx.shape[1]
    depth = w_in.shape[0]
    n_lat = B * S
    ml_w = mlstm_norm_g.shape[1]
    da_dh = da_qnorm_g.shape[1]
    da_w = DA_HEADS * 2 * da_dh
    hy_w = hy_skip.shape[2]
    n_gates = 4 * ML_HEADS
    da_col = 0
    hy_col = da_col + 3 * da_w
    o_col = hy_col + 3 * hy_w
    gate_col = o_col + ml_w
    ml_in = 4 * ml_w + n_gates
    tm = 512

    X = jnp.concatenate([x.reshape(n_lat, D), ctx.reshape(B * n_ctx, D)], axis=0)
    sc = jax.nn.silu(jnp.concatenate([c, c_ctx[None]], axis=0))
    cos, sin = _axial_rope_tables(S, da_dh, da_w // da_dh, tm)
    mats_l = _dft_mats(S)
    mats_c = _dft_mats(n_ctx)
    for l in range(depth):
        last = l == depth - 1
        lam_init = 0.8 - 0.6 * math.exp(-0.3 * l)
        mods = (jnp.dot(sc, w_mod[l], precision=lax.Precision.HIGHEST) + b_mod[l]).reshape(B + 1, 6, 1, D)
        sh1, s1, g1, sh2, s2, g2 = [mods[:, i] for i in range(6)]
        wl = w_in[l]
        w_big = jnp.concatenate([wl[:, ml_in:], wl[:, 3 * ml_w:4 * ml_w], wl[:, 4 * ml_w:ml_in],
                                 jnp.zeros((D, LANES - n_gates), F32)], axis=1).astype(BF16)
        w_qkv_t = wl[:, :3 * ml_w].T.astype(BF16)
        U, UT = _norm_mod_matmul(X, norm1_g[l][None], s1, sh1, w_big, w_qkv_t, S, tm, w_big.shape[1] // 3)
        gb = jnp.concatenate([mlstm_gate_b[l], jnp.zeros((LANES - n_gates,), F32)])[None]
        m_out = _mlstm_mixer(UT, U, gb, mlstm_norm_g[l][:, None], B, S, n_ctx, o_col, gate_col)
        d_out = _diff_attn_mixer(U, cos, sin, da_qnorm_g[l], da_knorm_g[l], da_lambda[l], da_subln_g[l], lam_init,
                                 B, S, n_ctx, da_col, not last)
        hy_args = (hy_w1[l], hy_b1[l], hy_w2[l], hy_b2[l], hy_w3[l], hy_freq[l], hy_w)
        y_out = _hyena_seq(U, mats_l, _hyena_spectrum(mats_l, S, *hy_args), hy_conv_w[l], hy_conv_b[l][None],
                           hy_skip[l], B, S, 0, hy_col)
        n_rows = n_lat
        if not last:
            y_ctx = _hyena_seq(U, mats_c, _hyena_spectrum(mats_c, n_ctx, *hy_args), hy_conv_w[l], hy_conv_b[l][None],
                               hy_skip[l], B, n_ctx, n_lat // n_ctx, hy_col)
            y_out = jnp.concatenate([y_out, y_ctx], axis=0)
            n_rows = n_lat + B * n_ctx
        wo = w_out[l].astype(BF16)
        X = _out_proj_residual(m_out, d_out, y_out, wo[:ml_w], wo[ml_w:ml_w + da_w], wo[ml_w + da_w:], X, g1,
                               n_rows, S, tm)
        X = _moe_ffn_residual(X, n_rows, norm2_g[l][None], s2, sh2, g2, router_w[l], router_b[l],
                              exp_w1, exp_w3, exp_w2, l, sh_w1[l], sh_w3[l], sh_w2[l], S,
                              MOE_BLOCK)
    return X[:n_lat].reshape(B, S, D)
```

```python
import functools
import math

import jax
import jax.numpy as jnp
from jax import lax
from jax.experimental import pallas as pl
from jax.experimental.pallas import tpu as pltpu

F32 = jnp.float32
BF16 = jnp.bfloat16

EPS = 1e-6
GRID_W = 64
ROPE_THETA = 10000.0
ML_HEADS = 4
ML_CHUNK = 256
ML_M_INIT = -1e30
DA_HEADS = 4
HY_ORDER = 2
HY_BANDS = 8
HY_SHIFT = 0.05
HY_TARGET = 1e-2
HY_FAST = 0.3
HY_SLOW = 1.5
N_GROUPS = 8
TOPK_GROUPS = 4
TOP_K = 8
ROUTED_SCALE = 2.5
MOE_BLOCK = 128
LANES = 128
VMEM_LIMIT = 56 * 1024 * 1024


def _cparams(sem):
    return pltpu.CompilerParams(dimension_semantics=sem, vmem_limit_bytes=VMEM_LIMIT)


def _dot(a, b):
    return jnp.dot(a, b, preferred_element_type=F32)


def _dot_nt(a, b):
    return lax.dot_general(a, b, (((1,), (1,)), ((), ())), preferred_element_type=F32)


def _dot_tn(a, b):
    return lax.dot_general(a, b, (((0,), (0,)), ((), ())), preferred_element_type=F32)


def _split3(a):
    hi = a.astype(BF16)
    r = a - hi.astype(F32)
    mid = r.astype(BF16)
    lo = (r - mid.astype(F32)).astype(BF16)
    return hi, mid, lo


def _split2(a):
    hi = a.astype(BF16)
    lo = (a - hi.astype(F32)).astype(BF16)
    return hi, lo


def _norm_mod_mm_kernel(x_ref, g_ref, sc_ref, sh_ref, w_ref, wt_ref, o_ref, ot_ref, xn_ref):
    @pl.when(pl.program_id(1) == 0)
    def _():
        x = x_ref[...]
        y = x * lax.rsqrt(jnp.mean(x * x, axis=-1, keepdims=True) + EPS) * g_ref[...]
        xn_ref[...] = (y * (1.0 + sc_ref[0]) + sh_ref[0]).astype(BF16)
        ot_ref[...] = _dot_nt(wt_ref[...], xn_ref[...])

    o_ref[...] = _dot(xn_ref[...], w_ref[...])


def _norm_mod_matmul(x, g, scale, shift, w, wt, seg_rows, tm, tn):
    R, D = x.shape
    N = w.shape[1]
    NT = wt.shape[0]
    last = scale.shape[0] - 1
    mod_map = lambda i, j: (jnp.minimum(i * tm // seg_rows, last), 0, 0)
    return pl.pallas_call(
        _norm_mod_mm_kernel,
        grid=(R // tm, N // tn),
        in_specs=[
            pl.BlockSpec((tm, D), lambda i, j: (i, 0)),
            pl.BlockSpec((1, D), lambda i, j: (0, 0)),
            pl.BlockSpec((1, 1, D), mod_map),
            pl.BlockSpec((1, 1, D), mod_map),
            pl.BlockSpec((D, tn), lambda i, j: (0, j)),
            pl.BlockSpec((NT, D), lambda i, j: (0, 0)),
        ],
        out_specs=[pl.BlockSpec((tm, tn), lambda i, j: (i, j)), pl.BlockSpec((NT, tm), lambda i, j: (0, i))],
        out_shape=[jax.ShapeDtypeStruct((R, N), F32), jax.ShapeDtypeStruct((NT, R), F32)],
        scratch_shapes=[pltpu.VMEM((tm, D), BF16)],
        compiler_params=_cparams(("parallel", "arbitrary")),
        name="norm_mod_matmul",
    )(x, g, scale, shift, w, wt)


def _out_proj_kernel(m_ref, d_ref, y_ref, wm_ref, wd_ref, wy_ref, x_ref, gate_ref, o_ref):
    acc = _dot(m_ref[...], wm_ref[...]) + _dot(d_ref[...], wd_ref[...]) + _dot(y_ref[...], wy_ref[...])
    o_ref[...] = x_ref[...] + gate_ref[0] * acc


def _out_proj_residual(m, d, y, wm, wd, wy, x, gate, n_rows, seg_rows, tm):
    R, D = n_rows, x.shape[1]
    last = gate.shape[0] - 1
    row = lambda i: (i, 0)
    full = lambda i: (0, 0)
    return pl.pallas_call(
        _out_proj_kernel,
        grid=(R // tm,),
        in_specs=[
            pl.BlockSpec((tm, m.shape[1]), row),
            pl.BlockSpec((tm, d.shape[1]), row),
            pl.BlockSpec((tm, y.shape[1]), row),
            pl.BlockSpec(wm.shape, full),
            pl.BlockSpec(wd.shape, full),
            pl.BlockSpec(wy.shape, full),
            pl.BlockSpec((tm, D), row),
            pl.BlockSpec((1, 1, D), lambda i: (jnp.minimum(i * tm // seg_rows, last), 0, 0)),
        ],
        out_specs=pl.BlockSpec((tm, D), row),
        out_shape=jax.ShapeDtypeStruct((R, D), F32),
        compiler_params=_cparams(("parallel",)),
        name="out_proj_residual",
    )(m, d, y, wm, wd, wy, x, gate)


def _log_sigmoid(x):
    return jnp.minimum(x, 0.0) - jnp.log1p(jnp.exp(-jnp.abs(x)))


def _mlstm_gate_tables(g_ref, r0, L, gb, tril, triu):
    g = g_ref[pl.ds(r0, L), :] + gb
    lf = _log_sigmoid(g)
    gT = g.T
    lfT = lf.T
    parts = _split3(lf)
    partsT = _split3(lfT)
    cs_f = sum(_dot(tril, p) for p in parts)
    cs_b = sum(_dot(triu, p) for p in parts)
    rs_f = sum(_dot(p, triu) for p in partsT)
    rs_b = sum(_dot(p, tril) for p in partsT)
    return g, gT, cs_f, cs_b, rs_f, rs_b


def _mlstm_chunk(qT, kT, vT, i_row, b_row, c_col, b_end, mask, state):
    C, n, m = state
    qb = qT.astype(BF16)
    kb = kT.astype(BF16)
    dmat = jnp.where(mask, b_row + c_col, -jnp.inf)
    inter = b_row + m
    m_t = jnp.maximum(inter, jnp.max(dmat, axis=0, keepdims=True))
    s = _dot_tn(kb, qb) * jnp.exp(dmat - m_t)
    carry_w = jnp.exp(inter - m_t)
    num = _dot(vT.astype(BF16), s.astype(BF16)) + carry_w * _dot(C.astype(BF16), qb)
    den = jnp.sum(s, axis=0, keepdims=True) + carry_w * jnp.sum(qT * n, axis=0, keepdims=True)
    h = num / jnp.maximum(jnp.abs(den), jnp.exp(-m_t))
    g = b_end - b_row + i_row
    m_new = jnp.maximum(b_end + m, jnp.max(g, axis=-1, keepdims=True))
    ws = jnp.exp(g - m_new)
    decay = jnp.exp(b_end + m - m_new)
    C_new = decay * C + _dot_nt((vT * ws).astype(BF16), kb)
    n_new = decay * n + jnp.sum(kT * ws, axis=-1, keepdims=True)
    return h, (C_new, n_new, m_new)


def _mlstm_kernel(ql_ref, kl_ref, vl_ref, ol_ref, gl_ref, qc_ref, kc_ref, vc_ref, oc_ref, gc_ref,
                  gb_ref, ng_ref, outl_ref, outc_ref, hf_ref, hb_ref, *, L, H, dh):
    S = ql_ref.shape[1]
    n_ctx = qc_ref.shape[1]
    row = lax.broadcasted_iota(jnp.int32, (L, L), 0)
    col = lax.broadcasted_iota(jnp.int32, (L, L), 1)
    lower = col <= row
    upper = col >= row
    tril = lower.astype(BF16)
    triu = upper.astype(BF16)
    gb = gb_ref[...]
    k_scale = dh ** -0.5

    def both_dirs(refs_f, r0_f, refs_b, r0_b, hoff, state):
        new_state = []
        for d, (refs, r0, h_ref) in enumerate(((refs_f, r0_f, hf_ref), (refs_b, r0_b, hb_ref))):
            q_ref, k_ref, v_ref, g_ref = refs
            g, gT, cs_f, cs_b, rs_f, rs_b = _mlstm_gate_tables(g_ref, r0, L, gb, tril, triu)
            cs, rs, mask = (cs_f, rs_f, upper) if d == 0 else (cs_b, rs_b, lower)
            end = L - 1 if d == 0 else 0
            for hh in range(H):
                ic = 2 * d * H + hh
                fc = ic + H
                rows = slice(hh * dh, (hh + 1) * dh)
                qT = q_ref[rows, pl.ds(r0, L)]
                kT = k_ref[rows, pl.ds(r0, L)] * k_scale
                vT = v_ref[rows, pl.ds(r0, L)]
                h, st = _mlstm_chunk(qT, kT, vT, gT[ic:ic + 1, :], rs[fc:fc + 1, :],
                                     g[:, ic:ic + 1] - cs[:, fc:fc + 1], rs[fc:fc + 1, end:end + 1], mask,
                                     state[d * H + hh])
                off = hoff + r0
                h_ref[rows, pl.ds(off if isinstance(off, int) else pl.multiple_of(off, L), L)] = h
                new_state.append(st)
        return tuple(new_state)

    state = tuple((jnp.zeros((dh, dh), F32), jnp.zeros((dh, 1), F32), jnp.full((1, 1), ML_M_INIT, F32))
                  for _ in range(2 * H))
    ctx_refs = (qc_ref, kc_ref, vc_ref, gc_ref)
    lat_refs = (ql_ref, kl_ref, vl_ref, gl_ref)
    n_cc = n_ctx // L
    for c in range(n_cc):
        state = both_dirs(ctx_refs, c * L, ctx_refs, (n_cc - 1 - c) * L, 0, state)
    n_lc = S // L

    def body(c, st):
        r_f = pl.multiple_of(c * L, L)
        r_b = pl.multiple_of((n_lc - 1 - c) * L, L)
        return both_dirs(lat_refs, r_f, lat_refs, r_b, n_ctx, st)

    lax.fori_loop(0, n_lc, body, state)

    def finish(o_ref, out_ref, hoff, n_rows):
        def fbody(c, carry):
            r0 = pl.multiple_of(c * L, L)
            off = pl.multiple_of(hoff + r0, L)
            hs = hf_ref[:, pl.ds(off, L)] + hb_ref[:, pl.ds(off, L)]
            normed = []
            for hh in range(H):
                rows = slice(hh * dh, (hh + 1) * dh)
                hv = hs[rows]
                normed.append(hv * lax.rsqrt(jnp.mean(hv * hv, axis=0, keepdims=True) + EPS) * ng_ref[rows])
            hn = jnp.concatenate(normed, axis=0).T
            out_ref[pl.ds(r0, L), :] = (jax.nn.sigmoid(o_ref[pl.ds(r0, L), :]) * hn).astype(out_ref.dtype)
            return carry
        lax.fori_loop(0, n_rows // L, fbody, 0)

    finish(ol_ref, outl_ref, n_ctx, S)
    finish(oc_ref, outc_ref, 0, n_ctx)


def _mlstm_mixer(uT, u, gate_b, norm_g, B, S, n_ctx, o_col, gate_col):
    W = norm_g.shape[0]
    H = ML_HEADS
    dh = W // H
    ocb = o_col // W
    gcb = gate_col // LANES
    cblk = (B * S) // n_ctx

    def lat_t(j):
        return pl.BlockSpec((W, S), lambda b: (j, b))

    def ctx_t(j):
        return pl.BlockSpec((W, n_ctx), lambda b: (j, cblk + b))

    one = lambda b: (0, 0)
    out_l, out_c = pl.pallas_call(
        functools.partial(_mlstm_kernel, L=ML_CHUNK, H=H, dh=dh),
        grid=(B,),
        in_specs=[lat_t(0), lat_t(1), lat_t(2), pl.BlockSpec((S, W), lambda b: (b, ocb)),
                  pl.BlockSpec((S, LANES), lambda b: (b, gcb)),
                  ctx_t(0), ctx_t(1), ctx_t(2), pl.BlockSpec((n_ctx, W), lambda b: (cblk + b, ocb)),
                  pl.BlockSpec((n_ctx, LANES), lambda b: (cblk + b, gcb)),
                  pl.BlockSpec((1, LANES), one), pl.BlockSpec((W, 1), one)],
        out_specs=[pl.BlockSpec((S, W), lambda b: (b, 0)), pl.BlockSpec((n_ctx, W), lambda b: (b, 0))],
        out_shape=[jax.ShapeDtypeStruct((B * S, W), BF16), jax.ShapeDtypeStruct((B * n_ctx, W), BF16)],
        scratch_shapes=[pltpu.VMEM((W, n_ctx + S), F32), pltpu.VMEM((W, n_ctx + S), F32)],
        compiler_params=_cparams(("parallel",)),
        name="mlstm",
    )(uT, uT, uT, u, u, uT, uT, uT, u, u, gate_b, norm_g)
    return jnp.concatenate([out_l, out_c], axis=0)


def _da_prep_kernel(q_ref, k_ref, v_ref, cos_ref, sin_ref, qg_ref, kg_ref, seg_ref, qo_ref, ko_ref, vo_ref, *, dh):
    cos = cos_ref[...]
    sin = sin_ref[...]
    seg = seg_ref[...]
    W = q_ref.shape[1]
    lane = lax.broadcasted_iota(jnp.int32, (1, W), 1)
    quarter = dh // 4
    first = (lane % (2 * quarter)) < quarter

    def norm_rope(x, g):
        hi, lo = _split2(x * x)
        ms = (_dot(hi, seg) + _dot(lo, seg)) * (1.0 / dh)
        xn = x * lax.rsqrt(ms + EPS) * g
        rot = jnp.where(first, -pltpu.roll(xn, W - quarter, 1), pltpu.roll(xn, quarter, 1))
        return xn * cos + rot * sin

    qo_ref[...] = (norm_rope(q_ref[...], qg_ref[...]) * (dh ** -0.5 * math.log2(math.e))).astype(BF16)
    ko_ref[...] = norm_rope(k_ref[...], kg_ref[...]).astype(BF16)
    vo_ref[...] = v_ref[...].astype(BF16)


def _da_prep(u, cos, sin, qg, kg, seg, n_lat_rows, S, col0, tm, dh):
    R = u.shape[0]
    W = qg.shape[1]
    cb = col0 // W
    n_lat = n_lat_rows // tm
    per_seq = S // tm
    tab = lambda i: (jnp.where(i < n_lat, i % per_seq, per_seq), 0)
    one = lambda i: (0, 0)
    row = lambda i: (i, 0)

    def ucol(j):
        return pl.BlockSpec((tm, W), lambda i: (i, cb + j))

    return pl.pallas_call(
        functools.partial(_da_prep_kernel, dh=dh),
        grid=(R // tm,),
        in_specs=[ucol(0), ucol(1), ucol(2), pl.BlockSpec((tm, W), tab), pl.BlockSpec((tm, W), tab),
                  pl.BlockSpec((1, W), one), pl.BlockSpec((1, W), one), pl.BlockSpec((W, W), one)],
        out_specs=[pl.BlockSpec((tm, W), row)] * 3,
        out_shape=[jax.ShapeDtypeStruct((R, W), BF16)] * 3,
        compiler_params=_cparams(("parallel",)),
        name="da_prep",
    )(u, u, u, cos, sin, qg, kg, seg)


def _da_attn_kernel(*refs, n_kv, dh, lam_init):
    q_ref = refs[0]
    k_refs = refs[1:1 + n_kv]
    v_refs = refs[1 + n_kv:1 + 2 * n_kv]
    lam_ref, sg_ref, o_ref = refs[1 + 2 * n_kv:]
    lp = lam_ref[...]
    lam = (jnp.exp(jnp.sum(lp[0:1] * lp[1:2], axis=-1, keepdims=True))
           - jnp.exp(jnp.sum(lp[2:3] * lp[3:4], axis=-1, keepdims=True)) + lam_init)
    q = q_ref[...]
    acc = None
    nums, dens = [], []
    for mp in range(2):
        lanes = slice(mp * dh, (mp + 1) * dh)
        s = [_dot_nt(q[:, lanes], k_ref[:, lanes]) for k_ref in k_refs]
        mx = functools.reduce(jnp.maximum, [jnp.max(si, axis=-1, keepdims=True) for si in s])
        p = [jnp.exp2(si - mx) for si in s]
        nums.append(p)
        dens.append(sum(jnp.sum(pi, axis=-1, keepdims=True) for pi in p))
    c = lam * dens[0] / dens[1]
    for j in range(n_kv):
        a = (nums[0][j] - c * nums[1][j]).astype(BF16)
        t = _dot(a, v_refs[j][...])
        acc = t if acc is None else acc + t
    acc = acc / dens[0]
    o = acc * lax.rsqrt(jnp.mean(acc * acc, axis=-1, keepdims=True) + EPS) * sg_ref[...]
    o_ref[...] = (o * (1.0 - lam_init)).astype(o_ref.dtype)


def _da_attention(q, k, v, lam_p, subln_g, lam_init, B, q_rows, q_blk0, kv_segs, tq, dh):
    H = DA_HEADS
    vd = 2 * dh
    nq = q_rows // tq
    q0 = q_blk0

    def kv_spec(rows, blk0):
        return pl.BlockSpec((rows, vd), lambda b, h, i: (blk0 + b, h))

    kspecs = [kv_spec(r, b0) for r, b0 in kv_segs]
    one = lambda b, h, i: (0, 0)
    return pl.pallas_call(
        functools.partial(_da_attn_kernel, n_kv=len(kv_segs), dh=dh, lam_init=lam_init),
        grid=(B, H, nq),
        in_specs=[pl.BlockSpec((tq, vd), lambda b, h, i: (q0 + b * nq + i, h))] + kspecs + kspecs
                 + [pl.BlockSpec(lam_p.shape, one), pl.BlockSpec((1, vd), one)],
        out_specs=pl.BlockSpec((tq, vd), lambda b, h, i: (b * nq + i, h)),
        out_shape=jax.ShapeDtypeStruct((B * q_rows, H * vd), BF16),
        compiler_params=_cparams(("parallel", "parallel", "arbitrary")),
        name="da_attention",
    )(q, *([k] * len(kv_segs)), *([v] * len(kv_segs)), lam_p, subln_g)


def _axial_rope_tables(S, dh, reps, pad_rows):
    rows = S // GRID_W
    r = jnp.repeat(jnp.arange(rows, dtype=F32), GRID_W)
    col = jnp.tile(jnp.arange(GRID_W, dtype=F32), rows)
    n_freq = dh // 4
    inv = ROPE_THETA ** (-jnp.arange(n_freq, dtype=F32) / n_freq)
    ar = r[:, None] * inv
    ac = col[:, None] * inv
    ang = jnp.concatenate([ar, ar, ac, ac], axis=-1)
    cos = jnp.concatenate([jnp.tile(jnp.cos(ang), (1, reps)), jnp.ones((pad_rows, dh * reps), F32)], axis=0)
    sin = jnp.concatenate([jnp.tile(jnp.sin(ang), (1, reps)), jnp.zeros((pad_rows, dh * reps), F32)], axis=0)
    return cos, sin


def _diff_attn_mixer(u, cos, sin, qg, kg, lam_p, subln_g, lam_init, B, S, n_ctx, col0, need_ctx):
    dh = qg.shape[0]
    W = DA_HEADS * 2 * dh
    seg = (jnp.arange(W)[:, None] // dh == jnp.arange(W)[None, :] // dh).astype(BF16)
    tm = 512
    q, k, v = _da_prep(u, cos, sin, jnp.tile(qg, W // dh)[None], jnp.tile(kg, W // dh)[None], seg,
                       B * S, S, col0, tm, dh)
    sg = subln_g[None]
    ctx_blk0 = (B * S) // n_ctx
    tq = 256
    out_l = _da_attention(q, k, v, lam_p, sg, lam_init, B, S, 0, [(n_ctx, ctx_blk0), (S, 0)], tq, dh)
    if not need_ctx:
        return out_l
    out_c = _da_attention(q, k, v, lam_p, sg, lam_init, B, n_ctx, (B * S) // n_ctx, [(n_ctx, ctx_blk0)], n_ctx, dh)
    return jnp.concatenate([out_l, out_c], axis=0)


def _hy_conv_kernel(v_ref, x1_ref, x2_ref, w_ref, b_ref, vo_ref, x1o_ref, x2o_ref):
    L, W = v_ref.shape
    row = lax.broadcasted_iota(jnp.int32, (L, 1), 0)
    for j, (i_ref, o_ref) in enumerate(((v_ref, vo_ref), (x1_ref, x1o_ref), (x2_ref, x2o_ref))):
        lanes = slice(j * W, (j + 1) * W)
        u = i_ref[...]
        prev = jnp.where(row == 0, 0.0, pltpu.roll(u, 1, 0))
        nxt = jnp.where(row == L - 1, 0.0, pltpu.roll(u, L - 1, 0))
        o_ref[...] = prev * w_ref[0:1, lanes] + u * w_ref[1:2, lanes] + nxt * w_ref[2:3, lanes] + b_ref[:, lanes]


def _hy_short_conv(u, conv_w, conv_b, n_seg, L, blk0, col0):
    W = conv_w.shape[1] // 3
    cb = col0 // W
    one = lambda b: (0, 0)

    def ucol(j):
        return pl.BlockSpec((L, W), lambda b: (blk0 + b, cb + j))

    return pl.pallas_call(
        _hy_conv_kernel,
        grid=(n_seg,),
        in_specs=[ucol(0), ucol(1), ucol(2), pl.BlockSpec(conv_w.shape, one), pl.BlockSpec(conv_b.shape, one)],
        out_specs=[pl.BlockSpec((L, W), lambda b: (b, 0))] * 3,
        out_shape=[jax.ShapeDtypeStruct((n_seg * L, W), F32)] * 3,
        compiler_params=_cparams(("parallel",)),
        name="hy_short_conv",
    )(u, u, u, conv_w, conv_b)


def _hy_fwd_kernel(c_ref, s_ref, z_ref, *rest, raw):
    z = z_ref[...].astype(BF16)
    zr = _dot(c_ref[...], z)
    zi = _dot(s_ref[...], z)
    if raw:
        yr_ref, yi_ref = rest
        yr_ref[...] = zr
        yi_ref[...] = zi
    else:
        a_ref, b_ref, d_ref, yr_ref, yi_ref = rest
        yr_ref[...] = (zr * a_ref[...] - zi * b_ref[...]).astype(yr_ref.dtype)
        yi_ref[...] = (zr * b_ref[...] + zi * d_ref[...]).astype(yi_ref.dtype)


def _hy_fwd(cm, sm, z, coefs, n_seg, L, tk):
    W = z.shape[1]
    nk = L // tk
    raw = coefs is None
    mat = pl.BlockSpec((tk, L), lambda i, b: (i, 0))
    cf = pl.BlockSpec((tk, W), lambda i, b: (i, 0))
    out = pl.BlockSpec((tk, W), lambda i, b: (b * nk + i, 0))
    odt = F32 if raw else BF16
    return pl.pallas_call(
        functools.partial(_hy_fwd_kernel, raw=raw),
        grid=(nk, n_seg),
        in_specs=[mat, mat, pl.BlockSpec((L, W), lambda i, b: (b, 0))] + ([] if raw else [cf, cf, cf]),
        out_specs=[out, out],
        out_shape=[jax.ShapeDtypeStruct((n_seg * L, W), odt)] * 2,
        compiler_params=_cparams(("parallel", "arbitrary")),
        name="hy_dft_fwd",
    )(cm, sm, z, *(() if raw else coefs))


def _hy_inv_kernel(c_ref, st_ref, yr_ref, yi_ref, x_ref, vz_ref, skip_ref, o_ref):
    y = _dot(c_ref[...], yr_ref[...]) + _dot(st_ref[...], yi_ref[...])
    o_ref[...] = (x_ref[...] * (y + skip_ref[...] * vz_ref[...])).astype(o_ref.dtype)


def _hy_inv(cm, smt, yr, yi, xg, vz, skip, n_seg, L, tt, out_dtype):
    W = yr.shape[1]
    nt = L // tt
    mat = pl.BlockSpec((tt, L), lambda i, b: (i, 0))
    seq = pl.BlockSpec((L, W), lambda i, b: (b, 0))
    row = pl.BlockSpec((tt, W), lambda i, b: (b * nt + i, 0))
    return pl.pallas_call(
        _hy_inv_kernel,
        grid=(nt, n_seg),
        in_specs=[mat, mat, seq, seq, row, row, pl.BlockSpec((1, W), lambda i, b: (0, 0))],
        out_specs=row,
        out_shape=jax.ShapeDtypeStruct((n_seg * L, W), out_dtype),
        compiler_params=_cparams(("parallel", "arbitrary")),
        name="hy_dft_inv",
    )(cm, smt, yr, yi, xg, vz, skip)


def _dft_mats(L):
    split = 64
    n = jnp.arange(L, dtype=jnp.int32)

    def trig(mult):
        ang = ((mult[:, None] * n[None, :]) % (2 * L)).astype(F32) * (math.pi / L)
        return jnp.cos(ang), jnp.sin(ang)

    ca, sa = trig(split * jnp.arange(L // split, dtype=jnp.int32))
    cb, sb = trig(jnp.arange(split, dtype=jnp.int32))
    cm = (ca[:, None, :] * cb[None, :, :] - sa[:, None, :] * sb[None, :, :]).reshape(L, L)
    sm = -(sa[:, None, :] * cb[None, :, :] + ca[:, None, :] * sb[None, :, :]).reshape(L, L)
    nyq = jnp.where(n % 2 == 0, 1.0, -1.0)
    first_row = n[:, None] == 0
    first_col = n[None, :] == 0
    return (cm.astype(BF16), jnp.where(first_row, nyq[None, :], sm).astype(BF16),
            jnp.where(first_col, nyq[:, None], sm).astype(BF16))


def _hyena_filters(L, w1, b1, w2, b2, w3, freq, W):
    t01 = jnp.linspace(0.0, 1.0, L, dtype=F32)[:, None]
    wpos = (2.0 * math.pi / L) * jnp.arange(L, dtype=F32)[:, None]
    bands = jnp.linspace(1e-4, HY_BANDS - 1, HY_BANDS, dtype=F32)
    feats = jnp.concatenate([t01, jnp.cos(wpos * bands), -jnp.sin(wpos * bands)], axis=-1)
    hp = lax.Precision.HIGHEST
    h = jnp.sin(freq[0] * (jnp.dot(feats, w1, precision=hp) + b1))
    h = jnp.sin(freq[1] * (jnp.dot(h, w2, precision=hp) + b2))
    h = jnp.dot(h, w3, precision=hp).reshape(L, HY_ORDER, 2, W)
    deltas = jnp.abs(jnp.linspace(math.log(HY_TARGET) / HY_SLOW, math.log(HY_TARGET) / HY_FAST, W, dtype=F32))
    h = h * (jnp.exp(-t01 * deltas) + HY_SHIFT)[:, None, None, :]
    hf, hb = h[:, :, 0], h[:, :, 1]
    hf = hf.at[0].add(hb[0])
    hb = hb.at[0].set(0.0)
    scale = lax.rsqrt(jnp.sum(hf * hf, axis=0, keepdims=True) + jnp.sum(hb * hb, axis=0, keepdims=True) + EPS)
    return (hf * scale).reshape(L, HY_ORDER * W), (hb * scale).reshape(L, HY_ORDER * W)


def _hyena_spectrum(mats, L, w1, b1, w2, b2, w3, freq, W):
    cm, sm, _ = mats
    hf, hb = _hyena_filters(L, w1, b1, w2, b2, w3, freq, W)
    cols = jnp.concatenate([hf[:, :W], hf[:, W:], hb[:, :W], hb[:, W:]], axis=0)
    tk = min(L, 512)
    gr, gi = _hy_fwd(cm, sm, cols, None, 2 * HY_ORDER, L, tk)
    gr = gr.reshape(2, HY_ORDER, L, W)
    gi = gi.reshape(2, HY_ORDER, L, W)
    kr = gr[0] + gr[1]
    ki = gi[0] - gi[1]
    nyq = gi[0, :, 0] + gi[1, :, 0]
    n = 2.0 * L
    wk = jnp.full((L, 1), 2.0 / n, F32).at[0].set(1.0 / n)
    a = kr * wk
    bm = (ki * wk).at[:, 0].set(0.0)
    dd = a.at[:, 0].set(nyq / n)
    return [(a[o], bm[o], dd[o]) for o in range(HY_ORDER)]


def _hyena_seq(u, mats, spec, conv_w, conv_b, skip, n_seg, L, blk0, col0):
    cm, sm, smt = mats
    t = min(L, 512)
    v, x1, x2 = _hy_short_conv(u, conv_w, conv_b, n_seg, L, blk0, col0)
    yr, yi = _hy_fwd(cm, sm, v, spec[0], n_seg, L, t)
    z = _hy_inv(cm, smt, yr, yi, x1, v, skip[0:1], n_seg, L, t, F32)
    yr, yi = _hy_fwd(cm, sm, z, spec[1], n_seg, L, t)
    return _hy_inv(cm, smt, yr, yi, x2, z, skip[1:2], n_seg, L, t, BF16)


def _pack_rows(x):
    n = x.shape[1] // 2
    lo = pltpu.bitcast(x[:, :n].astype(BF16).astype(F32), jnp.uint32)
    hi = pltpu.bitcast(x[:, n:].astype(BF16).astype(F32), jnp.uint32)
    return (lo >> 16) | (hi & jnp.uint32(0xFFFF0000))


def _unpack_rows(p):
    return pltpu.bitcast(p << 16, F32), pltpu.bitcast(p & jnp.uint32(0xFFFF0000), F32)


def _moe_router_kernel(x_ref, g_ref, sc_ref, sh_ref, whi_ref, wlo_ref, rb_ref, h_ref, idx_ref, wts_ref, cnt_ref):
    x = x_ref[...]
    y = x * lax.rsqrt(jnp.mean(x * x, axis=-1, keepdims=True) + EPS) * g_ref[...]
    h = y * (1.0 + sc_ref[0]) + sh_ref[0]
    h_ref[...] = _pack_rows(h)
    hi, lo = _split2(h)
    logits = _dot_nt(whi_ref[...], hi) + _dot_nt(wlo_ref[...], hi) + _dot_nt(whi_ref[...], lo)
    scores = jax.nn.sigmoid(logits)
    sel = scores + rb_ref[...]
    E, tm = sel.shape
    gsz = E // N_GROUPS
    neg = -jnp.inf
    erow = lax.broadcasted_iota(jnp.int32, (E, 1), 0).astype(F32)
    grow = lax.broadcasted_iota(jnp.int32, (gsz, 1), 0).astype(F32)
    blocks = [sel[g * gsz:(g + 1) * gsz] for g in range(N_GROUPS)]
    gscore = []
    for blk in blocks:
        m1 = jnp.max(blk, axis=0, keepdims=True)
        first = jnp.min(jnp.where(blk == m1, grow, float(gsz)), axis=0, keepdims=True)
        m2 = jnp.max(jnp.where(grow == first, neg, blk), axis=0, keepdims=True)
        gscore.append(m1 + m2)
    kept = []
    for g in range(N_GROUPS):
        rank = jnp.zeros((1, tm), F32)
        for o in range(N_GROUPS):
            if o != g:
                ahead = (gscore[o] >= gscore[g]) if o < g else (gscore[o] > gscore[g])
                rank = rank + jnp.where(ahead, 1.0, 0.0)
        kept.append(jnp.where(rank < TOPK_GROUPS, blocks[g], neg))
    work = jnp.concatenate(kept, axis=0)
    ids, ws = [], []
    total = jnp.zeros((1, tm), F32)
    chosen = jnp.zeros((E, tm), F32)
    for j in range(TOP_K):
        mx = jnp.max(work, axis=0, keepdims=True)
        am = jnp.min(jnp.where(work == mx, erow, float(E)), axis=0, keepdims=True)
        hit = erow == am
        wj = jnp.sum(jnp.where(hit, scores, 0.0), axis=0, keepdims=True)
        work = jnp.where(hit, neg, work)
        chosen = jnp.where(hit, 1.0, chosen)
        ids.append(am)
        ws.append(wj)
        total = total + wj
    idx_ref[...] = jnp.concatenate(ids, axis=0).astype(jnp.int32)
    wts_ref[...] = jnp.concatenate(ws, axis=0) / total * ROUTED_SCALE
    ones = jnp.ones((8, tm), BF16)
    cnt_ref[0] = _dot_nt(ones, chosen.astype(BF16))[0:1]


def _moe_router(x, n_tok, g, scale, shift, whi, wlo, rb, seg_rows, tm):
    D = x.shape[1]
    E = whi.shape[0]
    K = TOP_K
    last = scale.shape[0] - 1
    row = lambda i: (i, 0)
    col = lambda i: (0, i)
    one = lambda i: (0, 0)
    mod = lambda i: (jnp.minimum(i * tm // seg_rows, last), 0, 0)
    return pl.pallas_call(
        _moe_router_kernel,
        grid=(n_tok // tm,),
        in_specs=[pl.BlockSpec((tm, D), row), pl.BlockSpec((1, D), one), pl.BlockSpec((1, 1, D), mod),
                  pl.BlockSpec((1, 1, D), mod), pl.BlockSpec((E, D), one), pl.BlockSpec((E, D), one),
                  pl.BlockSpec((E, 1), one)],
        out_specs=[pl.BlockSpec((tm, D // 2), row), pl.BlockSpec((K, tm), col), pl.BlockSpec((K, tm), col),
                   pl.BlockSpec((1, 1, E), lambda i: (i, 0, 0))],
        out_shape=[jax.ShapeDtypeStruct((n_tok, D // 2), jnp.uint32), jax.ShapeDtypeStruct((K, n_tok), jnp.int32),
                   jax.ShapeDtypeStruct((K, n_tok), F32), jax.ShapeDtypeStruct((n_tok // tm, 1, E), F32)],
        compiler_params=_cparams(("parallel",)),
        name="moe_router",
    )(x, g, scale, shift, whi, wlo, rb)


def _moe_pos_kernel(idx_ref, base_ref, pos_ref):
    K, tm = idx_ref.shape
    E = base_ref.shape[1]
    idx = idx_ref[...]
    erow = lax.broadcasted_iota(jnp.int32, (E, 1), 0)
    hits = [erow == idx[j:j + 1, :] for j in range(K)]
    onehot = jnp.zeros((E, tm), F32)
    for hit in hits:
        onehot = jnp.where(hit, 1.0, onehot)
    row = lax.broadcasted_iota(jnp.int32, (tm, tm), 0)
    col = lax.broadcasted_iota(jnp.int32, (tm, tm), 1)
    before = _dot(onehot.astype(BF16), (row < col).astype(BF16))
    dest = before + base_ref[0]
    pos = [jnp.sum(jnp.where(hit, dest, 0.0), axis=0, keepdims=True) for hit in hits]
    pos_ref[...] = jnp.concatenate(pos, axis=0).astype(jnp.int32)


def _moe_plan(idx, cnt, bm, tm):
    K, n_tok = idx.shape
    E = cnt.shape[2]
    cnt = cnt[:, 0, :]
    counts = jnp.sum(cnt, axis=0)
    pcounts = jnp.ceil(counts / bm) * bm
    pends = jnp.cumsum(pcounts)
    base = (pends - pcounts)[None, :] + jnp.cumsum(cnt, axis=0) - cnt
    pos = pl.pallas_call(
        _moe_pos_kernel,
        grid=(n_tok // tm,),
        in_specs=[pl.BlockSpec((K, tm), lambda i: (0, i)), pl.BlockSpec((1, E, 1), lambda i: (i, 0, 0))],
        out_specs=pl.BlockSpec((K, tm), lambda i: (0, i)),
        out_shape=jax.ShapeDtypeStruct((K, n_tok), jnp.int32),
        compiler_params=_cparams(("parallel",)),
        name="moe_positions",
    )(idx, base[:, :, None])
    n_blocks = -(-(n_tok * K + E * (bm - 1)) // bm)
    block_start = jnp.concatenate([jnp.zeros((1,), F32), pends / bm]).astype(jnp.int32)
    return pos.T.reshape(-1), block_start, n_blocks


def _per_token_rows(tm, K, copy):
    def start(t, carry):
        for j in range(K):
            copy(t, j).start(priority=j % 2)
        return carry

    def wait(t, carry):
        for j in range(K):
            copy(t, j).wait()
        return carry

    return (lambda: lax.fori_loop(0, tm, start, 0, unroll=2)), (lambda: lax.fori_loop(0, tm, wait, 0, unroll=2))


def _moe_dispatch_kernel(pos_ref, prev_ref, h_ref, xs_in_ref, xs_hbm, stage_ref, sems, *, K):
    del xs_in_ref
    tm = h_ref.shape[0]
    i = pl.program_id(0)
    last = pl.num_programs(0) - 1
    slot = lax.bitwise_and(i, 1)

    def tile_copies(s, p_ref):
        def row_copy(t, j):
            return pltpu.make_async_copy(stage_ref.at[s, pl.ds(t, 1)], xs_hbm.at[pl.ds(p_ref[t * K + j], 1)],
                                         sems.at[s])
        return _per_token_rows(tm, K, row_copy)

    start, wait = tile_copies(slot, pos_ref)
    _, wait_prev = tile_copies(1 - slot, prev_ref)
    stage_ref[slot] = h_ref[...]
    start()

    @pl.when(i > 0)
    def _():
        wait_prev()

    @pl.when(i == last)
    def _():
        wait()


def _moe_dispatch(pos, h, n_rows, K, tm):
    n_tok, W = h.shape
    xs0 = jnp.zeros((n_rows, W), h.dtype)
    return pl.pallas_call(
        functools.partial(_moe_dispatch_kernel, K=K),
        grid=(n_tok // tm,),
        in_specs=[pl.BlockSpec((tm * K,), lambda i: (i,), memory_space=pltpu.SMEM),
                  pl.BlockSpec((tm * K,), lambda i: (jnp.maximum(i - 1, 0),), memory_space=pltpu.SMEM),
                  pl.BlockSpec((tm, W), lambda i: (i, 0)),
                  pl.BlockSpec(memory_space=pl.ANY)],
        out_specs=pl.BlockSpec(memory_space=pl.ANY),
        out_shape=jax.ShapeDtypeStruct((n_rows, W), h.dtype),
        scratch_shapes=[pltpu.VMEM((2, tm, W), h.dtype), pltpu.SemaphoreType.DMA((2,))],
        input_output_aliases={3: 0},
        compiler_params=_cparams(("arbitrary",)),
        name="moe_dispatch",
    )(pos, pos, h, xs0)


def _moe_expert_kernel(bs_ref, x_hbm, w1_ref, w3_ref, w2_ref, y_hbm, xbuf, ybuf, w1b_ref, w3b_ref, w2b_ref,
                       xsem, ysem, *, bm, nbuf, n_blocks):
    e = pl.program_id(0)
    last = pl.num_programs(0) - 1
    b0 = bs_ref[e]
    b1 = bs_ref[e + 1]
    n_used = bs_ref[last + 1]

    def x_copy(g):
        slot = lax.bitwise_and(g, nbuf - 1)
        return pltpu.make_async_copy(x_hbm.at[pl.ds(g * bm, bm)], xbuf.at[slot], xsem.at[slot])

    def y_copy(g):
        slot = lax.bitwise_and(g, nbuf - 1)
        return pltpu.make_async_copy(ybuf.at[slot], y_hbm.at[pl.ds(g * bm, bm)], ysem.at[slot])

    @pl.when(e == 0)
    def _():
        for p in range(nbuf - 1):
            @pl.when(p < n_used)
            def _():
                x_copy(p).start()

    @pl.when(b1 > b0)
    def _():
        w1b_ref[...] = w1_ref[0, 0].astype(BF16)
        w3b_ref[...] = w3_ref[0, 0].astype(BF16)
        w2b_ref[...] = w2_ref[0, 0].astype(BF16)

    def block(g, carry):
        slot = lax.bitwise_and(g, nbuf - 1)
        x_copy(g).wait()

        @pl.when(g + nbuf - 1 < n_used)
        def _():
            x_copy(g + nbuf - 1).start()

        @pl.when(g >= nbuf)
        def _():
            y_copy(g - nbuf).wait()

        x = jnp.concatenate(_unpack_rows(xbuf[slot]), axis=1).astype(BF16)
        a = _dot(x, w1b_ref[...])
        b = _dot(x, w3b_ref[...])
        ybuf[slot] = _pack_rows(_dot((jax.nn.silu(a) * b).astype(BF16), w2b_ref[...]))
        y_copy(g).start()
        return carry

    lax.fori_loop(b0, b1, block, 0)

    @pl.when(e == last)
    def _():
        def drain(g, carry):
            y_copy(g).wait()
            return carry

        lax.fori_loop(jnp.maximum(n_used - nbuf, 0), n_used, drain, 0)
        ybuf[0] = jnp.zeros(ybuf.shape[1:], ybuf.dtype)

        def zero_copy(g):
            return pltpu.make_async_copy(ybuf.at[0], y_hbm.at[pl.ds(g * bm, bm)], ysem.at[0])

        def fill(g, carry):
            zero_copy(g).start()
            return carry

        def fill_wait(g, carry):
            zero_copy(g).wait()
            return carry

        lax.fori_loop(n_used, n_blocks, fill, 0)
        lax.fori_loop(n_used, n_blocks, fill_wait, 0)


def _moe_experts(block_start, xs, w1, w3, w2, layer, bm):
    P, W = xs.shape
    E, D, F = w1.shape[1:]
    nbuf = 4
    grid_spec = pltpu.PrefetchScalarGridSpec(
        num_scalar_prefetch=1,
        grid=(E,),
        in_specs=[pl.BlockSpec(memory_space=pl.ANY),
                  pl.BlockSpec((1, 1, D, F), lambda e, bs: (layer, e, 0, 0)),
                  pl.BlockSpec((1, 1, D, F), lambda e, bs: (layer, e, 0, 0)),
                  pl.BlockSpec((1, 1, F, D), lambda e, bs: (layer, e, 0, 0))],
        out_specs=pl.BlockSpec(memory_space=pl.ANY),
        scratch_shapes=[pltpu.VMEM((nbuf, bm, W), xs.dtype), pltpu.VMEM((nbuf, bm, W), xs.dtype),
                        pltpu.VMEM((D, F), BF16), pltpu.VMEM((D, F), BF16), pltpu.VMEM((F, D), BF16),
                        pltpu.SemaphoreType.DMA((nbuf,)), pltpu.SemaphoreType.DMA((nbuf,))],
    )
    return pl.pallas_call(
        functools.partial(_moe_expert_kernel, bm=bm, nbuf=nbuf, n_blocks=P // bm),
        grid_spec=grid_spec,
        out_shape=jax.ShapeDtypeStruct((P, W), xs.dtype),
        compiler_params=_cparams(("arbitrary",)),
        name="moe_experts",
    )(block_start, xs, w1, w3, w2)


def _moe_combine_kernel(pos_ref, next_ref, wts_ref, x_ref, h_ref, sw1_ref, sw3_ref, sw2_ref, gate_ref, ys_hbm, o_ref,
                        gath_ref, sems, *, K):
    tm = x_ref.shape[0]
    i = pl.program_id(0)
    last = pl.num_programs(0) - 1

    slot = lax.bitwise_and(i, 1)

    def tile_copies(s, p_ref):
        def row_copy(t, j):
            return pltpu.make_async_copy(ys_hbm.at[pl.ds(p_ref[t * K + j], 1)], gath_ref.at[s, j, pl.ds(t, 1)],
                                         sems.at[s])
        return _per_token_rows(tm, K, row_copy)

    start, wait = tile_copies(slot, pos_ref)
    start_next, _ = tile_copies(1 - slot, next_ref)

    @pl.when(i == 0)
    def _():
        start()

    @pl.when(i < last)
    def _():
        start_next()

    h = jnp.concatenate(_unpack_rows(h_ref[...]), axis=1).astype(BF16)
    mid = jax.nn.silu(_dot(h, sw1_ref[...])) * _dot(h, sw3_ref[...])
    shared = _dot(mid.astype(BF16), sw2_ref[...])
    wait()
    wts = wts_ref[...]
    W = gath_ref.shape[3]
    acc_lo = shared[:, :W]
    acc_hi = shared[:, W:]
    for j in range(K):
        lo, hi = _unpack_rows(gath_ref[slot, j])
        acc_lo = acc_lo + wts[:, j:j + 1] * lo
        acc_hi = acc_hi + wts[:, j:j + 1] * hi
    gate = gate_ref[0]
    o_ref[:, :W] = x_ref[:, :W] + gate[:, :W] * acc_lo
    o_ref[:, W:] = x_ref[:, W:] + gate[:, W:] * acc_hi


def _moe_combine(pos, wts, x, h, sw1, sw3, sw2, gate, ys, n_tok, K, seg_rows, tm):
    D = x.shape[1]
    W = ys.shape[1]
    n_tiles = n_tok // tm
    last = gate.shape[0] - 1
    row = lambda i: (i, 0)
    one = lambda i: (0, 0)
    return pl.pallas_call(
        functools.partial(_moe_combine_kernel, K=K),
        grid=(n_tiles,),
        in_specs=[pl.BlockSpec((tm * K,), lambda i: (i,), memory_space=pltpu.SMEM),
                  pl.BlockSpec((tm * K,), lambda i: (jnp.minimum(i + 1, n_tiles - 1),), memory_space=pltpu.SMEM),
                  pl.BlockSpec((tm, K), row), pl.BlockSpec((tm, D), row), pl.BlockSpec((tm, W), row),
                  pl.BlockSpec(sw1.shape, one), pl.BlockSpec(sw3.shape, one), pl.BlockSpec(sw2.shape, one),
                  pl.BlockSpec((1, 1, D), lambda i: (jnp.minimum(i * tm // seg_rows, last), 0, 0)),
                  pl.BlockSpec(memory_space=pl.ANY)],
        out_specs=pl.BlockSpec((tm, D), row),
        out_shape=jax.ShapeDtypeStruct((n_tok, D), F32),
        scratch_shapes=[pltpu.VMEM((2, K, tm, W), ys.dtype), pltpu.SemaphoreType.DMA((2,))],
        compiler_params=_cparams(("arbitrary",)),
        name="moe_combine",
    )(pos, pos, wts, x, h, sw1, sw3, sw2, gate, ys)


def _moe_ffn_residual(x, n_tok, norm_g, scale, shift, gate, router_w, router_b, ew1, ew3, ew2, layer, sw1, sw3, sw2,
                      seg_rows, bm):
    K = TOP_K
    tm = 256
    whi, wlo = _split2(router_w.T)
    h, idx, wts, cnt = _moe_router(x, n_tok, norm_g, scale, shift, whi, wlo, router_b[:, None], seg_rows, tm)
    pos, block_start, n_blocks = _moe_plan(idx, cnt, bm, tm)
    wts = wts.T
    xs = _moe_dispatch(pos, h, n_blocks * bm, K, 128)
    ys = _moe_experts(block_start, xs, ew1, ew3, ew2, layer, bm)
    return _moe_combine(pos, wts, x, h, sw1.astype(BF16), sw3.astype(BF16), sw2.astype(BF16), gate, ys,
                        n_tok, K, seg_rows, 128)


def kernel(x, c, ctx, c_ctx, w_mod, b_mod, norm1_g, norm2_g, w_in, mlstm_gate_b, mlstm_norm_g, da_qnorm_g, da_knorm_g, da_lambda, da_subln_g, hy_conv_w, hy_conv_b, hy_w1, hy_b1, hy_w2, hy_b2, hy_w3, hy_freq, hy_skip, w_out, router_w, router_b, exp_w1, exp_w3, exp_w2, sh_w1, sh_w3, sh_w2):
    B, S, D = x.shape
    n_ctx = ctx.shape[1]
    depth = w_in.shape[0]
    n_lat = B * S
    ml_w = mlstm_norm_g.shape[1]
    da_dh = da_qnorm_g.shape[1]
    da_w = DA_HEADS * 2 * da_dh
    hy_w = hy_skip.shape[2]
    n_gates = 4 * ML_HEADS
    da_col = 0
    hy_col = da_col + 3 * da_w
    o_col = hy_col + 3 * hy_w
    gate_col = o_col + ml_w
    ml_in = 4 * ml_w + n_gates
    tm = 512

    X = jnp.concatenate([x.reshape(n_lat, D), ctx.reshape(B * n_ctx, D)], axis=0)
    sc = jax.nn.silu(jnp.concatenate([c, c_ctx[None]], axis=0))
    cos, sin = _axial_rope_tables(S, da_dh, da_w // da_dh, tm)
    mats_l = _dft_mats(S)
    mats_c = _dft_mats(n_ctx)
    for l in range(depth):
        last = l == depth - 1
        lam_init = 0.8 - 0.6 * math.exp(-0.3 * l)
        mods = (jnp.dot(sc, w_mod[l], precision=lax.Precision.HIGHEST) + b_mod[l]).reshape(B + 1, 6, 1, D)
        sh1, s1, g1, sh2, s2, g2 = [mods[:, i] for i in range(6)]
        wl = w_in[l]
        w_big = jnp.concatenate([wl[:, ml_in:], wl[:, 3 * ml_w:4 * ml_w], wl[:, 4 * ml_w:ml_in],
                                 jnp.zeros((D, LANES - n_gates), F32)], axis=1).astype(BF16)
        w_qkv_t = wl[:, :3 * ml_w].T.astype(BF16)
        U, UT = _norm_mod_matmul(X, norm1_g[l][None], s1, sh1, w_big, w_qkv_t, S, tm, w_big.shape[1] // 3)
        gb = jnp.concatenate([mlstm_gate_b[l], jnp.zeros((LANES - n_gates,), F32)])[None]
        m_out = _mlstm_mixer(UT, U, gb, mlstm_norm_g[l][:, None], B, S, n_ctx, o_col, gate_col)
        d_out = _diff_attn_mixer(U, cos, sin, da_qnorm_g[l], da_knorm_g[l], da_lambda[l], da_subln_g[l], lam_init,
                                 B, S, n_ctx, da_col, not last)
        hy_args = (hy_w1[l], hy_b1[l], hy_w2[l], hy_b2[l], hy_w3[l], hy_freq[l], hy_w)
        y_out = _hyena_seq(U, mats_l, _hyena_spectrum(mats_l, S, *hy_args), hy_conv_w[l], hy_conv_b[l][None],
                           hy_skip[l], B, S, 0, hy_col)
        n_rows = n_lat
        if not last:
            y_ctx = _hyena_seq(U, mats_c, _hyena_spectrum(mats_c, n_ctx, *hy_args), hy_conv_w[l], hy_conv_b[l][None],
                               hy_skip[l], B, n_ctx, n_lat // n_ctx, hy_col)
            y_out = jnp.concatenate([y_out, y_ctx], axis=0)
            n_rows = n_lat + B * n_ctx
        wo = w_out[l].astype(BF16)
        X = _out_proj_residual(m_out, d_out, y_out, wo[:ml_w], wo[ml_w:ml_w + da_w], wo[ml_w + da_w:], X, g1,
                               n_rows, S, tm)
        X = _moe_ffn_residual(X, n_rows, norm2_g[l][None], s2, sh2, g2, router_w[l], router_b[l],
                              exp_w1, exp_w3, exp_w2, l, sh_w1[l], sh_w3[l], sh_w2[l], S,
                              MOE_BLOCK)
    return X[:n_lat].reshape(B, S, D)
```

```python
import functools
import math

import jax
import jax.numpy as jnp
from jax import lax
from jax.experimental import pallas as pl
from jax.experimental.pallas import tpu as pltpu

F32 = jnp.float32
BF16 = jnp.bfloat16

EPS = 1e-6
GRID_W = 64
ROPE_THETA = 10000.0
ML_HEADS = 4
ML_CHUNK = 256
ML_M_INIT = -1e30
DA_HEADS = 4
HY_ORDER = 2
HY_BANDS = 8
HY_SHIFT = 0.05
HY_TARGET = 1e-2
HY_FAST = 0.3
HY_SLOW = 1.5
N_GROUPS = 8
TOPK_GROUPS = 4
TOP_K = 8
ROUTED_SCALE = 2.5
MOE_BLOCK = 128
LANES = 128
VMEM_LIMIT = 56 * 1024 * 1024


def _cparams(sem):
    return pltpu.CompilerParams(dimension_semantics=sem, vmem_limit_bytes=VMEM_LIMIT)


def _dot(a, b):
    return jnp.dot(a, b, preferred_element_type=F32)


def _dot_nt(a, b):
    return lax.dot_general(a, b, (((1,), (1,)), ((), ())), preferred_element_type=F32)


def _dot_tn(a, b):
    return lax.dot_general(a, b, (((0,), (0,)), ((), ())), preferred_element_type=F32)


def _split3(a):
    hi = a.astype(BF16)
    r = a - hi.astype(F32)
    mid = r.astype(BF16)
    lo = (r - mid.astype(F32)).astype(BF16)
    return hi, mid, lo


def _split2(a):
    hi = a.astype(BF16)
    lo = (a - hi.astype(F32)).astype(BF16)
    return hi, lo


def _norm_mod_mm_kernel(x_ref, g_ref, sc_ref, sh_ref, w_ref, wt_ref, o_ref, ot_ref, xn_ref):
    @pl.when(pl.program_id(1) == 0)
    def _():
        x = x_ref[...]
        y = x * lax.rsqrt(jnp.mean(x * x, axis=-1, keepdims=True) + EPS) * g_ref[...]
        xn_ref[...] = (y * (1.0 + sc_ref[0]) + sh_ref[0]).astype(BF16)
        ot_ref[...] = _dot_nt(wt_ref[...], xn_ref[...])

    o_ref[...] = _dot(xn_ref[...], w_ref[...])


def _norm_mod_matmul(x, g, scale, shift, w, wt, seg_rows, tm, tn):
    R, D = x.shape
    N = w.shape[1]
    NT = wt.shape[0]
    last = scale.shape[0] - 1
    mod_map = lambda i, j: (jnp.minimum(i * tm // seg_rows, last), 0, 0)
    return pl.pallas_call(
        _norm_mod_mm_kernel,
        grid=(R // tm, N // tn),
        in_specs=[
            pl.BlockSpec((tm, D), lambda i, j: (i, 0)),
            pl.BlockSpec((1, D), lambda i, j: (0, 0)),
            pl.BlockSpec((1, 1, D), mod_map),
            pl.BlockSpec((1, 1, D), mod_map),
            pl.BlockSpec((D, tn), lambda i, j: (0, j)),
            pl.BlockSpec((NT, D), lambda i, j: (0, 0)),
        ],
        out_specs=[pl.BlockSpec((tm, tn), lambda i, j: (i, j)), pl.BlockSpec((NT, tm), lambda i, j: (0, i))],
        out_shape=[jax.ShapeDtypeStruct((R, N), F32), jax.ShapeDtypeStruct((NT, R), F32)],
        scratch_shapes=[pltpu.VMEM((tm, D), BF16)],
        compiler_params=_cparams(("parallel", "arbitrary")),
        name="norm_mod_matmul",
    )(x, g, scale, shift, w, wt)


def _out_proj_kernel(m_ref, d_ref, y_ref, wm_ref, wd_ref, wy_ref, x_ref, gate_ref, o_ref):
    acc = _dot(m_ref[...], wm_ref[...]) + _dot(d_ref[...], wd_ref[...]) + _dot(y_ref[...], wy_ref[...])
    o_ref[...] = x_ref[...] + gate_ref[0] * acc


def _out_proj_residual(m, d, y, wm, wd, wy, x, gate, n_rows, seg_rows, tm):
    R, D = n_rows, x.shape[1]
    last = gate.shape[0] - 1
    row = lambda i: (i, 0)
    full = lambda i: (0, 0)
    return pl.pallas_call(
        _out_proj_kernel,
        grid=(R // tm,),
        in_specs=[
            pl.BlockSpec((tm, m.shape[1]), row),
            pl.BlockSpec((tm, d.shape[1]), row),
            pl.BlockSpec((tm, y.shape[1]), row),
            pl.BlockSpec(wm.shape, full),
            pl.BlockSpec(wd.shape, full),
            pl.BlockSpec(wy.shape, full),
            pl.BlockSpec((tm, D), row),
            pl.BlockSpec((1, 1, D), lambda i: (jnp.minimum(i * tm // seg_rows, last), 0, 0)),
        ],
        out_specs=pl.BlockSpec((tm, D), row),
        out_shape=jax.ShapeDtypeStruct((R, D), F32),
        compiler_params=_cparams(("parallel",)),
        name="out_proj_residual",
    )(m, d, y, wm, wd, wy, x, gate)


def _log_sigmoid(x):
    return jnp.minimum(x, 0.0) - jnp.log1p(jnp.exp(-jnp.abs(x)))


def _mlstm_gate_tables(g_ref, r0, L, gb, tril, triu):
    g = g_ref[pl.ds(r0, L), :] + gb
    lf = _log_sigmoid(g)
    gT = g.T
    lfT = lf.T
    parts = _split3(lf)
    partsT = _split3(lfT)
    cs_f = sum(_dot(tril, p) for p in parts)
    cs_b = sum(_dot(triu, p) for p in parts)
    rs_f = sum(_dot(p, triu) for p in partsT)
    rs_b = sum(_dot(p, tril) for p in partsT)
    return g, gT, cs_f, cs_b, rs_f, rs_b


def _mlstm_chunk(qT, kT, vT, i_row, b_row, c_col, b_end, mask, state):
    C, n, m = state
    qb = qT.astype(BF16)
    kb = kT.astype(BF16)
    dmat = jnp.where(mask, b_row + c_col, -jnp.inf)
    inter = b_row + m
    m_t = jnp.maximum(inter, jnp.max(dmat, axis=0, keepdims=True))
    s = _dot_tn(kb, qb) * jnp.exp(dmat - m_t)
    carry_w = jnp.exp(inter - m_t)
    num = _dot(vT.astype(BF16), s.astype(BF16)) + carry_w * _dot(C.astype(BF16), qb)
    den = jnp.sum(s, axis=0, keepdims=True) + carry_w * jnp.sum(qT * n, axis=0, keepdims=True)
    h = num / jnp.maximum(jnp.abs(den), jnp.exp(-m_t))
    g = b_end - b_row + i_row
    m_new = jnp.maximum(b_end + m, jnp.max(g, axis=-1, keepdims=True))
    ws = jnp.exp(g - m_new)
    decay = jnp.exp(b_end + m - m_new)
    C_new = decay * C + _dot_nt((vT * ws).astype(BF16), kb)
    n_new = decay * n + jnp.sum(kT * ws, axis=-1, keepdims=True)
    return h, (C_new, n_new, m_new)


def _mlstm_kernel(ql_ref, kl_ref, vl_ref, ol_ref, gl_ref, qc_ref, kc_ref, vc_ref, oc_ref, gc_ref,
                  gb_ref, ng_ref, outl_ref, outc_ref, hf_ref, hb_ref, *, L, H, dh):
    S = ql_ref.shape[1]
    n_ctx = qc_ref.shape[1]
    row = lax.broadcasted_iota(jnp.int32, (L, L), 0)
    col = lax.broadcasted_iota(jnp.int32, (L, L), 1)
    lower = col <= row
    upper = col >= row
    tril = lower.astype(BF16)
    triu = upper.astype(BF16)
    gb = gb_ref[...]
    k_scale = dh ** -0.5

    def both_dirs(refs_f, r0_f, refs_b, r0_b, hoff, state):
        new_state = []
        for d, (refs, r0, h_ref) in enumerate(((refs_f, r0_f, hf_ref), (refs_b, r0_b, hb_ref))):
            q_ref, k_ref, v_ref, g_ref = refs
            g, gT, cs_f, cs_b, rs_f, rs_b = _mlstm_gate_tables(g_ref, r0, L, gb, tril, triu)
            cs, rs, mask = (cs_f, rs_f, upper) if d == 0 else (cs_b, rs_b, lower)
            end = L - 1 if d == 0 else 0
            for hh in range(H):
                ic = 2 * d * H + hh
                fc = ic + H
                rows = slice(hh * dh, (hh + 1) * dh)
                qT = q_ref[rows, pl.ds(r0, L)]
                kT = k_ref[rows, pl.ds(r0, L)] * k_scale
                vT = v_ref[rows, pl.ds(r0, L)]
                h, st = _mlstm_chunk(qT, kT, vT, gT[ic:ic + 1, :], rs[fc:fc + 1, :],
                                     g[:, ic:ic + 1] - cs[:, fc:fc + 1], rs[fc:fc + 1, end:end + 1], mask,
                                     state[d * H + hh])
                off = hoff + r0
                h_ref[rows, pl.ds(off if isinstance(off, int) else pl.multiple_of(off, L), L)] = h
                new_state.append(st)
        return tuple(new_state)

    state = tuple((jnp.zeros((dh, dh), F32), jnp.zeros((dh, 1), F32), jnp.full((1, 1), ML_M_INIT, F32))
                  for _ in range(2 * H))
    ctx_refs = (qc_ref, kc_ref, vc_ref, gc_ref)
    lat_refs = (ql_ref, kl_ref, vl_ref, gl_ref)
    n_cc = n_ctx // L
    for c in range(n_cc):
        state = both_dirs(ctx_refs, c * L, ctx_refs, (n_cc - 1 - c) * L, 0, state)
    n_lc = S // L

    def body(c, st):
        r_f = pl.multiple_of(c * L, L)
        r_b = pl.multiple_of((n_lc - 1 - c) * L, L)
        return both_dirs(lat_refs, r_f, lat_refs, r_b, n_ctx, st)

    lax.fori_loop(0, n_lc, body, state)

    def finish(o_ref, out_ref, hoff, n_rows):
        def fbody(c, carry):
            r0 = pl.multiple_of(c * L, L)
            off = pl.multiple_of(hoff + r0, L)
            hs = hf_ref[:, pl.ds(off, L)] + hb_ref[:, pl.ds(off, L)]
            normed = []
            for hh in range(H):
                rows = slice(hh * dh, (hh + 1) * dh)
                hv = hs[rows]
                normed.append(hv * lax.rsqrt(jnp.mean(hv * hv, axis=0, keepdims=True) + EPS) * ng_ref[rows])
            hn = jnp.concatenate(normed, axis=0).T
            out_ref[pl.ds(r0, L), :] = (jax.nn.sigmoid(o_ref[pl.ds(r0, L), :]) * hn).astype(out_ref.dtype)
            return carry
        lax.fori_loop(0, n_rows // L, fbody, 0)

    finish(ol_ref, outl_ref, n_ctx, S)
    finish(oc_ref, outc_ref, 0, n_ctx)


def _mlstm_mixer(uT, u, gate_b, norm_g, B, S, n_ctx, o_col, gate_col):
    W = norm_g.shape[0]
    H = ML_HEADS
    dh = W // H
    ocb = o_col // W
    gcb = gate_col // LANES
    cblk = (B * S) // n_ctx

    def lat_t(j):
        return pl.BlockSpec((W, S), lambda b: (j, b))

    def ctx_t(j):
        return pl.BlockSpec((W, n_ctx), lambda b: (j, cblk + b))

    one = lambda b: (0, 0)
    out_l, out_c = pl.pallas_call(
        functools.partial(_mlstm_kernel, L=ML_CHUNK, H=H, dh=dh),
        grid=(B,),
        in_specs=[lat_t(0), lat_t(1), lat_t(2), pl.BlockSpec((S, W), lambda b: (b, ocb)),
                  pl.BlockSpec((S, LANES), lambda b: (b, gcb)),
                  ctx_t(0), ctx_t(1), ctx_t(2), pl.BlockSpec((n_ctx, W), lambda b: (cblk + b, ocb)),
                  pl.BlockSpec((n_ctx, LANES), lambda b: (cblk + b, gcb)),
                  pl.BlockSpec((1, LANES), one), pl.BlockSpec((W, 1), one)],
        out_specs=[pl.BlockSpec((S, W), lambda b: (b, 0)), pl.BlockSpec((n_ctx, W), lambda b: (b, 0))],
        out_shape=[jax.ShapeDtypeStruct((B * S, W), BF16), jax.ShapeDtypeStruct((B * n_ctx, W), BF16)],
        scratch_shapes=[pltpu.VMEM((W, n_ctx + S), F32), pltpu.VMEM((W, n_ctx + S), F32)],
        compiler_params=_cparams(("parallel",)),
        name="mlstm",
    )(uT, uT, uT, u, u, uT, uT, uT, u, u, gate_b, norm_g)
    return jnp.concatenate([out_l, out_c], axis=0)


def _da_prep_kernel(q_ref, k_ref, v_ref, cos_ref, sin_ref, qg_ref, kg_ref, seg_ref, qo_ref, ko_ref, vo_ref, *, dh):
    cos = cos_ref[...]
    sin = sin_ref[...]
    seg = seg_ref[...]
    W = q_ref.shape[1]
    lane = lax.broadcasted_iota(jnp.int32, (1, W), 1)
    quarter = dh // 4
    first = (lane % (2 * quarter)) < quarter

    def norm_rope(x, g):
        hi, lo = _split2(x * x)
        ms = (_dot(hi, seg) + _dot(lo, seg)) * (1.0 / dh)
        xn = x * lax.rsqrt(ms + EPS) * g
        rot = jnp.where(first, -pltpu.roll(xn, W - quarter, 1), pltpu.roll(xn, quarter, 1))
        return xn * cos + rot * sin

    qo_ref[...] = (norm_rope(q_ref[...], qg_ref[...]) * (dh ** -0.5 * math.log2(math.e))).astype(BF16)
    ko_ref[...] = norm_rope(k_ref[...], kg_ref[...]).astype(BF16)
    vo_ref[...] = v_ref[...].astype(BF16)


def _da_prep(u, cos, sin, qg, kg, seg, n_lat_rows, S, col0, tm, dh):
    R = u.shape[0]
    W = qg.shape[1]
    cb = col0 // W
    n_lat = n_lat_rows // tm
    per_seq = S // tm
    tab = lambda i: (jnp.where(i < n_lat, i % per_seq, per_seq), 0)
    one = lambda i: (0, 0)
    row = lambda i: (i, 0)

    def ucol(j):
        return pl.BlockSpec((tm, W), lambda i: (i, cb + j))

    return pl.pallas_call(
        functools.partial(_da_prep_kernel, dh=dh),
        grid=(R // tm,),
        in_specs=[ucol(0), ucol(1), ucol(2), pl.BlockSpec((tm, W), tab), pl.BlockSpec((tm, W), tab),
                  pl.BlockSpec((1, W), one), pl.BlockSpec((1, W), one), pl.BlockSpec((W, W), one)],
        out_specs=[pl.BlockSpec((tm, W), row)] * 3,
        out_shape=[jax.ShapeDtypeStruct((R, W), BF16)] * 3,
        compiler_params=_cparams(("parallel",)),
        name="da_prep",
    )(u, u, u, cos, sin, qg, kg, seg)


def _da_attn_kernel(*refs, n_kv, dh, lam_init):
    q_ref = refs[0]
    k_refs = refs[1:1 + n_kv]
    v_refs = refs[1 + n_kv:1 + 2 * n_kv]
    lam_ref, sg_ref, o_ref = refs[1 + 2 * n_kv:]
    lp = lam_ref[...]
    lam = (jnp.exp(jnp.sum(lp[0:1] * lp[1:2], axis=-1, keepdims=True))
           - jnp.exp(jnp.sum(lp[2:3] * lp[3:4], axis=-1, keepdims=True)) + lam_init)
    q = q_ref[...]
    acc = None
    nums, dens = [], []
    for mp in range(2):
        lanes = slice(mp * dh, (mp + 1) * dh)
        s = [_dot_nt(q[:, lanes], k_ref[:, lanes]) for k_ref in k_refs]
        mx = functools.reduce(jnp.maximum, [jnp.max(si, axis=-1, keepdims=True) for si in s])
        p = [jnp.exp2(si - mx) for si in s]
        nums.append(p)
        dens.append(sum(jnp.sum(pi, axis=-1, keepdims=True) for pi in p))
    c = lam * dens[0] / dens[1]
    for j in range(n_kv):
        a = (nums[0][j] - c * nums[1][j]).astype(BF16)
        t = _dot(a, v_refs[j][...])
        acc = t if acc is None else acc + t
    acc = acc / dens[0]
    o = acc * lax.rsqrt(jnp.mean(acc * acc, axis=-1, keepdims=True) + EPS) * sg_ref[...]
    o_ref[...] = (o * (1.0 - lam_init)).astype(o_ref.dtype)


def _da_attention(q, k, v, lam_p, subln_g, lam_init, B, q_rows, q_blk0, kv_segs, tq, dh):
    H = DA_HEADS
    vd = 2 * dh
    nq = q_rows // tq
    q0 = q_blk0

    def kv_spec(rows, blk0):
        return pl.BlockSpec((rows, vd), lambda b, h, i: (blk0 + b, h))

    kspecs = [kv_spec(r, b0) for r, b0 in kv_segs]
    one = lambda b, h, i: (0, 0)
    return pl.pallas_call(
        functools.partial(_da_attn_kernel, n_kv=len(kv_segs), dh=dh, lam_init=lam_init),
        grid=(B, H, nq),
        in_specs=[pl.BlockSpec((tq, vd), lambda b, h, i: (q0 + b * nq + i, h))] + kspecs + kspecs
                 + [pl.BlockSpec(lam_p.shape, one), pl.BlockSpec((1, vd), one)],
        out_specs=pl.BlockSpec((tq, vd), lambda b, h, i: (b * nq + i, h)),
        out_shape=jax.ShapeDtypeStruct((B * q_rows, H * vd), BF16),
        compiler_params=_cparams(("parallel", "parallel", "arbitrary")),
        name="da_attention",
    )(q, *([k] * len(kv_segs)), *([v] * len(kv_segs)), lam_p, subln_g)


def _axial_rope_tables(S, dh, reps, pad_rows):
    rows = S // GRID_W
    r = jnp.repeat(jnp.arange(rows, dtype=F32), GRID_W)
    col = jnp.tile(jnp.arange(GRID_W, dtype=F32), rows)
    n_freq = dh // 4
    inv = ROPE_THETA ** (-jnp.arange(n_freq, dtype=F32) / n_freq)
    ar = r[:, None] * inv
    ac = col[:, None] * inv
    ang = jnp.concatenate([ar, ar, ac, ac], axis=-1)
    cos = jnp.concatenate([jnp.tile(jnp.cos(ang), (1, reps)), jnp.ones((pad_rows, dh * reps), F32)], axis=0)
    sin = jnp.concatenate([jnp.tile(jnp.sin(ang), (1, reps)), jnp.zeros((pad_rows, dh * reps), F32)], axis=0)
    return cos, sin


def _diff_attn_mixer(u, cos, sin, qg, kg, lam_p, subln_g, lam_init, B, S, n_ctx, col0, need_ctx):
    dh = qg.shape[0]
    W = DA_HEADS * 2 * dh
    seg = (jnp.arange(W)[:, None] // dh == jnp.arange(W)[None, :] // dh).astype(BF16)
    tm = 512
    q, k, v = _da_prep(u, cos, sin, jnp.tile(qg, W // dh)[None], jnp.tile(kg, W // dh)[None], seg,
                       B * S, S, col0, tm, dh)
    sg = subln_g[None]
    ctx_blk0 = (B * S) // n_ctx
    tq = 256
    out_l = _da_attention(q, k, v, lam_p, sg, lam_init, B, S, 0, [(n_ctx, ctx_blk0), (S, 0)], tq, dh)
    if not need_ctx:
        return out_l
    out_c = _da_attention(q, k, v, lam_p, sg, lam_init, B, n_ctx, (B * S) // n_ctx, [(n_ctx, ctx_blk0)], n_ctx, dh)
    return jnp.concatenate([out_l, out_c], axis=0)


def _hy_conv_kernel(v_ref, x1_ref, x2_ref, w_ref, b_ref, vo_ref, x1o_ref, x2o_ref):
    L, W = v_ref.shape
    row = lax.broadcasted_iota(jnp.int32, (L, 1), 0)
    for j, (i_ref, o_ref) in enumerate(((v_ref, vo_ref), (x1_ref, x1o_ref), (x2_ref, x2o_ref))):
        lanes = slice(j * W, (j + 1) * W)
        u = i_ref[...]
        prev = jnp.where(row == 0, 0.0, pltpu.roll(u, 1, 0))
        nxt = jnp.where(row == L - 1, 0.0, pltpu.roll(u, L - 1, 0))
        o_ref[...] = prev * w_ref[0:1, lanes] + u * w_ref[1:2, lanes] + nxt * w_ref[2:3, lanes] + b_ref[:, lanes]


def _hy_short_conv(u, conv_w, conv_b, n_seg, L, blk0, col0):
    W = conv_w.shape[1] // 3
    cb = col0 // W
    one = lambda b: (0, 0)

    def ucol(j):
        return pl.BlockSpec((L, W), lambda b: (blk0 + b, cb + j))

    return pl.pallas_call(
        _hy_conv_kernel,
        grid=(n_seg,),
        in_specs=[ucol(0), ucol(1), ucol(2), pl.BlockSpec(conv_w.shape, one), pl.BlockSpec(conv_b.shape, one)],
        out_specs=[pl.BlockSpec((L, W), lambda b: (b, 0))] * 3,
        out_shape=[jax.ShapeDtypeStruct((n_seg * L, W), F32)] * 3,
        compiler_params=_cparams(("parallel",)),
        name="hy_short_conv",
    )(u, u, u, conv_w, conv_b)


def _hy_fwd_kernel(c_ref, s_ref, z_ref, *rest, raw):
    z = z_ref[...].astype(BF16)
    zr = _dot(c_ref[...], z)
    zi = _dot(s_ref[...], z)
    if raw:
        yr_ref, yi_ref = rest
        yr_ref[...] = zr
        yi_ref[...] = zi
    else:
        a_ref, b_ref, d_ref, yr_ref, yi_ref = rest
        yr_ref[...] = (zr * a_ref[...] - zi * b_ref[...]).astype(yr_ref.dtype)
        yi_ref[...] = (zr * b_ref[...] + zi * d_ref[...]).astype(yi_ref.dtype)


def _hy_fwd(cm, sm, z, coefs, n_seg, L, tk):
    W = z.shape[1]
    nk = L // tk
    raw = coefs is None
    mat = pl.BlockSpec((tk, L), lambda i, b: (i, 0))
    cf = pl.BlockSpec((tk, W), lambda i, b: (i, 0))
    out = pl.BlockSpec((tk, W), lambda i, b: (b * nk + i, 0))
    odt = F32 if raw else BF16
    return pl.pallas_call(
        functools.partial(_hy_fwd_kernel, raw=raw),
        grid=(nk, n_seg),
        in_specs=[mat, mat, pl.BlockSpec((L, W), lambda i, b: (b, 0))] + ([] if raw else [cf, cf, cf]),
        out_specs=[out, out],
        out_shape=[jax.ShapeDtypeStruct((n_seg * L, W), odt)] * 2,
        compiler_params=_cparams(("parallel", "arbitrary")),
        name="hy_dft_fwd",
    )(cm, sm, z, *(() if raw else coefs))


def _hy_inv_kernel(c_ref, st_ref, yr_ref, yi_ref, x_ref, vz_ref, skip_ref, o_ref):
    y = _dot(c_ref[...], yr_ref[...]) + _dot(st_ref[...], yi_ref[...])
    o_ref[...] = (x_ref[...] * (y + skip_ref[...] * vz_ref[...])).astype(o_ref.dtype)


def _hy_inv(cm, smt, yr, yi, xg, vz, skip, n_seg, L, tt, out_dtype):
    W = yr.shape[1]
    nt = L // tt
    mat = pl.BlockSpec((tt, L), lambda i, b: (i, 0))
    seq = pl.BlockSpec((L, W), lambda i, b: (b, 0))
    row = pl.BlockSpec((tt, W), lambda i, b: (b * nt + i, 0))
    return pl.pallas_call(
        _hy_inv_kernel,
        grid=(nt, n_seg),
        in_specs=[mat, mat, seq, seq, row, row, pl.BlockSpec((1, W), lambda i, b: (0, 0))],
        out_specs=row,
        out_shape=jax.ShapeDtypeStruct((n_seg * L, W), out_dtype),
        compiler_params=_cparams(("parallel", "arbitrary")),
        name="hy_dft_inv",
    )(cm, smt, yr, yi, xg, vz, skip)


def _dft_mats(L):
    split = 64
    n = jnp.arange(L, dtype=jnp.int32)

    def trig(mult):
        ang = ((mult[:, None] * n[None, :]) % (2 * L)).astype(F32) * (math.pi / L)
        return jnp.cos(ang), jnp.sin(ang)

    ca, sa = trig(split * jnp.arange(L // split, dtype=jnp.int32))
    cb, sb = trig(jnp.arange(split, dtype=jnp.int32))
    cm = (ca[:, None, :] * cb[None, :, :] - sa[:, None, :] * sb[None, :, :]).reshape(L, L)
    sm = -(sa[:, None, :] * cb[None, :, :] + ca[:, None, :] * sb[None, :, :]).reshape(L, L)
    nyq = jnp.where(n % 2 == 0, 1.0, -1.0)
    first_row = n[:, None] == 0
    first_col = n[None, :] == 0
    return (cm.astype(BF16), jnp.where(first_row, nyq[None, :], sm).astype(BF16),
            jnp.where(first_col, nyq[:, None], sm).astype(BF16))


def _hyena_filters(L, w1, b1, w2, b2, w3, freq, W):
    t01 = jnp.linspace(0.0, 1.0, L, dtype=F32)[:, None]
    wpos = (2.0 * math.pi / L) * jnp.arange(L, dtype=F32)[:, None]
    bands = jnp.linspace(1e-4, HY_BANDS - 1, HY_BANDS, dtype=F32)
    feats = jnp.concatenate([t01, jnp.cos(wpos * bands), -jnp.sin(wpos * bands)], axis=-1)
    hp = lax.Precision.HIGHEST
    h = jnp.sin(freq[0] * (jnp.dot(feats, w1, precision=hp) + b1))
    h = jnp.sin(freq[1] * (jnp.dot(h, w2, precision=hp) + b2))
    h = jnp.dot(h, w3, precision=hp).reshape(L, HY_ORDER, 2, W)
    deltas = jnp.abs(jnp.linspace(math.log(HY_TARGET) / HY_SLOW, math.log(HY_TARGET) / HY_FAST, W, dtype=F32))
    h = h * (jnp.exp(-t01 * deltas) + HY_SHIFT)[:, None, None, :]
    hf, hb = h[:, :, 0], h[:, :, 1]
    hf = hf.at[0].add(hb[0])
    hb = hb.at[0].set(0.0)
    scale = lax.rsqrt(jnp.sum(hf * hf, axis=0, keepdims=True) + jnp.sum(hb * hb, axis=0, keepdims=True) + EPS)
    return (hf * scale).reshape(L, HY_ORDER * W), (hb * scale).reshape(L, HY_ORDER * W)


def _hyena_spectrum(mats, L, w1, b1, w2, b2, w3, freq, W):
    cm, sm, _ = mats
    hf, hb = _hyena_filters(L, w1, b1, w2, b2, w3, freq, W)
    cols = jnp.concatenate([hf[:, :W], hf[:, W:], hb[:, :W], hb[:, W:]], axis=0)
    tk = min(L, 512)
    gr, gi = _hy_fwd(cm, sm, cols, None, 2 * HY_ORDER, L, tk)
    gr = gr.reshape(2, HY_ORDER, L, W)
    gi = gi.reshape(2, HY_ORDER, L, W)
    kr = gr[0] + gr[1]
    ki = gi[0] - gi[1]
    nyq = gi[0, :, 0] + gi[1, :, 0]
    n = 2.0 * L
    wk = jnp.full((L, 1), 2.0 / n, F32).at[0].set(1.0 / n)
    a = kr * wk
    bm = (ki * wk).at[:, 0].set(0.0)
    dd = a.at[:, 0].set(nyq / n)
    return [(a[o], bm[o], dd[o]) for o in range(HY_ORDER)]


def _hyena_seq(u, mats, spec, conv_w, conv_b, skip, n_seg, L, blk0, col0):
    cm, sm, smt = mats
    t = min(L, 512)
    v, x1, x2 = _hy_short_conv(u, conv_w, conv_b, n_seg, L, blk0, col0)
    yr, yi = _hy_fwd(cm, sm, v, spec[0], n_seg, L, t)
    z = _hy_inv(cm, smt, yr, yi, x1, v, skip[0:1], n_seg, L, t, F32)
    yr, yi = _hy_fwd(cm, sm, z, spec[1], n_seg, L, t)
    return _hy_inv(cm, smt, yr, yi, x2, z, skip[1:2], n_seg, L, t, BF16)


def _pack_rows(x):
    n = x.shape[1] // 2
    lo = pltpu.bitcast(x[:, :n].astype(BF16).astype(F32), jnp.uint32)
    hi = pltpu.bitcast(x[:, n:].astype(BF16).astype(F32), jnp.uint32)
    return (lo >> 16) | (hi & jnp.uint32(0xFFFF0000))


def _unpack_rows(p):
    return pltpu.bitcast(p << 16, F32), pltpu.bitcast(p & jnp.uint32(0xFFFF0000), F32)


def _moe_router_kernel(x_ref, g_ref, sc_ref, sh_ref, whi_ref, wlo_ref, rb_ref, h_ref, idx_ref, wts_ref, cnt_ref):
    x = x_ref[...]
    y = x * lax.rsqrt(jnp.mean(x * x, axis=-1, keepdims=True) + EPS) * g_ref[...]
    h = y * (1.0 + sc_ref[0]) + sh_ref[0]
    h_ref[...] = _pack_rows(h)
    hi, lo = _split2(h)
    logits = _dot_nt(whi_ref[...], hi) + _dot_nt(wlo_ref[...], hi) + _dot_nt(whi_ref[...], lo)
    scores = jax.nn.sigmoid(logits)
    sel = scores + rb_ref[...]
    E, tm = sel.shape
    gsz = E // N_GROUPS
    neg = -jnp.inf
    erow = lax.broadcasted_iota(jnp.int32, (E, 1), 0).astype(F32)
    grow = lax.broadcasted_iota(jnp.int32, (gsz, 1), 0).astype(F32)
    blocks = [sel[g * gsz:(g + 1) * gsz] for g in range(N_GROUPS)]
    gscore = []
    for blk in blocks:
        m1 = jnp.max(blk, axis=0, keepdims=True)
        first = jnp.min(jnp.where(blk == m1, grow, float(gsz)), axis=0, keepdims=True)
        m2 = jnp.max(jnp.where(grow == first, neg, blk), axis=0, keepdims=True)
        gscore.append(m1 + m2)
    kept = []
    for g in range(N_GROUPS):
        rank = jnp.zeros((1, tm), F32)
        for o in range(N_GROUPS):
            if o != g:
                ahead = (gscore[o] >= gscore[g]) if o < g else (gscore[o] > gscore[g])
                rank = rank + jnp.where(ahead, 1.0, 0.0)
        kept.append(jnp.where(rank < TOPK_GROUPS, blocks[g], neg))
    work = jnp.concatenate(kept, axis=0)
    ids, ws = [], []
    total = jnp.zeros((1, tm), F32)
    chosen = jnp.zeros((E, tm), F32)
    for j in range(TOP_K):
        mx = jnp.max(work, axis=0, keepdims=True)
        am = jnp.min(jnp.where(work == mx, erow, float(E)), axis=0, keepdims=True)
        hit = erow == am
        wj = jnp.sum(jnp.where(hit, scores, 0.0), axis=0, keepdims=True)
        work = jnp.where(hit, neg, work)
        chosen = jnp.where(hit, 1.0, chosen)
        ids.append(am)
        ws.append(wj)
        total = total + wj
    idx_ref[...] = jnp.concatenate(ids, axis=0).astype(jnp.int32)
    wts_ref[...] = jnp.concatenate(ws, axis=0) / total * ROUTED_SCALE
    ones = jnp.ones((8, tm), BF16)
    cnt_ref[0] = _dot_nt(ones, chosen.astype(BF16))[0:1]


def _moe_router(x, n_tok, g, scale, shift, whi, wlo, rb, seg_rows, tm):
    D = x.shape[1]
    E = whi.shape[0]
    K = TOP_K
    last = scale.shape[0] - 1
    row = lambda i: (i, 0)
    col = lambda i: (0, i)
    one = lambda i: (0, 0)
    mod = lambda i: (jnp.minimum(i * tm // seg_rows, last), 0, 0)
    return pl.pallas_call(
        _moe_router_kernel,
        grid=(n_tok // tm,),
        in_specs=[pl.BlockSpec((tm, D), row), pl.BlockSpec((1, D), one), pl.BlockSpec((1, 1, D), mod),
                  pl.BlockSpec((1, 1, D), mod), pl.BlockSpec((E, D), one), pl.BlockSpec((E, D), one),
                  pl.BlockSpec((E, 1), one)],
        out_specs=[pl.BlockSpec((tm, D // 2), row), pl.BlockSpec((K, tm), col), pl.BlockSpec((K, tm), col),
                   pl.BlockSpec((1, 1, E), lambda i: (i, 0, 0))],
        out_shape=[jax.ShapeDtypeStruct((n_tok, D // 2), jnp.uint32), jax.ShapeDtypeStruct((K, n_tok), jnp.int32),
                   jax.ShapeDtypeStruct((K, n_tok), F32), jax.ShapeDtypeStruct((n_tok // tm, 1, E), F32)],
        compiler_params=_cparams(("parallel",)),
        name="moe_router",
    )(x, g, scale, shift, whi, wlo, rb)


def _moe_pos_kernel(idx_ref, base_ref, pos_ref):
    K, tm = idx_ref.shape
    E = base_ref.shape[1]
    idx = idx_ref[...]
    erow = lax.broadcasted_iota(jnp.int32, (E, 1), 0)
    hits = [erow == idx[j:j + 1, :] for j in range(K)]
    onehot = jnp.zeros((E, tm), F32)
    for hit in hits:
        onehot = jnp.where(hit, 1.0, onehot)
    row = lax.broadcasted_iota(jnp.int32, (tm, tm), 0)
    col = lax.broadcasted_iota(jnp.int32, (tm, tm), 1)
    before = _dot(onehot.astype(BF16), (row < col).astype(BF16))
    dest = before + base_ref[0]
    pos = [jnp.sum(jnp.where(hit, dest, 0.0), axis=0, keepdims=True) for hit in hits]
    pos_ref[...] = jnp.concatenate(pos, axis=0).astype(jnp.int32)


def _moe_plan(idx, cnt, bm, tm):
    K, n_tok = idx.shape
    E = cnt.shape[2]
    cnt = cnt[:, 0, :]
    counts = jnp.sum(cnt, axis=0)
    pcounts = jnp.ceil(counts / bm) * bm
    pends = jnp.cumsum(pcounts)
    base = (pends - pcounts)[None, :] + jnp.cumsum(cnt, axis=0) - cnt
    pos = pl.pallas_call(
        _moe_pos_kernel,
        grid=(n_tok // tm,),
        in_specs=[pl.BlockSpec((K, tm), lambda i: (0, i)), pl.BlockSpec((1, E, 1), lambda i: (i, 0, 0))],
        out_specs=pl.BlockSpec((K, tm), lambda i: (0, i)),
        out_shape=jax.ShapeDtypeStruct((K, n_tok), jnp.int32),
        compiler_params=_cparams(("parallel",)),
        name="moe_positions",
    )(idx, base[:, :, None])
    n_blocks = -(-(n_tok * K + E * (bm - 1)) // bm)
    block_start = jnp.concatenate([jnp.zeros((1,), F32), pends / bm]).astype(jnp.int32)
    return pos.T.reshape(-1), block_start, n_blocks


def _per_token_rows(tm, K, copy):
    def start(t, carry):
        for j in range(K):
            copy(t, j).start(priority=j % 2)
        return carry

    def wait(t, carry):
        for j in range(K):
            copy(t, j).wait()
        return carry

    return (lambda: lax.fori_loop(0, tm, start, 0, unroll=2)), (lambda: lax.fori_loop(0, tm, wait, 0, unroll=2))


def _moe_dispatch_kernel(pos_ref, prev_ref, h_ref, xs_in_ref, xs_hbm, stage_ref, sems, *, K):
    del xs_in_ref
    tm = h_ref.shape[0]
    i = pl.program_id(0)
    last = pl.num_programs(0) - 1
    slot = lax.bitwise_and(i, 1)

    def tile_copies(s, p_ref):
        def row_copy(t, j):
            return pltpu.make_async_copy(stage_ref.at[s, pl.ds(t, 1)], xs_hbm.at[pl.ds(p_ref[t * K + j], 1)],
                                         sems.at[s])
        return _per_token_rows(tm, K, row_copy)

    start, wait = tile_copies(slot, pos_ref)
    _, wait_prev = tile_copies(1 - slot, prev_ref)
    stage_ref[slot] = h_ref[...]
    start()

    @pl.when(i > 0)
    def _():
        wait_prev()

    @pl.when(i == last)
    def _():
        wait()


def _moe_dispatch(pos, h, n_rows, K, tm):
    n_tok, W = h.shape
    xs0 = jnp.zeros((n_rows, W), h.dtype)
    return pl.pallas_call(
        functools.partial(_moe_dispatch_kernel, K=K),
        grid=(n_tok // tm,),
        in_specs=[pl.BlockSpec((tm * K,), lambda i: (i,), memory_space=pltpu.SMEM),
                  pl.BlockSpec((tm * K,), lambda i: (jnp.maximum(i - 1, 0),), memory_space=pltpu.SMEM),
                  pl.BlockSpec((tm, W), lambda i: (i, 0)),
                  pl.BlockSpec(memory_space=pl.ANY)],
        out_specs=pl.BlockSpec(memory_space=pl.ANY),
        out_shape=jax.ShapeDtypeStruct((n_rows, W), h.dtype),
        scratch_shapes=[pltpu.VMEM((2, tm, W), h.dtype), pltpu.SemaphoreType.DMA((2,))],
        input_output_aliases={3: 0},
        compiler_params=_cparams(("arbitrary",)),
        name="moe_dispatch",
    )(pos, pos, h, xs0)


def _moe_expert_kernel(bs_ref, x_hbm, w1_ref, w3_ref, w2_ref, y_hbm, xbuf, ybuf, w1b_ref, w3b_ref, w2b_ref,
                       xsem, ysem, *, bm, nbuf, n_blocks):
    e = pl.program_id(0)
    last = pl.num_programs(0) - 1
    b0 = bs_ref[e]
    b1 = bs_ref[e + 1]
    n_used = bs_ref[last + 1]

    def x_copy(g):
        slot = lax.bitwise_and(g, nbuf - 1)
        return pltpu.make_async_copy(x_hbm.at[pl.ds(g * bm, bm)], xbuf.at[slot], xsem.at[slot])

    def y_copy(g):
        slot = lax.bitwise_and(g, nbuf - 1)
        return pltpu.make_async_copy(ybuf.at[slot], y_hbm.at[pl.ds(g * bm, bm)], ysem.at[slot])

    @pl.when(e == 0)
    def _():
        for p in range(nbuf - 1):
            @pl.when(p < n_used)
            def _():
                x_copy(p).start()

    @pl.when(b1 > b0)
    def _():
        w1b_ref[...] = w1_ref[0, 0].astype(BF16)
        w3b_ref[...] = w3_ref[0, 0].astype(BF16)
        w2b_ref[...] = w2_ref[0, 0].astype(BF16)

    def block(g, carry):
        slot = lax.bitwise_and(g, nbuf - 1)
        x_copy(g).wait()

        @pl.when(g + nbuf - 1 < n_used)
        def _():
            x_copy(g + nbuf - 1).start()

        @pl.when(g >= nbuf)
        def _():
            y_copy(g - nbuf).wait()

        x = jnp.concatenate(_unpack_rows(xbuf[slot]), axis=1).astype(BF16)
        a = _dot(x, w1b_ref[...])
        b = _dot(x, w3b_ref[...])
        ybuf[slot] = _pack_rows(_dot((jax.nn.silu(a) * b).astype(BF16), w2b_ref[...]))
        y_copy(g).start()
        return carry

    lax.fori_loop(b0, b1, block, 0)

    @pl.when(e == last)
    def _():
        def drain(g, carry):
            y_copy(g).wait()
            return carry

        lax.fori_loop(jnp.maximum(n_used - nbuf, 0), n_used, drain, 0)
        ybuf[0] = jnp.zeros(ybuf.shape[1:], ybuf.dtype)

        def zero_copy(g):
            return pltpu.make_async_copy(ybuf.at[0], y_hbm.at[pl.ds(g * bm, bm)], ysem.at[0])

        def fill(g, carry):
            zero_copy(g).start()
            return carry

        def fill_wait(g, carry):
            zero_copy(g).wait()
            return carry

        lax.fori_loop(n_used, n_blocks, fill, 0)
        lax.fori_loop(n_used, n_blocks, fill_wait, 0)


def _moe_experts(block_start, xs, w1, w3, w2, layer, bm):
    P, W = xs.shape
    E, D, F = w1.shape[1:]
    nbuf = 4
    grid_spec = pltpu.PrefetchScalarGridSpec(
        num_scalar_prefetch=1,
        grid=(E,),
        in_specs=[pl.BlockSpec(memory_space=pl.ANY),
                  pl.BlockSpec((1, 1, D, F), lambda e, bs: (layer, e, 0, 0)),
                  pl.BlockSpec((1, 1, D, F), lambda e, bs: (layer, e, 0, 0)),
                  pl.BlockSpec((1, 1, F, D), lambda e, bs: (layer, e, 0, 0))],
        out_specs=pl.BlockSpec(memory_space=pl.ANY),
        scratch_shapes=[pltpu.VMEM((nbuf, bm, W), xs.dtype), pltpu.VMEM((nbuf, bm, W), xs.dtype),
                        pltpu.VMEM((D, F), BF16), pltpu.VMEM((D, F), BF16), pltpu.VMEM((F, D), BF16),
                        pltpu.SemaphoreType.DMA((nbuf,)), pltpu.SemaphoreType.DMA((nbuf,))],
    )
    return pl.pallas_call(
        functools.partial(_moe_expert_kernel, bm=bm, nbuf=nbuf, n_blocks=P // bm),
        grid_spec=grid_spec,
        out_shape=jax.ShapeDtypeStruct((P, W), xs.dtype),
        compiler_params=_cparams(("arbitrary",)),
        name="moe_experts",
    )(block_start, xs, w1, w3, w2)


def _moe_combine_kernel(pos_ref, next_ref, wts_ref, x_ref, h_ref, sw1_ref, sw3_ref, sw2_ref, gate_ref, ys_hbm, o_ref,
                        gath_ref, sems, *, K):
    tm = x_ref.shape[0]
    i = pl.program_id(0)
    last = pl.num_programs(0) - 1

    slot = lax.bitwise_and(i, 1)

    def tile_copies(s, p_ref):
        def row_copy(t, j):
            return pltpu.make_async_copy(ys_hbm.at[pl.ds(p_ref[t * K + j], 1)], gath_ref.at[s, j, pl.ds(t, 1)],
                                         sems.at[s])
        return _per_token_rows(tm, K, row_copy)

    start, wait = tile_copies(slot, pos_ref)
    start_next, _ = tile_copies(1 - slot, next_ref)

    @pl.when(i == 0)
    def _():
        start()

    @pl.when(i < last)
    def _():
        start_next()

    h = jnp.concatenate(_unpack_rows(h_ref[...]), axis=1).astype(BF16)
    mid = jax.nn.silu(_dot(h, sw1_ref[...])) * _dot(h, sw3_ref[...])
    shared = _dot(mid.astype(BF16), sw2_ref[...])
    wait()
    wts = wts_ref[...]
    W = gath_ref.shape[3]
    acc_lo = shared[:, :W]
    acc_hi = shared[:, W:]
    for j in range(K):
        lo, hi = _unpack_rows(gath_ref[slot, j])
        acc_lo = acc_lo + wts[:, j:j + 1] * lo
        acc_hi = acc_hi + wts[:, j:j + 1] * hi
    gate = gate_ref[0]
    o_ref[:, :W] = x_ref[:, :W] + gate[:, :W] * acc_lo
    o_ref[:, W:] = x_ref[:, W:] + gate[:, W:] * acc_hi


def _moe_combine(pos, wts, x, h, sw1, sw3, sw2, gate, ys, n_tok, K, seg_rows, tm):
    D = x.shape[1]
    W = ys.shape[1]
    n_tiles = n_tok // tm
    last = gate.shape[0] - 1
    row = lambda i: (i, 0)
    one = lambda i: (0, 0)
    return pl.pallas_call(
        functools.partial(_moe_combine_kernel, K=K),
        grid=(n_tiles,),
        in_specs=[pl.BlockSpec((tm * K,), lambda i: (i,), memory_space=pltpu.SMEM),
                  pl.BlockSpec((tm * K,), lambda i: (jnp.minimum(i + 1, n_tiles - 1),), memory_space=pltpu.SMEM),
                  pl.BlockSpec((tm, K), row), pl.BlockSpec((tm, D), row), pl.BlockSpec((tm, W), row),
                  pl.BlockSpec(sw1.shape, one), pl.BlockSpec(sw3.shape, one), pl.BlockSpec(sw2.shape, one),
                  pl.BlockSpec((1, 1, D), lambda i: (jnp.minimum(i * tm // seg_rows, last), 0, 0)),
                  pl.BlockSpec(memory_space=pl.ANY)],
        out_specs=pl.BlockSpec((tm, D), row),
        out_shape=jax.ShapeDtypeStruct((n_tok, D), F32),
        scratch_shapes=[pltpu.VMEM((2, K, tm, W), ys.dtype), pltpu.SemaphoreType.DMA((2,))],
        compiler_params=_cparams(("arbitrary",)),
        name="moe_combine",
    )(pos, pos, wts, x, h, sw1, sw3, sw2, gate, ys)


def _moe_ffn_residual(x, n_tok, norm_g, scale, shift, gate, router_w, router_b, ew1, ew3, ew2, layer, sw1, sw3, sw2,
                      seg_rows, bm):
    K = TOP_K
    tm = 256
    whi, wlo = _split2(router_w.T)
    h, idx, wts, cnt = _moe_router(x, n_tok, norm_g, scale, shift, whi, wlo, router_b[:, None], seg_rows, tm)
    pos, block_start, n_blocks = _moe_plan(idx, cnt, bm, tm)
    wts = wts.T
    xs = _moe_dispatch(pos, h, n_blocks * bm, K, 128)
    ys = _moe_experts(block_start, xs, ew1, ew3, ew2, layer, bm)
    return _moe_combine(pos, wts, x, h, sw1.astype(BF16), sw3.astype(BF16), sw2.astype(BF16), gate, ys,
                        n_tok, K, seg_rows, 128)


def kernel(x, c, ctx, c_ctx, w_mod, b_mod, norm1_g, norm2_g, w_in, mlstm_gate_b, mlstm_norm_g, da_qnorm_g, da_knorm_g, da_lambda, da_subln_g, hy_conv_w, hy_conv_b, hy_w1, hy_b1, hy_w2, hy_b2, hy_w3, hy_freq, hy_skip, w_out, router_w, router_b, exp_w1, exp_w3, exp_w2, sh_w1, sh_w3, sh_w2):
    B, S, D = x.shape
    n_ctx = ctx.shape[1]
    depth = w_in.shape[0]
    n_lat = B * S
    ml_w = mlstm_norm_g.shape[1]
    da_dh = da_qnorm_g.shape[1]
    da_w = DA_HEADS * 2 * da_dh
    hy_w = hy_skip.shape[2]
    n_gates = 4 * ML_HEADS
    da_col = 0
    hy_col = da_col + 3 * da_w
    o_col = hy_col + 3 * hy_w
    gate_col = o_col + ml_w
    ml_in = 4 * ml_w + n_gates
    tm = 512

    X = jnp.concatenate([x.reshape(n_lat, D), ctx.reshape(B * n_ctx, D)], axis=0)
    sc = jax.nn.silu(jnp.concatenate([c, c_ctx[None]], axis=0))
    cos, sin = _axial_rope_tables(S, da_dh, da_w // da_dh, tm)
    mats_l = _dft_mats(S)
    mats_c = _dft_mats(n_ctx)
    for l in range(depth):
        last = l == depth - 1
        lam_init = 0.8 - 0.6 * math.exp(-0.3 * l)
        mods = (jnp.dot(sc, w_mod[l], precision=lax.Precision.HIGHEST) + b_mod[l]).reshape(B + 1, 6, 1, D)
        sh1, s1, g1, sh2, s2, g2 = [mods[:, i] for i in range(6)]
        wl = w_in[l]
        w_big = jnp.concatenate([wl[:, ml_in:], wl[:, 3 * ml_w:4 * ml_w], wl[:, 4 * ml_w:ml_in],
                                 jnp.zeros((D, LANES - n_gates), F32)], axis=1).astype(BF16)
        w_qkv_t = wl[:, :3 * ml_w].T.astype(BF16)
        U, UT = _norm_mod_matmul(X, norm1_g[l][None], s1, sh1, w_big, w_qkv_t, S, tm, w_big.shape[1] // 3)
        gb = jnp.concatenate([mlstm_gate_b[l], jnp.zeros((LANES - n_gates,), F32)])[None]
        m_out = _mlstm_mixer(UT, U, gb, mlstm_norm_g[l][:, None], B, S, n_ctx, o_col, gate_col)
        d_out = _diff_attn_mixer(U, cos, sin, da_qnorm_g[l], da_knorm_g[l], da_lambda[l], da_subln_g[l], lam_init,
                                 B, S, n_ctx, da_col, not last)
        hy_args = (hy_w1[l], hy_b1[l], hy_w2[l], hy_b2[l], hy_w3[l], hy_freq[l], hy_w)
        y_out = _hyena_seq(U, mats_l, _hyena_spectrum(mats_l, S, *hy_args), hy_conv_w[l], hy_conv_b[l][None],
                           hy_skip[l], B, S, 0, hy_col)
        n_rows = n_lat
        if not last:
            y_ctx = _hyena_seq(U, mats_c, _hyena_spectrum(mats_c, n_ctx, *hy_args), hy_conv_w[l], hy_conv_b[l][None],
                               hy_skip[l], B, n_ctx, n_lat // n_ctx, hy_col)
            y_out = jnp.concatenate([y_out, y_ctx], axis=0)
            n_rows = n_lat + B * n_ctx
        wo = w_out[l].astype(BF16)
        X = _out_proj_residual(m_out, d_out, y_out, wo[:ml_w], wo[ml_w:ml_w + da_w], wo[ml_w + da_w:], X, g1,
                               n_rows, S, tm)
        X = _moe_ffn_residual(X, n_rows, norm2_g[l][None], s2, sh2, g2, router_w[l], router_b[l],
                              exp_w1, exp_w3, exp_w2, l, sh_w1[l], sh_w3[l], sh_w2[l], S,
                              MOE_BLOCK if l == 0 else 2 * MOE_BLOCK)
    return X[:n_lat].reshape(B, S, D)
```

```python
import functools
import math

import jax
import jax.numpy as jnp
from jax import lax
from jax.experimental import pallas as pl
from jax.experimental.pallas import tpu as pltpu

F32 = jnp.float32
BF16 = jnp.bfloat16

EPS = 1e-6
GRID_W = 64
ROPE_THETA = 10000.0
ML_HEADS = 4
ML_CHUNK = 256
ML_M_INIT = -1e30
DA_HEADS = 4
HY_ORDER = 2
HY_BANDS = 8
HY_SHIFT = 0.05
HY_TARGET = 1e-2
HY_FAST = 0.3
HY_SLOW = 1.5
N_GROUPS = 8
TOPK_GROUPS = 4
TOP_K = 8
ROUTED_SCALE = 2.5
MOE_BLOCK = 256
LANES = 128
VMEM_LIMIT = 56 * 1024 * 1024


def _cparams(sem):
    return pltpu.CompilerParams(dimension_semantics=sem, vmem_limit_bytes=VMEM_LIMIT)


def _dot(a, b):
    return jnp.dot(a, b, preferred_element_type=F32)


def _dot_nt(a, b):
    return lax.dot_general(a, b, (((1,), (1,)), ((), ())), preferred_element_type=F32)


def _dot_tn(a, b):
    return lax.dot_general(a, b, (((0,), (0,)), ((), ())), preferred_element_type=F32)


def _split3(a):
    hi = a.astype(BF16)
    r = a - hi.astype(F32)
    mid = r.astype(BF16)
    lo = (r - mid.astype(F32)).astype(BF16)
    return hi, mid, lo


def _split2(a):
    hi = a.astype(BF16)
    lo = (a - hi.astype(F32)).astype(BF16)
    return hi, lo


def _norm_mod_mm_kernel(x_ref, g_ref, sc_ref, sh_ref, w_ref, wt_ref, o_ref, ot_ref, xn_ref):
    @pl.when(pl.program_id(1) == 0)
    def _():
        x = x_ref[...]
        y = x * lax.rsqrt(jnp.mean(x * x, axis=-1, keepdims=True) + EPS) * g_ref[...]
        xn_ref[...] = (y * (1.0 + sc_ref[0]) + sh_ref[0]).astype(BF16)
        ot_ref[...] = _dot_nt(wt_ref[...], xn_ref[...])

    o_ref[...] = _dot(xn_ref[...], w_ref[...])


def _norm_mod_matmul(x, g, scale, shift, w, wt, seg_rows, tm, tn):
    R, D = x.shape
    N = w.shape[1]
    NT = wt.shape[0]
    last = scale.shape[0] - 1
    mod_map = lambda i, j: (jnp.minimum(i * tm // seg_rows, last), 0, 0)
    return pl.pallas_call(
        _norm_mod_mm_kernel,
        grid=(R // tm, N // tn),
        in_specs=[
            pl.BlockSpec((tm, D), lambda i, j: (i, 0)),
            pl.BlockSpec((1, D), lambda i, j: (0, 0)),
            pl.BlockSpec((1, 1, D), mod_map),
            pl.BlockSpec((1, 1, D), mod_map),
            pl.BlockSpec((D, tn), lambda i, j: (0, j)),
            pl.BlockSpec((NT, D), lambda i, j: (0, 0)),
        ],
        out_specs=[pl.BlockSpec((tm, tn), lambda i, j: (i, j)), pl.BlockSpec((NT, tm), lambda i, j: (0, i))],
        out_shape=[jax.ShapeDtypeStruct((R, N), F32), jax.ShapeDtypeStruct((NT, R), F32)],
        scratch_shapes=[pltpu.VMEM((tm, D), BF16)],
        compiler_params=_cparams(("parallel", "arbitrary")),
        name="norm_mod_matmul",
    )(x, g, scale, shift, w, wt)


def _out_proj_kernel(m_ref, d_ref, y_ref, wm_ref, wd_ref, wy_ref, x_ref, gate_ref, o_ref):
    acc = _dot(m_ref[...], wm_ref[...]) + _dot(d_ref[...], wd_ref[...]) + _dot(y_ref[...], wy_ref[...])
    o_ref[...] = x_ref[...] + gate_ref[0] * acc


def _out_proj_residual(m, d, y, wm, wd, wy, x, gate, n_rows, seg_rows, tm):
    R, D = n_rows, x.shape[1]
    last = gate.shape[0] - 1
    row = lambda i: (i, 0)
    full = lambda i: (0, 0)
    return pl.pallas_call(
        _out_proj_kernel,
        grid=(R // tm,),
        in_specs=[
            pl.BlockSpec((tm, m.shape[1]), row),
            pl.BlockSpec((tm, d.shape[1]), row),
            pl.BlockSpec((tm, y.shape[1]), row),
            pl.BlockSpec(wm.shape, full),
            pl.BlockSpec(wd.shape, full),
            pl.BlockSpec(wy.shape, full),
            pl.BlockSpec((tm, D), row),
            pl.BlockSpec((1, 1, D), lambda i: (jnp.minimum(i * tm // seg_rows, last), 0, 0)),
        ],
        out_specs=pl.BlockSpec((tm, D), row),
        out_shape=jax.ShapeDtypeStruct((R, D), F32),
        compiler_params=_cparams(("parallel",)),
        name="out_proj_residual",
    )(m, d, y, wm, wd, wy, x, gate)


def _log_sigmoid(x):
    return jnp.minimum(x, 0.0) - jnp.log1p(jnp.exp(-jnp.abs(x)))


def _mlstm_gate_tables(g_ref, r0, L, gb, tril, triu):
    g = g_ref[pl.ds(r0, L), :] + gb
    lf = _log_sigmoid(g)
    gT = g.T
    lfT = lf.T
    parts = _split3(lf)
    partsT = _split3(lfT)
    cs_f = sum(_dot(tril, p) for p in parts)
    cs_b = sum(_dot(triu, p) for p in parts)
    rs_f = sum(_dot(p, triu) for p in partsT)
    rs_b = sum(_dot(p, tril) for p in partsT)
    return g, gT, cs_f, cs_b, rs_f, rs_b


def _mlstm_chunk(qT, kT, vT, i_row, b_row, c_col, b_end, mask, state):
    C, n, m = state
    qb = qT.astype(BF16)
    kb = kT.astype(BF16)
    dmat = jnp.where(mask, b_row + c_col, -jnp.inf)
    inter = b_row + m
    m_t = jnp.maximum(inter, jnp.max(dmat, axis=0, keepdims=True))
    s = _dot_tn(kb, qb) * jnp.exp(dmat - m_t)
    carry_w = jnp.exp(inter - m_t)
    num = _dot(vT.astype(BF16), s.astype(BF16)) + carry_w * _dot(C.astype(BF16), qb)
    den = jnp.sum(s, axis=0, keepdims=True) + carry_w * jnp.sum(qT * n, axis=0, keepdims=True)
    h = num / jnp.maximum(jnp.abs(den), jnp.exp(-m_t))
    g = b_end - b_row + i_row
    m_new = jnp.maximum(b_end + m, jnp.max(g, axis=-1, keepdims=True))
    ws = jnp.exp(g - m_new)
    decay = jnp.exp(b_end + m - m_new)
    C_new = decay * C + _dot_nt((vT * ws).astype(BF16), kb)
    n_new = decay * n + jnp.sum(kT * ws, axis=-1, keepdims=True)
    return h, (C_new, n_new, m_new)


def _mlstm_kernel(ql_ref, kl_ref, vl_ref, ol_ref, gl_ref, qc_ref, kc_ref, vc_ref, oc_ref, gc_ref,
                  gb_ref, ng_ref, outl_ref, outc_ref, hf_ref, hb_ref, *, L, H, dh):
    S = ql_ref.shape[1]
    n_ctx = qc_ref.shape[1]
    row = lax.broadcasted_iota(jnp.int32, (L, L), 0)
    col = lax.broadcasted_iota(jnp.int32, (L, L), 1)
    lower = col <= row
    upper = col >= row
    tril = lower.astype(BF16)
    triu = upper.astype(BF16)
    gb = gb_ref[...]
    k_scale = dh ** -0.5

    def both_dirs(refs_f, r0_f, refs_b, r0_b, hoff, state):
        new_state = []
        for d, (refs, r0, h_ref) in enumerate(((refs_f, r0_f, hf_ref), (refs_b, r0_b, hb_ref))):
            q_ref, k_ref, v_ref, g_ref = refs
            g, gT, cs_f, cs_b, rs_f, rs_b = _mlstm_gate_tables(g_ref, r0, L, gb, tril, triu)
            cs, rs, mask = (cs_f, rs_f, upper) if d == 0 else (cs_b, rs_b, lower)
            end = L - 1 if d == 0 else 0
            for hh in range(H):
                ic = 2 * d * H + hh
                fc = ic + H
                rows = slice(hh * dh, (hh + 1) * dh)
                qT = q_ref[rows, pl.ds(r0, L)]
                kT = k_ref[rows, pl.ds(r0, L)] * k_scale
                vT = v_ref[rows, pl.ds(r0, L)]
                h, st = _mlstm_chunk(qT, kT, vT, gT[ic:ic + 1, :], rs[fc:fc + 1, :],
                                     g[:, ic:ic + 1] - cs[:, fc:fc + 1], rs[fc:fc + 1, end:end + 1], mask,
                                     state[d * H + hh])
                off = hoff + r0
                h_ref[rows, pl.ds(off if isinstance(off, int) else pl.multiple_of(off, L), L)] = h
                new_state.append(st)
        return tuple(new_state)

    state = tuple((jnp.zeros((dh, dh), F32), jnp.zeros((dh, 1), F32), jnp.full((1, 1), ML_M_INIT, F32))
                  for _ in range(2 * H))
    ctx_refs = (qc_ref, kc_ref, vc_ref, gc_ref)
    lat_refs = (ql_ref, kl_ref, vl_ref, gl_ref)
    n_cc = n_ctx // L
    for c in range(n_cc):
        state = both_dirs(ctx_refs, c * L, ctx_refs, (n_cc - 1 - c) * L, 0, state)
    n_lc = S // L

    def body(c, st):
        r_f = pl.multiple_of(c * L, L)
        r_b = pl.multiple_of((n_lc - 1 - c) * L, L)
        return both_dirs(lat_refs, r_f, lat_refs, r_b, n_ctx, st)

    lax.fori_loop(0, n_lc, body, state)

    def finish(o_ref, out_ref, hoff, n_rows):
        def fbody(c, carry):
            r0 = pl.multiple_of(c * L, L)
            off = pl.multiple_of(hoff + r0, L)
            hs = hf_ref[:, pl.ds(off, L)] + hb_ref[:, pl.ds(off, L)]
            normed = []
            for hh in range(H):
                rows = slice(hh * dh, (hh + 1) * dh)
                hv = hs[rows]
                normed.append(hv * lax.rsqrt(jnp.mean(hv * hv, axis=0, keepdims=True) + EPS) * ng_ref[rows])
            hn = jnp.concatenate(normed, axis=0).T
            out_ref[pl.ds(r0, L), :] = (jax.nn.sigmoid(o_ref[pl.ds(r0, L), :]) * hn).astype(out_ref.dtype)
            return carry
        lax.fori_loop(0, n_rows // L, fbody, 0)

    finish(ol_ref, outl_ref, n_ctx, S)
    finish(oc_ref, outc_ref, 0, n_ctx)


def _mlstm_mixer(uT, u, gate_b, norm_g, B, S, n_ctx, o_col, gate_col):
    W = norm_g.shape[0]
    H = ML_HEADS
    dh = W // H
    ocb = o_col // W
    gcb = gate_col // LANES
    cblk = (B * S) // n_ctx

    def lat_t(j):
        return pl.BlockSpec((W, S), lambda b: (j, b))

    def ctx_t(j):
        return pl.BlockSpec((W, n_ctx), lambda b: (j, cblk + b))

    one = lambda b: (0, 0)
    out_l, out_c = pl.pallas_call(
        functools.partial(_mlstm_kernel, L=ML_CHUNK, H=H, dh=dh),
        grid=(B,),
        in_specs=[lat_t(0), lat_t(1), lat_t(2), pl.BlockSpec((S, W), lambda b: (b, ocb)),
                  pl.BlockSpec((S, LANES), lambda b: (b, gcb)),
                  ctx_t(0), ctx_t(1), ctx_t(2), pl.BlockSpec((n_ctx, W), lambda b: (cblk + b, ocb)),
                  pl.BlockSpec((n_ctx, LANES), lambda b: (cblk + b, gcb)),
                  pl.BlockSpec((1, LANES), one), pl.BlockSpec((W, 1), one)],
        out_specs=[pl.BlockSpec((S, W), lambda b: (b, 0)), pl.BlockSpec((n_ctx, W), lambda b: (b, 0))],
        out_shape=[jax.ShapeDtypeStruct((B * S, W), BF16), jax.ShapeDtypeStruct((B * n_ctx, W), BF16)],
        scratch_shapes=[pltpu.VMEM((W, n_ctx + S), F32), pltpu.VMEM((W, n_ctx + S), F32)],
        compiler_params=_cparams(("parallel",)),
        name="mlstm",
    )(uT, uT, uT, u, u, uT, uT, uT, u, u, gate_b, norm_g)
    return jnp.concatenate([out_l, out_c], axis=0)


def _da_prep_kernel(q_ref, k_ref, v_ref, cos_ref, sin_ref, qg_ref, kg_ref, seg_ref, qo_ref, ko_ref, vo_ref, *, dh):
    cos = cos_ref[...]
    sin = sin_ref[...]
    seg = seg_ref[...]
    W = q_ref.shape[1]
    lane = lax.broadcasted_iota(jnp.int32, (1, W), 1)
    quarter = dh // 4
    first = (lane % (2 * quarter)) < quarter

    def norm_rope(x, g):
        hi, lo = _split2(x * x)
        ms = (_dot(hi, seg) + _dot(lo, seg)) * (1.0 / dh)
        xn = x * lax.rsqrt(ms + EPS) * g
        rot = jnp.where(first, -pltpu.roll(xn, W - quarter, 1), pltpu.roll(xn, quarter, 1))
        return xn * cos + rot * sin

    qo_ref[...] = (norm_rope(q_ref[...], qg_ref[...]) * (dh ** -0.5 * math.log2(math.e))).astype(BF16)
    ko_ref[...] = norm_rope(k_ref[...], kg_ref[...]).astype(BF16)
    vo_ref[...] = v_ref[...].astype(BF16)


def _da_prep(u, cos, sin, qg, kg, seg, n_lat_rows, S, col0, tm, dh):
    R = u.shape[0]
    W = qg.shape[1]
    cb = col0 // W
    n_lat = n_lat_rows // tm
    per_seq = S // tm
    tab = lambda i: (jnp.where(i < n_lat, i % per_seq, per_seq), 0)
    one = lambda i: (0, 0)
    row = lambda i: (i, 0)

    def ucol(j):
        return pl.BlockSpec((tm, W), lambda i: (i, cb + j))

    return pl.pallas_call(
        functools.partial(_da_prep_kernel, dh=dh),
        grid=(R // tm,),
        in_specs=[ucol(0), ucol(1), ucol(2), pl.BlockSpec((tm, W), tab), pl.BlockSpec((tm, W), tab),
                  pl.BlockSpec((1, W), one), pl.BlockSpec((1, W), one), pl.BlockSpec((W, W), one)],
        out_specs=[pl.BlockSpec((tm, W), row)] * 3,
        out_shape=[jax.ShapeDtypeStruct((R, W), BF16)] * 3,
        compiler_params=_cparams(("parallel",)),
        name="da_prep",
    )(u, u, u, cos, sin, qg, kg, seg)


def _da_attn_kernel(*refs, n_kv, dh, lam_init):
    q_ref = refs[0]
    k_refs = refs[1:1 + n_kv]
    v_refs = refs[1 + n_kv:1 + 2 * n_kv]
    lam_ref, sg_ref, o_ref = refs[1 + 2 * n_kv:]
    lp = lam_ref[...]
    lam = (jnp.exp(jnp.sum(lp[0:1] * lp[1:2], axis=-1, keepdims=True))
           - jnp.exp(jnp.sum(lp[2:3] * lp[3:4], axis=-1, keepdims=True)) + lam_init)
    q = q_ref[...]
    acc = None
    nums, dens = [], []
    for mp in range(2):
        lanes = slice(mp * dh, (mp + 1) * dh)
        s = [_dot_nt(q[:, lanes], k_ref[:, lanes]) for k_ref in k_refs]
        mx = functools.reduce(jnp.maximum, [jnp.max(si, axis=-1, keepdims=True) for si in s])
        p = [jnp.exp2(si - mx) for si in s]
        nums.append(p)
        dens.append(sum(jnp.sum(pi, axis=-1, keepdims=True) for pi in p))
    c = lam * dens[0] / dens[1]
    for j in range(n_kv):
        a = (nums[0][j] - c * nums[1][j]).astype(BF16)
        t = _dot(a, v_refs[j][...])
        acc = t if acc is None else acc + t
    acc = acc / dens[0]
    o = acc * lax.rsqrt(jnp.mean(acc * acc, axis=-1, keepdims=True) + EPS) * sg_ref[...]
    o_ref[...] = (o * (1.0 - lam_init)).astype(o_ref.dtype)


def _da_attention(q, k, v, lam_p, subln_g, lam_init, B, q_rows, q_blk0, kv_segs, tq, dh):
    H = DA_HEADS
    vd = 2 * dh
    nq = q_rows // tq
    q0 = q_blk0

    def kv_spec(rows, blk0):
        return pl.BlockSpec((rows, vd), lambda b, h, i: (blk0 + b, h))

    kspecs = [kv_spec(r, b0) for r, b0 in kv_segs]
    one = lambda b, h, i: (0, 0)
    return pl.pallas_call(
        functools.partial(_da_attn_kernel, n_kv=len(kv_segs), dh=dh, lam_init=lam_init),
        grid=(B, H, nq),
        in_specs=[pl.BlockSpec((tq, vd), lambda b, h, i: (q0 + b * nq + i, h))] + kspecs + kspecs
                 + [pl.BlockSpec(lam_p.shape, one), pl.BlockSpec((1, vd), one)],
        out_specs=pl.BlockSpec((tq, vd), lambda b, h, i: (b * nq + i, h)),
        out_shape=jax.ShapeDtypeStruct((B * q_rows, H * vd), BF16),
        compiler_params=_cparams(("parallel", "parallel", "arbitrary")),
        name="da_attention",
    )(q, *([k] * len(kv_segs)), *([v] * len(kv_segs)), lam_p, subln_g)


def _axial_rope_tables(S, dh, reps, pad_rows):
    rows = S // GRID_W
    r = jnp.repeat(jnp.arange(rows, dtype=F32), GRID_W)
    col = jnp.tile(jnp.arange(GRID_W, dtype=F32), rows)
    n_freq = dh // 4
    inv = ROPE_THETA ** (-jnp.arange(n_freq, dtype=F32) / n_freq)
    ar = r[:, None] * inv
    ac = col[:, None] * inv
    ang = jnp.concatenate([ar, ar, ac, ac], axis=-1)
    cos = jnp.concatenate([jnp.tile(jnp.cos(ang), (1, reps)), jnp.ones((pad_rows, dh * reps), F32)], axis=0)
    sin = jnp.concatenate([jnp.tile(jnp.sin(ang), (1, reps)), jnp.zeros((pad_rows, dh * reps), F32)], axis=0)
    return cos, sin


def _diff_attn_mixer(u, cos, sin, qg, kg, lam_p, subln_g, lam_init, B, S, n_ctx, col0, need_ctx):
    dh = qg.shape[0]
    W = DA_HEADS * 2 * dh
    seg = (jnp.arange(W)[:, None] // dh == jnp.arange(W)[None, :] // dh).astype(BF16)
    tm = 512
    q, k, v = _da_prep(u, cos, sin, jnp.tile(qg, W // dh)[None], jnp.tile(kg, W // dh)[None], seg,
                       B * S, S, col0, tm, dh)
    sg = subln_g[None]
    ctx_blk0 = (B * S) // n_ctx
    tq = 256
    out_l = _da_attention(q, k, v, lam_p, sg, lam_init, B, S, 0, [(n_ctx, ctx_blk0), (S, 0)], tq, dh)
    if not need_ctx:
        return out_l
    out_c = _da_attention(q, k, v, lam_p, sg, lam_init, B, n_ctx, (B * S) // n_ctx, [(n_ctx, ctx_blk0)], n_ctx, dh)
    return jnp.concatenate([out_l, out_c], axis=0)


def _hy_conv_kernel(v_ref, x1_ref, x2_ref, w_ref, b_ref, vo_ref, x1o_ref, x2o_ref):
    L, W = v_ref.shape
    row = lax.broadcasted_iota(jnp.int32, (L, 1), 0)
    for j, (i_ref, o_ref) in enumerate(((v_ref, vo_ref), (x1_ref, x1o_ref), (x2_ref, x2o_ref))):
        lanes = slice(j * W, (j + 1) * W)
        u = i_ref[...]
        prev = jnp.where(row == 0, 0.0, pltpu.roll(u, 1, 0))
        nxt = jnp.where(row == L - 1, 0.0, pltpu.roll(u, L - 1, 0))
        o_ref[...] = prev * w_ref[0:1, lanes] + u * w_ref[1:2, lanes] + nxt * w_ref[2:3, lanes] + b_ref[:, lanes]


def _hy_short_conv(u, conv_w, conv_b, n_seg, L, blk0, col0):
    W = conv_w.shape[1] // 3
    cb = col0 // W
    one = lambda b: (0, 0)

    def ucol(j):
        return pl.BlockSpec((L, W), lambda b: (blk0 + b, cb + j))

    return pl.pallas_call(
        _hy_conv_kernel,
        grid=(n_seg,),
        in_specs=[ucol(0), ucol(1), ucol(2), pl.BlockSpec(conv_w.shape, one), pl.BlockSpec(conv_b.shape, one)],
        out_specs=[pl.BlockSpec((L, W), lambda b: (b, 0))] * 3,
        out_shape=[jax.ShapeDtypeStruct((n_seg * L, W), F32)] * 3,
        compiler_params=_cparams(("parallel",)),
        name="hy_short_conv",
    )(u, u, u, conv_w, conv_b)


def _hy_fwd_kernel(c_ref, s_ref, z_ref, *rest, raw):
    z = z_ref[...].astype(BF16)
    zr = _dot(c_ref[...], z)
    zi = _dot(s_ref[...], z)
    if raw:
        yr_ref, yi_ref = rest
        yr_ref[...] = zr
        yi_ref[...] = zi
    else:
        a_ref, b_ref, d_ref, yr_ref, yi_ref = rest
        yr_ref[...] = (zr * a_ref[...] - zi * b_ref[...]).astype(yr_ref.dtype)
        yi_ref[...] = (zr * b_ref[...] + zi * d_ref[...]).astype(yi_ref.dtype)


def _hy_fwd(cm, sm, z, coefs, n_seg, L, tk):
    W = z.shape[1]
    nk = L // tk
    raw = coefs is None
    mat = pl.BlockSpec((tk, L), lambda i, b: (i, 0))
    cf = pl.BlockSpec((tk, W), lambda i, b: (i, 0))
    out = pl.BlockSpec((tk, W), lambda i, b: (b * nk + i, 0))
    odt = F32 if raw else BF16
    return pl.pallas_call(
        functools.partial(_hy_fwd_kernel, raw=raw),
        grid=(nk, n_seg),
        in_specs=[mat, mat, pl.BlockSpec((L, W), lambda i, b: (b, 0))] + ([] if raw else [cf, cf, cf]),
        out_specs=[out, out],
        out_shape=[jax.ShapeDtypeStruct((n_seg * L, W), odt)] * 2,
        compiler_params=_cparams(("parallel", "arbitrary")),
        name="hy_dft_fwd",
    )(cm, sm, z, *(() if raw else coefs))


def _hy_inv_kernel(c_ref, st_ref, yr_ref, yi_ref, x_ref, vz_ref, skip_ref, o_ref):
    y = _dot(c_ref[...], yr_ref[...]) + _dot(st_ref[...], yi_ref[...])
    o_ref[...] = (x_ref[...] * (y + skip_ref[...] * vz_ref[...])).astype(o_ref.dtype)


def _hy_inv(cm, smt, yr, yi, xg, vz, skip, n_seg, L, tt, out_dtype):
    W = yr.shape[1]
    nt = L // tt
    mat = pl.BlockSpec((tt, L), lambda i, b: (i, 0))
    seq = pl.BlockSpec((L, W), lambda i, b: (b, 0))
    row = pl.BlockSpec((tt, W), lambda i, b: (b * nt + i, 0))
    return pl.pallas_call(
        _hy_inv_kernel,
        grid=(nt, n_seg),
        in_specs=[mat, mat, seq, seq, row, row, pl.BlockSpec((1, W), lambda i, b: (0, 0))],
        out_specs=row,
        out_shape=jax.ShapeDtypeStruct((n_seg * L, W), out_dtype),
        compiler_params=_cparams(("parallel", "arbitrary")),
        name="hy_dft_inv",
    )(cm, smt, yr, yi, xg, vz, skip)


def _dft_mats(L):
    split = 64
    n = jnp.arange(L, dtype=jnp.int32)

    def trig(mult):
        ang = ((mult[:, None] * n[None, :]) % (2 * L)).astype(F32) * (math.pi / L)
        return jnp.cos(ang), jnp.sin(ang)

    ca, sa = trig(split * jnp.arange(L // split, dtype=jnp.int32))
    cb, sb = trig(jnp.arange(split, dtype=jnp.int32))
    cm = (ca[:, None, :] * cb[None, :, :] - sa[:, None, :] * sb[None, :, :]).reshape(L, L)
    sm = -(sa[:, None, :] * cb[None, :, :] + ca[:, None, :] * sb[None, :, :]).reshape(L, L)
    nyq = jnp.where(n % 2 == 0, 1.0, -1.0)
    first_row = n[:, None] == 0
    first_col = n[None, :] == 0
    return (cm.astype(BF16), jnp.where(first_row, nyq[None, :], sm).astype(BF16),
            jnp.where(first_col, nyq[:, None], sm).astype(BF16))


def _hyena_filters(L, w1, b1, w2, b2, w3, freq, W):
    t01 = jnp.linspace(0.0, 1.0, L, dtype=F32)[:, None]
    wpos = (2.0 * math.pi / L) * jnp.arange(L, dtype=F32)[:, None]
    bands = jnp.linspace(1e-4, HY_BANDS - 1, HY_BANDS, dtype=F32)
    feats = jnp.concatenate([t01, jnp.cos(wpos * bands), -jnp.sin(wpos * bands)], axis=-1)
    hp = lax.Precision.HIGHEST
    h = jnp.sin(freq[0] * (jnp.dot(feats, w1, precision=hp) + b1))
    h = jnp.sin(freq[1] * (jnp.dot(h, w2, precision=hp) + b2))
    h = jnp.dot(h, w3, precision=hp).reshape(L, HY_ORDER, 2, W)
    deltas = jnp.abs(jnp.linspace(math.log(HY_TARGET) / HY_SLOW, math.log(HY_TARGET) / HY_FAST, W, dtype=F32))
    h = h * (jnp.exp(-t01 * deltas) + HY_SHIFT)[:, None, None, :]
    hf, hb = h[:, :, 0], h[:, :, 1]
    hf = hf.at[0].add(hb[0])
    hb = hb.at[0].set(0.0)
    scale = lax.rsqrt(jnp.sum(hf * hf, axis=0, keepdims=True) + jnp.sum(hb * hb, axis=0, keepdims=True) + EPS)
    return (hf * scale).reshape(L, HY_ORDER * W), (hb * scale).reshape(L, HY_ORDER * W)


def _hyena_spectrum(mats, L, w1, b1, w2, b2, w3, freq, W):
    cm, sm, _ = mats
    hf, hb = _hyena_filters(L, w1, b1, w2, b2, w3, freq, W)
    cols = jnp.concatenate([hf[:, :W], hf[:, W:], hb[:, :W], hb[:, W:]], axis=0)
    tk = min(L, 512)
    gr, gi = _hy_fwd(cm, sm, cols, None, 2 * HY_ORDER, L, tk)
    gr = gr.reshape(2, HY_ORDER, L, W)
    gi = gi.reshape(2, HY_ORDER, L, W)
    kr = gr[0] + gr[1]
    ki = gi[0] - gi[1]
    nyq = gi[0, :, 0] + gi[1, :, 0]
    n = 2.0 * L
    wk = jnp.full((L, 1), 2.0 / n, F32).at[0].set(1.0 / n)
    a = kr * wk
    bm = (ki * wk).at[:, 0].set(0.0)
    dd = a.at[:, 0].set(nyq / n)
    return [(a[o], bm[o], dd[o]) for o in range(HY_ORDER)]


def _hyena_seq(u, mats, spec, conv_w, conv_b, skip, n_seg, L, blk0, col0):
    cm, sm, smt = mats
    t = min(L, 512)
    v, x1, x2 = _hy_short_conv(u, conv_w, conv_b, n_seg, L, blk0, col0)
    yr, yi = _hy_fwd(cm, sm, v, spec[0], n_seg, L, t)
    z = _hy_inv(cm, smt, yr, yi, x1, v, skip[0:1], n_seg, L, t, F32)
    yr, yi = _hy_fwd(cm, sm, z, spec[1], n_seg, L, t)
    return _hy_inv(cm, smt, yr, yi, x2, z, skip[1:2], n_seg, L, t, BF16)


def _pack_rows(x):
    n = x.shape[1] // 2
    lo = pltpu.bitcast(x[:, :n].astype(BF16).astype(F32), jnp.uint32)
    hi = pltpu.bitcast(x[:, n:].astype(BF16).astype(F32), jnp.uint32)
    return (lo >> 16) | (hi & jnp.uint32(0xFFFF0000))


def _unpack_rows(p):
    return pltpu.bitcast(p << 16, F32), pltpu.bitcast(p & jnp.uint32(0xFFFF0000), F32)


def _moe_router_kernel(x_ref, g_ref, sc_ref, sh_ref, whi_ref, wlo_ref, rb_ref, h_ref, idx_ref, wts_ref, cnt_ref):
    x = x_ref[...]
    y = x * lax.rsqrt(jnp.mean(x * x, axis=-1, keepdims=True) + EPS) * g_ref[...]
    h = y * (1.0 + sc_ref[0]) + sh_ref[0]
    h_ref[...] = _pack_rows(h)
    hi, lo = _split2(h)
    logits = _dot_nt(whi_ref[...], hi) + _dot_nt(wlo_ref[...], hi) + _dot_nt(whi_ref[...], lo)
    scores = jax.nn.sigmoid(logits)
    sel = scores + rb_ref[...]
    E, tm = sel.shape
    gsz = E // N_GROUPS
    neg = -jnp.inf
    erow = lax.broadcasted_iota(jnp.int32, (E, 1), 0).astype(F32)
    grow = lax.broadcasted_iota(jnp.int32, (gsz, 1), 0).astype(F32)
    blocks = [sel[g * gsz:(g + 1) * gsz] for g in range(N_GROUPS)]
    gscore = []
    for blk in blocks:
        m1 = jnp.max(blk, axis=0, keepdims=True)
        first = jnp.min(jnp.where(blk == m1, grow, float(gsz)), axis=0, keepdims=True)
        m2 = jnp.max(jnp.where(grow == first, neg, blk), axis=0, keepdims=True)
        gscore.append(m1 + m2)
    kept = []
    for g in range(N_GROUPS):
        rank = jnp.zeros((1, tm), F32)
        for o in range(N_GROUPS):
            if o != g:
                ahead = (gscore[o] >= gscore[g]) if o < g else (gscore[o] > gscore[g])
                rank = rank + jnp.where(ahead, 1.0, 0.0)
        kept.append(jnp.where(rank < TOPK_GROUPS, blocks[g], neg))
    work = jnp.concatenate(kept, axis=0)
    ids, ws = [], []
    total = jnp.zeros((1, tm), F32)
    chosen = jnp.zeros((E, tm), F32)
    for j in range(TOP_K):
        mx = jnp.max(work, axis=0, keepdims=True)
        am = jnp.min(jnp.where(work == mx, erow, float(E)), axis=0, keepdims=True)
        hit = erow == am
        wj = jnp.sum(jnp.where(hit, scores, 0.0), axis=0, keepdims=True)
        work = jnp.where(hit, neg, work)
        chosen = jnp.where(hit, 1.0, chosen)
        ids.append(am)
        ws.append(wj)
        total = total + wj
    idx_ref[...] = jnp.concatenate(ids, axis=0).astype(jnp.int32)
    wts_ref[...] = jnp.concatenate(ws, axis=0) / total * ROUTED_SCALE
    ones = jnp.ones((8, tm), BF16)
    cnt_ref[0] = _dot_nt(ones, chosen.astype(BF16))[0:1]


def _moe_router(x, n_tok, g, scale, shift, whi, wlo, rb, seg_rows, tm):
    D = x.shape[1]
    E = whi.shape[0]
    K = TOP_K
    last = scale.shape[0] - 1
    row = lambda i: (i, 0)
    col = lambda i: (0, i)
    one = lambda i: (0, 0)
    mod = lambda i: (jnp.minimum(i * tm // seg_rows, last), 0, 0)
    return pl.pallas_call(
        _moe_router_kernel,
        grid=(n_tok // tm,),
        in_specs=[pl.BlockSpec((tm, D), row), pl.BlockSpec((1, D), one), pl.BlockSpec((1, 1, D), mod),
                  pl.BlockSpec((1, 1, D), mod), pl.BlockSpec((E, D), one), pl.BlockSpec((E, D), one),
                  pl.BlockSpec((E, 1), one)],
        out_specs=[pl.BlockSpec((tm, D // 2), row), pl.BlockSpec((K, tm), col), pl.BlockSpec((K, tm), col),
                   pl.BlockSpec((1, 1, E), lambda i: (i, 0, 0))],
        out_shape=[jax.ShapeDtypeStruct((n_tok, D // 2), jnp.uint32), jax.ShapeDtypeStruct((K, n_tok), jnp.int32),
                   jax.ShapeDtypeStruct((K, n_tok), F32), jax.ShapeDtypeStruct((n_tok // tm, 1, E), F32)],
        compiler_params=_cparams(("parallel",)),
        name="moe_router",
    )(x, g, scale, shift, whi, wlo, rb)


def _moe_pos_kernel(idx_ref, base_ref, pos_ref):
    K, tm = idx_ref.shape
    E = base_ref.shape[1]
    idx = idx_ref[...]
    erow = lax.broadcasted_iota(jnp.int32, (E, 1), 0)
    hits = [erow == idx[j:j + 1, :] for j in range(K)]
    onehot = jnp.zeros((E, tm), F32)
    for hit in hits:
        onehot = jnp.where(hit, 1.0, onehot)
    row = lax.broadcasted_iota(jnp.int32, (tm, tm), 0)
    col = lax.broadcasted_iota(jnp.int32, (tm, tm), 1)
    before = _dot(onehot.astype(BF16), (row < col).astype(BF16))
    dest = before + base_ref[0]
    pos = [jnp.sum(jnp.where(hit, dest, 0.0), axis=0, keepdims=True) for hit in hits]
    pos_ref[...] = jnp.concatenate(pos, axis=0).astype(jnp.int32)


def _moe_plan(idx, cnt, bm, tm):
    K, n_tok = idx.shape
    E = cnt.shape[2]
    cnt = cnt[:, 0, :]
    counts = jnp.sum(cnt, axis=0)
    pcounts = jnp.ceil(counts / bm) * bm
    pends = jnp.cumsum(pcounts)
    base = (pends - pcounts)[None, :] + jnp.cumsum(cnt, axis=0) - cnt
    pos = pl.pallas_call(
        _moe_pos_kernel,
        grid=(n_tok // tm,),
        in_specs=[pl.BlockSpec((K, tm), lambda i: (0, i)), pl.BlockSpec((1, E, 1), lambda i: (i, 0, 0))],
        out_specs=pl.BlockSpec((K, tm), lambda i: (0, i)),
        out_shape=jax.ShapeDtypeStruct((K, n_tok), jnp.int32),
        compiler_params=_cparams(("parallel",)),
        name="moe_positions",
    )(idx, base[:, :, None])
    n_blocks = -(-(n_tok * K + E * (bm - 1)) // bm)
    block_start = jnp.concatenate([jnp.zeros((1,), F32), pends / bm]).astype(jnp.int32)
    return pos.T.reshape(-1), block_start, n_blocks


def _per_token_rows(tm, K, copy):
    def start(t, carry):
        for j in range(K):
            copy(t, j).start(priority=j % 2)
        return carry

    def wait(t, carry):
        for j in range(K):
            copy(t, j).wait()
        return carry

    return (lambda: lax.fori_loop(0, tm, start, 0, unroll=2)), (lambda: lax.fori_loop(0, tm, wait, 0, unroll=2))


def _moe_dispatch_kernel(pos_ref, prev_ref, h_ref, xs_in_ref, xs_hbm, stage_ref, sems, *, K):
    del xs_in_ref
    tm = h_ref.shape[0]
    i = pl.program_id(0)
    last = pl.num_programs(0) - 1
    slot = lax.bitwise_and(i, 1)

    def tile_copies(s, p_ref):
        def row_copy(t, j):
            return pltpu.make_async_copy(stage_ref.at[s, pl.ds(t, 1)], xs_hbm.at[pl.ds(p_ref[t * K + j], 1)],
                                         sems.at[s])
        return _per_token_rows(tm, K, row_copy)

    start, wait = tile_copies(slot, pos_ref)
    _, wait_prev = tile_copies(1 - slot, prev_ref)
    stage_ref[slot] = h_ref[...]
    start()

    @pl.when(i > 0)
    def _():
        wait_prev()

    @pl.when(i == last)
    def _():
        wait()


def _moe_dispatch(pos, h, n_rows, K, tm):
    n_tok, W = h.shape
    xs0 = jnp.zeros((n_rows, W), h.dtype)
    return pl.pallas_call(
        functools.partial(_moe_dispatch_kernel, K=K),
        grid=(n_tok // tm,),
        in_specs=[pl.BlockSpec((tm * K,), lambda i: (i,), memory_space=pltpu.SMEM),
                  pl.BlockSpec((tm * K,), lambda i: (jnp.maximum(i - 1, 0),), memory_space=pltpu.SMEM),
                  pl.BlockSpec((tm, W), lambda i: (i, 0)),
                  pl.BlockSpec(memory_space=pl.ANY)],
        out_specs=pl.BlockSpec(memory_space=pl.ANY),
        out_shape=jax.ShapeDtypeStruct((n_rows, W), h.dtype),
        scratch_shapes=[pltpu.VMEM((2, tm, W), h.dtype), pltpu.SemaphoreType.DMA((2,))],
        input_output_aliases={3: 0},
        compiler_params=_cparams(("arbitrary",)),
        name="moe_dispatch",
    )(pos, pos, h, xs0)


def _moe_expert_kernel(bs_ref, x_hbm, w1_ref, w3_ref, w2_ref, y_hbm, xbuf, ybuf, w1b_ref, w3b_ref, w2b_ref,
                       xsem, ysem, *, bm, nbuf, n_blocks):
    e = pl.program_id(0)
    last = pl.num_programs(0) - 1
    b0 = bs_ref[e]
    b1 = bs_ref[e + 1]
    n_used = bs_ref[last + 1]

    def x_copy(g):
        slot = lax.bitwise_and(g, nbuf - 1)
        return pltpu.make_async_copy(x_hbm.at[pl.ds(g * bm, bm)], xbuf.at[slot], xsem.at[slot])

    def y_copy(g):
        slot = lax.bitwise_and(g, nbuf - 1)
        return pltpu.make_async_copy(ybuf.at[slot], y_hbm.at[pl.ds(g * bm, bm)], ysem.at[slot])

    @pl.when(e == 0)
    def _():
        for p in range(nbuf - 1):
            @pl.when(p < n_used)
            def _():
                x_copy(p).start()

    @pl.when(b1 > b0)
    def _():
        w1b_ref[...] = w1_ref[0, 0].astype(BF16)
        w3b_ref[...] = w3_ref[0, 0].astype(BF16)
        w2b_ref[...] = w2_ref[0, 0].astype(BF16)

    def block(g, carry):
        slot = lax.bitwise_and(g, nbuf - 1)
        x_copy(g).wait()

        @pl.when(g + nbuf - 1 < n_used)
        def _():
            x_copy(g + nbuf - 1).start()

        @pl.when(g >= nbuf)
        def _():
            y_copy(g - nbuf).wait()

        x = jnp.concatenate(_unpack_rows(xbuf[slot]), axis=1).astype(BF16)
        a = _dot(x, w1b_ref[...])
        b = _dot(x, w3b_ref[...])
        ybuf[slot] = _pack_rows(_dot((jax.nn.silu(a) * b).astype(BF16), w2b_ref[...]))
        y_copy(g).start()
        return carry

    lax.fori_loop(b0, b1, block, 0)

    @pl.when(e == last)
    def _():
        def drain(g, carry):
            y_copy(g).wait()
            return carry

        lax.fori_loop(jnp.maximum(n_used - nbuf, 0), n_used, drain, 0)
        ybuf[0] = jnp.zeros(ybuf.shape[1:], ybuf.dtype)

        def zero_copy(g):
            return pltpu.make_async_copy(ybuf.at[0], y_hbm.at[pl.ds(g * bm, bm)], ysem.at[0])

        def fill(g, carry):
            zero_copy(g).start()
            return carry

        def fill_wait(g, carry):
            zero_copy(g).wait()
            return carry

        lax.fori_loop(n_used, n_blocks, fill, 0)
        lax.fori_loop(n_used, n_blocks, fill_wait, 0)


def _moe_experts(block_start, xs, w1, w3, w2, layer, bm):
    P, W = xs.shape
    E, D, F = w1.shape[1:]
    nbuf = 4
    grid_spec = pltpu.PrefetchScalarGridSpec(
        num_scalar_prefetch=1,
        grid=(E,),
        in_specs=[pl.BlockSpec(memory_space=pl.ANY),
                  pl.BlockSpec((1, 1, D, F), lambda e, bs: (layer, e, 0, 0)),
                  pl.BlockSpec((1, 1, D, F), lambda e, bs: (layer, e, 0, 0)),
                  pl.BlockSpec((1, 1, F, D), lambda e, bs: (layer, e, 0, 0))],
        out_specs=pl.BlockSpec(memory_space=pl.ANY),
        scratch_shapes=[pltpu.VMEM((nbuf, bm, W), xs.dtype), pltpu.VMEM((nbuf, bm, W), xs.dtype),
                        pltpu.VMEM((D, F), BF16), pltpu.VMEM((D, F), BF16), pltpu.VMEM((F, D), BF16),
                        pltpu.SemaphoreType.DMA((nbuf,)), pltpu.SemaphoreType.DMA((nbuf,))],
    )
    return pl.pallas_call(
        functools.partial(_moe_expert_kernel, bm=bm, nbuf=nbuf, n_blocks=P // bm),
        grid_spec=grid_spec,
        out_shape=jax.ShapeDtypeStruct((P, W), xs.dtype),
        compiler_params=_cparams(("arbitrary",)),
        name="moe_experts",
    )(block_start, xs, w1, w3, w2)


def _moe_combine_kernel(pos_ref, next_ref, wts_ref, x_ref, h_ref, sw1_ref, sw3_ref, sw2_ref, gate_ref, ys_hbm, o_ref,
                        gath_ref, sems, *, K):
    tm = x_ref.shape[0]
    i = pl.program_id(0)
    last = pl.num_programs(0) - 1

    slot = lax.bitwise_and(i, 1)

    def tile_copies(s, p_ref):
        def row_copy(t, j):
            return pltpu.make_async_copy(ys_hbm.at[pl.ds(p_ref[t * K + j], 1)], gath_ref.at[s, j, pl.ds(t, 1)],
                                         sems.at[s])
        return _per_token_rows(tm, K, row_copy)

    start, wait = tile_copies(slot, pos_ref)
    start_next, _ = tile_copies(1 - slot, next_ref)

    @pl.when(i == 0)
    def _():
        start()

    @pl.when(i < last)
    def _():
        start_next()

    h = jnp.concatenate(_unpack_rows(h_ref[...]), axis=1).astype(BF16)
    mid = jax.nn.silu(_dot(h, sw1_ref[...])) * _dot(h, sw3_ref[...])
    shared = _dot(mid.astype(BF16), sw2_ref[...])
    wait()
    wts = wts_ref[...]
    W = gath_ref.shape[3]
    acc_lo = shared[:, :W]
    acc_hi = shared[:, W:]
    for j in range(K):
        lo, hi = _unpack_rows(gath_ref[slot, j])
        acc_lo = acc_lo + wts[:, j:j + 1] * lo
        acc_hi = acc_hi + wts[:, j:j + 1] * hi
    gate = gate_ref[0]
    o_ref[:, :W] = x_ref[:, :W] + gate[:, :W] * acc_lo
    o_ref[:, W:] = x_ref[:, W:] + gate[:, W:] * acc_hi


def _moe_combine(pos, wts, x, h, sw1, sw3, sw2, gate, ys, n_tok, K, seg_rows, tm):
    D = x.shape[1]
    W = ys.shape[1]
    n_tiles = n_tok // tm
    last = gate.shape[0] - 1
    row = lambda i: (i, 0)
    one = lambda i: (0, 0)
    return pl.pallas_call(
        functools.partial(_moe_combine_kernel, K=K),
        grid=(n_tiles,),
        in_specs=[pl.BlockSpec((tm * K,), lambda i: (i,), memory_space=pltpu.SMEM),
                  pl.BlockSpec((tm * K,), lambda i: (jnp.minimum(i + 1, n_tiles - 1),), memory_space=pltpu.SMEM),
                  pl.BlockSpec((tm, K), row), pl.BlockSpec((tm, D), row), pl.BlockSpec((tm, W), row),
                  pl.BlockSpec(sw1.shape, one), pl.BlockSpec(sw3.shape, one), pl.BlockSpec(sw2.shape, one),
                  pl.BlockSpec((1, 1, D), lambda i: (jnp.minimum(i * tm // seg_rows, last), 0, 0)),
                  pl.BlockSpec(memory_space=pl.ANY)],
        out_specs=pl.BlockSpec((tm, D), row),
        out_shape=jax.ShapeDtypeStruct((n_tok, D), F32),
        scratch_shapes=[pltpu.VMEM((2, K, tm, W), ys.dtype), pltpu.SemaphoreType.DMA((2,))],
        compiler_params=_cparams(("arbitrary",)),
        name="moe_combine",
    )(pos, pos, wts, x, h, sw1, sw3, sw2, gate, ys)


def _moe_ffn_residual(x, n_tok, norm_g, scale, shift, gate, router_w, router_b, ew1, ew3, ew2, layer, sw1, sw3, sw2,
                      seg_rows, bm):
    K = TOP_K
    tm = 256
    whi, wlo = _split2(router_w.T)
    h, idx, wts, cnt = _moe_router(x, n_tok, norm_g, scale, shift, whi, wlo, router_b[:, None], seg_rows, tm)
    pos, block_start, n_blocks = _moe_plan(idx, cnt, bm, tm)
    wts = wts.T
    xs = _moe_dispatch(pos, h, n_blocks * bm, K, 128)
    ys = _moe_experts(block_start, xs, ew1, ew3, ew2, layer, bm)
    return _moe_combine(pos, wts, x, h, sw1.astype(BF16), sw3.astype(BF16), sw2.astype(BF16), gate, ys,
                        n_tok, K, seg_rows, 128)


def kernel(x, c, ctx, c_ctx, w_mod, b_mod, norm1_g, norm2_g, w_in, mlstm_gate_b, mlstm_norm_g, da_qnorm_g, da_knorm_g, da_lambda, da_subln_g, hy_conv_w, hy_conv_b, hy_w1, hy_b1, hy_w2, hy_b2, hy_w3, hy_freq, hy_skip, w_out, router_w, router_b, exp_w1, exp_w3, exp_w2, sh_w1, sh_w3, sh_w2):
    B, S, D = x.shape
    n_ctx = ctx.shape[1]
    depth = w_in.shape[0]
    n_lat = B * S
    ml_w = mlstm_norm_g.shape[1]
    da_dh = da_qnorm_g.shape[1]
    da_w = DA_HEADS * 2 * da_dh
    hy_w = hy_skip.shape[2]
    n_gates = 4 * ML_HEADS
    da_col = 0
    hy_col = da_col + 3 * da_w
    o_col = hy_col + 3 * hy_w
    gate_col = o_col + ml_w
    ml_in = 4 * ml_w + n_gates
    tm = 512

    X = jnp.concatenate([x.reshape(n_lat, D), ctx.reshape(B * n_ctx, D)], axis=0)
    sc = jax.nn.silu(jnp.concatenate([c, c_ctx[None]], axis=0))
    cos, sin = _axial_rope_tables(S, da_dh, da_w // da_dh, tm)
    mats_l = _dft_mats(S)
    mats_c = _dft_mats(n_ctx)
    for l in range(depth):
        last = l == depth - 1
        lam_init = 0.8 - 0.6 * math.exp(-0.3 * l)
        mods = (jnp.dot(sc, w_mod[l], precision=lax.Precision.HIGHEST) + b_mod[l]).reshape(B + 1, 6, 1, D)
        sh1, s1, g1, sh2, s2, g2 = [mods[:, i] for i in range(6)]
        wl = w_in[l]
        w_big = jnp.concatenate([wl[:, ml_in:], wl[:, 3 * ml_w:4 * ml_w], wl[:, 4 * ml_w:ml_in],
                                 jnp.zeros((D, LANES - n_gates), F32)], axis=1).astype(BF16)
        w_qkv_t = wl[:, :3 * ml_w].T.astype(BF16)
        U, UT = _norm_mod_matmul(X, norm1_g[l][None], s1, sh1, w_big, w_qkv_t, S, tm, w_big.shape[1] // 3)
        gb = jnp.concatenate([mlstm_gate_b[l], jnp.zeros((LANES - n_gates,), F32)])[None]
        m_out = _mlstm_mixer(UT, U, gb, mlstm_norm_g[l][:, None], B, S, n_ctx, o_col, gate_col)
        d_out = _diff_attn_mixer(U, cos, sin, da_qnorm_g[l], da_knorm_g[l], da_lambda[l], da_subln_g[l], lam_init,
                                 B, S, n_ctx, da_col, not last)
        hy_args = (hy_w1[l], hy_b1[l], hy_w2[l], hy_b2[l], hy_w3[l], hy_freq[l], hy_w)
        y_out = _hyena_seq(U, mats_l, _hyena_spectrum(mats_l, S, *hy_args), hy_conv_w[l], hy_conv_b[l][None],
                           hy_skip[l], B, S, 0, hy_col)
        n_rows = n_lat
        if not last:
            y_ctx = _hyena_seq(U, mats_c, _hyena_spectrum(mats_c, n_ctx, *hy_args), hy_conv_w[l], hy_conv_b[l][None],
                               hy_skip[l], B, n_ctx, n_lat // n_ctx, hy_col)
            y_out = jnp.concatenate([y_out, y_ctx], axis=0)
            n_rows = n_lat + B * n_ctx
        wo = w_out[l].astype(BF16)
        X = _out_proj_residual(m_out, d_out, y_out, wo[:ml_w], wo[ml_w:ml_w + da_w], wo[ml_w + da_w:], X, g1,
                               n_rows, S, tm)
        X = _moe_ffn_residual(X, n_rows, norm2_g[l][None], s2, sh2, g2, router_w[l], router_b[l],
                              exp_w1, exp_w3, exp_w2, l, sh_w1[l], sh_w3[l], sh_w2[l], S,
                              MOE_BLOCK)
    return X[:n_lat].reshape(B, S, D)
```

```python
import functools
import math

import jax
import jax.numpy as jnp
from jax import lax
from jax.experimental import pallas as pl
from jax.experimental.pallas import tpu as pltpu

F32 = jnp.float32
BF16 = jnp.bfloat16

EPS = 1e-6
GRID_W = 64
ROPE_THETA = 10000.0
ML_HEADS = 4
ML_CHUNK = 256
ML_M_INIT = -1e30
DA_HEADS = 4
HY_ORDER = 2
HY_BANDS = 8
HY_SHIFT = 0.05
HY_TARGET = 1e-2
HY_FAST = 0.3
HY_SLOW = 1.5
N_GROUPS = 8
TOPK_GROUPS = 4
TOP_K = 8
ROUTED_SCALE = 2.5
MOE_BLOCK = 256
LANES = 128
VMEM_LIMIT = 56 * 1024 * 1024


def _cparams(sem):
    return pltpu.CompilerParams(dimension_semantics=sem, vmem_limit_bytes=VMEM_LIMIT)


def _dot(a, b):
    return jnp.dot(a, b, preferred_element_type=F32)


def _dot_nt(a, b):
    return lax.dot_general(a, b, (((1,), (1,)), ((), ())), preferred_element_type=F32)


def _dot_tn(a, b):
    return lax.dot_general(a, b, (((0,), (0,)), ((), ())), preferred_element_type=F32)


def _split3(a):
    hi = a.astype(BF16)
    r = a - hi.astype(F32)
    mid = r.astype(BF16)
    lo = (r - mid.astype(F32)).astype(BF16)
    return hi, mid, lo


def _split2(a):
    hi = a.astype(BF16)
    lo = (a - hi.astype(F32)).astype(BF16)
    return hi, lo


def _norm_mod_mm_kernel(x_ref, g_ref, sc_ref, sh_ref, w_ref, wt_ref, o_ref, ot_ref, xn_ref):
    @pl.when(pl.program_id(1) == 0)
    def _():
        x = x_ref[...]
        y = x * lax.rsqrt(jnp.mean(x * x, axis=-1, keepdims=True) + EPS) * g_ref[...]
        xn_ref[...] = (y * (1.0 + sc_ref[0]) + sh_ref[0]).astype(BF16)
        ot_ref[...] = _dot_nt(wt_ref[...], xn_ref[...])

    o_ref[...] = _dot(xn_ref[...], w_ref[...])


def _norm_mod_matmul(x, g, scale, shift, w, wt, seg_rows, tm, tn):
    R, D = x.shape
    N = w.shape[1]
    NT = wt.shape[0]
    last = scale.shape[0] - 1
    mod_map = lambda i, j: (jnp.minimum(i * tm // seg_rows, last), 0, 0)
    return pl.pallas_call(
        _norm_mod_mm_kernel,
        grid=(R // tm, N // tn),
        in_specs=[
            pl.BlockSpec((tm, D), lambda i, j: (i, 0)),
            pl.BlockSpec((1, D), lambda i, j: (0, 0)),
            pl.BlockSpec((1, 1, D), mod_map),
            pl.BlockSpec((1, 1, D), mod_map),
            pl.BlockSpec((D, tn), lambda i, j: (0, j)),
            pl.BlockSpec((NT, D), lambda i, j: (0, 0)),
        ],
        out_specs=[pl.BlockSpec((tm, tn), lambda i, j: (i, j)), pl.BlockSpec((NT, tm), lambda i, j: (0, i))],
        out_shape=[jax.ShapeDtypeStruct((R, N), F32), jax.ShapeDtypeStruct((NT, R), F32)],
        scratch_shapes=[pltpu.VMEM((tm, D), BF16)],
        compiler_params=_cparams(("parallel", "arbitrary")),
        name="norm_mod_matmul",
    )(x, g, scale, shift, w, wt)


def _out_proj_kernel(m_ref, d_ref, y_ref, wm_ref, wd_ref, wy_ref, x_ref, gate_ref, o_ref):
    acc = _dot(m_ref[...], wm_ref[...]) + _dot(d_ref[...], wd_ref[...]) + _dot(y_ref[...], wy_ref[...])
    o_ref[...] = x_ref[...] + gate_ref[0] * acc


def _out_proj_residual(m, d, y, wm, wd, wy, x, gate, n_rows, seg_rows, tm):
    R, D = n_rows, x.shape[1]
    last = gate.shape[0] - 1
    row = lambda i: (i, 0)
    full = lambda i: (0, 0)
    return pl.pallas_call(
        _out_proj_kernel,
        grid=(R // tm,),
        in_specs=[
            pl.BlockSpec((tm, m.shape[1]), row),
            pl.BlockSpec((tm, d.shape[1]), row),
            pl.BlockSpec((tm, y.shape[1]), row),
            pl.BlockSpec(wm.shape, full),
            pl.BlockSpec(wd.shape, full),
            pl.BlockSpec(wy.shape, full),
            pl.BlockSpec((tm, D), row),
            pl.BlockSpec((1, 1, D), lambda i: (jnp.minimum(i * tm // seg_rows, last), 0, 0)),
        ],
        out_specs=pl.BlockSpec((tm, D), row),
        out_shape=jax.ShapeDtypeStruct((R, D), F32),
        compiler_params=_cparams(("parallel",)),
        name="out_proj_residual",
    )(m, d, y, wm, wd, wy, x, gate)


def _log_sigmoid(x):
    return jnp.minimum(x, 0.0) - jnp.log1p(jnp.exp(-jnp.abs(x)))


def _mlstm_gate_tables(g_ref, r0, L, gb, tril, triu):
    g = g_ref[pl.ds(r0, L), :] + gb
    lf = _log_sigmoid(g)
    gT = g.T
    lfT = lf.T
    parts = _split3(lf)
    partsT = _split3(lfT)
    cs_f = sum(_dot(tril, p) for p in parts)
    cs_b = sum(_dot(triu, p) for p in parts)
    rs_f = sum(_dot(p, triu) for p in partsT)
    rs_b = sum(_dot(p, tril) for p in partsT)
    return g, gT, cs_f, cs_b, rs_f, rs_b


def _mlstm_chunk(qT, kT, vT, i_row, b_row, c_col, b_end, mask, state):
    C, n, m = state
    qb = qT.astype(BF16)
    kb = kT.astype(BF16)
    dmat = jnp.where(mask, b_row + c_col, -jnp.inf)
    inter = b_row + m
    m_t = jnp.maximum(inter, jnp.max(dmat, axis=0, keepdims=True))
    s = _dot_tn(kb, qb) * jnp.exp(dmat - m_t)
    carry_w = jnp.exp(inter - m_t)
    num = _dot(vT.astype(BF16), s.astype(BF16)) + carry_w * _dot(C.astype(BF16), qb)
    den = jnp.sum(s, axis=0, keepdims=True) + carry_w * jnp.sum(qT * n, axis=0, keepdims=True)
    h = num / jnp.maximum(jnp.abs(den), jnp.exp(-m_t))
    g = b_end - b_row + i_row
    m_new = jnp.maximum(b_end + m, jnp.max(g, axis=-1, keepdims=True))
    ws = jnp.exp(g - m_new)
    decay = jnp.exp(b_end + m - m_new)
    C_new = decay * C + _dot_nt((vT * ws).astype(BF16), kb)
    n_new = decay * n + jnp.sum(kT * ws, axis=-1, keepdims=True)
    return h, (C_new, n_new, m_new)


def _mlstm_kernel(ql_ref, kl_ref, vl_ref, ol_ref, gl_ref, qc_ref, kc_ref, vc_ref, oc_ref, gc_ref,
                  gb_ref, ng_ref, outl_ref, outc_ref, hf_ref, hb_ref, *, L, H, dh):
    S = ql_ref.shape[1]
    n_ctx = qc_ref.shape[1]
    row = lax.broadcasted_iota(jnp.int32, (L, L), 0)
    col = lax.broadcasted_iota(jnp.int32, (L, L), 1)
    lower = col <= row
    upper = col >= row
    tril = lower.astype(BF16)
    triu = upper.astype(BF16)
    gb = gb_ref[...]
    k_scale = dh ** -0.5

    def both_dirs(refs_f, r0_f, refs_b, r0_b, hoff, state):
        new_state = []
        for d, (refs, r0, h_ref) in enumerate(((refs_f, r0_f, hf_ref), (refs_b, r0_b, hb_ref))):
            q_ref, k_ref, v_ref, g_ref = refs
            g, gT, cs_f, cs_b, rs_f, rs_b = _mlstm_gate_tables(g_ref, r0, L, gb, tril, triu)
            cs, rs, mask = (cs_f, rs_f, upper) if d == 0 else (cs_b, rs_b, lower)
            end = L - 1 if d == 0 else 0
            for hh in range(H):
                ic = 2 * d * H + hh
                fc = ic + H
                rows = slice(hh * dh, (hh + 1) * dh)
                qT = q_ref[rows, pl.ds(r0, L)]
                kT = k_ref[rows, pl.ds(r0, L)] * k_scale
                vT = v_ref[rows, pl.ds(r0, L)]
                h, st = _mlstm_chunk(qT, kT, vT, gT[ic:ic + 1, :], rs[fc:fc + 1, :],
                                     g[:, ic:ic + 1] - cs[:, fc:fc + 1], rs[fc:fc + 1, end:end + 1], mask,
                                     state[d * H + hh])
                off = hoff + r0
                h_ref[rows, pl.ds(off if isinstance(off, int) else pl.multiple_of(off, L), L)] = h
                new_state.append(st)
        return tuple(new_state)

    state = tuple((jnp.zeros((dh, dh), F32), jnp.zeros((dh, 1), F32), jnp.full((1, 1), ML_M_INIT, F32))
                  for _ in range(2 * H))
    ctx_refs = (qc_ref, kc_ref, vc_ref, gc_ref)
    lat_refs = (ql_ref, kl_ref, vl_ref, gl_ref)
    n_cc = n_ctx // L
    for c in range(n_cc):
        state = both_dirs(ctx_refs, c * L, ctx_refs, (n_cc - 1 - c) * L, 0, state)
    n_lc = S // L

    def body(c, st):
        r_f = pl.multiple_of(c * L, L)
        r_b = pl.multiple_of((n_lc - 1 - c) * L, L)
        return both_dirs(lat_refs, r_f, lat_refs, r_b, n_ctx, st)

    lax.fori_loop(0, n_lc, body, state)

    def finish(o_ref, out_ref, hoff, n_rows):
        def fbody(c, carry):
            r0 = pl.multiple_of(c * L, L)
            off = pl.multiple_of(hoff + r0, L)
            hs = hf_ref[:, pl.ds(off, L)] + hb_ref[:, pl.ds(off, L)]
            normed = []
            for hh in range(H):
                rows = slice(hh * dh, (hh + 1) * dh)
                hv = hs[rows]
                normed.append(hv * lax.rsqrt(jnp.mean(hv * hv, axis=0, keepdims=True) + EPS) * ng_ref[rows])
            hn = jnp.concatenate(normed, axis=0).T
            out_ref[pl.ds(r0, L), :] = (jax.nn.sigmoid(o_ref[pl.ds(r0, L), :]) * hn).astype(out_ref.dtype)
            return carry
        lax.fori_loop(0, n_rows // L, fbody, 0)

    finish(ol_ref, outl_ref, n_ctx, S)
    finish(oc_ref, outc_ref, 0, n_ctx)


def _mlstm_mixer(uT, u, gate_b, norm_g, B, S, n_ctx, o_col, gate_col):
    W = norm_g.shape[0]
    H = ML_HEADS
    dh = W // H
    ocb = o_col // W
    gcb = gate_col // LANES
    cblk = (B * S) // n_ctx

    def lat_t(j):
        return pl.BlockSpec((W, S), lambda b: (j, b))

    def ctx_t(j):
        return pl.BlockSpec((W, n_ctx), lambda b: (j, cblk + b))

    one = lambda b: (0, 0)
    out_l, out_c = pl.pallas_call(
        functools.partial(_mlstm_kernel, L=ML_CHUNK, H=H, dh=dh),
        grid=(B,),
        in_specs=[lat_t(0), lat_t(1), lat_t(2), pl.BlockSpec((S, W), lambda b: (b, ocb)),
                  pl.BlockSpec((S, LANES), lambda b: (b, gcb)),
                  ctx_t(0), ctx_t(1), ctx_t(2), pl.BlockSpec((n_ctx, W), lambda b: (cblk + b, ocb)),
                  pl.BlockSpec((n_ctx, LANES), lambda b: (cblk + b, gcb)),
                  pl.BlockSpec((1, LANES), one), pl.BlockSpec((W, 1), one)],
        out_specs=[pl.BlockSpec((S, W), lambda b: (b, 0)), pl.BlockSpec((n_ctx, W), lambda b: (b, 0))],
        out_shape=[jax.ShapeDtypeStruct((B * S, W), BF16), jax.ShapeDtypeStruct((B * n_ctx, W), BF16)],
        scratch_shapes=[pltpu.VMEM((W, n_ctx + S), F32), pltpu.VMEM((W, n_ctx + S), F32)],
        compiler_params=_cparams(("parallel",)),
        name="mlstm",
    )(uT, uT, uT, u, u, uT, uT, uT, u, u, gate_b, norm_g)
    return jnp.concatenate([out_l, out_c], axis=0)


def _da_prep_kernel(q_ref, k_ref, v_ref, cos_ref, sin_ref, qg_ref, kg_ref, seg_ref, qo_ref, ko_ref, vo_ref, *, dh):
    cos = cos_ref[...]
    sin = sin_ref[...]
    seg = seg_ref[...]
    W = q_ref.shape[1]
    lane = lax.broadcasted_iota(jnp.int32, (1, W), 1)
    quarter = dh // 4
    first = (lane % (2 * quarter)) < quarter

    def norm_rope(x, g):
        hi, lo = _split2(x * x)
        ms = (_dot(hi, seg) + _dot(lo, seg)) * (1.0 / dh)
        xn = x * lax.rsqrt(ms + EPS) * g
        rot = jnp.where(first, -pltpu.roll(xn, W - quarter, 1), pltpu.roll(xn, quarter, 1))
        return xn * cos + rot * sin

    qo_ref[...] = (norm_rope(q_ref[...], qg_ref[...]) * (dh ** -0.5 * math.log2(math.e))).astype(BF16)
    ko_ref[...] = norm_rope(k_ref[...], kg_ref[...]).astype(BF16)
    vo_ref[...] = v_ref[...].astype(BF16)


def _da_prep(u, cos, sin, qg, kg, seg, n_lat_rows, S, col0, tm, dh):
    R = u.shape[0]
    W = qg.shape[1]
    cb = col0 // W
    n_lat = n_lat_rows // tm
    per_seq = S // tm
    tab = lambda i: (jnp.where(i < n_lat, i % per_seq, per_seq), 0)
    one = lambda i: (0, 0)
    row = lambda i: (i, 0)

    def ucol(j):
        return pl.BlockSpec((tm, W), lambda i: (i, cb + j))

    return pl.pallas_call(
        functools.partial(_da_prep_kernel, dh=dh),
        grid=(R // tm,),
        in_specs=[ucol(0), ucol(1), ucol(2), pl.BlockSpec((tm, W), tab), pl.BlockSpec((tm, W), tab),
                  pl.BlockSpec((1, W), one), pl.BlockSpec((1, W), one), pl.BlockSpec((W, W), one)],
        out_specs=[pl.BlockSpec((tm, W), row)] * 3,
        out_shape=[jax.ShapeDtypeStruct((R, W), BF16)] * 3,
        compiler_params=_cparams(("parallel",)),
        name="da_prep",
    )(u, u, u, cos, sin, qg, kg, seg)


def _da_attn_kernel(*refs, n_kv, dh, lam_init):
    q_ref = refs[0]
    k_refs = refs[1:1 + n_kv]
    v_refs = refs[1 + n_kv:1 + 2 * n_kv]
    lam_ref, sg_ref, o_ref = refs[1 + 2 * n_kv:]
    lp = lam_ref[...]
    lam = (jnp.exp(jnp.sum(lp[0:1] * lp[1:2], axis=-1, keepdims=True))
           - jnp.exp(jnp.sum(lp[2:3] * lp[3:4], axis=-1, keepdims=True)) + lam_init)
    q = q_ref[...]
    acc = None
    nums, dens = [], []
    for mp in range(2):
        lanes = slice(mp * dh, (mp + 1) * dh)
        s = [_dot_nt(q[:, lanes], k_ref[:, lanes]) for k_ref in k_refs]
        mx = functools.reduce(jnp.maximum, [jnp.max(si, axis=-1, keepdims=True) for si in s])
        p = [jnp.exp2(si - mx) for si in s]
        nums.append(p)
        dens.append(sum(jnp.sum(pi, axis=-1, keepdims=True) for pi in p))
    c = lam * dens[0] / dens[1]
    for j in range(n_kv):
        a = (nums[0][j] - c * nums[1][j]).astype(BF16)
        t = _dot(a, v_refs[j][...])
        acc = t if acc is None else acc + t
    acc = acc / dens[0]
    o = acc * lax.rsqrt(jnp.mean(acc * acc, axis=-1, keepdims=True) + EPS) * sg_ref[...]
    o_ref[...] = (o * (1.0 - lam_init)).astype(o_ref.dtype)


def _da_attention(q, k, v, lam_p, subln_g, lam_init, B, q_rows, q_blk0, kv_segs, tq, dh):
    H = DA_HEADS
    vd = 2 * dh
    nq = q_rows // tq
    q0 = q_blk0

    def kv_spec(rows, blk0):
        return pl.BlockSpec((rows, vd), lambda b, h, i: (blk0 + b, h))

    kspecs = [kv_spec(r, b0) for r, b0 in kv_segs]
    one = lambda b, h, i: (0, 0)
    return pl.pallas_call(
        functools.partial(_da_attn_kernel, n_kv=len(kv_segs), dh=dh, lam_init=lam_init),
        grid=(B, H, nq),
        in_specs=[pl.BlockSpec((tq, vd), lambda b, h, i: (q0 + b * nq + i, h))] + kspecs + kspecs
                 + [pl.BlockSpec(lam_p.shape, one), pl.BlockSpec((1, vd), one)],
        out_specs=pl.BlockSpec((tq, vd), lambda b, h, i: (b * nq + i, h)),
        out_shape=jax.ShapeDtypeStruct((B * q_rows, H * vd), BF16),
        compiler_params=_cparams(("parallel", "parallel", "arbitrary")),
        name="da_attention",
    )(q, *([k] * len(kv_segs)), *([v] * len(kv_segs)), lam_p, subln_g)


def _axial_rope_tables(S, dh, reps, pad_rows):
    rows = S // GRID_W
    r = jnp.repeat(jnp.arange(rows, dtype=F32), GRID_W)
    col = jnp.tile(jnp.arange(GRID_W, dtype=F32), rows)
    n_freq = dh // 4
    inv = ROPE_THETA ** (-jnp.arange(n_freq, dtype=F32) / n_freq)
    ar = r[:, None] * inv
    ac = col[:, None] * inv
    ang = jnp.concatenate([ar, ar, ac, ac], axis=-1)
    cos = jnp.concatenate([jnp.tile(jnp.cos(ang), (1, reps)), jnp.ones((pad_rows, dh * reps), F32)], axis=0)
    sin = jnp.concatenate([jnp.tile(jnp.sin(ang), (1, reps)), jnp.zeros((pad_rows, dh * reps), F32)], axis=0)
    return cos, sin


def _diff_attn_mixer(u, cos, sin, qg, kg, lam_p, subln_g, lam_init, B, S, n_ctx, col0, need_ctx):
    dh = qg.shape[0]
    W = DA_HEADS * 2 * dh
    seg = (jnp.arange(W)[:, None] // dh == jnp.arange(W)[None, :] // dh).astype(BF16)
    tm = 512
    q, k, v = _da_prep(u, cos, sin, jnp.tile(qg, W // dh)[None], jnp.tile(kg, W // dh)[None], seg,
                       B * S, S, col0, tm, dh)
    sg = subln_g[None]
    ctx_blk0 = (B * S) // n_ctx
    tq = 256
    out_l = _da_attention(q, k, v, lam_p, sg, lam_init, B, S, 0, [(n_ctx, ctx_blk0), (S, 0)], tq, dh)
    if not need_ctx:
        return out_l
    out_c = _da_attention(q, k, v, lam_p, sg, lam_init, B, n_ctx, (B * S) // n_ctx, [(n_ctx, ctx_blk0)], n_ctx, dh)
    return jnp.concatenate([out_l, out_c], axis=0)


def _hy_conv_kernel(v_ref, x1_ref, x2_ref, w_ref, b_ref, vo_ref, x1o_ref, x2o_ref):
    L, W = v_ref.shape
    row = lax.broadcasted_iota(jnp.int32, (L, 1), 0)
    for j, (i_ref, o_ref) in enumerate(((v_ref, vo_ref), (x1_ref, x1o_ref), (x2_ref, x2o_ref))):
        lanes = slice(j * W, (j + 1) * W)
        u = i_ref[...]
        prev = jnp.where(row == 0, 0.0, pltpu.roll(u, 1, 0))
        nxt = jnp.where(row == L - 1, 0.0, pltpu.roll(u, L - 1, 0))
        o_ref[...] = prev * w_ref[0:1, lanes] + u * w_ref[1:2, lanes] + nxt * w_ref[2:3, lanes] + b_ref[:, lanes]


def _hy_short_conv(u, conv_w, conv_b, n_seg, L, blk0, col0):
    W = conv_w.shape[1] // 3
    cb = col0 // W
    one = lambda b: (0, 0)

    def ucol(j):
        return pl.BlockSpec((L, W), lambda b: (blk0 + b, cb + j))

    return pl.pallas_call(
        _hy_conv_kernel,
        grid=(n_seg,),
        in_specs=[ucol(0), ucol(1), ucol(2), pl.BlockSpec(conv_w.shape, one), pl.BlockSpec(conv_b.shape, one)],
        out_specs=[pl.BlockSpec((L, W), lambda b: (b, 0))] * 3,
        out_shape=[jax.ShapeDtypeStruct((n_seg * L, W), F32)] * 3,
        compiler_params=_cparams(("parallel",)),
        name="hy_short_conv",
    )(u, u, u, conv_w, conv_b)


def _hy_fwd_kernel(c_ref, s_ref, z_ref, *rest, raw):
    z = z_ref[...].astype(BF16)
    zr = _dot(c_ref[...], z)
    zi = _dot(s_ref[...], z)
    if raw:
        yr_ref, yi_ref = rest
        yr_ref[...] = zr
        yi_ref[...] = zi
    else:
        a_ref, b_ref, d_ref, yr_ref, yi_ref = rest
        yr_ref[...] = (zr * a_ref[...] - zi * b_ref[...]).astype(yr_ref.dtype)
        yi_ref[...] = (zr * b_ref[...] + zi * d_ref[...]).astype(yi_ref.dtype)


def _hy_fwd(cm, sm, z, coefs, n_seg, L, tk):
    W = z.shape[1]
    nk = L // tk
    raw = coefs is None
    mat = pl.BlockSpec((tk, L), lambda i, b: (i, 0))
    cf = pl.BlockSpec((tk, W), lambda i, b: (i, 0))
    out = pl.BlockSpec((tk, W), lambda i, b: (b * nk + i, 0))
    odt = F32 if raw else BF16
    return pl.pallas_call(
        functools.partial(_hy_fwd_kernel, raw=raw),
        grid=(nk, n_seg),
        in_specs=[mat, mat, pl.BlockSpec((L, W), lambda i, b: (b, 0))] + ([] if raw else [cf, cf, cf]),
        out_specs=[out, out],
        out_shape=[jax.ShapeDtypeStruct((n_seg * L, W), odt)] * 2,
        compiler_params=_cparams(("parallel", "arbitrary")),
        name="hy_dft_fwd",
    )(cm, sm, z, *(() if raw else coefs))


def _hy_inv_kernel(c_ref, st_ref, yr_ref, yi_ref, x_ref, vz_ref, skip_ref, o_ref):
    y = _dot(c_ref[...], yr_ref[...]) + _dot(st_ref[...], yi_ref[...])
    o_ref[...] = (x_ref[...] * (y + skip_ref[...] * vz_ref[...])).astype(o_ref.dtype)


def _hy_inv(cm, smt, yr, yi, xg, vz, skip, n_seg, L, tt, out_dtype):
    W = yr.shape[1]
    nt = L // tt
    mat = pl.BlockSpec((tt, L), lambda i, b: (i, 0))
    seq = pl.BlockSpec((L, W), lambda i, b: (b, 0))
    row = pl.BlockSpec((tt, W), lambda i, b: (b * nt + i, 0))
    return pl.pallas_call(
        _hy_inv_kernel,
        grid=(nt, n_seg),
        in_specs=[mat, mat, seq, seq, row, row, pl.BlockSpec((1, W), lambda i, b: (0, 0))],
        out_specs=row,
        out_shape=jax.ShapeDtypeStruct((n_seg * L, W), out_dtype),
        compiler_params=_cparams(("parallel", "arbitrary")),
        name="hy_dft_inv",
    )(cm, smt, yr, yi, xg, vz, skip)


def _dft_mats(L):
    split = 64
    n = jnp.arange(L, dtype=jnp.int32)

    def trig(mult):
        ang = ((mult[:, None] * n[None, :]) % (2 * L)).astype(F32) * (math.pi / L)
        return jnp.cos(ang), jnp.sin(ang)

    ca, sa = trig(split * jnp.arange(L // split, dtype=jnp.int32))
    cb, sb = trig(jnp.arange(split, dtype=jnp.int32))
    cm = (ca[:, None, :] * cb[None, :, :] - sa[:, None, :] * sb[None, :, :]).reshape(L, L)
    sm = -(sa[:, None, :] * cb[None, :, :] + ca[:, None, :] * sb[None, :, :]).reshape(L, L)
    nyq = jnp.where(n % 2 == 0, 1.0, -1.0)
    first_row = n[:, None] == 0
    first_col = n[None, :] == 0
    return (cm.astype(BF16), jnp.where(first_row, nyq[None, :], sm).astype(BF16),
            jnp.where(first_col, nyq[:, None], sm).astype(BF16))


def _hyena_filters(L, w1, b1, w2, b2, w3, freq, W):
    t01 = jnp.linspace(0.0, 1.0, L, dtype=F32)[:, None]
    wpos = (2.0 * math.pi / L) * jnp.arange(L, dtype=F32)[:, None]
    bands = jnp.linspace(1e-4, HY_BANDS - 1, HY_BANDS, dtype=F32)
    feats = jnp.concatenate([t01, jnp.cos(wpos * bands), -jnp.sin(wpos * bands)], axis=-1)
    hp = lax.Precision.HIGHEST
    h = jnp.sin(freq[0] * (jnp.dot(feats, w1, precision=hp) + b1))
    h = jnp.sin(freq[1] * (jnp.dot(h, w2, precision=hp) + b2))
    h = jnp.dot(h, w3, precision=hp).reshape(L, HY_ORDER, 2, W)
    deltas = jnp.abs(jnp.linspace(math.log(HY_TARGET) / HY_SLOW, math.log(HY_TARGET) / HY_FAST, W, dtype=F32))
    h = h * (jnp.exp(-t01 * deltas) + HY_SHIFT)[:, None, None, :]
    hf, hb = h[:, :, 0], h[:, :, 1]
    hf = hf.at[0].add(hb[0])
    hb = hb.at[0].set(0.0)
    scale = lax.rsqrt(jnp.sum(hf * hf, axis=0, keepdims=True) + jnp.sum(hb * hb, axis=0, keepdims=True) + EPS)
    return (hf * scale).reshape(L, HY_ORDER * W), (hb * scale).reshape(L, HY_ORDER * W)


def _hyena_spectrum(mats, L, w1, b1, w2, b2, w3, freq, W):
    cm, sm, _ = mats
    hf, hb = _hyena_filters(L, w1, b1, w2, b2, w3, freq, W)
    cols = jnp.concatenate([hf[:, :W], hf[:, W:], hb[:, :W], hb[:, W:]], axis=0)
    tk = min(L, 512)
    gr, gi = _hy_fwd(cm, sm, cols, None, 2 * HY_ORDER, L, tk)
    gr = gr.reshape(2, HY_ORDER, L, W)
    gi = gi.reshape(2, HY_ORDER, L, W)
    kr = gr[0] + gr[1]
    ki = gi[0] - gi[1]
    nyq = gi[0, :, 0] + gi[1, :, 0]
    n = 2.0 * L
    wk = jnp.full((L, 1), 2.0 / n, F32).at[0].set(1.0 / n)
    a = kr * wk
    bm = (ki * wk).at[:, 0].set(0.0)
    dd = a.at[:, 0].set(nyq / n)
    return [(a[o], bm[o], dd[o]) for o in range(HY_ORDER)]


def _hyena_seq(u, mats, spec, conv_w, conv_b, skip, n_seg, L, blk0, col0):
    cm, sm, smt = mats
    t = min(L, 512)
    v, x1, x2 = _hy_short_conv(u, conv_w, conv_b, n_seg, L, blk0, col0)
    yr, yi = _hy_fwd(cm, sm, v, spec[0], n_seg, L, t)
    z = _hy_inv(cm, smt, yr, yi, x1, v, skip[0:1], n_seg, L, t, F32)
    yr, yi = _hy_fwd(cm, sm, z, spec[1], n_seg, L, t)
    return _hy_inv(cm, smt, yr, yi, x2, z, skip[1:2], n_seg, L, t, BF16)


def _pack_rows(x):
    n = x.shape[1] // 2
    lo = pltpu.bitcast(x[:, :n].astype(BF16).astype(F32), jnp.uint32)
    hi = pltpu.bitcast(x[:, n:].astype(BF16).astype(F32), jnp.uint32)
    return (lo >> 16) | (hi & jnp.uint32(0xFFFF0000))


def _unpack_rows(p):
    return pltpu.bitcast(p << 16, F32), pltpu.bitcast(p & jnp.uint32(0xFFFF0000), F32)


def _moe_router_kernel(x_ref, g_ref, sc_ref, sh_ref, whi_ref, wlo_ref, rb_ref, h_ref, idx_ref, wts_ref, cnt_ref):
    x = x_ref[...]
    y = x * lax.rsqrt(jnp.mean(x * x, axis=-1, keepdims=True) + EPS) * g_ref[...]
    h = y * (1.0 + sc_ref[0]) + sh_ref[0]
    h_ref[...] = _pack_rows(h)
    hi, lo = _split2(h)
    logits = _dot_nt(whi_ref[...], hi) + _dot_nt(wlo_ref[...], hi) + _dot_nt(whi_ref[...], lo)
    scores = jax.nn.sigmoid(logits)
    sel = scores + rb_ref[...]
    E, tm = sel.shape
    gsz = E // N_GROUPS
    neg = -jnp.inf
    erow = lax.broadcasted_iota(jnp.int32, (E, 1), 0).astype(F32)
    grow = lax.broadcasted_iota(jnp.int32, (gsz, 1), 0).astype(F32)
    blocks = [sel[g * gsz:(g + 1) * gsz] for g in range(N_GROUPS)]
    gscore = []
    for blk in blocks:
        m1 = jnp.max(blk, axis=0, keepdims=True)
        first = jnp.min(jnp.where(blk == m1, grow, float(gsz)), axis=0, keepdims=True)
        m2 = jnp.max(jnp.where(grow == first, neg, blk), axis=0, keepdims=True)
        gscore.append(m1 + m2)
    kept = []
    for g in range(N_GROUPS):
        rank = jnp.zeros((1, tm), F32)
        for o in range(N_GROUPS):
            if o != g:
                ahead = (gscore[o] >= gscore[g]) if o < g else (gscore[o] > gscore[g])
                rank = rank + jnp.where(ahead, 1.0, 0.0)
        kept.append(jnp.where(rank < TOPK_GROUPS, blocks[g], neg))
    work = jnp.concatenate(kept, axis=0)
    ids, ws = [], []
    total = jnp.zeros((1, tm), F32)
    chosen = jnp.zeros((E, tm), F32)
    for j in range(TOP_K):
        mx = jnp.max(work, axis=0, keepdims=True)
        am = jnp.min(jnp.where(work == mx, erow, float(E)), axis=0, keepdims=True)
        hit = erow == am
        wj = jnp.sum(jnp.where(hit, scores, 0.0), axis=0, keepdims=True)
        work = jnp.where(hit, neg, work)
        chosen = jnp.where(hit, 1.0, chosen)
        ids.append(am)
        ws.append(wj)
        total = total + wj
    idx_ref[...] = jnp.concatenate(ids, axis=0).astype(jnp.int32)
    wts_ref[...] = jnp.concatenate(ws, axis=0) / total * ROUTED_SCALE
    ones = jnp.ones((8, tm), BF16)
    cnt_ref[0] = _dot_nt(ones, chosen.astype(BF16))[0:1]


def _moe_router(x, n_tok, g, scale, shift, whi, wlo, rb, seg_rows, tm):
    D = x.shape[1]
    E = whi.shape[0]
    K = TOP_K
    last = scale.shape[0] - 1
    row = lambda i: (i, 0)
    col = lambda i: (0, i)
    one = lambda i: (0, 0)
    mod = lambda i: (jnp.minimum(i * tm // seg_rows, last), 0, 0)
    return pl.pallas_call(
        _moe_router_kernel,
        grid=(n_tok // tm,),
        in_specs=[pl.BlockSpec((tm, D), row), pl.BlockSpec((1, D), one), pl.BlockSpec((1, 1, D), mod),
                  pl.BlockSpec((1, 1, D), mod), pl.BlockSpec((E, D), one), pl.BlockSpec((E, D), one),
                  pl.BlockSpec((E, 1), one)],
        out_specs=[pl.BlockSpec((tm, D // 2), row), pl.BlockSpec((K, tm), col), pl.BlockSpec((K, tm), col),
                   pl.BlockSpec((1, 1, E), lambda i: (i, 0, 0))],
        out_shape=[jax.ShapeDtypeStruct((n_tok, D // 2), jnp.uint32), jax.ShapeDtypeStruct((K, n_tok), jnp.int32),
                   jax.ShapeDtypeStruct((K, n_tok), F32), jax.ShapeDtypeStruct((n_tok // tm, 1, E), F32)],
        compiler_params=_cparams(("parallel",)),
        name="moe_router",
    )(x, g, scale, shift, whi, wlo, rb)


def _moe_pos_kernel(idx_ref, base_ref, pos_ref):
    K, tm = idx_ref.shape
    E = base_ref.shape[1]
    idx = idx_ref[...]
    erow = lax.broadcasted_iota(jnp.int32, (E, 1), 0)
    hits = [erow == idx[j:j + 1, :] for j in range(K)]
    onehot = jnp.zeros((E, tm), F32)
    for hit in hits:
        onehot = jnp.where(hit, 1.0, onehot)
    row = lax.broadcasted_iota(jnp.int32, (tm, tm), 0)
    col = lax.broadcasted_iota(jnp.int32, (tm, tm), 1)
    before = _dot(onehot.astype(BF16), (row < col).astype(BF16))
    dest = before + base_ref[0]
    pos = [jnp.sum(jnp.where(hit, dest, 0.0), axis=0, keepdims=True) for hit in hits]
    pos_ref[...] = jnp.concatenate(pos, axis=0).astype(jnp.int32)


def _moe_plan(idx, cnt, bm, tm):
    K, n_tok = idx.shape
    E = cnt.shape[2]
    cnt = cnt[:, 0, :]
    counts = jnp.sum(cnt, axis=0)
    pcounts = jnp.ceil(counts / bm) * bm
    pends = jnp.cumsum(pcounts)
    base = (pends - pcounts)[None, :] + jnp.cumsum(cnt, axis=0) - cnt
    pos = pl.pallas_call(
        _moe_pos_kernel,
        grid=(n_tok // tm,),
        in_specs=[pl.BlockSpec((K, tm), lambda i: (0, i)), pl.BlockSpec((1, E, 1), lambda i: (i, 0, 0))],
        out_specs=pl.BlockSpec((K, tm), lambda i: (0, i)),
        out_shape=jax.ShapeDtypeStruct((K, n_tok), jnp.int32),
        compiler_params=_cparams(("parallel",)),
        name="moe_positions",
    )(idx, base[:, :, None])
    n_blocks = -(-(n_tok * K + E * (bm - 1)) // bm)
    block_start = jnp.concatenate([jnp.zeros((1,), F32), pends / bm]).astype(jnp.int32)
    return pos.T.reshape(-1), block_start, n_blocks


def _per_token_rows(tm, K, copy):
    def start(t, carry):
        for j in range(K):
            copy(t, j).start(priority=j % 2)
        return carry

    def wait(t, carry):
        for j in range(K):
            copy(t, j).wait()
        return carry

    return (lambda: lax.fori_loop(0, tm, start, 0, unroll=2)), (lambda: lax.fori_loop(0, tm, wait, 0, unroll=2))


def _moe_dispatch_kernel(pos_ref, prev_ref, h_ref, xs_in_ref, xs_hbm, stage_ref, sems, *, K):
    del xs_in_ref
    tm = h_ref.shape[0]
    i = pl.program_id(0)
    last = pl.num_programs(0) - 1
    slot = lax.bitwise_and(i, 1)

    def tile_copies(s, p_ref):
        def row_copy(t, j):
            return pltpu.make_async_copy(stage_ref.at[s, pl.ds(t, 1)], xs_hbm.at[pl.ds(p_ref[t * K + j], 1)],
                                         sems.at[s])
        return _per_token_rows(tm, K, row_copy)

    start, wait = tile_copies(slot, pos_ref)
    _, wait_prev = tile_copies(1 - slot, prev_ref)
    stage_ref[slot] = h_ref[...]
    start()

    @pl.when(i > 0)
    def _():
        wait_prev()

    @pl.when(i == last)
    def _():
        wait()


def _moe_dispatch(pos, h, n_rows, K, tm):
    n_tok, W = h.shape
    xs0 = jnp.zeros((n_rows, W), h.dtype)
    return pl.pallas_call(
        functools.partial(_moe_dispatch_kernel, K=K),
        grid=(n_tok // tm,),
        in_specs=[pl.BlockSpec((tm * K,), lambda i: (i,), memory_space=pltpu.SMEM),
                  pl.BlockSpec((tm * K,), lambda i: (jnp.maximum(i - 1, 0),), memory_space=pltpu.SMEM),
                  pl.BlockSpec((tm, W), lambda i: (i, 0)),
                  pl.BlockSpec(memory_space=pl.ANY)],
        out_specs=pl.BlockSpec(memory_space=pl.ANY),
        out_shape=jax.ShapeDtypeStruct((n_rows, W), h.dtype),
        scratch_shapes=[pltpu.VMEM((2, tm, W), h.dtype), pltpu.SemaphoreType.DMA((2,))],
        input_output_aliases={3: 0},
        compiler_params=_cparams(("arbitrary",)),
        name="moe_dispatch",
    )(pos, pos, h, xs0)


def _moe_expert_kernel(bs_ref, x_hbm, w1_ref, w3_ref, w2_ref, y_hbm, xbuf, ybuf, w1b_ref, w3b_ref, w2b_ref,
                       xsem, ysem, *, bm, nbuf, n_blocks):
    e = pl.program_id(0)
    last = pl.num_programs(0) - 1
    b0 = bs_ref[e]
    b1 = bs_ref[e + 1]
    n_used = bs_ref[last + 1]

    def x_copy(g):
        slot = lax.bitwise_and(g, nbuf - 1)
        return pltpu.make_async_copy(x_hbm.at[pl.ds(g * bm, bm)], xbuf.at[slot], xsem.at[slot])

    def y_copy(g):
        slot = lax.bitwise_and(g, nbuf - 1)
        return pltpu.make_async_copy(ybuf.at[slot], y_hbm.at[pl.ds(g * bm, bm)], ysem.at[slot])

    @pl.when(e == 0)
    def _():
        for p in range(nbuf - 1):
            @pl.when(p < n_used)
            def _():
                x_copy(p).start()

    @pl.when(b1 > b0)
    def _():
        w1b_ref[...] = w1_ref[0, 0].astype(BF16)
        w3b_ref[...] = w3_ref[0, 0].astype(BF16)
        w2b_ref[...] = w2_ref[0, 0].astype(BF16)

    def block(g, carry):
        slot = lax.bitwise_and(g, nbuf - 1)
        x_copy(g).wait()

        @pl.when(g + nbuf - 1 < n_used)
        def _():
            x_copy(g + nbuf - 1).start()

        @pl.when(g >= nbuf)
        def _():
            y_copy(g - nbuf).wait()

        x = jnp.concatenate(_unpack_rows(xbuf[slot]), axis=1).astype(BF16)
        a = _dot(x, w1b_ref[...])
        b = _dot(x, w3b_ref[...])
        ybuf[slot] = _pack_rows(_dot((jax.nn.silu(a) * b).astype(BF16), w2b_ref[...]))
        y_copy(g).start()
        return carry

    lax.fori_loop(b0, b1, block, 0)

    @pl.when(e == last)
    def _():
        def drain(g, carry):
            y_copy(g).wait()
            return carry

        lax.fori_loop(jnp.maximum(n_used - nbuf, 0), n_used, drain, 0)
        ybuf[0] = jnp.zeros(ybuf.shape[1:], ybuf.dtype)

        def zero_copy(g):
            return pltpu.make_async_copy(ybuf.at[0], y_hbm.at[pl.ds(g * bm, bm)], ysem.at[0])

        def fill(g, carry):
            zero_copy(g).start()
            return carry

        def fill_wait(g, carry):
            zero_copy(g).wait()
            return carry

        lax.fori_loop(n_used, n_blocks, fill, 0)
        lax.fori_loop(n_used, n_blocks, fill_wait, 0)


def _moe_experts(block_start, xs, w1, w3, w2, layer, bm):
    P, W = xs.shape
    E, D, F = w1.shape[1:]
    nbuf = 4
    grid_spec = pltpu.PrefetchScalarGridSpec(
        num_scalar_prefetch=1,
        grid=(E,),
        in_specs=[pl.BlockSpec(memory_space=pl.ANY),
                  pl.BlockSpec((1, 1, D, F), lambda e, bs: (layer, e, 0, 0)),
                  pl.BlockSpec((1, 1, D, F), lambda e, bs: (layer, e, 0, 0)),
                  pl.BlockSpec((1, 1, F, D), lambda e, bs: (layer, e, 0, 0))],
        out_specs=pl.BlockSpec(memory_space=pl.ANY),
        scratch_shapes=[pltpu.VMEM((nbuf, bm, W), xs.dtype), pltpu.VMEM((nbuf, bm, W), xs.dtype),
                        pltpu.VMEM((D, F), BF16), pltpu.VMEM((D, F), BF16), pltpu.VMEM((F, D), BF16),
                        pltpu.SemaphoreType.DMA((nbuf,)), pltpu.SemaphoreType.DMA((nbuf,))],
    )
    return pl.pallas_call(
        functools.partial(_moe_expert_kernel, bm=bm, nbuf=nbuf, n_blocks=P // bm),
        grid_spec=grid_spec,
        out_shape=jax.ShapeDtypeStruct((P, W), xs.dtype),
        compiler_params=_cparams(("arbitrary",)),
        name="moe_experts",
    )(block_start, xs, w1, w3, w2)


def _moe_combine_kernel(pos_ref, next_ref, wts_ref, x_ref, h_ref, sw1_ref, sw3_ref, sw2_ref, gate_ref, ys_hbm, o_ref,
                        gath_ref, sems, *, K):
    tm = x_ref.shape[0]
    i = pl.program_id(0)
    last = pl.num_programs(0) - 1

    slot = lax.bitwise_and(i, 1)

    def tile_copies(s, p_ref):
        def row_copy(t, j):
            return pltpu.make_async_copy(ys_hbm.at[pl.ds(p_ref[t * K + j], 1)], gath_ref.at[s, j, pl.ds(t, 1)],
                                         sems.at[s])
        return _per_token_rows(tm, K, row_copy)

    start, wait = tile_copies(slot, pos_ref)
    start_next, _ = tile_copies(1 - slot, next_ref)

    @pl.when(i == 0)
    def _():
        start()

    @pl.when(i < last)
    def _():
        start_next()

    h = jnp.concatenate(_unpack_rows(h_ref[...]), axis=1).astype(BF16)
    mid = jax.nn.silu(_dot(h, sw1_ref[...])) * _dot(h, sw3_ref[...])
    shared = _dot(mid.astype(BF16), sw2_ref[...])
    wait()
    wts = wts_ref[...]
    W = gath_ref.shape[3]
    acc_lo = shared[:, :W]
    acc_hi = shared[:, W:]
    for j in range(K):
        lo, hi = _unpack_rows(gath_ref[slot, j])
        acc_lo = acc_lo + wts[:, j:j + 1] * lo
        acc_hi = acc_hi + wts[:, j:j + 1] * hi
    gate = gate_ref[0]
    o_ref[:, :W] = x_ref[:, :W] + gate[:, :W] * acc_lo
    o_ref[:, W:] = x_ref[:, W:] + gate[:, W:] * acc_hi


def _moe_combine(pos, wts, x, h, sw1, sw3, sw2, gate, ys, n_tok, K, seg_rows, tm):
    D = x.shape[1]
    W = ys.shape[1]
    n_tiles = n_tok // tm
    last = gate.shape[0] - 1
    row = lambda i: (i, 0)
    one = lambda i: (0, 0)
    return pl.pallas_call(
        functools.partial(_moe_combine_kernel, K=K),
        grid=(n_tiles,),
        in_specs=[pl.BlockSpec((tm * K,), lambda i: (i,), memory_space=pltpu.SMEM),
                  pl.BlockSpec((tm * K,), lambda i: (jnp.minimum(i + 1, n_tiles - 1),), memory_space=pltpu.SMEM),
                  pl.BlockSpec((tm, K), row), pl.BlockSpec((tm, D), row), pl.BlockSpec((tm, W), row),
                  pl.BlockSpec(sw1.shape, one), pl.BlockSpec(sw3.shape, one), pl.BlockSpec(sw2.shape, one),
                  pl.BlockSpec((1, 1, D), lambda i: (jnp.minimum(i * tm // seg_rows, last), 0, 0)),
                  pl.BlockSpec(memory_space=pl.ANY)],
        out_specs=pl.BlockSpec((tm, D), row),
        out_shape=jax.ShapeDtypeStruct((n_tok, D), F32),
        scratch_shapes=[pltpu.VMEM((2, K, tm, W), ys.dtype), pltpu.SemaphoreType.DMA((2,))],
        compiler_params=_cparams(("arbitrary",)),
        name="moe_combine",
    )(pos, pos, wts, x, h, sw1, sw3, sw2, gate, ys)


def _moe_ffn_residual(x, n_tok, norm_g, scale, shift, gate, router_w, router_b, ew1, ew3, ew2, layer, sw1, sw3, sw2,
                      seg_rows, bm):
    K = TOP_K
    tm = 256
    whi, wlo = _split2(router_w.T)
    h, idx, wts, cnt = _moe_router(x, n_tok, norm_g, scale, shift, whi, wlo, router_b[:, None], seg_rows, tm)
    pos, block_start, n_blocks = _moe_plan(idx, cnt, bm, tm)
    wts = wts.T
    xs = _moe_dispatch(pos, h, n_blocks * bm, K, tm)
    ys = _moe_experts(block_start, xs, ew1, ew3, ew2, layer, bm)
    return _moe_combine(pos, wts, x, h, sw1.astype(BF16), sw3.astype(BF16), sw2.astype(BF16), gate, ys,
                        n_tok, K, seg_rows, tm)


def kernel(x, c, ctx, c_ctx, w_mod, b_mod, norm1_g, norm2_g, w_in, mlstm_gate_b, mlstm_norm_g, da_qnorm_g, da_knorm_g, da_lambda, da_subln_g, hy_conv_w, hy_conv_b, hy_w1, hy_b1, hy_w2, hy_b2, hy_w3, hy_freq, hy_skip, w_out, router_w, router_b, exp_w1, exp_w3, exp_w2, sh_w1, sh_w3, sh_w2):
    B, S, D = x.shape
    n_ctx = ctx.shape[1]
    depth = w_in.shape[0]
    n_lat = B * S
    ml_w = mlstm_norm_g.shape[1]
    da_dh = da_qnorm_g.shape[1]
    da_w = DA_HEADS * 2 * da_dh
    hy_w = hy_skip.shape[2]
    n_gates = 4 * ML_HEADS
    da_col = 0
    hy_col = da_col + 3 * da_w
    o_col = hy_col + 3 * hy_w
    gate_col = o_col + ml_w
    ml_in = 4 * ml_w + n_gates
    tm = 512

    X = jnp.concatenate([x.reshape(n_lat, D), ctx.reshape(B * n_ctx, D)], axis=0)
    sc = jax.nn.silu(jnp.concatenate([c, c_ctx[None]], axis=0))
    cos, sin = _axial_rope_tables(S, da_dh, da_w // da_dh, tm)
    mats_l = _dft_mats(S)
    mats_c = _dft_mats(n_ctx)
    for l in range(depth):
        last = l == depth - 1
        lam_init = 0.8 - 0.6 * math.exp(-0.3 * l)
        mods = (jnp.dot(sc, w_mod[l], precision=lax.Precision.HIGHEST) + b_mod[l]).reshape(B + 1, 6, 1, D)
        sh1, s1, g1, sh2, s2, g2 = [mods[:, i] for i in range(6)]
        wl = w_in[l]
        w_big = jnp.concatenate([wl[:, ml_in:], wl[:, 3 * ml_w:4 * ml_w], wl[:, 4 * ml_w:ml_in],
                                 jnp.zeros((D, LANES - n_gates), F32)], axis=1).astype(BF16)
        w_qkv_t = wl[:, :3 * ml_w].T.astype(BF16)
        U, UT = _norm_mod_matmul(X, norm1_g[l][None], s1, sh1, w_big, w_qkv_t, S, tm, w_big.shape[1] // 3)
        gb = jnp.concatenate([mlstm_gate_b[l], jnp.zeros((LANES - n_gates,), F32)])[None]
        m_out = _mlstm_mixer(UT, U, gb, mlstm_norm_g[l][:, None], B, S, n_ctx, o_col, gate_col)
        d_out = _diff_attn_mixer(U, cos, sin, da_qnorm_g[l], da_knorm_g[l], da_lambda[l], da_subln_g[l], lam_init,
                                 B, S, n_ctx, da_col, not last)
        hy_args = (hy_w1[l], hy_b1[l], hy_w2[l], hy_b2[l], hy_w3[l], hy_freq[l], hy_w)
        y_out = _hyena_seq(U, mats_l, _hyena_spectrum(mats_l, S, *hy_args), hy_conv_w[l], hy_conv_b[l][None],
                           hy_skip[l], B, S, 0, hy_col)
        n_rows = n_lat
        if not last:
            y_ctx = _hyena_seq(U, mats_c, _hyena_spectrum(mats_c, n_ctx, *hy_args), hy_conv_w[l], hy_conv_b[l][None],
                               hy_skip[l], B, n_ctx, n_lat // n_ctx, hy_col)
            y_out = jnp.concatenate([y_out, y_ctx], axis=0)
            n_rows = n_lat + B * n_ctx
        wo = w_out[l].astype(BF16)
        X = _out_proj_residual(m_out, d_out, y_out, wo[:ml_w], wo[ml_w:ml_w + da_w], wo[ml_w + da_w:], X, g1,
                               n_rows, S, tm)
        X = _moe_ffn_residual(X, n_rows, norm2_g[l][None], s2, sh2, g2, router_w[l], router_b[l],
                              exp_w1, exp_w3, exp_w2, l, sh_w1[l], sh_w3[l], sh_w2[l], S,
                              MOE_BLOCK)
    return X[:n_lat].reshape(B, S, D)
```

```python
import functools
import math

import jax
import jax.numpy as jnp
from jax import lax
from jax.experimental import pallas as pl
from jax.experimental.pallas import tpu as pltpu

F32 = jnp.float32
BF16 = jnp.bfloat16

EPS = 1e-6
GRID_W = 64
ROPE_THETA = 10000.0
ML_HEADS = 4
ML_CHUNK = 256
ML_M_INIT = -1e30
DA_HEADS = 4
HY_ORDER = 2
HY_BANDS = 8
HY_SHIFT = 0.05
HY_TARGET = 1e-2
HY_FAST = 0.3
HY_SLOW = 1.5
N_GROUPS = 8
TOPK_GROUPS = 4
TOP_K = 8
ROUTED_SCALE = 2.5
MOE_BLOCK = 256
LANES = 128
VMEM_LIMIT = 56 * 1024 * 1024


def _cparams(sem):
    return pltpu.CompilerParams(dimension_semantics=sem, vmem_limit_bytes=VMEM_LIMIT)


def _dot(a, b):
    return jnp.dot(a, b, preferred_element_type=F32)


def _dot_nt(a, b):
    return lax.dot_general(a, b, (((1,), (1,)), ((), ())), preferred_element_type=F32)


def _dot_tn(a, b):
    return lax.dot_general(a, b, (((0,), (0,)), ((), ())), preferred_element_type=F32)


def _split3(a):
    hi = a.astype(BF16)
    r = a - hi.astype(F32)
    mid = r.astype(BF16)
    lo = (r - mid.astype(F32)).astype(BF16)
    return hi, mid, lo


def _split2(a):
    hi = a.astype(BF16)
    lo = (a - hi.astype(F32)).astype(BF16)
    return hi, lo


def _norm_mod_mm_kernel(x_ref, g_ref, sc_ref, sh_ref, w_ref, wt_ref, o_ref, ot_ref, xn_ref):
    @pl.when(pl.program_id(1) == 0)
    def _():
        x = x_ref[...]
        y = x * lax.rsqrt(jnp.mean(x * x, axis=-1, keepdims=True) + EPS) * g_ref[...]
        xn_ref[...] = (y * (1.0 + sc_ref[0]) + sh_ref[0]).astype(BF16)
        ot_ref[...] = _dot_nt(wt_ref[...], xn_ref[...])

    o_ref[...] = _dot(xn_ref[...], w_ref[...])


def _norm_mod_matmul(x, g, scale, shift, w, wt, seg_rows, tm, tn):
    R, D = x.shape
    N = w.shape[1]
    NT = wt.shape[0]
    last = scale.shape[0] - 1
    mod_map = lambda i, j: (jnp.minimum(i * tm // seg_rows, last), 0, 0)
    return pl.pallas_call(
        _norm_mod_mm_kernel,
        grid=(R // tm, N // tn),
        in_specs=[
            pl.BlockSpec((tm, D), lambda i, j: (i, 0)),
            pl.BlockSpec((1, D), lambda i, j: (0, 0)),
            pl.BlockSpec((1, 1, D), mod_map),
            pl.BlockSpec((1, 1, D), mod_map),
            pl.BlockSpec((D, tn), lambda i, j: (0, j)),
            pl.BlockSpec((NT, D), lambda i, j: (0, 0)),
        ],
        out_specs=[pl.BlockSpec((tm, tn), lambda i, j: (i, j)), pl.BlockSpec((NT, tm), lambda i, j: (0, i))],
        out_shape=[jax.ShapeDtypeStruct((R, N), F32), jax.ShapeDtypeStruct((NT, R), F32)],
        scratch_shapes=[pltpu.VMEM((tm, D), BF16)],
        compiler_params=_cparams(("parallel", "arbitrary")),
        name="norm_mod_matmul",
    )(x, g, scale, shift, w, wt)


def _out_proj_kernel(m_ref, d_ref, y_ref, wm_ref, wd_ref, wy_ref, x_ref, gate_ref, o_ref):
    acc = _dot(m_ref[...], wm_ref[...]) + _dot(d_ref[...], wd_ref[...]) + _dot(y_ref[...], wy_ref[...])
    o_ref[...] = x_ref[...] + gate_ref[0] * acc


def _out_proj_residual(m, d, y, wm, wd, wy, x, gate, n_rows, seg_rows, tm):
    R, D = n_rows, x.shape[1]
    last = gate.shape[0] - 1
    row = lambda i: (i, 0)
    full = lambda i: (0, 0)
    return pl.pallas_call(
        _out_proj_kernel,
        grid=(R // tm,),
        in_specs=[
            pl.BlockSpec((tm, m.shape[1]), row),
            pl.BlockSpec((tm, d.shape[1]), row),
            pl.BlockSpec((tm, y.shape[1]), row),
            pl.BlockSpec(wm.shape, full),
            pl.BlockSpec(wd.shape, full),
            pl.BlockSpec(wy.shape, full),
            pl.BlockSpec((tm, D), row),
            pl.BlockSpec((1, 1, D), lambda i: (jnp.minimum(i * tm // seg_rows, last), 0, 0)),
        ],
        out_specs=pl.BlockSpec((tm, D), row),
        out_shape=jax.ShapeDtypeStruct((R, D), F32),
        compiler_params=_cparams(("parallel",)),
        name="out_proj_residual",
    )(m, d, y, wm, wd, wy, x, gate)


def _log_sigmoid(x):
    return jnp.minimum(x, 0.0) - jnp.log1p(jnp.exp(-jnp.abs(x)))


def _mlstm_gate_tables(g_ref, r0, L, gb, tril, triu):
    g = g_ref[pl.ds(r0, L), :] + gb
    lf = _log_sigmoid(g)
    gT = g.T
    lfT = lf.T
    parts = _split3(lf)
    partsT = _split3(lfT)
    cs_f = sum(_dot(tril, p) for p in parts)
    cs_b = sum(_dot(triu, p) for p in parts)
    rs_f = sum(_dot(p, triu) for p in partsT)
    rs_b = sum(_dot(p, tril) for p in partsT)
    return g, gT, cs_f, cs_b, rs_f, rs_b


def _mlstm_chunk(qT, kT, vT, i_row, b_row, c_col, b_end, mask, state):
    C, n, m = state
    qb = qT.astype(BF16)
    kb = kT.astype(BF16)
    dmat = jnp.where(mask, b_row + c_col, -jnp.inf)
    inter = b_row + m
    m_t = jnp.maximum(inter, jnp.max(dmat, axis=0, keepdims=True))
    s = _dot_tn(kb, qb) * jnp.exp(dmat - m_t)
    carry_w = jnp.exp(inter - m_t)
    num = _dot(vT.astype(BF16), s.astype(BF16)) + carry_w * _dot(C.astype(BF16), qb)
    den = jnp.sum(s, axis=0, keepdims=True) + carry_w * jnp.sum(qT * n, axis=0, keepdims=True)
    h = num / jnp.maximum(jnp.abs(den), jnp.exp(-m_t))
    g = b_end - b_row + i_row
    m_new = jnp.maximum(b_end + m, jnp.max(g, axis=-1, keepdims=True))
    ws = jnp.exp(g - m_new)
    decay = jnp.exp(b_end + m - m_new)
    C_new = decay * C + _dot_nt((vT * ws).astype(BF16), kb)
    n_new = decay * n + jnp.sum(kT * ws, axis=-1, keepdims=True)
    return h, (C_new, n_new, m_new)


def _mlstm_kernel(ql_ref, kl_ref, vl_ref, ol_ref, gl_ref, qc_ref, kc_ref, vc_ref, oc_ref, gc_ref,
                  gb_ref, ng_ref, outl_ref, outc_ref, hf_ref, hb_ref, *, L, H, dh):
    S = ql_ref.shape[1]
    n_ctx = qc_ref.shape[1]
    row = lax.broadcasted_iota(jnp.int32, (L, L), 0)
    col = lax.broadcasted_iota(jnp.int32, (L, L), 1)
    lower = col <= row
    upper = col >= row
    tril = lower.astype(BF16)
    triu = upper.astype(BF16)
    gb = gb_ref[...]
    k_scale = dh ** -0.5

    def both_dirs(refs_f, r0_f, refs_b, r0_b, hoff, state):
        new_state = []
        for d, (refs, r0, h_ref) in enumerate(((refs_f, r0_f, hf_ref), (refs_b, r0_b, hb_ref))):
            q_ref, k_ref, v_ref, g_ref = refs
            g, gT, cs_f, cs_b, rs_f, rs_b = _mlstm_gate_tables(g_ref, r0, L, gb, tril, triu)
            cs, rs, mask = (cs_f, rs_f, upper) if d == 0 else (cs_b, rs_b, lower)
            end = L - 1 if d == 0 else 0
            for hh in range(H):
                ic = 2 * d * H + hh
                fc = ic + H
                rows = slice(hh * dh, (hh + 1) * dh)
                qT = q_ref[rows, pl.ds(r0, L)]
                kT = k_ref[rows, pl.ds(r0, L)] * k_scale
                vT = v_ref[rows, pl.ds(r0, L)]
                h, st = _mlstm_chunk(qT, kT, vT, gT[ic:ic + 1, :], rs[fc:fc + 1, :],
                                     g[:, ic:ic + 1] - cs[:, fc:fc + 1], rs[fc:fc + 1, end:end + 1], mask,
                                     state[d * H + hh])
                off = hoff + r0
                h_ref[rows, pl.ds(off if isinstance(off, int) else pl.multiple_of(off, L), L)] = h
                new_state.append(st)
        return tuple(new_state)

    state = tuple((jnp.zeros((dh, dh), F32), jnp.zeros((dh, 1), F32), jnp.full((1, 1), ML_M_INIT, F32))
                  for _ in range(2 * H))
    ctx_refs = (qc_ref, kc_ref, vc_ref, gc_ref)
    lat_refs = (ql_ref, kl_ref, vl_ref, gl_ref)
    n_cc = n_ctx // L
    for c in range(n_cc):
        state = both_dirs(ctx_refs, c * L, ctx_refs, (n_cc - 1 - c) * L, 0, state)
    n_lc = S // L

    def body(c, st):
        r_f = pl.multiple_of(c * L, L)
        r_b = pl.multiple_of((n_lc - 1 - c) * L, L)
        return both_dirs(lat_refs, r_f, lat_refs, r_b, n_ctx, st)

    lax.fori_loop(0, n_lc, body, state)

    def finish(o_ref, out_ref, hoff, n_rows):
        def fbody(c, carry):
            r0 = pl.multiple_of(c * L, L)
            off = pl.multiple_of(hoff + r0, L)
            hs = hf_ref[:, pl.ds(off, L)] + hb_ref[:, pl.ds(off, L)]
            normed = []
            for hh in range(H):
                rows = slice(hh * dh, (hh + 1) * dh)
                hv = hs[rows]
                normed.append(hv * lax.rsqrt(jnp.mean(hv * hv, axis=0, keepdims=True) + EPS) * ng_ref[rows])
            hn = jnp.concatenate(normed, axis=0).T
            out_ref[pl.ds(r0, L), :] = (jax.nn.sigmoid(o_ref[pl.ds(r0, L), :]) * hn).astype(out_ref.dtype)
            return carry
        lax.fori_loop(0, n_rows // L, fbody, 0)

    finish(ol_ref, outl_ref, n_ctx, S)
    finish(oc_ref, outc_ref, 0, n_ctx)


def _mlstm_mixer(uT, u, gate_b, norm_g, B, S, n_ctx, o_col, gate_col):
    W = norm_g.shape[0]
    H = ML_HEADS
    dh = W // H
    ocb = o_col // W
    gcb = gate_col // LANES
    cblk = (B * S) // n_ctx

    def lat_t(j):
        return pl.BlockSpec((W, S), lambda b: (j, b))

    def ctx_t(j):
        return pl.BlockSpec((W, n_ctx), lambda b: (j, cblk + b))

    one = lambda b: (0, 0)
    out_l, out_c = pl.pallas_call(
        functools.partial(_mlstm_kernel, L=ML_CHUNK, H=H, dh=dh),
        grid=(B,),
        in_specs=[lat_t(0), lat_t(1), lat_t(2), pl.BlockSpec((S, W), lambda b: (b, ocb)),
                  pl.BlockSpec((S, LANES), lambda b: (b, gcb)),
                  ctx_t(0), ctx_t(1), ctx_t(2), pl.BlockSpec((n_ctx, W), lambda b: (cblk + b, ocb)),
                  pl.BlockSpec((n_ctx, LANES), lambda b: (cblk + b, gcb)),
                  pl.BlockSpec((1, LANES), one), pl.BlockSpec((W, 1), one)],
        out_specs=[pl.BlockSpec((S, W), lambda b: (b, 0)), pl.BlockSpec((n_ctx, W), lambda b: (b, 0))],
        out_shape=[jax.ShapeDtypeStruct((B * S, W), BF16), jax.ShapeDtypeStruct((B * n_ctx, W), BF16)],
        scratch_shapes=[pltpu.VMEM((W, n_ctx + S), F32), pltpu.VMEM((W, n_ctx + S), F32)],
        compiler_params=_cparams(("parallel",)),
        name="mlstm",
    )(uT, uT, uT, u, u, uT, uT, uT, u, u, gate_b, norm_g)
    return jnp.concatenate([out_l, out_c], axis=0)


def _da_prep_kernel(q_ref, k_ref, v_ref, cos_ref, sin_ref, qg_ref, kg_ref, seg_ref, qo_ref, ko_ref, vo_ref, *, dh):
    cos = cos_ref[...]
    sin = sin_ref[...]
    seg = seg_ref[...]
    W = q_ref.shape[1]
    lane = lax.broadcasted_iota(jnp.int32, (1, W), 1)
    quarter = dh // 4
    first = (lane % (2 * quarter)) < quarter

    def norm_rope(x, g):
        hi, lo = _split2(x * x)
        ms = (_dot(hi, seg) + _dot(lo, seg)) * (1.0 / dh)
        xn = x * lax.rsqrt(ms + EPS) * g
        rot = jnp.where(first, -pltpu.roll(xn, W - quarter, 1), pltpu.roll(xn, quarter, 1))
        return xn * cos + rot * sin

    qo_ref[...] = (norm_rope(q_ref[...], qg_ref[...]) * (dh ** -0.5 * math.log2(math.e))).astype(BF16)
    ko_ref[...] = norm_rope(k_ref[...], kg_ref[...]).astype(BF16)
    vo_ref[...] = v_ref[...].astype(BF16)


def _da_prep(u, cos, sin, qg, kg, seg, n_lat_rows, S, col0, tm, dh):
    R = u.shape[0]
    W = qg.shape[1]
    cb = col0 // W
    n_lat = n_lat_rows // tm
    per_seq = S // tm
    tab = lambda i: (jnp.where(i < n_lat, i % per_seq, per_seq), 0)
    one = lambda i: (0, 0)
    row = lambda i: (i, 0)

    def ucol(j):
        return pl.BlockSpec((tm, W), lambda i: (i, cb + j))

    return pl.pallas_call(
        functools.partial(_da_prep_kernel, dh=dh),
        grid=(R // tm,),
        in_specs=[ucol(0), ucol(1), ucol(2), pl.BlockSpec((tm, W), tab), pl.BlockSpec((tm, W), tab),
                  pl.BlockSpec((1, W), one), pl.BlockSpec((1, W), one), pl.BlockSpec((W, W), one)],
        out_specs=[pl.BlockSpec((tm, W), row)] * 3,
        out_shape=[jax.ShapeDtypeStruct((R, W), BF16)] * 3,
        compiler_params=_cparams(("parallel",)),
        name="da_prep",
    )(u, u, u, cos, sin, qg, kg, seg)


def _da_attn_kernel(*refs, n_kv, dh, lam_init):
    q_ref = refs[0]
    k_refs = refs[1:1 + n_kv]
    v_refs = refs[1 + n_kv:1 + 2 * n_kv]
    lam_ref, sg_ref, o_ref = refs[1 + 2 * n_kv:]
    lp = lam_ref[...]
    lam = (jnp.exp(jnp.sum(lp[0:1] * lp[1:2], axis=-1, keepdims=True))
           - jnp.exp(jnp.sum(lp[2:3] * lp[3:4], axis=-1, keepdims=True)) + lam_init)
    vd = 2 * dh
    for hh in range(q_ref.shape[1] // vd):
        head = slice(hh * vd, (hh + 1) * vd)
        acc = None
        nums, dens = [], []
        for mp in range(2):
            lanes = slice(hh * vd + mp * dh, hh * vd + (mp + 1) * dh)
            s = [_dot_nt(q_ref[:, lanes], k_ref[:, lanes]) for k_ref in k_refs]
            mx = functools.reduce(jnp.maximum, [jnp.max(si, axis=-1, keepdims=True) for si in s])
            p = [jnp.exp2(si - mx) for si in s]
            nums.append(p)
            dens.append(sum(jnp.sum(pi, axis=-1, keepdims=True) for pi in p))
        c = lam * dens[0] / dens[1]
        for j in range(n_kv):
            a = (nums[0][j] - c * nums[1][j]).astype(BF16)
            t = _dot(a, v_refs[j][:, head])
            acc = t if acc is None else acc + t
        acc = acc / dens[0]
        o = acc * lax.rsqrt(jnp.mean(acc * acc, axis=-1, keepdims=True) + EPS) * sg_ref[...]
        o_ref[:, head] = (o * (1.0 - lam_init)).astype(o_ref.dtype)


def _da_attention(q, k, v, lam_p, subln_g, lam_init, B, q_rows, q_blk0, kv_segs, tq, dh):
    H = DA_HEADS
    vd = 2 * dh
    hp = 2
    nq = q_rows // tq
    q0 = q_blk0

    def kv_spec(rows, blk0):
        return pl.BlockSpec((rows, hp * vd), lambda b, h, i: (blk0 + b, h))

    kspecs = [kv_spec(r, b0) for r, b0 in kv_segs]
    one = lambda b, h, i: (0, 0)
    return pl.pallas_call(
        functools.partial(_da_attn_kernel, n_kv=len(kv_segs), dh=dh, lam_init=lam_init),
        grid=(B, H // hp, nq),
        in_specs=[pl.BlockSpec((tq, hp * vd), lambda b, h, i: (q0 + b * nq + i, h))] + kspecs + kspecs
                 + [pl.BlockSpec(lam_p.shape, one), pl.BlockSpec((1, vd), one)],
        out_specs=pl.BlockSpec((tq, hp * vd), lambda b, h, i: (b * nq + i, h)),
        out_shape=jax.ShapeDtypeStruct((B * q_rows, H * vd), BF16),
        compiler_params=_cparams(("parallel", "parallel", "arbitrary")),
        name="da_attention",
    )(q, *([k] * len(kv_segs)), *([v] * len(kv_segs)), lam_p, subln_g)


def _axial_rope_tables(S, dh, reps, pad_rows):
    rows = S // GRID_W
    r = jnp.repeat(jnp.arange(rows, dtype=F32), GRID_W)
    col = jnp.tile(jnp.arange(GRID_W, dtype=F32), rows)
    n_freq = dh // 4
    inv = ROPE_THETA ** (-jnp.arange(n_freq, dtype=F32) / n_freq)
    ar = r[:, None] * inv
    ac = col[:, None] * inv
    ang = jnp.concatenate([ar, ar, ac, ac], axis=-1)
    cos = jnp.concatenate([jnp.tile(jnp.cos(ang), (1, reps)), jnp.ones((pad_rows, dh * reps), F32)], axis=0)
    sin = jnp.concatenate([jnp.tile(jnp.sin(ang), (1, reps)), jnp.zeros((pad_rows, dh * reps), F32)], axis=0)
    return cos, sin


def _diff_attn_mixer(u, cos, sin, qg, kg, lam_p, subln_g, lam_init, B, S, n_ctx, col0, need_ctx):
    dh = qg.shape[0]
    W = DA_HEADS * 2 * dh
    seg = (jnp.arange(W)[:, None] // dh == jnp.arange(W)[None, :] // dh).astype(BF16)
    tm = 512
    q, k, v = _da_prep(u, cos, sin, jnp.tile(qg, W // dh)[None], jnp.tile(kg, W // dh)[None], seg,
                       B * S, S, col0, tm, dh)
    sg = subln_g[None]
    ctx_blk0 = (B * S) // n_ctx
    tq = 256
    out_l = _da_attention(q, k, v, lam_p, sg, lam_init, B, S, 0, [(n_ctx, ctx_blk0), (S, 0)], tq, dh)
    if not need_ctx:
        return out_l
    out_c = _da_attention(q, k, v, lam_p, sg, lam_init, B, n_ctx, (B * S) // n_ctx, [(n_ctx, ctx_blk0)], n_ctx, dh)
    return jnp.concatenate([out_l, out_c], axis=0)


def _hy_conv_kernel(v_ref, x1_ref, x2_ref, w_ref, b_ref, vo_ref, x1o_ref, x2o_ref):
    L, W = v_ref.shape
    row = lax.broadcasted_iota(jnp.int32, (L, 1), 0)
    for j, (i_ref, o_ref) in enumerate(((v_ref, vo_ref), (x1_ref, x1o_ref), (x2_ref, x2o_ref))):
        lanes = slice(j * W, (j + 1) * W)
        u = i_ref[...]
        prev = jnp.where(row == 0, 0.0, pltpu.roll(u, 1, 0))
        nxt = jnp.where(row == L - 1, 0.0, pltpu.roll(u, L - 1, 0))
        o_ref[...] = prev * w_ref[0:1, lanes] + u * w_ref[1:2, lanes] + nxt * w_ref[2:3, lanes] + b_ref[:, lanes]


def _hy_short_conv(u, conv_w, conv_b, n_seg, L, blk0, col0):
    W = conv_w.shape[1] // 3
    cb = col0 // W
    one = lambda b: (0, 0)

    def ucol(j):
        return pl.BlockSpec((L, W), lambda b: (blk0 + b, cb + j))

    return pl.pallas_call(
        _hy_conv_kernel,
        grid=(n_seg,),
        in_specs=[ucol(0), ucol(1), ucol(2), pl.BlockSpec(conv_w.shape, one), pl.BlockSpec(conv_b.shape, one)],
        out_specs=[pl.BlockSpec((L, W), lambda b: (b, 0))] * 3,
        out_shape=[jax.ShapeDtypeStruct((n_seg * L, W), F32)] * 3,
        compiler_params=_cparams(("parallel",)),
        name="hy_short_conv",
    )(u, u, u, conv_w, conv_b)


def _hy_fwd_kernel(c_ref, s_ref, z_ref, *rest, raw):
    z = z_ref[...].astype(BF16)
    zr = _dot(c_ref[...], z)
    zi = _dot(s_ref[...], z)
    if raw:
        yr_ref, yi_ref = rest
        yr_ref[...] = zr
        yi_ref[...] = zi
    else:
        a_ref, b_ref, d_ref, yr_ref, yi_ref = rest
        yr_ref[...] = (zr * a_ref[...] - zi * b_ref[...]).astype(yr_ref.dtype)
        yi_ref[...] = (zr * b_ref[...] + zi * d_ref[...]).astype(yi_ref.dtype)


def _hy_fwd(cm, sm, z, coefs, n_seg, L, tk):
    W = z.shape[1]
    nk = L // tk
    raw = coefs is None
    mat = pl.BlockSpec((tk, L), lambda i, b: (i, 0))
    cf = pl.BlockSpec((tk, W), lambda i, b: (i, 0))
    out = pl.BlockSpec((tk, W), lambda i, b: (b * nk + i, 0))
    odt = F32 if raw else BF16
    return pl.pallas_call(
        functools.partial(_hy_fwd_kernel, raw=raw),
        grid=(nk, n_seg),
        in_specs=[mat, mat, pl.BlockSpec((L, W), lambda i, b: (b, 0))] + ([] if raw else [cf, cf, cf]),
        out_specs=[out, out],
        out_shape=[jax.ShapeDtypeStruct((n_seg * L, W), odt)] * 2,
        compiler_params=_cparams(("parallel", "arbitrary")),
        name="hy_dft_fwd",
    )(cm, sm, z, *(() if raw else coefs))


def _hy_inv_kernel(c_ref, st_ref, yr_ref, yi_ref, x_ref, vz_ref, skip_ref, o_ref):
    y = _dot(c_ref[...], yr_ref[...]) + _dot(st_ref[...], yi_ref[...])
    o_ref[...] = (x_ref[...] * (y + skip_ref[...] * vz_ref[...])).astype(o_ref.dtype)


def _hy_inv(cm, smt, yr, yi, xg, vz, skip, n_seg, L, tt, out_dtype):
    W = yr.shape[1]
    nt = L // tt
    mat = pl.BlockSpec((tt, L), lambda i, b: (i, 0))
    seq = pl.BlockSpec((L, W), lambda i, b: (b, 0))
    row = pl.BlockSpec((tt, W), lambda i, b: (b * nt + i, 0))
    return pl.pallas_call(
        _hy_inv_kernel,
        grid=(nt, n_seg),
        in_specs=[mat, mat, seq, seq, row, row, pl.BlockSpec((1, W), lambda i, b: (0, 0))],
        out_specs=row,
        out_shape=jax.ShapeDtypeStruct((n_seg * L, W), out_dtype),
        compiler_params=_cparams(("parallel", "arbitrary")),
        name="hy_dft_inv",
    )(cm, smt, yr, yi, xg, vz, skip)


def _dft_mats(L):
    split = 64
    n = jnp.arange(L, dtype=jnp.int32)

    def trig(mult):
        ang = ((mult[:, None] * n[None, :]) % (2 * L)).astype(F32) * (math.pi / L)
        return jnp.cos(ang), jnp.sin(ang)

    ca, sa = trig(split * jnp.arange(L // split, dtype=jnp.int32))
    cb, sb = trig(jnp.arange(split, dtype=jnp.int32))
    cm = (ca[:, None, :] * cb[None, :, :] - sa[:, None, :] * sb[None, :, :]).reshape(L, L)
    sm = -(sa[:, None, :] * cb[None, :, :] + ca[:, None, :] * sb[None, :, :]).reshape(L, L)
    nyq = jnp.where(n % 2 == 0, 1.0, -1.0)
    first_row = n[:, None] == 0
    first_col = n[None, :] == 0
    return (cm.astype(BF16), jnp.where(first_row, nyq[None, :], sm).astype(BF16),
            jnp.where(first_col, nyq[:, None], sm).astype(BF16))


def _hyena_filters(L, w1, b1, w2, b2, w3, freq, W):
    t01 = jnp.linspace(0.0, 1.0, L, dtype=F32)[:, None]
    wpos = (2.0 * math.pi / L) * jnp.arange(L, dtype=F32)[:, None]
    bands = jnp.linspace(1e-4, HY_BANDS - 1, HY_BANDS, dtype=F32)
    feats = jnp.concatenate([t01, jnp.cos(wpos * bands), -jnp.sin(wpos * bands)], axis=-1)
    hp = lax.Precision.HIGHEST
    h = jnp.sin(freq[0] * (jnp.dot(feats, w1, precision=hp) + b1))
    h = jnp.sin(freq[1] * (jnp.dot(h, w2, precision=hp) + b2))
    h = jnp.dot(h, w3, precision=hp).reshape(L, HY_ORDER, 2, W)
    deltas = jnp.abs(jnp.linspace(math.log(HY_TARGET) / HY_SLOW, math.log(HY_TARGET) / HY_FAST, W, dtype=F32))
    h = h * (jnp.exp(-t01 * deltas) + HY_SHIFT)[:, None, None, :]
    hf, hb = h[:, :, 0], h[:, :, 1]
    hf = hf.at[0].add(hb[0])
    hb = hb.at[0].set(0.0)
    scale = lax.rsqrt(jnp.sum(hf * hf, axis=0, keepdims=True) + jnp.sum(hb * hb, axis=0, keepdims=True) + EPS)
    return (hf * scale).reshape(L, HY_ORDER * W), (hb * scale).reshape(L, HY_ORDER * W)


def _hyena_spectrum(mats, L, w1, b1, w2, b2, w3, freq, W):
    cm, sm, _ = mats
    hf, hb = _hyena_filters(L, w1, b1, w2, b2, w3, freq, W)
    cols = jnp.concatenate([hf[:, :W], hf[:, W:], hb[:, :W], hb[:, W:]], axis=0)
    tk = min(L, 512)
    gr, gi = _hy_fwd(cm, sm, cols, None, 2 * HY_ORDER, L, tk)
    gr = gr.reshape(2, HY_ORDER, L, W)
    gi = gi.reshape(2, HY_ORDER, L, W)
    kr = gr[0] + gr[1]
    ki = gi[0] - gi[1]
    nyq = gi[0, :, 0] + gi[1, :, 0]
    n = 2.0 * L
    wk = jnp.full((L, 1), 2.0 / n, F32).at[0].set(1.0 / n)
    a = kr * wk
    bm = (ki * wk).at[:, 0].set(0.0)
    dd = a.at[:, 0].set(nyq / n)
    return [(a[o], bm[o], dd[o]) for o in range(HY_ORDER)]


def _hyena_seq(u, mats, spec, conv_w, conv_b, skip, n_seg, L, blk0, col0):
    cm, sm, smt = mats
    t = min(L, 512)
    v, x1, x2 = _hy_short_conv(u, conv_w, conv_b, n_seg, L, blk0, col0)
    yr, yi = _hy_fwd(cm, sm, v, spec[0], n_seg, L, t)
    z = _hy_inv(cm, smt, yr, yi, x1, v, skip[0:1], n_seg, L, t, F32)
    yr, yi = _hy_fwd(cm, sm, z, spec[1], n_seg, L, t)
    return _hy_inv(cm, smt, yr, yi, x2, z, skip[1:2], n_seg, L, t, BF16)


def _pack_rows(x):
    n = x.shape[1] // 2
    lo = pltpu.bitcast(x[:, :n].astype(BF16).astype(F32), jnp.uint32)
    hi = pltpu.bitcast(x[:, n:].astype(BF16).astype(F32), jnp.uint32)
    return (lo >> 16) | (hi & jnp.uint32(0xFFFF0000))


def _unpack_rows(p):
    return pltpu.bitcast(p << 16, F32), pltpu.bitcast(p & jnp.uint32(0xFFFF0000), F32)


def _moe_router_kernel(x_ref, g_ref, sc_ref, sh_ref, whi_ref, wlo_ref, rb_ref, h_ref, idx_ref, wts_ref, cnt_ref):
    x = x_ref[...]
    y = x * lax.rsqrt(jnp.mean(x * x, axis=-1, keepdims=True) + EPS) * g_ref[...]
    h = y * (1.0 + sc_ref[0]) + sh_ref[0]
    h_ref[...] = _pack_rows(h)
    hi, lo = _split2(h)
    logits = _dot_nt(whi_ref[...], hi) + _dot_nt(wlo_ref[...], hi) + _dot_nt(whi_ref[...], lo)
    scores = jax.nn.sigmoid(logits)
    sel = scores + rb_ref[...]
    E, tm = sel.shape
    gsz = E // N_GROUPS
    neg = -jnp.inf
    erow = lax.broadcasted_iota(jnp.int32, (E, 1), 0).astype(F32)
    grow = lax.broadcasted_iota(jnp.int32, (gsz, 1), 0).astype(F32)
    blocks = [sel[g * gsz:(g + 1) * gsz] for g in range(N_GROUPS)]
    gscore = []
    for blk in blocks:
        m1 = jnp.max(blk, axis=0, keepdims=True)
        first = jnp.min(jnp.where(blk == m1, grow, float(gsz)), axis=0, keepdims=True)
        m2 = jnp.max(jnp.where(grow == first, neg, blk), axis=0, keepdims=True)
        gscore.append(m1 + m2)
    kept = []
    for g in range(N_GROUPS):
        rank = jnp.zeros((1, tm), F32)
        for o in range(N_GROUPS):
            if o != g:
                ahead = (gscore[o] >= gscore[g]) if o < g else (gscore[o] > gscore[g])
                rank = rank + jnp.where(ahead, 1.0, 0.0)
        kept.append(jnp.where(rank < TOPK_GROUPS, blocks[g], neg))
    work = jnp.concatenate(kept, axis=0)
    ids, ws = [], []
    total = jnp.zeros((1, tm), F32)
    chosen = jnp.zeros((E, tm), F32)
    for j in range(TOP_K):
        mx = jnp.max(work, axis=0, keepdims=True)
        am = jnp.min(jnp.where(work == mx, erow, float(E)), axis=0, keepdims=True)
        hit = erow == am
        wj = jnp.sum(jnp.where(hit, scores, 0.0), axis=0, keepdims=True)
        work = jnp.where(hit, neg, work)
        chosen = jnp.where(hit, 1.0, chosen)
        ids.append(am)
        ws.append(wj)
        total = total + wj
    idx_ref[...] = jnp.concatenate(ids, axis=0).astype(jnp.int32)
    wts_ref[...] = jnp.concatenate(ws, axis=0) / total * ROUTED_SCALE
    ones = jnp.ones((8, tm), BF16)
    cnt_ref[0] = _dot_nt(ones, chosen.astype(BF16))[0:1]


def _moe_router(x, n_tok, g, scale, shift, whi, wlo, rb, seg_rows, tm):
    D = x.shape[1]
    E = whi.shape[0]
    K = TOP_K
    last = scale.shape[0] - 1
    row = lambda i: (i, 0)
    col = lambda i: (0, i)
    one = lambda i: (0, 0)
    mod = lambda i: (jnp.minimum(i * tm // seg_rows, last), 0, 0)
    return pl.pallas_call(
        _moe_router_kernel,
        grid=(n_tok // tm,),
        in_specs=[pl.BlockSpec((tm, D), row), pl.BlockSpec((1, D), one), pl.BlockSpec((1, 1, D), mod),
                  pl.BlockSpec((1, 1, D), mod), pl.BlockSpec((E, D), one), pl.BlockSpec((E, D), one),
                  pl.BlockSpec((E, 1), one)],
        out_specs=[pl.BlockSpec((tm, D // 2), row), pl.BlockSpec((K, tm), col), pl.BlockSpec((K, tm), col),
                   pl.BlockSpec((1, 1, E), lambda i: (i, 0, 0))],
        out_shape=[jax.ShapeDtypeStruct((n_tok, D // 2), jnp.uint32), jax.ShapeDtypeStruct((K, n_tok), jnp.int32),
                   jax.ShapeDtypeStruct((K, n_tok), F32), jax.ShapeDtypeStruct((n_tok // tm, 1, E), F32)],
        compiler_params=_cparams(("parallel",)),
        name="moe_router",
    )(x, g, scale, shift, whi, wlo, rb)


def _moe_pos_kernel(idx_ref, base_ref, pos_ref):
    K, tm = idx_ref.shape
    E = base_ref.shape[1]
    idx = idx_ref[...]
    erow = lax.broadcasted_iota(jnp.int32, (E, 1), 0)
    hits = [erow == idx[j:j + 1, :] for j in range(K)]
    onehot = jnp.zeros((E, tm), F32)
    for hit in hits:
        onehot = jnp.where(hit, 1.0, onehot)
    row = lax.broadcasted_iota(jnp.int32, (tm, tm), 0)
    col = lax.broadcasted_iota(jnp.int32, (tm, tm), 1)
    before = _dot(onehot.astype(BF16), (row < col).astype(BF16))
    dest = before + base_ref[0]
    pos = [jnp.sum(jnp.where(hit, dest, 0.0), axis=0, keepdims=True) for hit in hits]
    pos_ref[...] = jnp.concatenate(pos, axis=0).astype(jnp.int32)


def _moe_plan(idx, cnt, bm, tm):
    K, n_tok = idx.shape
    E = cnt.shape[2]
    cnt = cnt[:, 0, :]
    counts = jnp.sum(cnt, axis=0)
    pcounts = jnp.ceil(counts / bm) * bm
    pends = jnp.cumsum(pcounts)
    base = (pends - pcounts)[None, :] + jnp.cumsum(cnt, axis=0) - cnt
    pos = pl.pallas_call(
        _moe_pos_kernel,
        grid=(n_tok // tm,),
        in_specs=[pl.BlockSpec((K, tm), lambda i: (0, i)), pl.BlockSpec((1, E, 1), lambda i: (i, 0, 0))],
        out_specs=pl.BlockSpec((K, tm), lambda i: (0, i)),
        out_shape=jax.ShapeDtypeStruct((K, n_tok), jnp.int32),
        compiler_params=_cparams(("parallel",)),
        name="moe_positions",
    )(idx, base[:, :, None])
    n_blocks = -(-(n_tok * K + E * (bm - 1)) // bm)
    block_start = jnp.concatenate([jnp.zeros((1,), F32), pends / bm]).astype(jnp.int32)
    return pos.T.reshape(-1), block_start, n_blocks


def _per_token_rows(tm, K, copy):
    def start(t, carry):
        for j in range(K):
            copy(t, j).start(priority=j % 2)
        return carry

    def wait(t, carry):
        for j in range(K):
            copy(t, j).wait()
        return carry

    return (lambda: lax.fori_loop(0, tm, start, 0, unroll=2)), (lambda: lax.fori_loop(0, tm, wait, 0, unroll=2))


def _moe_dispatch_kernel(pos_ref, prev_ref, h_ref, xs_in_ref, xs_hbm, stage_ref, sems, *, K):
    del xs_in_ref
    tm = h_ref.shape[0]
    i = pl.program_id(0)
    last = pl.num_programs(0) - 1
    slot = lax.bitwise_and(i, 1)

    def tile_copies(s, p_ref):
        def row_copy(t, j):
            return pltpu.make_async_copy(stage_ref.at[s, pl.ds(t, 1)], xs_hbm.at[pl.ds(p_ref[t * K + j], 1)],
                                         sems.at[s])
        return _per_token_rows(tm, K, row_copy)

    start, wait = tile_copies(slot, pos_ref)
    _, wait_prev = tile_copies(1 - slot, prev_ref)
    stage_ref[slot] = h_ref[...]
    start()

    @pl.when(i > 0)
    def _():
        wait_prev()

    @pl.when(i == last)
    def _():
        wait()


def _moe_dispatch(pos, h, n_rows, K, tm):
    n_tok, W = h.shape
    xs0 = jnp.zeros((n_rows, W), h.dtype)
    return pl.pallas_call(
        functools.partial(_moe_dispatch_kernel, K=K),
        grid=(n_tok // tm,),
        in_specs=[pl.BlockSpec((tm * K,), lambda i: (i,), memory_space=pltpu.SMEM),
                  pl.BlockSpec((tm * K,), lambda i: (jnp.maximum(i - 1, 0),), memory_space=pltpu.SMEM),
                  pl.BlockSpec((tm, W), lambda i: (i, 0)),
                  pl.BlockSpec(memory_space=pl.ANY)],
        out_specs=pl.BlockSpec(memory_space=pl.ANY),
        out_shape=jax.ShapeDtypeStruct((n_rows, W), h.dtype),
        scratch_shapes=[pltpu.VMEM((2, tm, W), h.dtype), pltpu.SemaphoreType.DMA((2,))],
        input_output_aliases={3: 0},
        compiler_params=_cparams(("arbitrary",)),
        name="moe_dispatch",
    )(pos, pos, h, xs0)


def _moe_expert_kernel(bs_ref, x_hbm, w1_ref, w3_ref, w2_ref, y_hbm, xbuf, ybuf, w1b_ref, w3b_ref, w2b_ref,
                       xsem, ysem, *, bm, nbuf, n_blocks):
    e = pl.program_id(0)
    last = pl.num_programs(0) - 1
    b0 = bs_ref[e]
    b1 = bs_ref[e + 1]
    n_used = bs_ref[last + 1]

    def x_copy(g):
        slot = lax.bitwise_and(g, nbuf - 1)
        return pltpu.make_async_copy(x_hbm.at[pl.ds(g * bm, bm)], xbuf.at[slot], xsem.at[slot])

    def y_copy(g):
        slot = lax.bitwise_and(g, nbuf - 1)
        return pltpu.make_async_copy(ybuf.at[slot], y_hbm.at[pl.ds(g * bm, bm)], ysem.at[slot])

    @pl.when(e == 0)
    def _():
        for p in range(nbuf - 1):
            @pl.when(p < n_used)
            def _():
                x_copy(p).start()

    @pl.when(b1 > b0)
    def _():
        w1b_ref[...] = w1_ref[0, 0].astype(BF16)
        w3b_ref[...] = w3_ref[0, 0].astype(BF16)
        w2b_ref[...] = w2_ref[0, 0].astype(BF16)

    def block(g, carry):
        slot = lax.bitwise_and(g, nbuf - 1)
        x_copy(g).wait()

        @pl.when(g + nbuf - 1 < n_used)
        def _():
            x_copy(g + nbuf - 1).start()

        @pl.when(g >= nbuf)
        def _():
            y_copy(g - nbuf).wait()

        x = jnp.concatenate(_unpack_rows(xbuf[slot]), axis=1).astype(BF16)
        a = _dot(x, w1b_ref[...])
        b = _dot(x, w3b_ref[...])
        ybuf[slot] = _pack_rows(_dot((jax.nn.silu(a) * b).astype(BF16), w2b_ref[...]))
        y_copy(g).start()
        return carry

    lax.fori_loop(b0, b1, block, 0)

    @pl.when(e == last)
    def _():
        def drain(g, carry):
            y_copy(g).wait()
            return carry

        lax.fori_loop(jnp.maximum(n_used - nbuf, 0), n_used, drain, 0)
        ybuf[0] = jnp.zeros(ybuf.shape[1:], ybuf.dtype)

        def zero_copy(g):
            return pltpu.make_async_copy(ybuf.at[0], y_hbm.at[pl.ds(g * bm, bm)], ysem.at[0])

        def fill(g, carry):
            zero_copy(g).start()
            return carry

        def fill_wait(g, carry):
            zero_copy(g).wait()
            return carry

        lax.fori_loop(n_used, n_blocks, fill, 0)
        lax.fori_loop(n_used, n_blocks, fill_wait, 0)


def _moe_experts(block_start, xs, w1, w3, w2, layer, bm):
    P, W = xs.shape
    E, D, F = w1.shape[1:]
    nbuf = 4
    grid_spec = pltpu.PrefetchScalarGridSpec(
        num_scalar_prefetch=1,
        grid=(E,),
        in_specs=[pl.BlockSpec(memory_space=pl.ANY),
                  pl.BlockSpec((1, 1, D, F), lambda e, bs: (layer, e, 0, 0)),
                  pl.BlockSpec((1, 1, D, F), lambda e, bs: (layer, e, 0, 0)),
                  pl.BlockSpec((1, 1, F, D), lambda e, bs: (layer, e, 0, 0))],
        out_specs=pl.BlockSpec(memory_space=pl.ANY),
        scratch_shapes=[pltpu.VMEM((nbuf, bm, W), xs.dtype), pltpu.VMEM((nbuf, bm, W), xs.dtype),
                        pltpu.VMEM((D, F), BF16), pltpu.VMEM((D, F), BF16), pltpu.VMEM((F, D), BF16),
                        pltpu.SemaphoreType.DMA((nbuf,)), pltpu.SemaphoreType.DMA((nbuf,))],
    )
    return pl.pallas_call(
        functools.partial(_moe_expert_kernel, bm=bm, nbuf=nbuf, n_blocks=P // bm),
        grid_spec=grid_spec,
        out_shape=jax.ShapeDtypeStruct((P, W), xs.dtype),
        compiler_params=_cparams(("arbitrary",)),
        name="moe_experts",
    )(block_start, xs, w1, w3, w2)


def _moe_combine_kernel(pos_ref, next_ref, wts_ref, x_ref, h_ref, sw1_ref, sw3_ref, sw2_ref, gate_ref, ys_hbm, o_ref,
                        gath_ref, sems, *, K):
    tm = x_ref.shape[0]
    i = pl.program_id(0)
    last = pl.num_programs(0) - 1

    slot = lax.bitwise_and(i, 1)

    def tile_copies(s, p_ref):
        def row_copy(t, j):
            return pltpu.make_async_copy(ys_hbm.at[pl.ds(p_ref[t * K + j], 1)], gath_ref.at[s, j, pl.ds(t, 1)],
                                         sems.at[s])
        return _per_token_rows(tm, K, row_copy)

    start, wait = tile_copies(slot, pos_ref)
    start_next, _ = tile_copies(1 - slot, next_ref)

    @pl.when(i == 0)
    def _():
        start()

    @pl.when(i < last)
    def _():
        start_next()

    h = jnp.concatenate(_unpack_rows(h_ref[...]), axis=1).astype(BF16)
    mid = jax.nn.silu(_dot(h, sw1_ref[...])) * _dot(h, sw3_ref[...])
    shared = _dot(mid.astype(BF16), sw2_ref[...])
    wait()
    wts = wts_ref[...]
    W = gath_ref.shape[3]
    acc_lo = shared[:, :W]
    acc_hi = shared[:, W:]
    for j in range(K):
        lo, hi = _unpack_rows(gath_ref[slot, j])
        acc_lo = acc_lo + wts[:, j:j + 1] * lo
        acc_hi = acc_hi + wts[:, j:j + 1] * hi
    gate = gate_ref[0]
    o_ref[:, :W] = x_ref[:, :W] + gate[:, :W] * acc_lo
    o_ref[:, W:] = x_ref[:, W:] + gate[:, W:] * acc_hi


def _moe_combine(pos, wts, x, h, sw1, sw3, sw2, gate, ys, n_tok, K, seg_rows, tm):
    D = x.shape[1]
    W = ys.shape[1]
    n_tiles = n_tok // tm
    last = gate.shape[0] - 1
    row = lambda i: (i, 0)
    one = lambda i: (0, 0)
    return pl.pallas_call(
        functools.partial(_moe_combine_kernel, K=K),
        grid=(n_tiles,),
        in_specs=[pl.BlockSpec((tm * K,), lambda i: (i,), memory_space=pltpu.SMEM),
                  pl.BlockSpec((tm * K,), lambda i: (jnp.minimum(i + 1, n_tiles - 1),), memory_space=pltpu.SMEM),
                  pl.BlockSpec((tm, K), row), pl.BlockSpec((tm, D), row), pl.BlockSpec((tm, W), row),
                  pl.BlockSpec(sw1.shape, one), pl.BlockSpec(sw3.shape, one), pl.BlockSpec(sw2.shape, one),
                  pl.BlockSpec((1, 1, D), lambda i: (jnp.minimum(i * tm // seg_rows, last), 0, 0)),
                  pl.BlockSpec(memory_space=pl.ANY)],
        out_specs=pl.BlockSpec((tm, D), row),
        out_shape=jax.ShapeDtypeStruct((n_tok, D), F32),
        scratch_shapes=[pltpu.VMEM((2, K, tm, W), ys.dtype), pltpu.SemaphoreType.DMA((2,))],
        compiler_params=_cparams(("arbitrary",)),
        name="moe_combine",
    )(pos, pos, wts, x, h, sw1, sw3, sw2, gate, ys)


def _moe_ffn_residual(x, n_tok, norm_g, scale, shift, gate, router_w, router_b, ew1, ew3, ew2, layer, sw1, sw3, sw2,
                      seg_rows, bm):
    K = TOP_K
    tm = 256
    whi, wlo = _split2(router_w.T)
    h, idx, wts, cnt = _moe_router(x, n_tok, norm_g, scale, shift, whi, wlo, router_b[:, None], seg_rows, tm)
    pos, block_start, n_blocks = _moe_plan(idx, cnt, bm, tm)
    wts = wts.T
    xs = _moe_dispatch(pos, h, n_blocks * bm, K, tm)
    ys = _moe_experts(block_start, xs, ew1, ew3, ew2, layer, bm)
    return _moe_combine(pos, wts, x, h, sw1.astype(BF16), sw3.astype(BF16), sw2.astype(BF16), gate, ys,
                        n_tok, K, seg_rows, tm)


def kernel(x, c, ctx, c_ctx, w_mod, b_mod, norm1_g, norm2_g, w_in, mlstm_gate_b, mlstm_norm_g, da_qnorm_g, da_knorm_g, da_lambda, da_subln_g, hy_conv_w, hy_conv_b, hy_w1, hy_b1, hy_w2, hy_b2, hy_w3, hy_freq, hy_skip, w_out, router_w, router_b, exp_w1, exp_w3, exp_w2, sh_w1, sh_w3, sh_w2):
    B, S, D = x.shape
    n_ctx = ctx.shape[1]
    depth = w_in.shape[0]
    n_lat = B * S
    ml_w = mlstm_norm_g.shape[1]
    da_dh = da_qnorm_g.shape[1]
    da_w = DA_HEADS * 2 * da_dh
    hy_w = hy_skip.shape[2]
    n_gates = 4 * ML_HEADS
    da_col = 0
    hy_col = da_col + 3 * da_w
    o_col = hy_col + 3 * hy_w
    gate_col = o_col + ml_w
    ml_in = 4 * ml_w + n_gates
    tm = 512

    X = jnp.concatenate([x.reshape(n_lat, D), ctx.reshape(B * n_ctx, D)], axis=0)
    sc = jax.nn.silu(jnp.concatenate([c, c_ctx[None]], axis=0))
    cos, sin = _axial_rope_tables(S, da_dh, da_w // da_dh, tm)
    mats_l = _dft_mats(S)
    mats_c = _dft_mats(n_ctx)
    for l in range(depth):
        last = l == depth - 1
        lam_init = 0.8 - 0.6 * math.exp(-0.3 * l)
        mods = (jnp.dot(sc, w_mod[l], precision=lax.Precision.HIGHEST) + b_mod[l]).reshape(B + 1, 6, 1, D)
        sh1, s1, g1, sh2, s2, g2 = [mods[:, i] for i in range(6)]
        wl = w_in[l]
        w_big = jnp.concatenate([wl[:, ml_in:], wl[:, 3 * ml_w:4 * ml_w], wl[:, 4 * ml_w:ml_in],
                                 jnp.zeros((D, LANES - n_gates), F32)], axis=1).astype(BF16)
        w_qkv_t = wl[:, :3 * ml_w].T.astype(BF16)
        U, UT = _norm_mod_matmul(X, norm1_g[l][None], s1, sh1, w_big, w_qkv_t, S, tm, w_big.shape[1] // 3)
        gb = jnp.concatenate([mlstm_gate_b[l], jnp.zeros((LANES - n_gates,), F32)])[None]
        m_out = _mlstm_mixer(UT, U, gb, mlstm_norm_g[l][:, None], B, S, n_ctx, o_col, gate_col)
        d_out = _diff_attn_mixer(U, cos, sin, da_qnorm_g[l], da_knorm_g[l], da_lambda[l], da_subln_g[l], lam_init,
                                 B, S, n_ctx, da_col, not last)
        hy_args = (hy_w1[l], hy_b1[l], hy_w2[l], hy_b2[l], hy_w3[l], hy_freq[l], hy_w)
        y_out = _hyena_seq(U, mats_l, _hyena_spectrum(mats_l, S, *hy_args), hy_conv_w[l], hy_conv_b[l][None],
                           hy_skip[l], B, S, 0, hy_col)
        n_rows = n_lat
        if not last:
            y_ctx = _hyena_seq(U, mats_c, _hyena_spectrum(mats_c, n_ctx, *hy_args), hy_conv_w[l], hy_conv_b[l][None],
                               hy_skip[l], B, n_ctx, n_lat // n_ctx, hy_col)
            y_out = jnp.concatenate([y_out, y_ctx], axis=0)
            n_rows = n_lat + B * n_ctx
        wo = w_out[l].astype(BF16)
        X = _out_proj_residual(m_out, d_out, y_out, wo[:ml_w], wo[ml_w:ml_w + da_w], wo[ml_w + da_w:], X, g1,
                               n_rows, S, tm)
        X = _moe_ffn_residual(X, n_rows, norm2_g[l][None], s2, sh2, g2, router_w[l], router_b[l],
                              exp_w1, exp_w3, exp_w2, l, sh_w1[l], sh_w3[l], sh_w2[l], S,
                              MOE_BLOCK)
    return X[:n_lat].reshape(B, S, D)
```

```python
import functools
import math

import jax
import jax.numpy as jnp
from jax import lax
from jax.experimental import pallas as pl
from jax.experimental.pallas import tpu as pltpu

F32 = jnp.float32
BF16 = jnp.bfloat16

EPS = 1e-6
GRID_W = 64
ROPE_THETA = 10000.0
ML_HEADS = 4
ML_CHUNK = 256
ML_M_INIT = -1e30
DA_HEADS = 4
HY_ORDER = 2
HY_BANDS = 8
HY_SHIFT = 0.05
HY_TARGET = 1e-2
HY_FAST = 0.3
HY_SLOW = 1.5
N_GROUPS = 8
TOPK_GROUPS = 4
TOP_K = 8
ROUTED_SCALE = 2.5
MOE_BLOCK = 256
LANES = 128
VMEM_LIMIT = 56 * 1024 * 1024


def _cparams(sem):
    return pltpu.CompilerParams(dimension_semantics=sem, vmem_limit_bytes=VMEM_LIMIT)


def _dot(a, b):
    return jnp.dot(a, b, preferred_element_type=F32)


def _dot_nt(a, b):
    return lax.dot_general(a, b, (((1,), (1,)), ((), ())), preferred_element_type=F32)


def _dot_tn(a, b):
    return lax.dot_general(a, b, (((0,), (0,)), ((), ())), preferred_element_type=F32)


def _split3(a):
    hi = a.astype(BF16)
    r = a - hi.astype(F32)
    mid = r.astype(BF16)
    lo = (r - mid.astype(F32)).astype(BF16)
    return hi, mid, lo


def _split2(a):
    hi = a.astype(BF16)
    lo = (a - hi.astype(F32)).astype(BF16)
    return hi, lo


def _norm_mod_mm_kernel(x_ref, g_ref, sc_ref, sh_ref, w_ref, wt_ref, o_ref, ot_ref, xn_ref):
    @pl.when(pl.program_id(1) == 0)
    def _():
        x = x_ref[...]
        y = x * lax.rsqrt(jnp.mean(x * x, axis=-1, keepdims=True) + EPS) * g_ref[...]
        xn_ref[...] = (y * (1.0 + sc_ref[0]) + sh_ref[0]).astype(BF16)
        ot_ref[...] = _dot_nt(wt_ref[...], xn_ref[...])

    o_ref[...] = _dot(xn_ref[...], w_ref[...])


def _norm_mod_matmul(x, g, scale, shift, w, wt, seg_rows, tm, tn):
    R, D = x.shape
    N = w.shape[1]
    NT = wt.shape[0]
    last = scale.shape[0] - 1
    mod_map = lambda i, j: (jnp.minimum(i * tm // seg_rows, last), 0, 0)
    return pl.pallas_call(
        _norm_mod_mm_kernel,
        grid=(R // tm, N // tn),
        in_specs=[
            pl.BlockSpec((tm, D), lambda i, j: (i, 0)),
            pl.BlockSpec((1, D), lambda i, j: (0, 0)),
            pl.BlockSpec((1, 1, D), mod_map),
            pl.BlockSpec((1, 1, D), mod_map),
            pl.BlockSpec((D, tn), lambda i, j: (0, j)),
            pl.BlockSpec((NT, D), lambda i, j: (0, 0)),
        ],
        out_specs=[pl.BlockSpec((tm, tn), lambda i, j: (i, j)), pl.BlockSpec((NT, tm), lambda i, j: (0, i))],
        out_shape=[jax.ShapeDtypeStruct((R, N), F32), jax.ShapeDtypeStruct((NT, R), F32)],
        scratch_shapes=[pltpu.VMEM((tm, D), BF16)],
        compiler_params=_cparams(("parallel", "arbitrary")),
        name="norm_mod_matmul",
    )(x, g, scale, shift, w, wt)


def _out_proj_kernel(m_ref, d_ref, y_ref, wm_ref, wd_ref, wy_ref, x_ref, gate_ref, o_ref):
    acc = _dot(m_ref[...], wm_ref[...]) + _dot(d_ref[...], wd_ref[...]) + _dot(y_ref[...], wy_ref[...])
    o_ref[...] = x_ref[...] + gate_ref[0] * acc


def _out_proj_residual(m, d, y, wm, wd, wy, x, gate, n_rows, seg_rows, tm):
    R, D = n_rows, x.shape[1]
    last = gate.shape[0] - 1
    row = lambda i: (i, 0)
    full = lambda i: (0, 0)
    return pl.pallas_call(
        _out_proj_kernel,
        grid=(R // tm,),
        in_specs=[
            pl.BlockSpec((tm, m.shape[1]), row),
            pl.BlockSpec((tm, d.shape[1]), row),
            pl.BlockSpec((tm, y.shape[1]), row),
            pl.BlockSpec(wm.shape, full),
            pl.BlockSpec(wd.shape, full),
            pl.BlockSpec(wy.shape, full),
            pl.BlockSpec((tm, D), row),
            pl.BlockSpec((1, 1, D), lambda i: (jnp.minimum(i * tm // seg_rows, last), 0, 0)),
        ],
        out_specs=pl.BlockSpec((tm, D), row),
        out_shape=jax.ShapeDtypeStruct((R, D), F32),
        compiler_params=_cparams(("parallel",)),
        name="out_proj_residual",
    )(m, d, y, wm, wd, wy, x, gate)


def _log_sigmoid(x):
    return jnp.minimum(x, 0.0) - jnp.log1p(jnp.exp(-jnp.abs(x)))


def _mlstm_gate_tables(g_ref, r0, L, gb, tril, triu):
    g = g_ref[pl.ds(r0, L), :] + gb
    lf = _log_sigmoid(g)
    gT = g.T
    lfT = lf.T
    parts = _split3(lf)
    partsT = _split3(lfT)
    cs_f = sum(_dot(tril, p) for p in parts)
    cs_b = sum(_dot(triu, p) for p in parts)
    rs_f = sum(_dot(p, triu) for p in partsT)
    rs_b = sum(_dot(p, tril) for p in partsT)
    return g, gT, cs_f, cs_b, rs_f, rs_b


def _mlstm_chunk(qT, kT, vT, i_row, b_row, c_col, b_end, mask, state):
    C, n, m = state
    qb = qT.astype(BF16)
    kb = kT.astype(BF16)
    dmat = jnp.where(mask, b_row + c_col, -jnp.inf)
    inter = b_row + m
    m_t = jnp.maximum(inter, jnp.max(dmat, axis=0, keepdims=True))
    s = _dot_tn(kb, qb) * jnp.exp(dmat - m_t)
    carry_w = jnp.exp(inter - m_t)
    num = _dot(vT.astype(BF16), s.astype(BF16)) + carry_w * _dot(C.astype(BF16), qb)
    den = jnp.sum(s, axis=0, keepdims=True) + carry_w * jnp.sum(qT * n, axis=0, keepdims=True)
    h = num / jnp.maximum(jnp.abs(den), jnp.exp(-m_t))
    g = b_end - b_row + i_row
    m_new = jnp.maximum(b_end + m, jnp.max(g, axis=-1, keepdims=True))
    ws = jnp.exp(g - m_new)
    decay = jnp.exp(b_end + m - m_new)
    C_new = decay * C + _dot_nt((vT * ws).astype(BF16), kb)
    n_new = decay * n + jnp.sum(kT * ws, axis=-1, keepdims=True)
    return h, (C_new, n_new, m_new)


def _mlstm_kernel(ql_ref, kl_ref, vl_ref, ol_ref, gl_ref, qc_ref, kc_ref, vc_ref, oc_ref, gc_ref,
                  gb_ref, ng_ref, outl_ref, outc_ref, hf_ref, hb_ref, *, L, H, dh):
    S = ql_ref.shape[1]
    n_ctx = qc_ref.shape[1]
    row = lax.broadcasted_iota(jnp.int32, (L, L), 0)
    col = lax.broadcasted_iota(jnp.int32, (L, L), 1)
    lower = col <= row
    upper = col >= row
    tril = lower.astype(BF16)
    triu = upper.astype(BF16)
    gb = gb_ref[...]
    k_scale = dh ** -0.5

    def both_dirs(refs_f, r0_f, refs_b, r0_b, hoff, state):
        new_state = []
        for d, (refs, r0, h_ref) in enumerate(((refs_f, r0_f, hf_ref), (refs_b, r0_b, hb_ref))):
            q_ref, k_ref, v_ref, g_ref = refs
            g, gT, cs_f, cs_b, rs_f, rs_b = _mlstm_gate_tables(g_ref, r0, L, gb, tril, triu)
            cs, rs, mask = (cs_f, rs_f, upper) if d == 0 else (cs_b, rs_b, lower)
            end = L - 1 if d == 0 else 0
            for hh in range(H):
                ic = 2 * d * H + hh
                fc = ic + H
                rows = slice(hh * dh, (hh + 1) * dh)
                qT = q_ref[rows, pl.ds(r0, L)]
                kT = k_ref[rows, pl.ds(r0, L)] * k_scale
                vT = v_ref[rows, pl.ds(r0, L)]
                h, st = _mlstm_chunk(qT, kT, vT, gT[ic:ic + 1, :], rs[fc:fc + 1, :],
                                     g[:, ic:ic + 1] - cs[:, fc:fc + 1], rs[fc:fc + 1, end:end + 1], mask,
                                     state[d * H + hh])
                off = hoff + r0
                h_ref[rows, pl.ds(off if isinstance(off, int) else pl.multiple_of(off, L), L)] = h
                new_state.append(st)
        return tuple(new_state)

    state = tuple((jnp.zeros((dh, dh), F32), jnp.zeros((dh, 1), F32), jnp.full((1, 1), ML_M_INIT, F32))
                  for _ in range(2 * H))
    ctx_refs = (qc_ref, kc_ref, vc_ref, gc_ref)
    lat_refs = (ql_ref, kl_ref, vl_ref, gl_ref)
    n_cc = n_ctx // L
    for c in range(n_cc):
        state = both_dirs(ctx_refs, c * L, ctx_refs, (n_cc - 1 - c) * L, 0, state)
    n_lc = S // L

    def body(c, st):
        r_f = pl.multiple_of(c * L, L)
        r_b = pl.multiple_of((n_lc - 1 - c) * L, L)
        return both_dirs(lat_refs, r_f, lat_refs, r_b, n_ctx, st)

    lax.fori_loop(0, n_lc, body, state)

    def finish(o_ref, out_ref, hoff, n_rows):
        def fbody(c, carry):
            r0 = pl.multiple_of(c * L, L)
            off = pl.multiple_of(hoff + r0, L)
            hs = hf_ref[:, pl.ds(off, L)] + hb_ref[:, pl.ds(off, L)]
            normed = []
            for hh in range(H):
                rows = slice(hh * dh, (hh + 1) * dh)
                hv = hs[rows]
                normed.append(hv * lax.rsqrt(jnp.mean(hv * hv, axis=0, keepdims=True) + EPS) * ng_ref[rows])
            hn = jnp.concatenate(normed, axis=0).T
            out_ref[pl.ds(r0, L), :] = (jax.nn.sigmoid(o_ref[pl.ds(r0, L), :]) * hn).astype(out_ref.dtype)
            return carry
        lax.fori_loop(0, n_rows // L, fbody, 0)

    finish(ol_ref, outl_ref, n_ctx, S)
    finish(oc_ref, outc_ref, 0, n_ctx)


def _mlstm_mixer(uT, u, gate_b, norm_g, B, S, n_ctx, o_col, gate_col):
    W = norm_g.shape[0]
    H = ML_HEADS
    dh = W // H
    ocb = o_col // W
    gcb = gate_col // LANES
    cblk = (B * S) // n_ctx

    def lat_t(j):
        return pl.BlockSpec((W, S), lambda b: (j, b))

    def ctx_t(j):
        return pl.BlockSpec((W, n_ctx), lambda b: (j, cblk + b))

    one = lambda b: (0, 0)
    out_l, out_c = pl.pallas_call(
        functools.partial(_mlstm_kernel, L=ML_CHUNK, H=H, dh=dh),
        grid=(B,),
        in_specs=[lat_t(0), lat_t(1), lat_t(2), pl.BlockSpec((S, W), lambda b: (b, ocb)),
                  pl.BlockSpec((S, LANES), lambda b: (b, gcb)),
                  ctx_t(0), ctx_t(1), ctx_t(2), pl.BlockSpec((n_ctx, W), lambda b: (cblk + b, ocb)),
                  pl.BlockSpec((n_ctx, LANES), lambda b: (cblk + b, gcb)),
                  pl.BlockSpec((1, LANES), one), pl.BlockSpec((W, 1), one)],
        out_specs=[pl.BlockSpec((S, W), lambda b: (b, 0)), pl.BlockSpec((n_ctx, W), lambda b: (b, 0))],
        out_shape=[jax.ShapeDtypeStruct((B * S, W), BF16), jax.ShapeDtypeStruct((B * n_ctx, W), BF16)],
        scratch_shapes=[pltpu.VMEM((W, n_ctx + S), F32), pltpu.VMEM((W, n_ctx + S), F32)],
        compiler_params=_cparams(("parallel",)),
        name="mlstm",
    )(uT, uT, uT, u, u, uT, uT, uT, u, u, gate_b, norm_g)
    return jnp.concatenate([out_l, out_c], axis=0)


def _da_prep_kernel(q_ref, k_ref, v_ref, cos_ref, sin_ref, qg_ref, kg_ref, seg_ref, qo_ref, ko_ref, vo_ref, *, dh):
    cos = cos_ref[...]
    sin = sin_ref[...]
    seg = seg_ref[...]
    W = q_ref.shape[1]
    lane = lax.broadcasted_iota(jnp.int32, (1, W), 1)
    quarter = dh // 4
    first = (lane % (2 * quarter)) < quarter

    def norm_rope(x, g):
        hi, lo = _split2(x * x)
        ms = (_dot(hi, seg) + _dot(lo, seg)) * (1.0 / dh)
        xn = x * lax.rsqrt(ms + EPS) * g
        rot = jnp.where(first, -pltpu.roll(xn, W - quarter, 1), pltpu.roll(xn, quarter, 1))
        return xn * cos + rot * sin

    qo_ref[...] = (norm_rope(q_ref[...], qg_ref[...]) * (dh ** -0.5 * math.log2(math.e))).astype(BF16)
    ko_ref[...] = norm_rope(k_ref[...], kg_ref[...]).astype(BF16)
    vo_ref[...] = v_ref[...].astype(BF16)


def _da_prep(u, cos, sin, qg, kg, seg, n_lat_rows, S, col0, tm, dh):
    R = u.shape[0]
    W = qg.shape[1]
    cb = col0 // W
    n_lat = n_lat_rows // tm
    per_seq = S // tm
    tab = lambda i: (jnp.where(i < n_lat, i % per_seq, per_seq), 0)
    one = lambda i: (0, 0)
    row = lambda i: (i, 0)

    def ucol(j):
        return pl.BlockSpec((tm, W), lambda i: (i, cb + j))

    return pl.pallas_call(
        functools.partial(_da_prep_kernel, dh=dh),
        grid=(R // tm,),
        in_specs=[ucol(0), ucol(1), ucol(2), pl.BlockSpec((tm, W), tab), pl.BlockSpec((tm, W), tab),
                  pl.BlockSpec((1, W), one), pl.BlockSpec((1, W), one), pl.BlockSpec((W, W), one)],
        out_specs=[pl.BlockSpec((tm, W), row)] * 3,
        out_shape=[jax.ShapeDtypeStruct((R, W), BF16)] * 3,
        compiler_params=_cparams(("parallel",)),
        name="da_prep",
    )(u, u, u, cos, sin, qg, kg, seg)


def _da_attn_kernel(*refs, n_kv, dh, lam_init):
    q_ref = refs[0]
    k_refs = refs[1:1 + n_kv]
    v_refs = refs[1 + n_kv:1 + 2 * n_kv]
    lam_ref, sg_ref, o_ref = refs[1 + 2 * n_kv:]
    lp = lam_ref[...]
    lam = (jnp.exp(jnp.sum(lp[0:1] * lp[1:2], axis=-1, keepdims=True))
           - jnp.exp(jnp.sum(lp[2:3] * lp[3:4], axis=-1, keepdims=True)) + lam_init)
    vd = 2 * dh
    for hh in range(q_ref.shape[1] // vd):
        head = slice(hh * vd, (hh + 1) * vd)
        acc = None
        nums, dens = [], []
        for mp in range(2):
            lanes = slice(hh * vd + mp * dh, hh * vd + (mp + 1) * dh)
            s = [_dot_nt(q_ref[:, lanes], k_ref[:, lanes]) for k_ref in k_refs]
            mx = functools.reduce(jnp.maximum, [jnp.max(si, axis=-1, keepdims=True) for si in s])
            p = [jnp.exp2(si - mx) for si in s]
            nums.append(p)
            dens.append(sum(jnp.sum(pi, axis=-1, keepdims=True) for pi in p))
        c = lam * dens[0] / dens[1]
        for j in range(n_kv):
            a = (nums[0][j] - c * nums[1][j]).astype(BF16)
            t = _dot(a, v_refs[j][:, head])
            acc = t if acc is None else acc + t
        acc = acc / dens[0]
        o = acc * lax.rsqrt(jnp.mean(acc * acc, axis=-1, keepdims=True) + EPS) * sg_ref[...]
        o_ref[:, head] = (o * (1.0 - lam_init)).astype(o_ref.dtype)


def _da_attention(q, k, v, lam_p, subln_g, lam_init, B, q_rows, q_blk0, kv_segs, tq, dh):
    H = DA_HEADS
    vd = 2 * dh
    hp = 2
    nq = q_rows // tq
    q0 = q_blk0

    def kv_spec(rows, blk0):
        return pl.BlockSpec((rows, hp * vd), lambda b, h, i: (blk0 + b, h))

    kspecs = [kv_spec(r, b0) for r, b0 in kv_segs]
    one = lambda b, h, i: (0, 0)
    return pl.pallas_call(
        functools.partial(_da_attn_kernel, n_kv=len(kv_segs), dh=dh, lam_init=lam_init),
        grid=(B, H // hp, nq),
        in_specs=[pl.BlockSpec((tq, hp * vd), lambda b, h, i: (q0 + b * nq + i, h))] + kspecs + kspecs
                 + [pl.BlockSpec(lam_p.shape, one), pl.BlockSpec((1, vd), one)],
        out_specs=pl.BlockSpec((tq, hp * vd), lambda b, h, i: (b * nq + i, h)),
        out_shape=jax.ShapeDtypeStruct((B * q_rows, H * vd), BF16),
        compiler_params=_cparams(("parallel", "parallel", "arbitrary")),
        name="da_attention",
    )(q, *([k] * len(kv_segs)), *([v] * len(kv_segs)), lam_p, subln_g)


def _axial_rope_tables(S, dh, reps, pad_rows):
    rows = S // GRID_W
    r = jnp.repeat(jnp.arange(rows, dtype=F32), GRID_W)
    col = jnp.tile(jnp.arange(GRID_W, dtype=F32), rows)
    n_freq = dh // 4
    inv = ROPE_THETA ** (-jnp.arange(n_freq, dtype=F32) / n_freq)
    ar = r[:, None] * inv
    ac = col[:, None] * inv
    ang = jnp.concatenate([ar, ar, ac, ac], axis=-1)
    cos = jnp.concatenate([jnp.tile(jnp.cos(ang), (1, reps)), jnp.ones((pad_rows, dh * reps), F32)], axis=0)
    sin = jnp.concatenate([jnp.tile(jnp.sin(ang), (1, reps)), jnp.zeros((pad_rows, dh * reps), F32)], axis=0)
    return cos, sin


def _diff_attn_mixer(u, cos, sin, qg, kg, lam_p, subln_g, lam_init, B, S, n_ctx, col0, need_ctx):
    dh = qg.shape[0]
    W = DA_HEADS * 2 * dh
    seg = (jnp.arange(W)[:, None] // dh == jnp.arange(W)[None, :] // dh).astype(BF16)
    tm = 512
    q, k, v = _da_prep(u, cos, sin, jnp.tile(qg, W // dh)[None], jnp.tile(kg, W // dh)[None], seg,
                       B * S, S, col0, tm, dh)
    sg = subln_g[None]
    ctx_blk0 = (B * S) // n_ctx
    tq = 256
    out_l = _da_attention(q, k, v, lam_p, sg, lam_init, B, S, 0, [(n_ctx, ctx_blk0), (S, 0)], tq, dh)
    if not need_ctx:
        return out_l
    out_c = _da_attention(q, k, v, lam_p, sg, lam_init, B, n_ctx, (B * S) // n_ctx, [(n_ctx, ctx_blk0)], n_ctx, dh)
    return jnp.concatenate([out_l, out_c], axis=0)


def _hy_conv_kernel(v_ref, x1_ref, x2_ref, w_ref, b_ref, vo_ref, x1o_ref, x2o_ref):
    L, W = v_ref.shape
    row = lax.broadcasted_iota(jnp.int32, (L, 1), 0)
    for j, (i_ref, o_ref) in enumerate(((v_ref, vo_ref), (x1_ref, x1o_ref), (x2_ref, x2o_ref))):
        lanes = slice(j * W, (j + 1) * W)
        u = i_ref[...]
        prev = jnp.where(row == 0, 0.0, pltpu.roll(u, 1, 0))
        nxt = jnp.where(row == L - 1, 0.0, pltpu.roll(u, L - 1, 0))
        o_ref[...] = prev * w_ref[0:1, lanes] + u * w_ref[1:2, lanes] + nxt * w_ref[2:3, lanes] + b_ref[:, lanes]


def _hy_short_conv(u, conv_w, conv_b, n_seg, L, blk0, col0):
    W = conv_w.shape[1] // 3
    cb = col0 // W
    one = lambda b: (0, 0)

    def ucol(j):
        return pl.BlockSpec((L, W), lambda b: (blk0 + b, cb + j))

    return pl.pallas_call(
        _hy_conv_kernel,
        grid=(n_seg,),
        in_specs=[ucol(0), ucol(1), ucol(2), pl.BlockSpec(conv_w.shape, one), pl.BlockSpec(conv_b.shape, one)],
        out_specs=[pl.BlockSpec((L, W), lambda b: (b, 0))] * 3,
        out_shape=[jax.ShapeDtypeStruct((n_seg * L, W), F32)] * 3,
        compiler_params=_cparams(("parallel",)),
        name="hy_short_conv",
    )(u, u, u, conv_w, conv_b)


def _hy_fwd_kernel(c_ref, s_ref, z_ref, *rest, raw):
    z = z_ref[...].astype(BF16)
    zr = _dot(c_ref[...], z)
    zi = _dot(s_ref[...], z)
    if raw:
        yr_ref, yi_ref = rest
        yr_ref[...] = zr
        yi_ref[...] = zi
    else:
        a_ref, b_ref, d_ref, yr_ref, yi_ref = rest
        yr_ref[...] = (zr * a_ref[...] - zi * b_ref[...]).astype(yr_ref.dtype)
        yi_ref[...] = (zr * b_ref[...] + zi * d_ref[...]).astype(yi_ref.dtype)


def _hy_fwd(cm, sm, z, coefs, n_seg, L, tk):
    W = z.shape[1]
    nk = L // tk
    raw = coefs is None
    mat = pl.BlockSpec((tk, L), lambda i, b: (i, 0))
    cf = pl.BlockSpec((tk, W), lambda i, b: (i, 0))
    out = pl.BlockSpec((tk, W), lambda i, b: (b * nk + i, 0))
    odt = F32 if raw else BF16
    return pl.pallas_call(
        functools.partial(_hy_fwd_kernel, raw=raw),
        grid=(nk, n_seg),
        in_specs=[mat, mat, pl.BlockSpec((L, W), lambda i, b: (b, 0))] + ([] if raw else [cf, cf, cf]),
        out_specs=[out, out],
        out_shape=[jax.ShapeDtypeStruct((n_seg * L, W), odt)] * 2,
        compiler_params=_cparams(("parallel", "arbitrary")),
        name="hy_dft_fwd",
    )(cm, sm, z, *(() if raw else coefs))


def _hy_inv_kernel(c_ref, st_ref, yr_ref, yi_ref, x_ref, vz_ref, skip_ref, o_ref):
    y = _dot(c_ref[...], yr_ref[...]) + _dot(st_ref[...], yi_ref[...])
    o_ref[...] = (x_ref[...] * (y + skip_ref[...] * vz_ref[...])).astype(o_ref.dtype)


def _hy_inv(cm, smt, yr, yi, xg, vz, skip, n_seg, L, tt, out_dtype):
    W = yr.shape[1]
    nt = L // tt
    mat = pl.BlockSpec((tt, L), lambda i, b: (i, 0))
    seq = pl.BlockSpec((L, W), lambda i, b: (b, 0))
    row = pl.BlockSpec((tt, W), lambda i, b: (b * nt + i, 0))
    return pl.pallas_call(
        _hy_inv_kernel,
        grid=(nt, n_seg),
        in_specs=[mat, mat, seq, seq, row, row, pl.BlockSpec((1, W), lambda i, b: (0, 0))],
        out_specs=row,
        out_shape=jax.ShapeDtypeStruct((n_seg * L, W), out_dtype),
        compiler_params=_cparams(("parallel", "arbitrary")),
        name="hy_dft_inv",
    )(cm, smt, yr, yi, xg, vz, skip)


def _dft_mats(L):
    split = 64
    n = jnp.arange(L, dtype=jnp.int32)

    def trig(mult):
        ang = ((mult[:, None] * n[None, :]) % (2 * L)).astype(F32) * (math.pi / L)
        return jnp.cos(ang), jnp.sin(ang)

    ca, sa = trig(split * jnp.arange(L // split, dtype=jnp.int32))
    cb, sb = trig(jnp.arange(split, dtype=jnp.int32))
    cm = (ca[:, None, :] * cb[None, :, :] - sa[:, None, :] * sb[None, :, :]).reshape(L, L)
    sm = -(sa[:, None, :] * cb[None, :, :] + ca[:, None, :] * sb[None, :, :]).reshape(L, L)
    nyq = jnp.where(n % 2 == 0, 1.0, -1.0)
    first_row = n[:, None] == 0
    first_col = n[None, :] == 0
    return (cm.astype(BF16), jnp.where(first_row, nyq[None, :], sm).astype(BF16),
            jnp.where(first_col, nyq[:, None], sm).astype(BF16))


def _hyena_filters(L, w1, b1, w2, b2, w3, freq, W):
    t01 = jnp.linspace(0.0, 1.0, L, dtype=F32)[:, None]
    wpos = (2.0 * math.pi / L) * jnp.arange(L, dtype=F32)[:, None]
    bands = jnp.linspace(1e-4, HY_BANDS - 1, HY_BANDS, dtype=F32)
    feats = jnp.concatenate([t01, jnp.cos(wpos * bands), -jnp.sin(wpos * bands)], axis=-1)
    hp = lax.Precision.HIGHEST
    h = jnp.sin(freq[0] * (jnp.dot(feats, w1, precision=hp) + b1))
    h = jnp.sin(freq[1] * (jnp.dot(h, w2, precision=hp) + b2))
    h = jnp.dot(h, w3, precision=hp).reshape(L, HY_ORDER, 2, W)
    deltas = jnp.abs(jnp.linspace(math.log(HY_TARGET) / HY_SLOW, math.log(HY_TARGET) / HY_FAST, W, dtype=F32))
    h = h * (jnp.exp(-t01 * deltas) + HY_SHIFT)[:, None, None, :]
    hf, hb = h[:, :, 0], h[:, :, 1]
    hf = hf.at[0].add(hb[0])
    hb = hb.at[0].set(0.0)
    scale = lax.rsqrt(jnp.sum(hf * hf, axis=0, keepdims=True) + jnp.sum(hb * hb, axis=0, keepdims=True) + EPS)
    return (hf * scale).reshape(L, HY_ORDER * W), (hb * scale).reshape(L, HY_ORDER * W)


def _hyena_spectrum(mats, L, w1, b1, w2, b2, w3, freq, W):
    cm, sm, _ = mats
    hf, hb = _hyena_filters(L, w1, b1, w2, b2, w3, freq, W)
    cols = jnp.concatenate([hf[:, :W], hf[:, W:], hb[:, :W], hb[:, W:]], axis=0)
    tk = min(L, 512)
    gr, gi = _hy_fwd(cm, sm, cols, None, 2 * HY_ORDER, L, tk)
    gr = gr.reshape(2, HY_ORDER, L, W)
    gi = gi.reshape(2, HY_ORDER, L, W)
    kr = gr[0] + gr[1]
    ki = gi[0] - gi[1]
    nyq = gi[0, :, 0] + gi[1, :, 0]
    n = 2.0 * L
    wk = jnp.full((L, 1), 2.0 / n, F32).at[0].set(1.0 / n)
    a = kr * wk
    bm = (ki * wk).at[:, 0].set(0.0)
    dd = a.at[:, 0].set(nyq / n)
    return [(a[o], bm[o], dd[o]) for o in range(HY_ORDER)]


def _hyena_seq(u, mats, spec, conv_w, conv_b, skip, n_seg, L, blk0, col0):
    cm, sm, smt = mats
    t = min(L, 1024)
    v, x1, x2 = _hy_short_conv(u, conv_w, conv_b, n_seg, L, blk0, col0)
    yr, yi = _hy_fwd(cm, sm, v, spec[0], n_seg, L, t)
    z = _hy_inv(cm, smt, yr, yi, x1, v, skip[0:1], n_seg, L, t, F32)
    yr, yi = _hy_fwd(cm, sm, z, spec[1], n_seg, L, t)
    return _hy_inv(cm, smt, yr, yi, x2, z, skip[1:2], n_seg, L, t, BF16)


def _pack_rows(x):
    n = x.shape[1] // 2
    lo = pltpu.bitcast(x[:, :n].astype(BF16).astype(F32), jnp.uint32)
    hi = pltpu.bitcast(x[:, n:].astype(BF16).astype(F32), jnp.uint32)
    return (lo >> 16) | (hi & jnp.uint32(0xFFFF0000))


def _unpack_rows(p):
    return pltpu.bitcast(p << 16, F32), pltpu.bitcast(p & jnp.uint32(0xFFFF0000), F32)


def _moe_router_kernel(x_ref, g_ref, sc_ref, sh_ref, whi_ref, wlo_ref, rb_ref, h_ref, idx_ref, wts_ref, cnt_ref):
    x = x_ref[...]
    y = x * lax.rsqrt(jnp.mean(x * x, axis=-1, keepdims=True) + EPS) * g_ref[...]
    h = y * (1.0 + sc_ref[0]) + sh_ref[0]
    h_ref[...] = _pack_rows(h)
    hi, lo = _split2(h)
    logits = _dot_nt(whi_ref[...], hi) + _dot_nt(wlo_ref[...], hi) + _dot_nt(whi_ref[...], lo)
    scores = jax.nn.sigmoid(logits)
    sel = scores + rb_ref[...]
    E, tm = sel.shape
    gsz = E // N_GROUPS
    neg = -jnp.inf
    erow = lax.broadcasted_iota(jnp.int32, (E, 1), 0).astype(F32)
    grow = lax.broadcasted_iota(jnp.int32, (gsz, 1), 0).astype(F32)
    blocks = [sel[g * gsz:(g + 1) * gsz] for g in range(N_GROUPS)]
    gscore = []
    for blk in blocks:
        m1 = jnp.max(blk, axis=0, keepdims=True)
        first = jnp.min(jnp.where(blk == m1, grow, float(gsz)), axis=0, keepdims=True)
        m2 = jnp.max(jnp.where(grow == first, neg, blk), axis=0, keepdims=True)
        gscore.append(m1 + m2)
    kept = []
    for g in range(N_GROUPS):
        rank = jnp.zeros((1, tm), F32)
        for o in range(N_GROUPS):
            if o != g:
                ahead = (gscore[o] >= gscore[g]) if o < g else (gscore[o] > gscore[g])
                rank = rank + jnp.where(ahead, 1.0, 0.0)
        kept.append(jnp.where(rank < TOPK_GROUPS, blocks[g], neg))
    work = jnp.concatenate(kept, axis=0)
    ids, ws = [], []
    total = jnp.zeros((1, tm), F32)
    chosen = jnp.zeros((E, tm), F32)
    for j in range(TOP_K):
        mx = jnp.max(work, axis=0, keepdims=True)
        am = jnp.min(jnp.where(work == mx, erow, float(E)), axis=0, keepdims=True)
        hit = erow == am
        wj = jnp.sum(jnp.where(hit, scores, 0.0), axis=0, keepdims=True)
        work = jnp.where(hit, neg, work)
        chosen = jnp.where(hit, 1.0, chosen)
        ids.append(am)
        ws.append(wj)
        total = total + wj
    idx_ref[...] = jnp.concatenate(ids, axis=0).astype(jnp.int32)
    wts_ref[...] = jnp.concatenate(ws, axis=0) / total * ROUTED_SCALE
    ones = jnp.ones((8, tm), BF16)
    cnt_ref[0] = _dot_nt(ones, chosen.astype(BF16))[0:1]


def _moe_router(x, n_tok, g, scale, shift, whi, wlo, rb, seg_rows, tm):
    D = x.shape[1]
    E = whi.shape[0]
    K = TOP_K
    last = scale.shape[0] - 1
    row = lambda i: (i, 0)
    col = lambda i: (0, i)
    one = lambda i: (0, 0)
    mod = lambda i: (jnp.minimum(i * tm // seg_rows, last), 0, 0)
    return pl.pallas_call(
        _moe_router_kernel,
        grid=(n_tok // tm,),
        in_specs=[pl.BlockSpec((tm, D), row), pl.BlockSpec((1, D), one), pl.BlockSpec((1, 1, D), mod),
                  pl.BlockSpec((1, 1, D), mod), pl.BlockSpec((E, D), one), pl.BlockSpec((E, D), one),
                  pl.BlockSpec((E, 1), one)],
        out_specs=[pl.BlockSpec((tm, D // 2), row), pl.BlockSpec((K, tm), col), pl.BlockSpec((K, tm), col),
                   pl.BlockSpec((1, 1, E), lambda i: (i, 0, 0))],
        out_shape=[jax.ShapeDtypeStruct((n_tok, D // 2), jnp.uint32), jax.ShapeDtypeStruct((K, n_tok), jnp.int32),
                   jax.ShapeDtypeStruct((K, n_tok), F32), jax.ShapeDtypeStruct((n_tok // tm, 1, E), F32)],
        compiler_params=_cparams(("parallel",)),
        name="moe_router",
    )(x, g, scale, shift, whi, wlo, rb)


def _moe_pos_kernel(idx_ref, base_ref, pos_ref):
    K, tm = idx_ref.shape
    E = base_ref.shape[1]
    idx = idx_ref[...]
    erow = lax.broadcasted_iota(jnp.int32, (E, 1), 0)
    hits = [erow == idx[j:j + 1, :] for j in range(K)]
    onehot = jnp.zeros((E, tm), F32)
    for hit in hits:
        onehot = jnp.where(hit, 1.0, onehot)
    row = lax.broadcasted_iota(jnp.int32, (tm, tm), 0)
    col = lax.broadcasted_iota(jnp.int32, (tm, tm), 1)
    before = _dot(onehot.astype(BF16), (row < col).astype(BF16))
    dest = before + base_ref[0]
    pos = [jnp.sum(jnp.where(hit, dest, 0.0), axis=0, keepdims=True) for hit in hits]
    pos_ref[...] = jnp.concatenate(pos, axis=0).astype(jnp.int32)


def _moe_plan(idx, cnt, bm, tm):
    K, n_tok = idx.shape
    E = cnt.shape[2]
    cnt = cnt[:, 0, :]
    counts = jnp.sum(cnt, axis=0)
    pcounts = jnp.ceil(counts / bm) * bm
    pends = jnp.cumsum(pcounts)
    base = (pends - pcounts)[None, :] + jnp.cumsum(cnt, axis=0) - cnt
    pos = pl.pallas_call(
        _moe_pos_kernel,
        grid=(n_tok // tm,),
        in_specs=[pl.BlockSpec((K, tm), lambda i: (0, i)), pl.BlockSpec((1, E, 1), lambda i: (i, 0, 0))],
        out_specs=pl.BlockSpec((K, tm), lambda i: (0, i)),
        out_shape=jax.ShapeDtypeStruct((K, n_tok), jnp.int32),
        compiler_params=_cparams(("parallel",)),
        name="moe_positions",
    )(idx, base[:, :, None])
    n_blocks = -(-(n_tok * K + E * (bm - 1)) // bm)
    block_start = jnp.concatenate([jnp.zeros((1,), F32), pends / bm]).astype(jnp.int32)
    return pos.T.reshape(-1), block_start, n_blocks


def _per_token_rows(tm, K, copy):
    def start(t, carry):
        for j in range(K):
            copy(t, j).start(priority=j % 2)
        return carry

    def wait(t, carry):
        for j in range(K):
            copy(t, j).wait()
        return carry

    return (lambda: lax.fori_loop(0, tm, start, 0, unroll=2)), (lambda: lax.fori_loop(0, tm, wait, 0, unroll=2))


def _moe_dispatch_kernel(pos_ref, prev_ref, h_ref, xs_in_ref, xs_hbm, stage_ref, sems, *, K):
    del xs_in_ref
    tm = h_ref.shape[0]
    i = pl.program_id(0)
    last = pl.num_programs(0) - 1
    slot = lax.bitwise_and(i, 1)

    def tile_copies(s, p_ref):
        def row_copy(t, j):
            return pltpu.make_async_copy(stage_ref.at[s, pl.ds(t, 1)], xs_hbm.at[pl.ds(p_ref[t * K + j], 1)],
                                         sems.at[s])
        return _per_token_rows(tm, K, row_copy)

    start, wait = tile_copies(slot, pos_ref)
    _, wait_prev = tile_copies(1 - slot, prev_ref)
    stage_ref[slot] = h_ref[...]
    start()

    @pl.when(i > 0)
    def _():
        wait_prev()

    @pl.when(i == last)
    def _():
        wait()


def _moe_dispatch(pos, h, n_rows, K, tm):
    n_tok, W = h.shape
    xs0 = jnp.zeros((n_rows, W), h.dtype)
    return pl.pallas_call(
        functools.partial(_moe_dispatch_kernel, K=K),
        grid=(n_tok // tm,),
        in_specs=[pl.BlockSpec((tm * K,), lambda i: (i,), memory_space=pltpu.SMEM),
                  pl.BlockSpec((tm * K,), lambda i: (jnp.maximum(i - 1, 0),), memory_space=pltpu.SMEM),
                  pl.BlockSpec((tm, W), lambda i: (i, 0)),
                  pl.BlockSpec(memory_space=pl.ANY)],
        out_specs=pl.BlockSpec(memory_space=pl.ANY),
        out_shape=jax.ShapeDtypeStruct((n_rows, W), h.dtype),
        scratch_shapes=[pltpu.VMEM((2, tm, W), h.dtype), pltpu.SemaphoreType.DMA((2,))],
        input_output_aliases={3: 0},
        compiler_params=_cparams(("arbitrary",)),
        name="moe_dispatch",
    )(pos, pos, h, xs0)


def _moe_expert_kernel(bs_ref, x_hbm, w1_ref, w3_ref, w2_ref, y_hbm, xbuf, ybuf, w1b_ref, w3b_ref, w2b_ref,
                       xsem, ysem, *, bm, nbuf, n_blocks):
    e = pl.program_id(0)
    last = pl.num_programs(0) - 1
    b0 = bs_ref[e]
    b1 = bs_ref[e + 1]
    n_used = bs_ref[last + 1]

    def x_copy(g):
        slot = lax.bitwise_and(g, nbuf - 1)
        return pltpu.make_async_copy(x_hbm.at[pl.ds(g * bm, bm)], xbuf.at[slot], xsem.at[slot])

    def y_copy(g):
        slot = lax.bitwise_and(g, nbuf - 1)
        return pltpu.make_async_copy(ybuf.at[slot], y_hbm.at[pl.ds(g * bm, bm)], ysem.at[slot])

    @pl.when(e == 0)
    def _():
        for p in range(nbuf - 1):
            @pl.when(p < n_used)
            def _():
                x_copy(p).start()

    @pl.when(b1 > b0)
    def _():
        w1b_ref[...] = w1_ref[0, 0].astype(BF16)
        w3b_ref[...] = w3_ref[0, 0].astype(BF16)
        w2b_ref[...] = w2_ref[0, 0].astype(BF16)

    def block(g, carry):
        slot = lax.bitwise_and(g, nbuf - 1)
        x_copy(g).wait()

        @pl.when(g + nbuf - 1 < n_used)
        def _():
            x_copy(g + nbuf - 1).start()

        @pl.when(g >= nbuf)
        def _():
            y_copy(g - nbuf).wait()

        x = jnp.concatenate(_unpack_rows(xbuf[slot]), axis=1).astype(BF16)
        a = _dot(x, w1b_ref[...])
        b = _dot(x, w3b_ref[...])
        ybuf[slot] = _pack_rows(_dot((jax.nn.silu(a) * b).astype(BF16), w2b_ref[...]))
        y_copy(g).start()
        return carry

    lax.fori_loop(b0, b1, block, 0)

    @pl.when(e == last)
    def _():
        def drain(g, carry):
            y_copy(g).wait()
            return carry

        lax.fori_loop(jnp.maximum(n_used - nbuf, 0), n_used, drain, 0)
        ybuf[0] = jnp.zeros(ybuf.shape[1:], ybuf.dtype)

        def zero_copy(g):
            return pltpu.make_async_copy(ybuf.at[0], y_hbm.at[pl.ds(g * bm, bm)], ysem.at[0])

        def fill(g, carry):
            zero_copy(g).start()
            return carry

        def fill_wait(g, carry):
            zero_copy(g).wait()
            return carry

        lax.fori_loop(n_used, n_blocks, fill, 0)
        lax.fori_loop(n_used, n_blocks, fill_wait, 0)


def _moe_experts(block_start, xs, w1, w3, w2, layer, bm):
    P, W = xs.shape
    E, D, F = w1.shape[1:]
    nbuf = 4
    grid_spec = pltpu.PrefetchScalarGridSpec(
        num_scalar_prefetch=1,
        grid=(E,),
        in_specs=[pl.BlockSpec(memory_space=pl.ANY),
                  pl.BlockSpec((1, 1, D, F), lambda e, bs: (layer, e, 0, 0)),
                  pl.BlockSpec((1, 1, D, F), lambda e, bs: (layer, e, 0, 0)),
                  pl.BlockSpec((1, 1, F, D), lambda e, bs: (layer, e, 0, 0))],
        out_specs=pl.BlockSpec(memory_space=pl.ANY),
        scratch_shapes=[pltpu.VMEM((nbuf, bm, W), xs.dtype), pltpu.VMEM((nbuf, bm, W), xs.dtype),
                        pltpu.VMEM((D, F), BF16), pltpu.VMEM((D, F), BF16), pltpu.VMEM((F, D), BF16),
                        pltpu.SemaphoreType.DMA((nbuf,)), pltpu.SemaphoreType.DMA((nbuf,))],
    )
    return pl.pallas_call(
        functools.partial(_moe_expert_kernel, bm=bm, nbuf=nbuf, n_blocks=P // bm),
        grid_spec=grid_spec,
        out_shape=jax.ShapeDtypeStruct((P, W), xs.dtype),
        compiler_params=_cparams(("arbitrary",)),
        name="moe_experts",
    )(block_start, xs, w1, w3, w2)


def _moe_combine_kernel(pos_ref, next_ref, wts_ref, x_ref, h_ref, sw1_ref, sw3_ref, sw2_ref, gate_ref, ys_hbm, o_ref,
                        gath_ref, sems, *, K):
    tm = x_ref.shape[0]
    i = pl.program_id(0)
    last = pl.num_programs(0) - 1

    slot = lax.bitwise_and(i, 1)

    def tile_copies(s, p_ref):
        def row_copy(t, j):
            return pltpu.make_async_copy(ys_hbm.at[pl.ds(p_ref[t * K + j], 1)], gath_ref.at[s, j, pl.ds(t, 1)],
                                         sems.at[s])
        return _per_token_rows(tm, K, row_copy)

    start, wait = tile_copies(slot, pos_ref)
    start_next, _ = tile_copies(1 - slot, next_ref)

    @pl.when(i == 0)
    def _():
        start()

    @pl.when(i < last)
    def _():
        start_next()

    h = jnp.concatenate(_unpack_rows(h_ref[...]), axis=1).astype(BF16)
    mid = jax.nn.silu(_dot(h, sw1_ref[...])) * _dot(h, sw3_ref[...])
    shared = _dot(mid.astype(BF16), sw2_ref[...])
    wait()
    wts = wts_ref[...]
    W = gath_ref.shape[3]
    acc_lo = shared[:, :W]
    acc_hi = shared[:, W:]
    for j in range(K):
        lo, hi = _unpack_rows(gath_ref[slot, j])
        acc_lo = acc_lo + wts[:, j:j + 1] * lo
        acc_hi = acc_hi + wts[:, j:j + 1] * hi
    gate = gate_ref[0]
    o_ref[:, :W] = x_ref[:, :W] + gate[:, :W] * acc_lo
    o_ref[:, W:] = x_ref[:, W:] + gate[:, W:] * acc_hi


def _moe_combine(pos, wts, x, h, sw1, sw3, sw2, gate, ys, n_tok, K, seg_rows, tm):
    D = x.shape[1]
    W = ys.shape[1]
    n_tiles = n_tok // tm
    last = gate.shape[0] - 1
    row = lambda i: (i, 0)
    one = lambda i: (0, 0)
    return pl.pallas_call(
        functools.partial(_moe_combine_kernel, K=K),
        grid=(n_tiles,),
        in_specs=[pl.BlockSpec((tm * K,), lambda i: (i,), memory_space=pltpu.SMEM),
                  pl.BlockSpec((tm * K,), lambda i: (jnp.minimum(i + 1, n_tiles - 1),), memory_space=pltpu.SMEM),
                  pl.BlockSpec((tm, K), row), pl.BlockSpec((tm, D), row), pl.BlockSpec((tm, W), row),
                  pl.BlockSpec(sw1.shape, one), pl.BlockSpec(sw3.shape, one), pl.BlockSpec(sw2.shape, one),
                  pl.BlockSpec((1, 1, D), lambda i: (jnp.minimum(i * tm // seg_rows, last), 0, 0)),
                  pl.BlockSpec(memory_space=pl.ANY)],
        out_specs=pl.BlockSpec((tm, D), row),
        out_shape=jax.ShapeDtypeStruct((n_tok, D), F32),
        scratch_shapes=[pltpu.VMEM((2, K, tm, W), ys.dtype), pltpu.SemaphoreType.DMA((2,))],
        compiler_params=_cparams(("arbitrary",)),
        name="moe_combine",
    )(pos, pos, wts, x, h, sw1, sw3, sw2, gate, ys)


def _moe_ffn_residual(x, n_tok, norm_g, scale, shift, gate, router_w, router_b, ew1, ew3, ew2, layer, sw1, sw3, sw2,
                      seg_rows, bm):
    K = TOP_K
    tm = 256
    whi, wlo = _split2(router_w.T)
    h, idx, wts, cnt = _moe_router(x, n_tok, norm_g, scale, shift, whi, wlo, router_b[:, None], seg_rows, tm)
    pos, block_start, n_blocks = _moe_plan(idx, cnt, bm, tm)
    wts = wts.T
    xs = _moe_dispatch(pos, h, n_blocks * bm, K, tm)
    ys = _moe_experts(block_start, xs, ew1, ew3, ew2, layer, bm)
    return _moe_combine(pos, wts, x, h, sw1.astype(BF16), sw3.astype(BF16), sw2.astype(BF16), gate, ys,
                        n_tok, K, seg_rows, tm)


def kernel(x, c, ctx, c_ctx, w_mod, b_mod, norm1_g, norm2_g, w_in, mlstm_gate_b, mlstm_norm_g, da_qnorm_g, da_knorm_g, da_lambda, da_subln_g, hy_conv_w, hy_conv_b, hy_w1, hy_b1, hy_w2, hy_b2, hy_w3, hy_freq, hy_skip, w_out, router_w, router_b, exp_w1, exp_w3, exp_w2, sh_w1, sh_w3, sh_w2):
    B, S, D = x.shape
    n_ctx = ctx.shape[1]
    depth = w_in.shape[0]
    n_lat = B * S
    ml_w = mlstm_norm_g.shape[1]
    da_dh = da_qnorm_g.shape[1]
    da_w = DA_HEADS * 2 * da_dh
    hy_w = hy_skip.shape[2]
    n_gates = 4 * ML_HEADS
    da_col = 0
    hy_col = da_col + 3 * da_w
    o_col = hy_col + 3 * hy_w
    gate_col = o_col + ml_w
    ml_in = 4 * ml_w + n_gates
    tm = 512

    X = jnp.concatenate([x.reshape(n_lat, D), ctx.reshape(B * n_ctx, D)], axis=0)
    sc = jax.nn.silu(jnp.concatenate([c, c_ctx[None]], axis=0))
    cos, sin = _axial_rope_tables(S, da_dh, da_w // da_dh, tm)
    mats_l = _dft_mats(S)
    mats_c = _dft_mats(n_ctx)
    for l in range(depth):
        last = l == depth - 1
        lam_init = 0.8 - 0.6 * math.exp(-0.3 * l)
        mods = (jnp.dot(sc, w_mod[l], precision=lax.Precision.HIGHEST) + b_mod[l]).reshape(B + 1, 6, 1, D)
        sh1, s1, g1, sh2, s2, g2 = [mods[:, i] for i in range(6)]
        wl = w_in[l]
        w_big = jnp.concatenate([wl[:, ml_in:], wl[:, 3 * ml_w:4 * ml_w], wl[:, 4 * ml_w:ml_in],
                                 jnp.zeros((D, LANES - n_gates), F32)], axis=1).astype(BF16)
        w_qkv_t = wl[:, :3 * ml_w].T.astype(BF16)
        U, UT = _norm_mod_matmul(X, norm1_g[l][None], s1, sh1, w_big, w_qkv_t, S, tm, w_big.shape[1] // 3)
        gb = jnp.concatenate([mlstm_gate_b[l], jnp.zeros((LANES - n_gates,), F32)])[None]
        m_out = _mlstm_mixer(UT, U, gb, mlstm_norm_g[l][:, None], B, S, n_ctx, o_col, gate_col)
        d_out = _diff_attn_mixer(U, cos, sin, da_qnorm_g[l], da_knorm_g[l], da_lambda[l], da_subln_g[l], lam_init,
                                 B, S, n_ctx, da_col, not last)
        hy_args = (hy_w1[l], hy_b1[l], hy_w2[l], hy_b2[l], hy_w3[l], hy_freq[l], hy_w)
        y_out = _hyena_seq(U, mats_l, _hyena_spectrum(mats_l, S, *hy_args), hy_conv_w[l], hy_conv_b[l][None],
                           hy_skip[l], B, S, 0, hy_col)
        n_rows = n_lat
        if not last:
            y_ctx = _hyena_seq(U, mats_c, _hyena_spectrum(mats_c, n_ctx, *hy_args), hy_conv_w[l], hy_conv_b[l][None],
                               hy_skip[l], B, n_ctx, n_lat // n_ctx, hy_col)
            y_out = jnp.concatenate([y_out, y_ctx], axis=0)
            n_rows = n_lat + B * n_ctx
        wo = w_out[l].astype(BF16)
        X = _out_proj_residual(m_out, d_out, y_out, wo[:ml_w], wo[ml_w:ml_w + da_w], wo[ml_w + da_w:], X, g1,
                               n_rows, S, tm)
        X = _moe_ffn_residual(X, n_rows, norm2_g[l][None], s2, sh2, g2, router_w[l], router_b[l],
                              exp_w1, exp_w3, exp_w2, l, sh_w1[l], sh_w3[l], sh_w2[l], S,
                              MOE_BLOCK)
    return X[:n_lat].reshape(B, S, D)
```
